```python
import math
import jax, jax.numpy as jnp
from jax import lax
import numpy as np

D_MODEL = 1024
BATCH = 4
SEQ = 8192
DEPTH = 1

HEAD_DIM = 64
MOBA_HEADS = 6
DSA_HEADS = 6
MEM_HEADS = 4
MOBA_W = MOBA_HEADS * HEAD_DIM
DSA_W = DSA_HEADS * HEAD_DIM
MEM_W = MEM_HEADS * HEAD_DIM
IDX_HEADS = 8
IDX_DIM = 64
MOBA_BLOCK = 256
MOBA_TOPK = 3
DSA_TOPK = 256
N_MEM = 256
REL_BUCKETS = 32
REL_MAX_DIST = 128
Q_CHUNK = 128
EPS = 1e-6
IN_SIZES = (MOBA_W,) * 4 + (DSA_W,) * 4 + (IDX_HEADS * IDX_DIM, IDX_DIM, IDX_HEADS) + (MEM_W, MEM_W) + (D_MODEL,) * 3
IN_WIDTH = sum(IN_SIZES)
BRANCH_WIDTH = MOBA_W + DSA_W + MEM_W

kernel_name = 'hybrid_moba_dsa_memory_gated_block'


def rmsnorm(x, g):
    xf = x.astype(jnp.float32)
    y = xf * lax.rsqrt(jnp.mean(xf * xf, axis=-1, keepdims=True) + EPS)
    return (y * g.astype(jnp.float32)).astype(x.dtype)


def rel_bucket(dist):
    n = jnp.maximum(dist, 0)
    exact = REL_BUCKETS // 2
    nf = jnp.maximum(n, 1).astype(jnp.float32)
    large = exact + (jnp.log(nf / exact) / math.log(REL_MAX_DIST / exact) * (REL_BUCKETS - exact)).astype(jnp.int32)
    return jnp.where(n < exact, n, jnp.minimum(large, REL_BUCKETS - 1))


def split_cols(t, sizes):
    outs, off = [], 0
    for s in sizes:
        outs.append(t[..., off:off + s])
        off += s
    return outs


def masked_softmax(logits, mask):
    return jax.nn.softmax(jnp.where(mask, logits, -jnp.inf), axis=-1)


def moba_attention(q, k, v, bias_hb):
    b, t_len, h, dh = q.shape
    nb = -(-t_len // MOBA_BLOCK)
    padw = ((0, 0), (0, nb * MOBA_BLOCK - t_len), (0, 0), (0, 0))
    k_bh = jnp.pad(k, padw).reshape(b, nb, MOBA_BLOCK, h, dh).transpose(0, 3, 1, 2, 4)
    v_bh = jnp.pad(v, padw).reshape(b, nb, MOBA_BLOCK, h, dh).transpose(0, 3, 1, 2, 4)
    k_mean = jnp.mean(k_bh.astype(jnp.float32), axis=3)
    n_sel = min(MOBA_TOPK, nb)
    n_past = n_sel * MOBA_BLOCK
    scale = dh ** -0.5
    n_chunks = t_len // Q_CHUNK
    q_c = q.reshape(b, n_chunks, Q_CHUNK, h, dh).transpose(1, 0, 3, 2, 4)
    blk = jnp.arange(nb)
    offs = jnp.arange(MOBA_BLOCK)
    h_idx = jnp.arange(h)[None, :, None, None, None]
    gather = jax.vmap(jax.vmap(lambda kb, ib: kb[ib]))

    def one_chunk(args):
        ci, qc = args
        t = ci * Q_CHUNK + jnp.arange(Q_CHUNK)
        own = (ci * Q_CHUNK) // MOBA_BLOCK
        gate = jnp.einsum('bhqd,bhnd->bhqn', qc.astype(jnp.float32), k_mean)
        gate = jnp.where(blk < own, gate, -jnp.inf)
        _, sel = lax.top_k(gate, n_sel)
        k_sel = gather(k_bh, sel)
        v_sel = gather(v_bh, sel)
        lp = jnp.einsum('bhqd,bhqnsd->bhqns', qc, k_sel, preferred_element_type=jnp.float32) * scale
        pos = sel[..., None] * MOBA_BLOCK + offs
        lp = lp + bias_hb[h_idx, rel_bucket(t[:, None, None] - pos)]
        mp = jnp.broadcast_to((sel < own)[..., None], lp.shape)
        k_own = lax.dynamic_index_in_dim(k_bh, own, axis=2, keepdims=False)
        v_own = lax.dynamic_index_in_dim(v_bh, own, axis=2, keepdims=False)
        lo = jnp.einsum('bhqd,bhsd->bhqs', qc, k_own, preferred_element_type=jnp.float32) * scale
        dist = t[:, None] - (own * MOBA_BLOCK + offs)[None, :]
        lo = lo + bias_hb[:, rel_bucket(dist)][None]
        mo = jnp.broadcast_to(dist >= 0, lo.shape)
        logits = jnp.concatenate([lp.reshape(b, h, Q_CHUNK, n_past), lo], axis=-1)
        mask = jnp.concatenate([mp.reshape(b, h, Q_CHUNK, n_past), mo], axis=-1)
        p = masked_softmax(logits, mask).astype(v.dtype)
        p_past = p[..., :n_past].reshape(b, h, Q_CHUNK, n_sel, MOBA_BLOCK)
        return (jnp.einsum('bhqns,bhqnsd->bqhd', p_past, v_sel)
                + jnp.einsum('bhqs,bhsd->bqhd', p[..., n_past:], v_own))

    out = lax.map(one_chunk, (jnp.arange(n_chunks), q_c))
    return out.transpose(1, 0, 2, 3, 4).reshape(b, t_len, h * dh)


def dsa_attention(q, k, v, iq, ik, iw, bias_hb):
    b, t_len, h, dh = q.shape
    n_top = min(DSA_TOPK, t_len // 4)
    scale = dh ** -0.5
    n_chunks = t_len // Q_CHUNK
    q_c = q.reshape(b, n_chunks, Q_CHUNK, h, dh).transpose(1, 0, 2, 3, 4)
    iq_c = iq.reshape(b, n_chunks, Q_CHUNK, IDX_HEADS, IDX_DIM).transpose(1, 0, 2, 3, 4)
    iw_c = iw.reshape(b, n_chunks, Q_CHUNK, IDX_HEADS).transpose(1, 0, 2, 3)
    keys = jnp.arange(t_len)
    gather = jax.vmap(lambda a, i: a[i])

    def one_chunk(args):
        ci, qc, iqc, iwc = args
        t = ci * Q_CHUNK + jnp.arange(Q_CHUNK)
        rel = jax.nn.relu(jnp.einsum('bqhd,bsd->bqhs', iqc, ik, preferred_element_type=jnp.float32) * IDX_DIM ** -0.5)
        score = jnp.einsum('bqh,bqhs->bqs', iwc.astype(jnp.float32) * IDX_HEADS ** -0.5, rel)
        score = jnp.where(keys[None, None, :] <= t[None, :, None], score, -jnp.inf)
        _, idx = lax.top_k(score, n_top)
        k_sel = gather(k, idx)
        v_sel = gather(v, idx)
        logits = jnp.einsum('bqhd,bqkhd->bhqk', qc, k_sel, preferred_element_type=jnp.float32) * scale
        dist = t[None, :, None] - idx
        logits = logits + jnp.moveaxis(bias_hb[:, rel_bucket(dist)], 0, 1)
        mask = jnp.broadcast_to((dist >= 0)[:, None], logits.shape)
        p = masked_softmax(logits, mask).astype(v.dtype)
        return jnp.einsum('bhqk,bqkhd->bqhd', p, v_sel)

    out = lax.map(one_chunk, (jnp.arange(n_chunks), q_c, iq_c, iw_c))
    return out.transpose(1, 0, 2, 3, 4).reshape(b, t_len, h * dh)


def memory_attention(q, mem_k, mem_v):
    logits = jnp.einsum('bthd,bmhd->bhtm', q, mem_k, preferred_element_type=jnp.float32) * HEAD_DIM ** -0.5
    p = jax.nn.softmax(logits, axis=-1).astype(mem_v.dtype)
    return jnp.einsum('bhtm,bmhd->bthd', p, mem_v)


def hybrid_layer(x, mem, norm_gain, w_in, rel_bias, mem_norm_gain, w_mem_kv, w_branch, w_out):
    b, t_len, _ = x.shape
    n_mem = mem.shape[1]
    heads = lambda a: a.reshape(a.shape[0], a.shape[1], -1, HEAD_DIM)
    h = rmsnorm(x, norm_gain)
    proj = jnp.einsum('btd,dp->btp', h, w_in)
    (qa, ka, va, za, qb, kb, vb, zb, iq, ik, iw, qm, zm, ga, gb, gm) = split_cols(proj, IN_SIZES)
    bias_a = rel_bias[:, :MOBA_HEADS].T
    bias_b = rel_bias[:, MOBA_HEADS:].T
    ya = moba_attention(heads(qa), heads(ka), heads(va), bias_a) * jax.nn.silu(za)
    yb = dsa_attention(heads(qb), heads(kb), heads(vb),
                       iq.reshape(b, t_len, IDX_HEADS, IDX_DIM), ik, iw, bias_b) * jax.nn.silu(zb)
    mem_n = rmsnorm(mem, mem_norm_gain)
    mk, mv = split_cols(jnp.einsum('bmd,dp->bmp', mem_n, w_mem_kv), (MEM_W, MEM_W))
    ym = memory_attention(heads(qm), heads(mk), heads(mv)).reshape(b, t_len, MEM_W) * jax.nn.silu(zm)
    pa = w_branch[:MOBA_W]
    pb = w_branch[MOBA_W:MOBA_W + DSA_W]
    pm = w_branch[MOBA_W + DSA_W:]
    merged = (jax.nn.sigmoid(ga) * jnp.einsum('btc,cd->btd', ya, pa)
              + jax.nn.sigmoid(gb) * jnp.einsum('btc,cd->btd', yb, pb)
              + jax.nn.sigmoid(gm) * jnp.einsum('btc,cd->btd', ym, pm))
    return x + jnp.einsum('btd,de->bte', merged, w_out)


def setup_inputs(seed: int = 0) -> dict:
    key = jax.random.key(seed)
    ks = jax.random.split(key, 10)
    f = jnp.float32
    nrm = jax.random.normal
    return {
        'x': nrm(ks[0], (BATCH, SEQ, D_MODEL), f),
        'mem': nrm(ks[1], (BATCH, N_MEM, D_MODEL), f),
        'norm_gain': 1.0 + 0.02 * nrm(ks[2], (DEPTH, D_MODEL), f),
        'w_in': nrm(ks[3], (DEPTH, D_MODEL, IN_WIDTH), f) * D_MODEL ** -0.5,
        'rel_bias': 0.5 * nrm(ks[4], (REL_BUCKETS, MOBA_HEADS + DSA_HEADS), f),
        'mem_norm_gain': 1.0 + 0.02 * nrm(ks[5], (DEPTH, D_MODEL), f),
        'w_mem_kv': nrm(ks[6], (DEPTH, D_MODEL, 2 * MEM_W), f) * D_MODEL ** -0.5,
        'w_branch': nrm(ks[7], (DEPTH, BRANCH_WIDTH, D_MODEL), f) * MOBA_W ** -0.5,
        'w_out': nrm(ks[8], (DEPTH, D_MODEL, D_MODEL), f) * D_MODEL ** -0.5,
        'final_norm_gain': 1.0 + 0.02 * nrm(ks[9], (D_MODEL,), f),
    }


def reference(x, mem, norm_gain, w_in, rel_bias, mem_norm_gain, w_mem_kv, w_branch, w_out, final_norm_gain):
    for layer in range(DEPTH):
        x = hybrid_layer(x, mem, norm_gain[layer], w_in[layer], rel_bias, mem_norm_gain[layer],
                         w_mem_kv[layer], w_branch[layer], w_out[layer])
    return rmsnorm(x, final_norm_gain)
```

```python
import functools
import math

import numpy as np
import jax
import jax.numpy as jnp
from jax import lax
from jax.experimental import pallas as pl
from jax.experimental.pallas import tpu as pltpu

D_MODEL = 1024
HEAD_DIM = 64
MOBA_HEADS = 6
DSA_HEADS = 6
MEM_HEADS = 4
MOBA_W = MOBA_HEADS * HEAD_DIM
DSA_W = DSA_HEADS * HEAD_DIM
MEM_W = MEM_HEADS * HEAD_DIM
IDX_HEADS = 8
IDX_DIM = 64
MOBA_BLOCK = 256
MOBA_TOPK = 3
DSA_TOPK = 256
REL_BUCKETS = 32
REL_MAX_DIST = 128
EPS = 1e-6
IN_SIZES = (MOBA_W,) * 4 + (DSA_W,) * 4 + (IDX_HEADS * IDX_DIM, IDX_DIM, IDX_HEADS) + (MEM_W, MEM_W) + (D_MODEL,) * 3

LANES = 128
TILE = 256
ROW_TILE = 512
LOG2E = math.log2(math.e)
NEG_BIG = -1e30
VMEM_LIMIT = 56 * 1024 * 1024
MAX_SEARCH_ITERS = 96

_NT = (((1,), (1,)), ((), ()))


def _dot_nt(a, b):
    return lax.dot_general(a, b, _NT, preferred_element_type=jnp.float32)


def _dot(a, b):
    return jnp.dot(a, b, preferred_element_type=jnp.float32)


def _rel_bucket_np(n):
    exact = REL_BUCKETS // 2
    nf = np.maximum(n, 1).astype(np.float32)
    large = exact + (np.log(nf / np.float32(exact)) / np.float32(math.log(REL_MAX_DIST / exact))
                     * np.float32(REL_BUCKETS - exact)).astype(np.int32)
    return np.where(n < exact, n, np.minimum(large, REL_BUCKETS - 1))


def _bias_tables(bias_hb, causal_fill):
    i = np.arange(TILE)[:, None]
    j = np.arange(TILE)[None, :]
    d0 = i - j
    b0 = _rel_bucket_np(np.maximum(d0, 0))
    b1 = _rel_bucket_np(TILE + d0)
    t0 = jnp.where(jnp.asarray(d0 >= 0), bias_hb[:, b0], causal_fill / LOG2E)
    t1 = bias_hb[:, b1]
    assert int(_rel_bucket_np(np.array([TILE + 1]))[0]) == REL_BUCKETS - 1
    return jnp.stack([t0, t1], axis=1) * LOG2E, bias_hb[:, REL_BUCKETS - 1] * LOG2E


def _rmsnorm_rows(x, g):
    xf = x.astype(jnp.float32)
    return xf * lax.rsqrt(jnp.mean(xf * xf, axis=-1, keepdims=True) + EPS) * g


_PROJ_SEGS = (
    ("qa", MOBA_W, HEAD_DIM ** -0.5 * LOG2E), ("ka", MOBA_W, 1.0), ("va", MOBA_W, 1.0), ("za", MOBA_W, 1.0),
    ("qb", DSA_W, HEAD_DIM ** -0.5 * LOG2E), ("kb", DSA_W, 1.0), ("vb", DSA_W, 1.0), ("zb", DSA_W, 1.0),
    ("iq", IDX_HEADS * IDX_DIM, IDX_DIM ** -0.5), ("ik2", 2 * IDX_DIM, 1.0),
    ("qm", MEM_W, HEAD_DIM ** -0.5 * LOG2E), ("zm", MEM_W, 1.0),
)
_PROJ_W = sum(w for _, w, _ in _PROJ_SEGS)


def _proj_kernel(x_ref, g_ref, w_ref, wiw_ref, *out_refs):
    seg_refs = out_refs[:len(_PROJ_SEGS)]
    iw_ref, kmean_ref = out_refs[len(_PROJ_SEGS):]
    h = _rmsnorm_rows(x_ref[...], g_ref[...]).astype(jnp.bfloat16)
    off = 0
    for (name, width, scale), o_ref in zip(_PROJ_SEGS, seg_refs):
        r = _dot(h, w_ref[:, off:off + width])
        if name == "ka":
            for blk in range(ROW_TILE // MOBA_BLOCK):
                kmean_ref[0, blk:blk + 1, :] = jnp.mean(
                    r[blk * MOBA_BLOCK:(blk + 1) * MOBA_BLOCK], axis=0, keepdims=True)
        if scale != 1.0:
            r = r * scale
        o_ref[...] = r.astype(o_ref.dtype)
        off += width
    iw_ref[...] = _dot(h, wiw_ref[...]) * IDX_HEADS ** -0.5


def _proj_call(x2, gain, w_all, w_iw):
    rows = x2.shape[0]
    n_tiles = rows // ROW_TILE
    row_spec = lambda width: pl.BlockSpec((ROW_TILE, width), lambda i: (i, 0))
    const_spec = lambda shape: pl.BlockSpec(shape, lambda i: (0,) * len(shape))
    out_shape = [jax.ShapeDtypeStruct((rows, w), jnp.bfloat16) for _, w, _ in _PROJ_SEGS]
    out_specs = [row_spec(w) for _, w, _ in _PROJ_SEGS]
    out_shape += [jax.ShapeDtypeStruct((rows, LANES), jnp.float32),
                  jax.ShapeDtypeStruct((n_tiles, ROW_TILE // MOBA_BLOCK, MOBA_W), jnp.float32)]
    out_specs += [row_spec(LANES), pl.BlockSpec((1, ROW_TILE // MOBA_BLOCK, MOBA_W), lambda i: (i, 0, 0))]
    return pl.pallas_call(
        _proj_kernel,
        grid=(n_tiles,),
        in_specs=[row_spec(D_MODEL), const_spec((1, D_MODEL)), const_spec((D_MODEL, _PROJ_W)),
                  const_spec((D_MODEL, LANES))],
        out_specs=out_specs,
        out_shape=out_shape,
        compiler_params=pltpu.CompilerParams(dimension_semantics=("arbitrary",), vmem_limit_bytes=VMEM_LIMIT),
        name="proj",
    )(x2, gain, w_all, w_iw)


def _memkv_kernel(mem_ref, g_ref, w_ref, mk_ref, mv_ref):
    h = _rmsnorm_rows(mem_ref[0], g_ref[...]).astype(jnp.bfloat16)
    r = _dot(h, w_ref[...])
    mk_ref[0] = r[:, :MEM_W].astype(jnp.bfloat16)
    mv_ref[0] = r[:, MEM_W:].astype(jnp.bfloat16)


def _memkv_call(mem, gain, w_kv):
    b, n_mem, _ = mem.shape
    return pl.pallas_call(
        _memkv_kernel,
        grid=(b,),
        in_specs=[pl.BlockSpec((1, n_mem, D_MODEL), lambda i: (i, 0, 0)),
                  pl.BlockSpec((1, D_MODEL), lambda i: (0, 0)),
                  pl.BlockSpec((D_MODEL, 2 * MEM_W), lambda i: (0, 0))],
        out_specs=[pl.BlockSpec((1, n_mem, MEM_W), lambda i: (i, 0, 0))] * 2,
        out_shape=[jax.ShapeDtypeStruct((b, n_mem, MEM_W), jnp.bfloat16)] * 2,
        compiler_params=pltpu.CompilerParams(dimension_semantics=("arbitrary",), vmem_limit_bytes=VMEM_LIMIT),
        name="memkv",
    )(mem, gain, w_kv)


def _half_mask(shape, half):
    lane = lax.broadcasted_iota(jnp.int32, shape, 1)
    return (lane >= half * HEAD_DIM) & (lane < (half + 1) * HEAD_DIM)


def _softmax_step(s, v_t, m_ref, l_ref, acc_ref, idx):
    m_old = m_ref[idx][:, :1]
    m_new = jnp.maximum(m_old, jnp.max(s, axis=1, keepdims=True))
    alpha = jnp.exp2(m_old - m_new)
    p = jnp.exp2(s - m_new)
    l_ref[idx] = jnp.broadcast_to(alpha * l_ref[idx][:, :1] + jnp.sum(p, axis=1, keepdims=True), (TILE, LANES))
    m_ref[idx] = jnp.broadcast_to(m_new, (TILE, LANES))
    acc_ref[idx] = alpha * acc_ref[idx] + _dot(p.astype(jnp.bfloat16), v_t)


def _init_softmax_state(m_ref, l_ref, acc_ref):
    m_ref[...] = jnp.full(m_ref.shape, NEG_BIG, jnp.float32)
    l_ref[...] = jnp.zeros(l_ref.shape, jnp.float32)
    acc_ref[...] = jnp.zeros(acc_ref.shape, jnp.float32)


def _moba_kernel(b31_ref, q_ref, k_ref, v_ref, kmean_ref, tab_ref, o_ref, qaug_ref, m_ref, l_ref, acc_ref):
    pair = pl.program_id(1)
    qi = pl.program_id(2)
    q = q_ref[0]
    kmean = kmean_ref[0].astype(jnp.bfloat16)
    lane = lax.broadcasted_iota(jnp.int32, (TILE, LANES), 1)
    lane_f = lane.astype(jnp.float32)
    past = lane < qi
    _init_softmax_state(m_ref, l_ref, acc_ref)

    for half in range(2):
        qh = jnp.where(_half_mask((TILE, LANES), half), q, jnp.zeros_like(q))
        gate = _dot_nt(qh, kmean)
        g = jnp.where(past, gate, -jnp.inf)
        sel = jnp.zeros((TILE, LANES), jnp.bool_)
        for _ in range(MOBA_TOPK):
            mx = jnp.max(g, axis=1, keepdims=True)
            first = jnp.min(jnp.where(g == mx, lane_f, float(LANES)), axis=1, keepdims=True)
            pick = lane_f == first
            sel = sel | (pick & past)
            g = jnp.where(pick, -jnp.inf, g)
        b31 = jnp.full((TILE, LANES), b31_ref[pair * 2 + half], jnp.float32)
        b31_hi = b31.astype(jnp.bfloat16).astype(jnp.float32)
        far = lane < qi - 1
        hi_part = jnp.where(sel, jnp.where(far, b31_hi, 0.0), NEG_BIG)
        hi_part = jnp.where(past, hi_part, 0.0)
        lo_sel = jnp.where(lane >= 32, lane - 32, LANES) < qi - 1
        lo_part = jnp.where(lo_sel, b31 - b31_hi, 0.0)
        feat = jnp.where(lane < 32, hi_part, lo_part).astype(jnp.bfloat16)
        qaug_ref[half] = jnp.concatenate([qh, feat], axis=1)

    def tile(kj, kind):
        row0 = pl.multiple_of(kj * TILE, TILE)
        k_t = k_ref[0, pl.ds(row0, TILE), :]
        v_t = v_ref[0, pl.ds(row0, TILE), :]
        onehot = ((lane == kj) | (lane == kj + 32)).astype(jnp.bfloat16)
        k_aug = jnp.concatenate([k_t, onehot], axis=1)
        for half in range(2):
            s = _dot_nt(qaug_ref[half], k_aug)
            if kind < 2:
                s = s + tab_ref[half, kind]
            _softmax_step(s, v_t, m_ref, l_ref, acc_ref, half)

    def far_body(kj, carry):
        tile(kj, 2)
        return carry

    lax.fori_loop(0, jnp.maximum(qi - 1, 0), far_body, 0)

    @pl.when(qi >= 1)
    def _():
        tile(qi - 1, 1)

    tile(qi, 0)
    out0 = acc_ref[0] / l_ref[0]
    out1 = acc_ref[1] / l_ref[1]
    o_ref[0] = jnp.where(lane < HEAD_DIM, out0, out1).astype(o_ref.dtype)


def _moba_call(q, k, v, kmean, tables, b31):
    b, t, _ = q.shape
    n_pairs = MOBA_HEADS // 2
    grid_spec = pltpu.PrefetchScalarGridSpec(
        num_scalar_prefetch=1,
        grid=(b, n_pairs, t // TILE),
        in_specs=[
            pl.BlockSpec((1, TILE, LANES), lambda bi, p, qi, s: (bi, qi, p)),
            pl.BlockSpec((1, t, LANES), lambda bi, p, qi, s: (bi, 0, p)),
            pl.BlockSpec((1, t, LANES), lambda bi, p, qi, s: (bi, 0, p)),
            pl.BlockSpec((1, LANES, LANES), lambda bi, p, qi, s: (bi, 0, p)),
            pl.BlockSpec((2, 2, TILE, TILE), lambda bi, p, qi, s: (p, 0, 0, 0)),
        ],
        out_specs=pl.BlockSpec((1, TILE, LANES), lambda bi, p, qi, s: (bi, qi, p)),
        scratch_shapes=[
            pltpu.VMEM((2, TILE, 2 * LANES), jnp.bfloat16),
            pltpu.VMEM((2, TILE, LANES), jnp.float32),
            pltpu.VMEM((2, TILE, LANES), jnp.float32),
            pltpu.VMEM((2, TILE, LANES), jnp.float32),
        ],
    )
    return pl.pallas_call(
        _moba_kernel,
        grid_spec=grid_spec,
        out_shape=jax.ShapeDtypeStruct((b, t, MOBA_W), jnp.bfloat16),
        compiler_params=pltpu.CompilerParams(dimension_semantics=("arbitrary",) * 3, vmem_limit_bytes=VMEM_LIMIT),
        name="moba",
    )(b31, q, k, v, kmean, tables)


def _dsa_kernel(b31_ref, q_ref, k_ref, v_ref, iq_ref, ik_ref, iw_ref, tab_ref, o_ref,
                sc_ref, iqm_ref, qm_ref, m_ref, l_ref, acc_ref):
    qi = pl.program_id(1)
    n_chunks = qi + 1
    lane = lax.broadcasted_iota(jnp.int32, (TILE, LANES), 1)
    row = lax.broadcasted_iota(jnp.int32, (TILE, TILE), 0)
    col = lax.broadcasted_iota(jnp.int32, (TILE, TILE), 1)
    causal = col <= row

    iw = iw_ref[0]
    for h in range(IDX_HEADS):
        blk = iq_ref[0, :, (h // 2) * LANES:(h // 2 + 1) * LANES]
        iqm_ref[h] = jnp.where(_half_mask((TILE, LANES), h % 2), blk, jnp.zeros_like(blk))
    for h in range(DSA_HEADS):
        blk = q_ref[0, :, (h // 2) * LANES:(h // 2 + 1) * LANES]
        qm_ref[h] = jnp.where(_half_mask((TILE, LANES), h % 2), blk, jnp.zeros_like(blk))

    def score_chunk(kc):
        row0 = pl.multiple_of(kc * TILE, TILE)
        ik_t = ik_ref[0, pl.ds(row0, TILE), :]
        tot = jnp.zeros((TILE, TILE), jnp.float32)
        for h in range(IDX_HEADS):
            x = _dot_nt(iqm_ref[h], ik_t)
            tot = tot + iw[:, h:h + 1] * jnp.maximum(x, 0.0)
        return tot

    def score_body(kc, carry):
        sc_ref[kc] = score_chunk(kc)
        return carry

    lax.fori_loop(0, qi, score_body, 0)
    sc_ref[qi] = jnp.where(causal, score_chunk(qi), -jnp.inf)

    def reduce_pass(fn, init, combine):
        def body(kc, part):
            x = sc_ref[kc]
            return combine(part, combine(fn(x[:, :LANES]), fn(x[:, LANES:])))
        return lax.fori_loop(0, n_chunks, body, jnp.full((TILE, LANES), init, jnp.float32))

    def count_ge(p):
        pb = jnp.broadcast_to(p, (TILE, LANES))
        part = reduce_pass(lambda x: jnp.where(x >= pb, 1.0, 0.0), 0.0, jnp.add)
        return jnp.sum(part, axis=1, keepdims=True)

    row_max = jnp.max(reduce_pass(lambda x: x, -jnp.inf, jnp.maximum), axis=1, keepdims=True)
    row_min = jnp.min(reduce_pass(lambda x: jnp.where(x == -jnp.inf, jnp.inf, x), jnp.inf, jnp.minimum),
                      axis=1, keepdims=True)
    n_adm = (qi * TILE + 1 + lax.broadcasted_iota(jnp.int32, (TILE, 1), 0)).astype(jnp.float32)
    k_top = float(DSA_TOPK)
    take_all = n_adm <= k_top

    def search_cond(st):
        it, _, _, _, _, _, done = st
        return jnp.logical_and(it < MAX_SEARCH_ITERS, jnp.min(done) < 0.5)

    def search_body(st):
        it, lo, hi, hi_fin, c_lo, c_hi, done = st
        mid = 0.5 * lo + 0.5 * hi_fin
        frac = (c_lo - k_top + 0.5) / jnp.maximum(c_lo - c_hi, 1.0)
        interp = lo + frac * (hi_fin - lo)
        p = jnp.where(it % 2 == 0, interp, mid)
        p = jnp.where((p > lo) & (p < hi), p, mid)
        stuck = jnp.logical_not((mid > lo) & (mid < hi))
        c = count_ge(p)
        up = c >= k_top
        move = jnp.logical_not((done > 0.5) | stuck)
        lo_n = jnp.where(move & up, p, lo)
        c_lo_n = jnp.where(move & up, c, c_lo)
        hi_n = jnp.where(move & jnp.logical_not(up), p, hi)
        c_hi_n = jnp.where(move & jnp.logical_not(up), c, c_hi)
        hi_fin_n = jnp.where(move & jnp.logical_not(up), p, hi_fin)
        done_n = jnp.where((c_lo_n - c_hi_n <= 1.0) | stuck, 1.0, done)
        return it + 1, lo_n, hi_n, hi_fin_n, c_lo_n, c_hi_n, done_n

    init = (jnp.int32(0), row_min, jnp.full((TILE, 1), jnp.inf, jnp.float32), row_max,
            n_adm, jnp.zeros((TILE, 1), jnp.float32), take_all.astype(jnp.float32))
    _, lo, hi, _, c_lo, c_hi, _ = lax.while_loop(search_cond, search_body, init)
    lo = jnp.where(take_all, -jnp.inf, lo)
    hi = jnp.where(take_all, jnp.inf, hi)
    need = k_top - jnp.where(take_all, 0.0, c_hi)

    _init_softmax_state(m_ref, l_ref, acc_ref)
    upper = (row < col).astype(jnp.bfloat16)

    def attend(kc, kind, seen):
        row0 = pl.multiple_of(kc * TILE, TILE)
        x = sc_ref[kc]
        above = x >= hi
        tie = (x >= lo) & jnp.logical_not(above)
        tie_f = tie.astype(jnp.bfloat16)
        rank = seen + _dot(tie_f, upper)
        selected = above | (tie & (rank < need))
        if kind == 0:
            selected = selected & causal
        seen = seen + jnp.sum(tie_f.astype(jnp.float32), axis=1, keepdims=True)
        for h in range(DSA_HEADS):
            k_t = k_ref[0, pl.ds(row0, TILE), (h // 2) * LANES:(h // 2 + 1) * LANES]
            v_t = v_ref[0, pl.ds(row0, TILE), (h // 2) * LANES:(h // 2 + 1) * LANES]
            s = _dot_nt(qm_ref[h], k_t)
            s = s + (tab_ref[h, kind] if kind < 2 else b31_ref[h])
            s = jnp.where(selected, s, NEG_BIG)
            _softmax_step(s, v_t, m_ref, l_ref, acc_ref, h)
        return seen

    seen = lax.fori_loop(0, jnp.maximum(qi - 1, 0), lambda kc, sn: attend(kc, 2, sn),
                         jnp.zeros((TILE, 1), jnp.float32))
    seen = lax.cond(qi >= 1, lambda sn: attend(qi - 1, 1, sn), lambda sn: sn, seen)
    attend(qi, 0, seen)
    for pr in range(DSA_HEADS // 2):
        out0 = acc_ref[2 * pr] / l_ref[2 * pr]
        out1 = acc_ref[2 * pr + 1] / l_ref[2 * pr + 1]
        o_ref[0, :, pr * LANES:(pr + 1) * LANES] = jnp.where(lane < HEAD_DIM, out0, out1).astype(o_ref.dtype)


def _dsa_call(q, k, v, iq, ik2, iw, tables, b31):
    b, t, _ = q.shape
    n_tiles = t // TILE
    tile_spec = lambda width: pl.BlockSpec((1, TILE, width), lambda bi, qi, s: (bi, qi, 0))
    seq_spec = lambda width: pl.BlockSpec((1, t, width), lambda bi, qi, s: (bi, 0, 0))
    grid_spec = pltpu.PrefetchScalarGridSpec(
        num_scalar_prefetch=1,
        grid=(b, n_tiles),
        in_specs=[tile_spec(DSA_W), seq_spec(DSA_W), seq_spec(DSA_W), tile_spec(IDX_HEADS * IDX_DIM),
                  seq_spec(LANES), tile_spec(LANES),
                  pl.BlockSpec((DSA_HEADS, 2, TILE, TILE), lambda bi, qi, s: (0, 0, 0, 0))],
        out_specs=tile_spec(DSA_W),
        scratch_shapes=[
            pltpu.VMEM((n_tiles, TILE, TILE), jnp.float32),
            pltpu.VMEM((IDX_HEADS, TILE, LANES), jnp.bfloat16),
            pltpu.VMEM((DSA_HEADS, TILE, LANES), jnp.bfloat16),
            pltpu.VMEM((DSA_HEADS, TILE, LANES), jnp.float32),
            pltpu.VMEM((DSA_HEADS, TILE, LANES), jnp.float32),
            pltpu.VMEM((DSA_HEADS, TILE, LANES), jnp.float32),
        ],
    )
    return pl.pallas_call(
        _dsa_kernel,
        grid_spec=grid_spec,
        out_shape=jax.ShapeDtypeStruct((b, t, DSA_W), jnp.bfloat16),
        compiler_params=pltpu.CompilerParams(dimension_semantics=("arbitrary",) * 2, vmem_limit_bytes=VMEM_LIMIT),
        name="dsa",
    )(b31, q, k, v, iq, ik2, iw, tables)


def _final_kernel(x_ref, g_ref, fg_ref, ya_ref, za_ref, yb_ref, zb_ref, qm_ref, zm_ref, mk_ref, mv_ref,
                  wg_ref, wb_ref, wo_ref, o_ref):
    x = x_ref[...]
    h = _rmsnorm_rows(x, g_ref[...]).astype(jnp.bfloat16)
    lane = lax.broadcasted_iota(jnp.int32, (ROW_TILE, LANES), 1)

    ym_parts = []
    for pr in range(MEM_HEADS // 2):
        qp = qm_ref[:, pr * LANES:(pr + 1) * LANES]
        mk = mk_ref[0, :, pr * LANES:(pr + 1) * LANES]
        mv = mv_ref[0, :, pr * LANES:(pr + 1) * LANES]
        outs = []
        for half in range(2):
            qh = jnp.where(_half_mask((ROW_TILE, LANES), half), qp, jnp.zeros_like(qp))
            s = _dot_nt(qh, mk)
            p = jnp.exp2(s - jnp.max(s, axis=1, keepdims=True))
            outs.append(_dot(p.astype(jnp.bfloat16), mv) / jnp.sum(p, axis=1, keepdims=True))
        ym_parts.append(jnp.where(lane < HEAD_DIM, outs[0], outs[1]))
    ym = jnp.concatenate(ym_parts, axis=1)

    def gated(y, z_ref):
        z = z_ref[...].astype(jnp.float32)
        return (y.astype(jnp.float32) * (z * jax.nn.sigmoid(z))).astype(jnp.bfloat16)

    ua = gated(ya_ref[...], za_ref)
    ub = gated(yb_ref[...], zb_ref)
    um = gated(ym, zm_ref)
    merged = jax.nn.sigmoid(_dot(h, wg_ref[:, :D_MODEL])) * _dot(ua, wb_ref[:MOBA_W])
    merged += jax.nn.sigmoid(_dot(h, wg_ref[:, D_MODEL:2 * D_MODEL])) * _dot(ub, wb_ref[MOBA_W:MOBA_W + DSA_W])
    merged += jax.nn.sigmoid(_dot(h, wg_ref[:, 2 * D_MODEL:])) * _dot(um, wb_ref[MOBA_W + DSA_W:])
    y = x + _dot(merged.astype(jnp.bfloat16), wo_ref[...])
    o_ref[...] = _rmsnorm_rows(y, fg_ref[...])


def _final_call(x2, gain, final_gain, ya, za, yb, zb, qm, zm, mk, mv, w_gates, w_branch, w_out, rows_per_batch):
    rows = x2.shape[0]
    tiles_per_batch = rows_per_batch // ROW_TILE
    row_spec = lambda width: pl.BlockSpec((ROW_TILE, width), lambda i: (i, 0))
    const_spec = lambda shape: pl.BlockSpec(shape, lambda i: (0,) * len(shape))
    mem_spec = pl.BlockSpec((1,) + mk.shape[1:], lambda i: (i // tiles_per_batch, 0, 0))
    return pl.pallas_call(
        _final_kernel,
        grid=(rows // ROW_TILE,),
        in_specs=[row_spec(D_MODEL), const_spec((1, D_MODEL)), const_spec((1, D_MODEL)),
                  row_spec(MOBA_W), row_spec(MOBA_W), row_spec(DSA_W), row_spec(DSA_W),
                  row_spec(MEM_W), row_spec(MEM_W), mem_spec, mem_spec,
                  const_spec(w_gates.shape), const_spec(w_branch.shape), const_spec(w_out.shape)],
        out_specs=row_spec(D_MODEL),
        out_shape=jax.ShapeDtypeStruct((rows, D_MODEL), jnp.float32),
        compiler_params=pltpu.CompilerParams(dimension_semantics=("arbitrary",), vmem_limit_bytes=VMEM_LIMIT),
        name="final",
    )(x2, gain, final_gain, ya, za, yb, zb, qm, zm, mk, mv, w_gates, w_branch, w_out)


def _split_cols(w, sizes):
    outs, off = [], 0
    for s in sizes:
        outs.append(w[:, off:off + s])
        off += s
    return outs


def _layer(x, mem, norm_gain, w_in, rel_bias, mem_norm_gain, w_mem_kv, w_branch, w_out, final_gain):
    b, t, _ = x.shape
    assert t % ROW_TILE == 0 and t // MOBA_BLOCK <= 32 and t >= 4 * DSA_TOPK
    bf = jnp.bfloat16
    (wqa, wka, wva, wza, wqb, wkb, wvb, wzb, wiq, wik, wiw, wqm, wzm, wga, wgb, wgm) = _split_cols(w_in, IN_SIZES)
    w_all = jnp.concatenate([wqa, wka, wva, wza, wqb, wkb, wvb, wzb, wiq, wik, wik, wqm, wzm], axis=1).astype(bf)
    w_iw = jnp.pad(wiw, ((0, 0), (0, LANES - IDX_HEADS))).astype(bf)
    w_gates = jnp.concatenate([wga, wgb, wgm], axis=1).astype(bf)

    x2 = x.reshape(b * t, D_MODEL)
    (qa, ka, va, za, qb, kb, vb, zb, iq, ik2, qm, zm, iw, kmean) = _proj_call(
        x2, norm_gain.reshape(1, D_MODEL), w_all, w_iw)
    seq = lambda a: a.reshape(b, t, a.shape[-1])
    n_blocks = t // MOBA_BLOCK
    kmean = jnp.pad(kmean.reshape(b, n_blocks, MOBA_W), ((0, 0), (0, LANES - n_blocks), (0, 0)))

    tab_a, b31_a = _bias_tables(rel_bias[:, :MOBA_HEADS].T, NEG_BIG)
    tab_b, b31_b = _bias_tables(rel_bias[:, MOBA_HEADS:].T, 0.0)
    ya = _moba_call(seq(qa), seq(ka), seq(va), kmean, tab_a, b31_a)
    yb = _dsa_call(seq(qb), seq(kb), seq(vb), seq(iq), seq(ik2), seq(iw), tab_b, b31_b)
    mk, mv = _memkv_call(mem, mem_norm_gain.reshape(1, D_MODEL), w_mem_kv.astype(bf))
    out = _final_call(x2, norm_gain.reshape(1, D_MODEL), final_gain.reshape(1, D_MODEL),
                      ya.reshape(b * t, MOBA_W), za, yb.reshape(b * t, DSA_W), zb, qm, zm, mk, mv,
                      w_gates, w_branch.astype(bf), w_out.astype(bf), t)
    return out.reshape(b, t, D_MODEL)


@jax.jit
def kernel(x, mem, norm_gain, w_in, rel_bias, mem_norm_gain, w_mem_kv, w_branch, w_out, final_norm_gain):
    assert norm_gain.shape[0] == 1, "one layer"
    return _layer(x, mem, norm_gain[0], w_in[0], rel_bias, mem_norm_gain[0], w_mem_kv[0], w_branch[0],
                  w_out[0], final_norm_gain)
```

```python
import math

import numpy as np
import jax
import jax.numpy as jnp
from jax import lax
from jax.experimental import pallas as pl
from jax.experimental.pallas import tpu as pltpu

D_MODEL = 1024
HEAD_DIM = 64
MOBA_HEADS = 6
DSA_HEADS = 6
MEM_HEADS = 4
MOBA_W = MOBA_HEADS * HEAD_DIM
DSA_W = DSA_HEADS * HEAD_DIM
MEM_W = MEM_HEADS * HEAD_DIM
IDX_HEADS = 8
IDX_DIM = 64
IDX_W = IDX_HEADS * IDX_DIM
MOBA_BLOCK = 256
MOBA_TOPK = 3
DSA_TOPK = 256
REL_BUCKETS = 32
REL_MAX_DIST = 128
N_BIAS_HEADS = MOBA_HEADS + DSA_HEADS
EPS = 1e-6
IN_SIZES = (MOBA_W,) * 4 + (DSA_W,) * 4 + (IDX_W, IDX_DIM, IDX_HEADS) + (MEM_W, MEM_W) + (D_MODEL,) * 3

LANES = 128
SUBLANES = 8
TILE = 256
ROW_TILE = 512
FEAT_LO = 32
LOG2E = math.log2(math.e)
NEG_BIG = -1e30
MASK_BUCKET = REL_BUCKETS
VMEM_LIMIT = 56 * 1024 * 1024
MAX_SEARCH_ITERS = 96

_NT = (((1,), (1,)), ((), ()))


def _dot_nt(a, b):
    return lax.dot_general(a, b, _NT, preferred_element_type=jnp.float32)


def _dot(a, b):
    return jnp.dot(a, b, preferred_element_type=jnp.float32)


def _rel_bucket_np(n):
    exact = REL_BUCKETS // 2
    nf = np.maximum(n, 1).astype(np.float32)
    large = exact + (np.log(nf / np.float32(exact)) / np.float32(math.log(REL_MAX_DIST / exact))
                     * np.float32(REL_BUCKETS - exact)).astype(np.int32)
    return np.where(n < exact, n, np.minimum(large, REL_BUCKETS - 1))


def _bucket_tiles(mask_future):
    j = np.arange(TILE)[:, None]
    i = np.arange(TILE)[None, :]
    d0 = i - j
    b0 = _rel_bucket_np(np.maximum(d0, 0))
    if mask_future:
        b0 = np.where(d0 >= 0, b0, MASK_BUCKET)
    b1 = _rel_bucket_np(TILE + d0)
    assert int(_rel_bucket_np(np.array([TILE + 1]))[0]) == REL_BUCKETS - 1
    return jnp.asarray(np.stack([b0, b1]).astype(np.int32))


def _build_bias_tile(bkt, bias_ref, head):
    acc = jnp.where(bkt == MASK_BUCKET, NEG_BIG, 0.0)
    for b in range(REL_BUCKETS):
        acc = jnp.where(bkt == b, bias_ref[b * N_BIAS_HEADS + head], acc)
    return acc


def _rmsnorm_rows(x, g):
    xf = x.astype(jnp.float32)
    return xf * lax.rsqrt(jnp.mean(xf * xf, axis=-1, keepdims=True) + EPS) * g


_Q_SCALE = HEAD_DIM ** -0.5 * LOG2E
_STD_SEGS = (("ka", MOBA_W, 1.0), ("za", MOBA_W, 1.0), ("kb", DSA_W, 1.0), ("zb", DSA_W, 1.0),
             ("qm", MEM_W, _Q_SCALE), ("zm", MEM_W, 1.0), ("ik", IDX_DIM, 1.0))
_T_SEGS = (("qaT", MOBA_W, _Q_SCALE), ("vaT", MOBA_W, 1.0), ("qbT", DSA_W, _Q_SCALE), ("vbT", DSA_W, 1.0),
           ("iqT", IDX_W, IDX_DIM ** -0.5), ("iwT", IDX_HEADS, IDX_HEADS ** -0.5))
_STD_W = sum(w for _, w, _ in _STD_SEGS)
_T_W = sum(w for _, w, _ in _T_SEGS)
_CHUNKS_PER_ROW_TILE = ROW_TILE // TILE


def _proj_kernel(x_ref, g_ref, w_ref, wt_ref, *out_refs):
    std_refs = out_refs[:len(_STD_SEGS)]
    t_refs = out_refs[len(_STD_SEGS):len(_STD_SEGS) + len(_T_SEGS)]
    kmean_ref = out_refs[-1]
    h = _rmsnorm_rows(x_ref[...], g_ref[...]).astype(jnp.bfloat16)
    off = 0
    for (name, width, scale), o_ref in zip(_STD_SEGS, std_refs):
        r = _dot(h, w_ref[:, off:off + width])
        if name == "ka":
            for blk in range(ROW_TILE // MOBA_BLOCK):
                kmean_ref[0, blk:blk + 1, :] = jnp.mean(
                    r[blk * MOBA_BLOCK:(blk + 1) * MOBA_BLOCK], axis=0, keepdims=True)
        if scale != 1.0:
            r = r * scale
        o_ref[...] = r.astype(o_ref.dtype)
        off += width
    off = 0
    for (name, width, scale), o_ref in zip(_T_SEGS, t_refs):
        r = _dot_nt(wt_ref[off:off + width, :], h)
        if scale != 1.0:
            r = r * scale
        r = r.astype(o_ref.dtype)
        if name in ("vaT", "vbT"):
            for c in range(_CHUNKS_PER_ROW_TILE):
                o_ref[0, c] = r[:, c * TILE:(c + 1) * TILE]
        else:
            o_ref[0] = r
        off += width


def _proj_call(x2, gain, w_std, w_t, b, t):
    rows = x2.shape[0]
    tiles_per_batch = t // ROW_TILE
    n_tiles = rows // ROW_TILE
    row_spec = lambda width: pl.BlockSpec((ROW_TILE, width), lambda i: (i, 0))
    const_spec = lambda shape: pl.BlockSpec(shape, lambda i: (0,) * len(shape))
    t_spec = lambda width: pl.BlockSpec((1, width, ROW_TILE),
                                        lambda i: (i // tiles_per_batch, 0, i % tiles_per_batch))
    chunk_spec = lambda width: pl.BlockSpec((1, _CHUNKS_PER_ROW_TILE, width, TILE),
                                            lambda i: (i // tiles_per_batch, i % tiles_per_batch, 0, 0))
    out_shape = [jax.ShapeDtypeStruct((rows, w), jnp.bfloat16) for _, w, _ in _STD_SEGS]
    out_specs = [row_spec(w) for _, w, _ in _STD_SEGS]
    for name, w, _ in _T_SEGS:
        if name in ("vaT", "vbT"):
            out_shape.append(jax.ShapeDtypeStruct((b, t // TILE, w, TILE), jnp.bfloat16))
            out_specs.append(chunk_spec(w))
        else:
            out_shape.append(jax.ShapeDtypeStruct((b, w, t), jnp.float32 if name == "iwT" else jnp.bfloat16))
            out_specs.append(t_spec(w))
    out_shape.append(jax.ShapeDtypeStruct((n_tiles, ROW_TILE // MOBA_BLOCK, MOBA_W), jnp.float32))
    out_specs.append(pl.BlockSpec((1, ROW_TILE // MOBA_BLOCK, MOBA_W), lambda i: (i, 0, 0)))
    return pl.pallas_call(
        _proj_kernel,
        grid=(n_tiles,),
        in_specs=[row_spec(D_MODEL), const_spec((1, D_MODEL)), const_spec((D_MODEL, _STD_W)),
                  const_spec((_T_W, D_MODEL))],
        out_specs=out_specs,
        out_shape=out_shape,
        compiler_params=pltpu.CompilerParams(dimension_semantics=("arbitrary",), vmem_limit_bytes=VMEM_LIMIT),
        name="proj",
    )(x2, gain, w_std, w_t)


def _memkv_kernel(mem_ref, g_ref, w_ref, mk_ref, mv_ref):
    h = _rmsnorm_rows(mem_ref[0], g_ref[...]).astype(jnp.bfloat16)
    r = _dot(h, w_ref[...])
    mk_ref[0] = r[:, :MEM_W].astype(jnp.bfloat16)
    mv_ref[0] = r[:, MEM_W:].astype(jnp.bfloat16)


def _memkv_call(mem, gain, w_kv):
    b, n_mem, _ = mem.shape
    return pl.pallas_call(
        _memkv_kernel,
        grid=(b,),
        in_specs=[pl.BlockSpec((1, n_mem, D_MODEL), lambda i: (i, 0, 0)),
                  pl.BlockSpec((1, D_MODEL), lambda i: (0, 0)),
                  pl.BlockSpec((D_MODEL, 2 * MEM_W), lambda i: (0, 0))],
        out_specs=[pl.BlockSpec((1, n_mem, MEM_W), lambda i: (i, 0, 0))] * 2,
        out_shape=[jax.ShapeDtypeStruct((b, n_mem, MEM_W), jnp.bfloat16)] * 2,
        compiler_params=pltpu.CompilerParams(dimension_semantics=("arbitrary",), vmem_limit_bytes=VMEM_LIMIT),
        name="memkv",
    )(mem, gain, w_kv)


def _row_half_mask(shape, half):
    r = lax.broadcasted_iota(jnp.int32, shape, 0)
    return (r >= half * HEAD_DIM) & (r < (half + 1) * HEAD_DIM)


def _softmax_step(s_t, v_t, m_ref, l_ref, acc_ref):
    m_old = m_ref[:1]
    m_new = jnp.maximum(m_old, jnp.max(s_t, axis=0, keepdims=True))
    alpha = jnp.exp2(m_old - m_new)
    p = jnp.exp2(s_t - m_new)
    l_new = alpha * l_ref[:1] + jnp.sum(p, axis=0, keepdims=True)
    l_ref[...] = jnp.broadcast_to(l_new, l_ref.shape)
    m_ref[...] = jnp.broadcast_to(m_new, m_ref.shape)
    acc_ref[...] = alpha * acc_ref[...] + _dot(v_t, p.astype(jnp.bfloat16))


def _init_softmax_state(m_ref, l_ref, acc_ref):
    m_ref[...] = jnp.full(m_ref.shape, NEG_BIG, jnp.float32)
    l_ref[...] = jnp.zeros(l_ref.shape, jnp.float32)
    acc_ref[...] = jnp.zeros(acc_ref.shape, jnp.float32)


_KIND_DIAG, _KIND_PREV, _KIND_FAR, _KIND_NONE = 0, 1, 2, 3


def _moba_kernel(bias_ref, qt_ref, k_ref, vt_ref, kmean_ref, bkt_ref, o_ref,
                 qa_ref, tab_ref, s_ref, m_ref, l_ref, acc_ref):
    pair = pl.program_id(1)
    qi = pl.program_id(2)
    q_t = qt_ref[0]
    kmean = kmean_ref[0].astype(jnp.bfloat16)
    blk = lax.broadcasted_iota(jnp.int32, (LANES, TILE), 0)
    blk_f = blk.astype(jnp.float32)
    past = blk < qi
    lane = lax.broadcasted_iota(jnp.int32, (TILE, LANES), 1)
    _init_softmax_state(m_ref, l_ref, acc_ref)

    @pl.when(qi == 0)
    def _():
        for half in range(2):
            for kind in (_KIND_DIAG, _KIND_PREV):
                tab_ref[kind, :, half * TILE:(half + 1) * TILE] = _build_bias_tile(
                    bkt_ref[kind], bias_ref, pair * 2 + half)
        tab_ref[_KIND_FAR] = jnp.zeros((TILE, 2 * TILE), jnp.float32)
        tab_ref[_KIND_NONE] = jnp.full((TILE, 2 * TILE), NEG_BIG, jnp.float32)

    for half in range(2):
        qh = jnp.where(_row_half_mask((LANES, TILE), half), q_t, jnp.zeros_like(q_t))
        gate = _dot(kmean, qh)
        g = jnp.where(past, gate, -jnp.inf)
        sel = jnp.zeros((LANES, TILE), jnp.bool_)
        for _ in range(MOBA_TOPK):
            mx = jnp.max(g, axis=0, keepdims=True)
            first = jnp.min(jnp.where(g == mx, blk_f, float(LANES)), axis=0, keepdims=True)
            pick = blk_f == first
            sel = sel | (pick & past)
            g = jnp.where(pick, -jnp.inf, g)
        b31 = jnp.full((LANES, TILE), bias_ref[(REL_BUCKETS - 1) * N_BIAS_HEADS + pair * 2 + half], jnp.float32)
        b31_hi = b31.astype(jnp.bfloat16).astype(jnp.float32)
        hi_part = jnp.where(sel, jnp.where(blk < qi - 1, b31_hi, 0.0), NEG_BIG)
        hi_part = jnp.where(past, hi_part, 0.0)
        lo_part = jnp.where((blk >= FEAT_LO) & (blk - FEAT_LO < qi - 1), b31 - b31_hi, 0.0)
        feat = jnp.where(blk < FEAT_LO, hi_part, lo_part).astype(jnp.bfloat16)
        qa_ref[:, half * TILE:(half + 1) * TILE] = jnp.concatenate([qh, feat], axis=0)

    def logits(slot, kj):
        kc = jnp.minimum(kj, qi)
        k_t = k_ref[0, pl.ds(pl.multiple_of(kc * TILE, TILE), TILE), :]
        onehot = ((lane == kc) | (lane == kc + FEAT_LO)).astype(jnp.bfloat16)
        s_ref[slot] = _dot(jnp.concatenate([k_t, onehot], axis=1), qa_ref[...])

    def consume(slot, kj):
        kind = jnp.where(kj > qi, _KIND_NONE, jnp.minimum(qi - kj, _KIND_FAR))
        s_t = s_ref[slot] + tab_ref[kind]
        _softmax_step(s_t, vt_ref[0, jnp.minimum(kj, qi)], m_ref, l_ref, acc_ref)

    def pair_body(j, carry):
        logits(1, 2 * j + 1)
        consume(0, 2 * j)
        logits(0, 2 * j + 2)
        consume(1, 2 * j + 1)
        return carry

    logits(0, 0)
    lax.fori_loop(0, (qi + 2) // 2, pair_body, 0)
    out0 = acc_ref[:, :TILE] / l_ref[:1, :TILE]
    out1 = acc_ref[:, TILE:] / l_ref[:1, TILE:]
    o_ref[0] = jnp.where(_row_half_mask((LANES, TILE), 0), out0, out1).T.astype(o_ref.dtype)


def _moba_call(bias_flat, q_t, k, v_t, kmean, bkt):
    b, _, t = q_t.shape
    n_pairs = MOBA_HEADS // 2
    n_chunks = t // TILE
    grid_spec = pltpu.PrefetchScalarGridSpec(
        num_scalar_prefetch=1,
        grid=(b, n_pairs, n_chunks),
        in_specs=[
            pl.BlockSpec((1, LANES, TILE), lambda bi, p, qi, s: (bi, p, qi)),
            pl.BlockSpec((1, t, LANES), lambda bi, p, qi, s: (bi, 0, p)),
            pl.BlockSpec((1, n_chunks, LANES, TILE), lambda bi, p, qi, s: (bi, 0, p, 0)),
            pl.BlockSpec((1, LANES, LANES), lambda bi, p, qi, s: (bi, 0, p)),
            pl.BlockSpec((2, TILE, TILE), lambda bi, p, qi, s: (0, 0, 0)),
        ],
        out_specs=pl.BlockSpec((1, TILE, LANES), lambda bi, p, qi, s: (bi, qi, p)),
        scratch_shapes=[
            pltpu.VMEM((2 * LANES, 2 * TILE), jnp.bfloat16),
            pltpu.VMEM((4, TILE, 2 * TILE), jnp.float32),
            pltpu.VMEM((2, TILE, 2 * TILE), jnp.float32),
            pltpu.VMEM((SUBLANES, 2 * TILE), jnp.float32),
            pltpu.VMEM((SUBLANES, 2 * TILE), jnp.float32),
            pltpu.VMEM((LANES, 2 * TILE), jnp.float32),
        ],
    )
    return pl.pallas_call(
        _moba_kernel,
        grid_spec=grid_spec,
        out_shape=jax.ShapeDtypeStruct((b, t, MOBA_W), jnp.bfloat16),
        compiler_params=pltpu.CompilerParams(dimension_semantics=("arbitrary",) * 3, vmem_limit_bytes=VMEM_LIMIT),
        name="moba",
    )(bias_flat, q_t, k, v_t, kmean, bkt)


def _dsa_kernel(bias_ref, qt_ref, k_ref, vt_ref, iqt_ref, ik_ref, iwt_ref, bkt_ref, o_ref,
                sc_ref, tab_ref, qz_ref, s_ref, m_ref, l_ref, acc_ref):
    qi = pl.program_id(1)
    n_chunks = qi + 1
    key = lax.broadcasted_iota(jnp.int32, (TILE, TILE), 0)
    qry = lax.broadcasted_iota(jnp.int32, (TILE, TILE), 1)
    causal = key <= qry

    @pl.when(qi == 0)
    def _():
        for h in range(DSA_HEADS):
            cols = slice((h % 2) * TILE, (h % 2 + 1) * TILE)
            for kind in (_KIND_DIAG, _KIND_PREV):
                tab_ref[h // 2, kind, :, cols] = _build_bias_tile(bkt_ref[kind], bias_ref, MOBA_HEADS + h)
            tab_ref[h // 2, _KIND_FAR, :, cols] = jnp.full(
                (TILE, TILE), bias_ref[(REL_BUCKETS - 1) * N_BIAS_HEADS + MOBA_HEADS + h], jnp.float32)

    for h in range(DSA_HEADS):
        blk = qt_ref[0, (h // 2) * LANES:(h // 2 + 1) * LANES, :]
        qz_ref[h // 2, :, (h % 2) * TILE:(h % 2 + 1) * TILE] = jnp.where(
            _row_half_mask((LANES, TILE), h % 2), blk, jnp.zeros_like(blk))

    iw_t = iwt_ref[0]

    def score_chunk(kc):
        row0 = pl.multiple_of(kc * TILE, TILE)
        ik_t = ik_ref[0, pl.ds(row0, TILE), :]
        tot = jnp.zeros((TILE, TILE), jnp.float32)
        for h in range(IDX_HEADS):
            x = _dot(ik_t, iqt_ref[0, h * IDX_DIM:(h + 1) * IDX_DIM, :])
            tot = tot + iw_t[h:h + 1, :] * jnp.maximum(x, 0.0)
        return tot

    def score_body(kc, carry):
        sc_ref[kc] = score_chunk(kc)
        return carry

    lax.fori_loop(0, qi, score_body, 0)
    sc_ref[qi] = jnp.where(causal, score_chunk(qi), -jnp.inf)

    def reduce_pass(fn, init, combine):
        def body(kc, part):
            return combine(part, fn(sc_ref[kc]))
        return lax.fori_loop(0, n_chunks, body, jnp.full((1, TILE), init, jnp.float32))

    def count_ge(p):
        return reduce_pass(lambda x: jnp.sum(jnp.where(x >= p, 1.0, 0.0), axis=0, keepdims=True), 0.0, jnp.add)

    row_max = reduce_pass(lambda x: jnp.max(x, axis=0, keepdims=True), -jnp.inf, jnp.maximum)
    row_min = reduce_pass(lambda x: jnp.min(jnp.where(x == -jnp.inf, jnp.inf, x), axis=0, keepdims=True),
                          jnp.inf, jnp.minimum)
    n_adm = (qi * TILE + 1 + lax.broadcasted_iota(jnp.int32, (1, TILE), 1)).astype(jnp.float32)
    k_top = float(DSA_TOPK)
    take_all = n_adm <= k_top

    def search_cond(st):
        it, _, _, _, _, _, done = st
        return jnp.logical_and(it < MAX_SEARCH_ITERS, jnp.min(done) < 0.5)

    def search_body(st):
        it, lo, hi, hi_fin, c_lo, c_hi, done = st
        mid = 0.5 * lo + 0.5 * hi_fin
        frac = (c_lo - k_top + 0.5) / jnp.maximum(c_lo - c_hi, 1.0)
        interp = lo + frac * (hi_fin - lo)
        p = jnp.where(it % 2 == 0, interp, mid)
        p = jnp.where((p > lo) & (p < hi), p, mid)
        stuck = jnp.logical_not((mid > lo) & (mid < hi))
        c = count_ge(p)
        up = c >= k_top
        move = jnp.logical_not((done > 0.5) | stuck)
        lo_n = jnp.where(move & up, p, lo)
        c_lo_n = jnp.where(move & up, c, c_lo)
        down = move & jnp.logical_not(up)
        hi_n = jnp.where(down, p, hi)
        c_hi_n = jnp.where(down, c, c_hi)
        hi_fin_n = jnp.where(down, p, hi_fin)
        done_n = jnp.where((c_lo_n - c_hi_n <= 1.0) | stuck, 1.0, done)
        return it + 1, lo_n, hi_n, hi_fin_n, c_lo_n, c_hi_n, done_n

    init = (jnp.int32(0), row_min, jnp.full((1, TILE), jnp.inf, jnp.float32), row_max,
            n_adm, jnp.zeros((1, TILE), jnp.float32), take_all.astype(jnp.float32))
    _, lo, hi, _, c_lo, c_hi, _ = lax.while_loop(search_cond, search_body, init)
    lo = jnp.where(take_all, -jnp.inf, lo)
    hi = jnp.where(take_all, jnp.inf, hi)
    need = k_top - jnp.where(take_all, 0.0, c_hi)

    _init_softmax_state(m_ref, l_ref, acc_ref)
    lower = (qry < key).astype(jnp.bfloat16)

    n_pairs = DSA_HEADS // 2

    def logits(slot, kc):
        row0 = pl.multiple_of(jnp.minimum(kc, qi) * TILE, TILE)
        for pr in range(n_pairs):
            k_t = k_ref[0, pl.ds(row0, TILE), pr * LANES:(pr + 1) * LANES]
            s_ref[slot, pr] = _dot(k_t, qz_ref[pr])

    def consume(slot, kc, seen):
        kcc = jnp.minimum(kc, qi)
        kind = jnp.minimum(qi - kcc, _KIND_FAR)
        x = sc_ref[kcc]
        above = x >= hi
        tie = (x >= lo) & jnp.logical_not(above)
        tie_f = jnp.where(tie, 1.0, 0.0)
        rank = seen + _dot(lower, tie_f.astype(jnp.bfloat16))
        selected = (above | (tie & (rank < need))) & ((key + kcc * TILE <= qry + qi * TILE) & (kc <= qi))
        selected = jnp.concatenate([selected, selected], axis=1)
        seen = seen + jnp.sum(tie_f, axis=0, keepdims=True)
        for pr in range(n_pairs):
            s_t = jnp.where(selected, s_ref[slot, pr] + tab_ref[pr, kind], NEG_BIG)
            _softmax_step(s_t, vt_ref[0, kcc, pr * LANES:(pr + 1) * LANES, :],
                          m_ref.at[pr], l_ref.at[pr], acc_ref.at[pr])
        return seen

    def pair_body(j, seen):
        logits(1, 2 * j + 1)
        seen = consume(0, 2 * j, seen)
        logits(0, 2 * j + 2)
        return consume(1, 2 * j + 1, seen)

    logits(0, 0)
    lax.fori_loop(0, (qi + 2) // 2, pair_body, jnp.zeros((1, TILE), jnp.float32))
    for pr in range(n_pairs):
        out0 = acc_ref[pr, :, :TILE] / l_ref[pr, :1, :TILE]
        out1 = acc_ref[pr, :, TILE:] / l_ref[pr, :1, TILE:]
        o_ref[0, :, pr * LANES:(pr + 1) * LANES] = jnp.where(
            _row_half_mask((LANES, TILE), 0), out0, out1).T.astype(o_ref.dtype)


def _dsa_call(bias_flat, q_t, k, v_t, iq_t, ik, iw_t, bkt):
    b, _, t = q_t.shape
    n_chunks = t // TILE
    n_pairs = DSA_HEADS // 2
    grid_spec = pltpu.PrefetchScalarGridSpec(
        num_scalar_prefetch=1,
        grid=(b, n_chunks),
        in_specs=[
            pl.BlockSpec((1, DSA_W, TILE), lambda bi, qi, s: (bi, 0, qi)),
            pl.BlockSpec((1, t, DSA_W), lambda bi, qi, s: (bi, 0, 0)),
            pl.BlockSpec((1, n_chunks, DSA_W, TILE), lambda bi, qi, s: (bi, 0, 0, 0)),
            pl.BlockSpec((1, IDX_W, TILE), lambda bi, qi, s: (bi, 0, qi)),
            pl.BlockSpec((1, t, IDX_DIM), lambda bi, qi, s: (bi, 0, 0)),
            pl.BlockSpec((1, IDX_HEADS, TILE), lambda bi, qi, s: (bi, 0, qi)),
            pl.BlockSpec((2, TILE, TILE), lambda bi, qi, s: (0, 0, 0)),
        ],
        out_specs=pl.BlockSpec((1, TILE, DSA_W), lambda bi, qi, s: (bi, qi, 0)),
        scratch_shapes=[
            pltpu.VMEM((n_chunks, TILE, TILE), jnp.float32),
            pltpu.VMEM((n_pairs, 3, TILE, 2 * TILE), jnp.float32),
            pltpu.VMEM((n_pairs, LANES, 2 * TILE), jnp.bfloat16),
            pltpu.VMEM((2, n_pairs, TILE, 2 * TILE), jnp.float32),
            pltpu.VMEM((n_pairs, SUBLANES, 2 * TILE), jnp.float32),
            pltpu.VMEM((n_pairs, SUBLANES, 2 * TILE), jnp.float32),
            pltpu.VMEM((n_pairs, LANES, 2 * TILE), jnp.float32),
        ],
    )
    return pl.pallas_call(
        _dsa_kernel,
        grid_spec=grid_spec,
        out_shape=jax.ShapeDtypeStruct((b, t, DSA_W), jnp.bfloat16),
        compiler_params=pltpu.CompilerParams(dimension_semantics=("arbitrary",) * 2, vmem_limit_bytes=VMEM_LIMIT),
        name="dsa",
    )(bias_flat, q_t, k, v_t, iq_t, ik, iw_t, bkt)


def _lane_half_mask(shape, half):
    lane = lax.broadcasted_iota(jnp.int32, shape, 1)
    return (lane >= half * HEAD_DIM) & (lane < (half + 1) * HEAD_DIM)


def _final_kernel(x_ref, g_ref, fg_ref, ya_ref, za_ref, yb_ref, zb_ref, qm_ref, zm_ref, mk_ref, mv_ref,
                  wg_ref, wb_ref, wo_ref, o_ref):
    x = x_ref[...]
    h = _rmsnorm_rows(x, g_ref[...]).astype(jnp.bfloat16)

    ym_parts = []
    for pr in range(MEM_HEADS // 2):
        qp = qm_ref[:, pr * LANES:(pr + 1) * LANES]
        mk = mk_ref[0, :, pr * LANES:(pr + 1) * LANES]
        mv = mv_ref[0, :, pr * LANES:(pr + 1) * LANES]
        outs = []
        for half in range(2):
            qh = jnp.where(_lane_half_mask((ROW_TILE, LANES), half), qp, jnp.zeros_like(qp))
            s = _dot_nt(qh, mk)
            p = jnp.exp2(s - jnp.max(s, axis=1, keepdims=True))
            outs.append(_dot(p.astype(jnp.bfloat16), mv) / jnp.sum(p, axis=1, keepdims=True))
        ym_parts.append(jnp.where(_lane_half_mask((ROW_TILE, LANES), 0), outs[0], outs[1]))
    ym = jnp.concatenate(ym_parts, axis=1)

    def gated(y, z_ref):
        z = z_ref[...].astype(jnp.float32)
        return (y.astype(jnp.float32) * (z * jax.nn.sigmoid(z))).astype(jnp.bfloat16)

    ua = gated(ya_ref[...], za_ref)
    ub = gated(yb_ref[...], zb_ref)
    um = gated(ym, zm_ref)
    merged = jax.nn.sigmoid(_dot(h, wg_ref[:, :D_MODEL])) * _dot(ua, wb_ref[:MOBA_W])
    merged += jax.nn.sigmoid(_dot(h, wg_ref[:, D_MODEL:2 * D_MODEL])) * _dot(ub, wb_ref[MOBA_W:MOBA_W + DSA_W])
    merged += jax.nn.sigmoid(_dot(h, wg_ref[:, 2 * D_MODEL:])) * _dot(um, wb_ref[MOBA_W + DSA_W:])
    y = x + _dot(merged.astype(jnp.bfloat16), wo_ref[...])
    o_ref[...] = _rmsnorm_rows(y, fg_ref[...])


def _final_call(x2, gain, final_gain, ya, za, yb, zb, qm, zm, mk, mv, w_gates, w_branch, w_out, rows_per_batch):
    rows = x2.shape[0]
    tiles_per_batch = rows_per_batch // ROW_TILE
    row_spec = lambda width: pl.BlockSpec((ROW_TILE, width), lambda i: (i, 0))
    const_spec = lambda shape: pl.BlockSpec(shape, lambda i: (0,) * len(shape))
    mem_spec = pl.BlockSpec((1,) + mk.shape[1:], lambda i: (i // tiles_per_batch, 0, 0))
    return pl.pallas_call(
        _final_kernel,
        grid=(rows // ROW_TILE,),
        in_specs=[row_spec(D_MODEL), const_spec((1, D_MODEL)), const_spec((1, D_MODEL)),
                  row_spec(MOBA_W), row_spec(MOBA_W), row_spec(DSA_W), row_spec(DSA_W),
                  row_spec(MEM_W), row_spec(MEM_W), mem_spec, mem_spec,
                  const_spec(w_gates.shape), const_spec(w_branch.shape), const_spec(w_out.shape)],
        out_specs=row_spec(D_MODEL),
        out_shape=jax.ShapeDtypeStruct((rows, D_MODEL), jnp.float32),
        compiler_params=pltpu.CompilerParams(dimension_semantics=("arbitrary",), vmem_limit_bytes=VMEM_LIMIT),
        name="final",
    )(x2, gain, final_gain, ya, za, yb, zb, qm, zm, mk, mv, w_gates, w_branch, w_out)


def _layer(x, mem, norm_gain, w_in, rel_bias, mem_norm_gain, w_mem_kv, w_branch, w_out, final_gain):
    b, t, _ = x.shape
    assert t % ROW_TILE == 0 and t // MOBA_BLOCK <= FEAT_LO and t >= 4 * DSA_TOPK
    bf = jnp.bfloat16
    names = ("qa", "ka", "va", "za", "qb", "kb", "vb", "zb", "iq", "ik", "iw", "qm", "zm", "ga", "gb", "gm")
    cols, off = {}, 0
    for name, size in zip(names, IN_SIZES):
        cols[name] = w_in[:, off:off + size]
        off += size
    w_std = jnp.concatenate([cols[n] for n, _, _ in _STD_SEGS], axis=1).astype(bf)
    w_t = jnp.concatenate([cols[n[:-1]] for n, _, _ in _T_SEGS], axis=1).T.astype(bf)
    w_gates = jnp.concatenate([cols["ga"], cols["gb"], cols["gm"]], axis=1).astype(bf)

    x2 = x.reshape(b * t, D_MODEL)
    (ka, za, kb, zb, qm, zm, ik, qa_t, va_t, qb_t, vb_t, iq_t, iw_t, kmean) = _proj_call(
        x2, norm_gain.reshape(1, D_MODEL), w_std, w_t, b, t)
    seq = lambda a: a.reshape(b, t, a.shape[-1])
    n_blocks = t // MOBA_BLOCK
    kmean = jnp.pad(kmean.reshape(b, n_blocks, MOBA_W), ((0, 0), (0, LANES - n_blocks), (0, 0)))

    bias_flat = (rel_bias * LOG2E).reshape(-1)
    ya = _moba_call(bias_flat, qa_t, seq(ka), va_t, kmean, _bucket_tiles(True))
    yb = _dsa_call(bias_flat, qb_t, seq(kb), vb_t, iq_t, seq(ik), iw_t, _bucket_tiles(False))
    mk, mv = _memkv_call(mem, mem_norm_gain.reshape(1, D_MODEL), w_mem_kv.astype(bf))
    out = _final_call(x2, norm_gain.reshape(1, D_MODEL), final_gain.reshape(1, D_MODEL),
                      ya.reshape(b * t, MOBA_W), za, yb.reshape(b * t, DSA_W), zb, qm, zm, mk, mv,
                      w_gates, w_branch.astype(bf), w_out.astype(bf), t)
    return out.reshape(b, t, D_MODEL)


@jax.jit
def kernel(x, mem, norm_gain, w_in, rel_bias, mem_norm_gain, w_mem_kv, w_branch, w_out, final_norm_gain):
    assert norm_gain.shape[0] == 1, "one layer"
    return _layer(x, mem, norm_gain[0], w_in[0], rel_bias, mem_norm_gain[0], w_mem_kv[0], w_branch[0],
                  w_out[0], final_norm_gain)
```

```python
import math

import numpy as np
import jax
import jax.numpy as jnp
from jax import lax
from jax.experimental import pallas as pl
from jax.experimental.pallas import tpu as pltpu

D_MODEL = 1024
HEAD_DIM = 64
MOBA_HEADS = 6
DSA_HEADS = 6
MEM_HEADS = 4
MOBA_W = MOBA_HEADS * HEAD_DIM
DSA_W = DSA_HEADS * HEAD_DIM
MEM_W = MEM_HEADS * HEAD_DIM
IDX_HEADS = 8
IDX_DIM = 64
IDX_W = IDX_HEADS * IDX_DIM
MOBA_BLOCK = 256
MOBA_TOPK = 3
DSA_TOPK = 256
REL_BUCKETS = 32
REL_MAX_DIST = 128
N_BIAS_HEADS = MOBA_HEADS + DSA_HEADS
EPS = 1e-6
IN_SIZES = (MOBA_W,) * 4 + (DSA_W,) * 4 + (IDX_W, IDX_DIM, IDX_HEADS) + (MEM_W, MEM_W) + (D_MODEL,) * 3

LANES = 128
SUBLANES = 8
TILE = 256
ROW_TILE = 512
FEAT_LO = 32
LOG2E = math.log2(math.e)
NEG_BIG = -1e30
MASK_BUCKET = REL_BUCKETS
VMEM_LIMIT = 56 * 1024 * 1024
MAX_SEARCH_ITERS = 96

_NT = (((1,), (1,)), ((), ()))


def _dot_nt(a, b):
    return lax.dot_general(a, b, _NT, preferred_element_type=jnp.float32)


def _dot(a, b):
    return jnp.dot(a, b, preferred_element_type=jnp.float32)


def _rel_bucket_np(n):
    exact = REL_BUCKETS // 2
    nf = np.maximum(n, 1).astype(np.float32)
    large = exact + (np.log(nf / np.float32(exact)) / np.float32(math.log(REL_MAX_DIST / exact))
                     * np.float32(REL_BUCKETS - exact)).astype(np.int32)
    return np.where(n < exact, n, np.minimum(large, REL_BUCKETS - 1))


def _bucket_tiles(mask_future):
    j = np.arange(TILE)[:, None]
    i = np.arange(TILE)[None, :]
    d0 = i - j
    b0 = _rel_bucket_np(np.maximum(d0, 0))
    if mask_future:
        b0 = np.where(d0 >= 0, b0, MASK_BUCKET)
    b1 = _rel_bucket_np(TILE + d0)
    assert int(_rel_bucket_np(np.array([TILE + 1]))[0]) == REL_BUCKETS - 1
    return jnp.asarray(np.stack([b0, b1]).astype(np.int32))


def _build_bias_tile(bkt, bias_ref, head):
    acc = jnp.where(bkt == MASK_BUCKET, NEG_BIG, 0.0)
    for b in range(REL_BUCKETS):
        acc = jnp.where(bkt == b, bias_ref[b * N_BIAS_HEADS + head], acc)
    return acc


def _rmsnorm_rows(x, g):
    xf = x.astype(jnp.float32)
    return xf * lax.rsqrt(jnp.mean(xf * xf, axis=-1, keepdims=True) + EPS) * g


_Q_SCALE = HEAD_DIM ** -0.5 * LOG2E
_STD_SEGS = (("ka", MOBA_W, 1.0), ("za", MOBA_W, 1.0), ("kb", DSA_W, 1.0), ("zb", DSA_W, 1.0),
             ("qm", MEM_W, _Q_SCALE), ("zm", MEM_W, 1.0), ("ik", IDX_DIM, 1.0))
_T_SEGS = (("qaT", MOBA_W, _Q_SCALE), ("vaT", MOBA_W, 1.0), ("qbT", DSA_W, _Q_SCALE), ("vbT", DSA_W, 1.0),
           ("iqT", IDX_W, IDX_DIM ** -0.5), ("iwT", IDX_HEADS, IDX_HEADS ** -0.5))
_STD_W = sum(w for _, w, _ in _STD_SEGS)
_T_W = sum(w for _, w, _ in _T_SEGS)
_CHUNKS_PER_ROW_TILE = ROW_TILE // TILE


def _proj_kernel(x_ref, g_ref, w_ref, wt_ref, *out_refs):
    std_refs = out_refs[:len(_STD_SEGS)]
    t_refs = out_refs[len(_STD_SEGS):len(_STD_SEGS) + len(_T_SEGS)]
    kmean_ref = out_refs[-1]
    h = _rmsnorm_rows(x_ref[...], g_ref[...]).astype(jnp.bfloat16)
    off = 0
    for (name, width, scale), o_ref in zip(_STD_SEGS, std_refs):
        r = _dot(h, w_ref[:, off:off + width])
        if name == "ka":
            for blk in range(ROW_TILE // MOBA_BLOCK):
                kmean_ref[0, blk:blk + 1, :] = jnp.mean(
                    r[blk * MOBA_BLOCK:(blk + 1) * MOBA_BLOCK], axis=0, keepdims=True)
        if scale != 1.0:
            r = r * scale
        o_ref[...] = r.astype(o_ref.dtype)
        off += width
    off = 0
    for (name, width, scale), o_ref in zip(_T_SEGS, t_refs):
        r = _dot_nt(wt_ref[off:off + width, :], h)
        if scale != 1.0:
            r = r * scale
        r = r.astype(o_ref.dtype)
        if name in ("vaT", "vbT"):
            for c in range(_CHUNKS_PER_ROW_TILE):
                o_ref[0, c] = r[:, c * TILE:(c + 1) * TILE]
        else:
            o_ref[0] = r
        off += width


def _proj_call(x2, gain, w_std, w_t, b, t):
    rows = x2.shape[0]
    tiles_per_batch = t // ROW_TILE
    n_tiles = rows // ROW_TILE
    row_spec = lambda width: pl.BlockSpec((ROW_TILE, width), lambda i: (i, 0))
    const_spec = lambda shape: pl.BlockSpec(shape, lambda i: (0,) * len(shape))
    t_spec = lambda width: pl.BlockSpec((1, width, ROW_TILE),
                                        lambda i: (i // tiles_per_batch, 0, i % tiles_per_batch))
    chunk_spec = lambda width: pl.BlockSpec((1, _CHUNKS_PER_ROW_TILE, width, TILE),
                                            lambda i: (i // tiles_per_batch, i % tiles_per_batch, 0, 0))
    out_shape = [jax.ShapeDtypeStruct((rows, w), jnp.bfloat16) for _, w, _ in _STD_SEGS]
    out_specs = [row_spec(w) for _, w, _ in _STD_SEGS]
    for name, w, _ in _T_SEGS:
        if name in ("vaT", "vbT"):
            out_shape.append(jax.ShapeDtypeStruct((b, t // TILE, w, TILE), jnp.bfloat16))
            out_specs.append(chunk_spec(w))
        else:
            out_shape.append(jax.ShapeDtypeStruct((b, w, t), jnp.float32 if name == "iwT" else jnp.bfloat16))
            out_specs.append(t_spec(w))
    out_shape.append(jax.ShapeDtypeStruct((n_tiles, ROW_TILE // MOBA_BLOCK, MOBA_W), jnp.float32))
    out_specs.append(pl.BlockSpec((1, ROW_TILE // MOBA_BLOCK, MOBA_W), lambda i: (i, 0, 0)))
    return pl.pallas_call(
        _proj_kernel,
        grid=(n_tiles,),
        in_specs=[row_spec(D_MODEL), const_spec((1, D_MODEL)), const_spec((D_MODEL, _STD_W)),
                  const_spec((_T_W, D_MODEL))],
        out_specs=out_specs,
        out_shape=out_shape,
        compiler_params=pltpu.CompilerParams(dimension_semantics=("arbitrary",), vmem_limit_bytes=VMEM_LIMIT),
        name="proj",
    )(x2, gain, w_std, w_t)


def _memkv_kernel(mem_ref, g_ref, w_ref, mk_ref, mv_ref):
    h = _rmsnorm_rows(mem_ref[0], g_ref[...]).astype(jnp.bfloat16)
    r = _dot(h, w_ref[...])
    mk_ref[0] = r[:, :MEM_W].astype(jnp.bfloat16)
    mv_ref[0] = r[:, MEM_W:].astype(jnp.bfloat16)


def _memkv_call(mem, gain, w_kv):
    b, n_mem, _ = mem.shape
    return pl.pallas_call(
        _memkv_kernel,
        grid=(b,),
        in_specs=[pl.BlockSpec((1, n_mem, D_MODEL), lambda i: (i, 0, 0)),
                  pl.BlockSpec((1, D_MODEL), lambda i: (0, 0)),
                  pl.BlockSpec((D_MODEL, 2 * MEM_W), lambda i: (0, 0))],
        out_specs=[pl.BlockSpec((1, n_mem, MEM_W), lambda i: (i, 0, 0))] * 2,
        out_shape=[jax.ShapeDtypeStruct((b, n_mem, MEM_W), jnp.bfloat16)] * 2,
        compiler_params=pltpu.CompilerParams(dimension_semantics=("arbitrary",), vmem_limit_bytes=VMEM_LIMIT),
        name="memkv",
    )(mem, gain, w_kv)


def _row_half_mask(shape, half):
    r = lax.broadcasted_iota(jnp.int32, shape, 0)
    return (r >= half * HEAD_DIM) & (r < (half + 1) * HEAD_DIM)


def _softmax_step(s_t, v_t, m_ref, l_ref, acc_ref):
    m_old = m_ref[:1]
    m_new = jnp.maximum(m_old, jnp.max(s_t, axis=0, keepdims=True))
    alpha = jnp.exp2(m_old - m_new)
    p = jnp.exp2(s_t - m_new)
    l_new = alpha * l_ref[:1] + jnp.sum(p, axis=0, keepdims=True)
    l_ref[...] = jnp.broadcast_to(l_new, l_ref.shape)
    m_ref[...] = jnp.broadcast_to(m_new, m_ref.shape)
    acc_ref[...] = alpha * acc_ref[...] + _dot(v_t, p.astype(jnp.bfloat16))


def _init_softmax_state(m_ref, l_ref, acc_ref):
    m_ref[...] = jnp.full(m_ref.shape, NEG_BIG, jnp.float32)
    l_ref[...] = jnp.zeros(l_ref.shape, jnp.float32)
    acc_ref[...] = jnp.zeros(acc_ref.shape, jnp.float32)


_KIND_DIAG, _KIND_PREV, _KIND_FAR, _KIND_NONE = 0, 1, 2, 3


def _moba_kernel(bias_ref, qt_ref, k_ref, vt_ref, kmean_ref, bkt_ref, o_ref,
                 qa_ref, tab_ref, s_ref, m_ref, l_ref, acc_ref):
    pair = pl.program_id(1)
    qi = pl.program_id(2)
    q_t = qt_ref[0]
    kmean = kmean_ref[0].astype(jnp.bfloat16)
    blk = lax.broadcasted_iota(jnp.int32, (LANES, TILE), 0)
    blk_f = blk.astype(jnp.float32)
    past = blk < qi
    lane = lax.broadcasted_iota(jnp.int32, (TILE, LANES), 1)
    _init_softmax_state(m_ref, l_ref, acc_ref)

    @pl.when(qi == 0)
    def _():
        for half in range(2):
            for kind in (_KIND_DIAG, _KIND_PREV):
                tab_ref[kind, :, half * TILE:(half + 1) * TILE] = _build_bias_tile(
                    bkt_ref[kind], bias_ref, pair * 2 + half)
        tab_ref[_KIND_FAR] = jnp.zeros((TILE, 2 * TILE), jnp.float32)
        tab_ref[_KIND_NONE] = jnp.full((TILE, 2 * TILE), NEG_BIG, jnp.float32)

    for half in range(2):
        qh = jnp.where(_row_half_mask((LANES, TILE), half), q_t, jnp.zeros_like(q_t))
        gate = _dot(kmean, qh)
        g = jnp.where(past, gate, -jnp.inf)
        sel = jnp.zeros((LANES, TILE), jnp.bool_)
        for _ in range(MOBA_TOPK):
            mx = jnp.max(g, axis=0, keepdims=True)
            first = jnp.min(jnp.where(g == mx, blk_f, float(LANES)), axis=0, keepdims=True)
            pick = blk_f == first
            sel = sel | (pick & past)
            g = jnp.where(pick, -jnp.inf, g)
        b31 = jnp.full((LANES, TILE), bias_ref[(REL_BUCKETS - 1) * N_BIAS_HEADS + pair * 2 + half], jnp.float32)
        b31_hi = b31.astype(jnp.bfloat16).astype(jnp.float32)
        hi_part = jnp.where(sel, jnp.where(blk < qi - 1, b31_hi, 0.0), NEG_BIG)
        hi_part = jnp.where(past, hi_part, 0.0)
        lo_part = jnp.where((blk >= FEAT_LO) & (blk - FEAT_LO < qi - 1), b31 - b31_hi, 0.0)
        feat = jnp.where(blk < FEAT_LO, hi_part, lo_part).astype(jnp.bfloat16)
        qa_ref[:, half * TILE:(half + 1) * TILE] = jnp.concatenate([qh, feat], axis=0)

    def logits(slot, kj):
        kc = jnp.minimum(kj, qi)
        k_t = k_ref[0, pl.ds(pl.multiple_of(kc * TILE, TILE), TILE), :]
        onehot = ((lane == kc) | (lane == kc + FEAT_LO)).astype(jnp.bfloat16)
        s_ref[slot] = _dot(jnp.concatenate([k_t, onehot], axis=1), qa_ref[...])

    def consume(slot, kj):
        kind = jnp.where(kj > qi, _KIND_NONE, jnp.minimum(qi - kj, _KIND_FAR))
        s_t = s_ref[slot] + tab_ref[kind]
        _softmax_step(s_t, vt_ref[0, jnp.minimum(kj, qi)], m_ref, l_ref, acc_ref)

    def pair_body(j, carry):
        logits(1, 2 * j + 1)
        consume(0, 2 * j)
        logits(0, 2 * j + 2)
        consume(1, 2 * j + 1)
        return carry

    logits(0, 0)
    lax.fori_loop(0, (qi + 2) // 2, pair_body, 0)
    out0 = acc_ref[:, :TILE] / l_ref[:1, :TILE]
    out1 = acc_ref[:, TILE:] / l_ref[:1, TILE:]
    o_ref[0] = jnp.where(_row_half_mask((LANES, TILE), 0), out0, out1).T.astype(o_ref.dtype)


def _moba_call(bias_flat, q_t, k, v_t, kmean, bkt):
    b, _, t = q_t.shape
    n_pairs = MOBA_HEADS // 2
    n_chunks = t // TILE
    grid_spec = pltpu.PrefetchScalarGridSpec(
        num_scalar_prefetch=1,
        grid=(b, n_pairs, n_chunks),
        in_specs=[
            pl.BlockSpec((1, LANES, TILE), lambda bi, p, qi, s: (bi, p, qi)),
            pl.BlockSpec((1, t, LANES), lambda bi, p, qi, s: (bi, 0, p)),
            pl.BlockSpec((1, n_chunks, LANES, TILE), lambda bi, p, qi, s: (bi, 0, p, 0)),
            pl.BlockSpec((1, LANES, LANES), lambda bi, p, qi, s: (bi, 0, p)),
            pl.BlockSpec((2, TILE, TILE), lambda bi, p, qi, s: (0, 0, 0)),
        ],
        out_specs=pl.BlockSpec((1, TILE, LANES), lambda bi, p, qi, s: (bi, qi, p)),
        scratch_shapes=[
            pltpu.VMEM((2 * LANES, 2 * TILE), jnp.bfloat16),
            pltpu.VMEM((4, TILE, 2 * TILE), jnp.float32),
            pltpu.VMEM((2, TILE, 2 * TILE), jnp.float32),
            pltpu.VMEM((SUBLANES, 2 * TILE), jnp.float32),
            pltpu.VMEM((SUBLANES, 2 * TILE), jnp.float32),
            pltpu.VMEM((LANES, 2 * TILE), jnp.float32),
        ],
    )
    return pl.pallas_call(
        _moba_kernel,
        grid_spec=grid_spec,
        out_shape=jax.ShapeDtypeStruct((b, t, MOBA_W), jnp.bfloat16),
        compiler_params=pltpu.CompilerParams(dimension_semantics=("arbitrary",) * 3, vmem_limit_bytes=VMEM_LIMIT),
        name="moba",
    )(bias_flat, q_t, k, v_t, kmean, bkt)


def _dsa_kernel(bias_ref, qt_ref, k_ref, vt_ref, iqt_ref, ik_ref, iwt_ref, bkt_ref, o_ref,
                sc_ref, tab_ref, qz_ref, s_ref, m_ref, l_ref, acc_ref):
    qi = pl.program_id(1)
    n_chunks = qi + 1
    key = lax.broadcasted_iota(jnp.int32, (TILE, TILE), 0)
    qry = lax.broadcasted_iota(jnp.int32, (TILE, TILE), 1)
    causal = key <= qry

    @pl.when(qi == 0)
    def _():
        for h in range(DSA_HEADS):
            cols = slice((h % 2) * TILE, (h % 2 + 1) * TILE)
            for kind in (_KIND_DIAG, _KIND_PREV):
                tab_ref[h // 2, kind, :, cols] = _build_bias_tile(bkt_ref[kind], bias_ref, MOBA_HEADS + h)
            tab_ref[h // 2, _KIND_FAR, :, cols] = jnp.full(
                (TILE, TILE), bias_ref[(REL_BUCKETS - 1) * N_BIAS_HEADS + MOBA_HEADS + h], jnp.float32)

    for h in range(DSA_HEADS):
        blk = qt_ref[0, (h // 2) * LANES:(h // 2 + 1) * LANES, :]
        qz_ref[h // 2, :, (h % 2) * TILE:(h % 2 + 1) * TILE] = jnp.where(
            _row_half_mask((LANES, TILE), h % 2), blk, jnp.zeros_like(blk))

    iw_t = iwt_ref[0]

    def score_chunk(kc):
        row0 = pl.multiple_of(kc * TILE, TILE)
        ik_t = ik_ref[0, pl.ds(row0, TILE), :]
        tot = jnp.zeros((TILE, TILE), jnp.float32)
        for h in range(IDX_HEADS):
            x = _dot(ik_t, iqt_ref[0, h * IDX_DIM:(h + 1) * IDX_DIM, :])
            tot = tot + iw_t[h:h + 1, :] * jnp.maximum(x, 0.0)
        return tot

    def score_body(kc, carry):
        sc_ref[kc] = score_chunk(kc)
        return carry

    lax.fori_loop(0, qi, score_body, 0)
    sc_ref[qi] = jnp.where(causal, score_chunk(qi), -jnp.inf)

    @pl.when(qi % 2 == 0)
    def _():
        sc_ref[qi + 1] = jnp.full((TILE, TILE), -jnp.inf, jnp.float32)

    def reduce_pass(fn, init, combine):
        def body(j, part):
            return combine(part, combine(fn(sc_ref[2 * j]), fn(sc_ref[2 * j + 1])))
        return lax.fori_loop(0, (n_chunks + 1) // 2, body, jnp.full((1, TILE), init, jnp.float32))

    def count(pred):
        return reduce_pass(lambda x: jnp.sum(jnp.where(pred(x), 1.0, 0.0), axis=0, keepdims=True), 0.0, jnp.add)

    def min_ge(p):
        return reduce_pass(lambda x: jnp.min(jnp.where(x >= p, x, jnp.inf), axis=0, keepdims=True),
                           jnp.inf, jnp.minimum)

    row_max = reduce_pass(lambda x: jnp.max(x, axis=0, keepdims=True), -jnp.inf, jnp.maximum)
    row_min = reduce_pass(lambda x: jnp.min(jnp.where(x == -jnp.inf, jnp.inf, x), axis=0, keepdims=True),
                          jnp.inf, jnp.minimum)
    n_adm = (qi * TILE + 1 + lax.broadcasted_iota(jnp.int32, (1, TILE), 1)).astype(jnp.float32)
    k_top = float(DSA_TOPK)
    take_all = n_adm <= k_top

    def search_cond(st):
        return jnp.logical_and(st[0] < MAX_SEARCH_ITERS, jnp.min(st[6]) < 0.5)

    def tie_check(st):
        it, lo, hi, hi_fin, c_lo, c_hi, done, last, streak = st
        a = min_ge(lo)
        c_gt = count(lambda x: x > a)
        tie = (c_gt < k_top) & (done < 0.5)
        return it, lo, hi, hi_fin, c_lo, jnp.where(tie, c_gt, c_hi), jnp.where(tie, 1.0, done), last, streak

    def search_body(st):
        it, lo, hi, hi_fin, c_lo, c_hi, done, last, streak = st
        target = k_top - 0.5 + jnp.where(streak >= 1.0, last * 0.25 * (c_lo - c_hi), 0.0)
        target = jnp.clip(target, c_hi + 0.5, c_lo - 0.5)
        frac = (c_lo - target) / jnp.maximum(c_lo - c_hi, 1.0)
        mid = 0.5 * lo + 0.5 * hi_fin
        p = jnp.where(it % 5 == 4, mid, lo + frac * (hi_fin - lo))
        p = jnp.where((p > lo) & (p < hi), p, mid)
        stuck = jnp.logical_not((mid > lo) & (mid < hi))
        c = count(lambda x: x >= p)
        up = c >= k_top
        move = jnp.logical_not((done > 0.5) | stuck)
        rise = move & up
        fall = move & jnp.logical_not(up)
        lo = jnp.where(rise, p, lo)
        c_lo = jnp.where(rise, c, c_lo)
        hi = jnp.where(fall, p, hi)
        c_hi = jnp.where(fall, c, c_hi)
        hi_fin = jnp.where(fall, p, hi_fin)
        side = jnp.where(up, -1.0, 1.0)
        streak = jnp.where(side == last, streak + 1.0, 0.0)
        done = jnp.where((c_lo - c_hi <= 1.0) | (c_lo == k_top) | stuck, 1.0, done)
        st = (it + 1, lo, hi, hi_fin, c_lo, c_hi, done, side, streak)
        check_now = (it + 1 >= 8) & ((it + 1) % 4 == 0) & (jnp.min(done) < 0.5)
        return lax.cond(check_now, tie_check, lambda s: s, st)

    zeros_row = jnp.zeros((1, TILE), jnp.float32)
    init = (jnp.int32(0), row_min, jnp.full((1, TILE), jnp.inf, jnp.float32), row_max,
            n_adm, zeros_row, take_all.astype(jnp.float32), zeros_row, zeros_row)
    _, lo, _, _, _, c_hi, _, _, _ = lax.while_loop(search_cond, search_body, init)
    thr = jnp.where(take_all, -jnp.inf, min_ge(lo))
    need = k_top - jnp.where(take_all, 0.0, c_hi)

    _init_softmax_state(m_ref, l_ref, acc_ref)
    lower = (qry < key).astype(jnp.bfloat16)

    n_pairs = DSA_HEADS // 2

    def logits(slot, kc):
        row0 = pl.multiple_of(jnp.minimum(kc, qi) * TILE, TILE)
        for pr in range(n_pairs):
            k_t = k_ref[0, pl.ds(row0, TILE), pr * LANES:(pr + 1) * LANES]
            s_ref[slot, pr] = _dot(k_t, qz_ref[pr])

    def consume(slot, kc, seen):
        kcc = jnp.minimum(kc, qi)
        kind = jnp.minimum(qi - kcc, _KIND_FAR)
        x = sc_ref[kcc]
        above = x > thr
        tie = x == thr
        tie_f = jnp.where(tie, 1.0, 0.0)
        rank = seen + _dot(lower, tie_f.astype(jnp.bfloat16))
        selected = (above | (tie & (rank < need))) & ((key + kcc * TILE <= qry + qi * TILE) & (kc <= qi))
        selected = jnp.concatenate([selected, selected], axis=1)
        seen = seen + jnp.sum(tie_f, axis=0, keepdims=True)
        for pr in range(n_pairs):
            s_t = jnp.where(selected, s_ref[slot, pr] + tab_ref[pr, kind], NEG_BIG)
            _softmax_step(s_t, vt_ref[0, kcc, pr * LANES:(pr + 1) * LANES, :],
                          m_ref.at[pr], l_ref.at[pr], acc_ref.at[pr])
        return seen

    def pair_body(j, seen):
        logits(1, 2 * j + 1)
        seen = consume(0, 2 * j, seen)
        logits(0, 2 * j + 2)
        return consume(1, 2 * j + 1, seen)

    logits(0, 0)
    lax.fori_loop(0, (qi + 2) // 2, pair_body, jnp.zeros((1, TILE), jnp.float32))
    for pr in range(n_pairs):
        out0 = acc_ref[pr, :, :TILE] / l_ref[pr, :1, :TILE]
        out1 = acc_ref[pr, :, TILE:] / l_ref[pr, :1, TILE:]
        o_ref[0, :, pr * LANES:(pr + 1) * LANES] = jnp.where(
            _row_half_mask((LANES, TILE), 0), out0, out1).T.astype(o_ref.dtype)


def _dsa_call(bias_flat, q_t, k, v_t, iq_t, ik, iw_t, bkt):
    b, _, t = q_t.shape
    n_chunks = t // TILE
    n_pairs = DSA_HEADS // 2
    grid_spec = pltpu.PrefetchScalarGridSpec(
        num_scalar_prefetch=1,
        grid=(b, n_chunks),
        in_specs=[
            pl.BlockSpec((1, DSA_W, TILE), lambda bi, qi, s: (bi, 0, qi)),
            pl.BlockSpec((1, t, DSA_W), lambda bi, qi, s: (bi, 0, 0)),
            pl.BlockSpec((1, n_chunks, DSA_W, TILE), lambda bi, qi, s: (bi, 0, 0, 0)),
            pl.BlockSpec((1, IDX_W, TILE), lambda bi, qi, s: (bi, 0, qi)),
            pl.BlockSpec((1, t, IDX_DIM), lambda bi, qi, s: (bi, 0, 0)),
            pl.BlockSpec((1, IDX_HEADS, TILE), lambda bi, qi, s: (bi, 0, qi)),
            pl.BlockSpec((2, TILE, TILE), lambda bi, qi, s: (0, 0, 0)),
        ],
        out_specs=pl.BlockSpec((1, TILE, DSA_W), lambda bi, qi, s: (bi, qi, 0)),
        scratch_shapes=[
            pltpu.VMEM((n_chunks, TILE, TILE), jnp.float32),
            pltpu.VMEM((n_pairs, 3, TILE, 2 * TILE), jnp.float32),
            pltpu.VMEM((n_pairs, LANES, 2 * TILE), jnp.bfloat16),
            pltpu.VMEM((2, n_pairs, TILE, 2 * TILE), jnp.float32),
            pltpu.VMEM((n_pairs, SUBLANES, 2 * TILE), jnp.float32),
            pltpu.VMEM((n_pairs, SUBLANES, 2 * TILE), jnp.float32),
            pltpu.VMEM((n_pairs, LANES, 2 * TILE), jnp.float32),
        ],
    )
    return pl.pallas_call(
        _dsa_kernel,
        grid_spec=grid_spec,
        out_shape=jax.ShapeDtypeStruct((b, t, DSA_W), jnp.bfloat16),
        compiler_params=pltpu.CompilerParams(dimension_semantics=("arbitrary",) * 2, vmem_limit_bytes=VMEM_LIMIT),
        name="dsa",
    )(bias_flat, q_t, k, v_t, iq_t, ik, iw_t, bkt)


def _lane_half_mask(shape, half):
    lane = lax.broadcasted_iota(jnp.int32, shape, 1)
    return (lane >= half * HEAD_DIM) & (lane < (half + 1) * HEAD_DIM)


def _final_kernel(x_ref, g_ref, fg_ref, ya_ref, za_ref, yb_ref, zb_ref, qm_ref, zm_ref, mk_ref, mv_ref,
                  wg_ref, wb_ref, wo_ref, o_ref):
    x = x_ref[...]
    h = _rmsnorm_rows(x, g_ref[...]).astype(jnp.bfloat16)

    ym_parts = []
    for pr in range(MEM_HEADS // 2):
        qp = qm_ref[:, pr * LANES:(pr + 1) * LANES]
        mk = mk_ref[0, :, pr * LANES:(pr + 1) * LANES]
        mv = mv_ref[0, :, pr * LANES:(pr + 1) * LANES]
        outs = []
        for half in range(2):
            qh = jnp.where(_lane_half_mask((ROW_TILE, LANES), half), qp, jnp.zeros_like(qp))
            s = _dot_nt(qh, mk)
            p = jnp.exp2(s - jnp.max(s, axis=1, keepdims=True))
            outs.append(_dot(p.astype(jnp.bfloat16), mv) / jnp.sum(p, axis=1, keepdims=True))
        ym_parts.append(jnp.where(_lane_half_mask((ROW_TILE, LANES), 0), outs[0], outs[1]))
    ym = jnp.concatenate(ym_parts, axis=1)

    def gated(y, z_ref):
        z = z_ref[...].astype(jnp.float32)
        return (y.astype(jnp.float32) * (z * jax.nn.sigmoid(z))).astype(jnp.bfloat16)

    ua = gated(ya_ref[...], za_ref)
    ub = gated(yb_ref[...], zb_ref)
    um = gated(ym, zm_ref)
    merged = jax.nn.sigmoid(_dot(h, wg_ref[:, :D_MODEL])) * _dot(ua, wb_ref[:MOBA_W])
    merged += jax.nn.sigmoid(_dot(h, wg_ref[:, D_MODEL:2 * D_MODEL])) * _dot(ub, wb_ref[MOBA_W:MOBA_W + DSA_W])
    merged += jax.nn.sigmoid(_dot(h, wg_ref[:, 2 * D_MODEL:])) * _dot(um, wb_ref[MOBA_W + DSA_W:])
    y = x + _dot(merged.astype(jnp.bfloat16), wo_ref[...])
    o_ref[...] = _rmsnorm_rows(y, fg_ref[...])


def _final_call(x2, gain, final_gain, ya, za, yb, zb, qm, zm, mk, mv, w_gates, w_branch, w_out, rows_per_batch):
    rows = x2.shape[0]
    tiles_per_batch = rows_per_batch // ROW_TILE
    row_spec = lambda width: pl.BlockSpec((ROW_TILE, width), lambda i: (i, 0))
    const_spec = lambda shape: pl.BlockSpec(shape, lambda i: (0,) * len(shape))
    mem_spec = pl.BlockSpec((1,) + mk.shape[1:], lambda i: (i // tiles_per_batch, 0, 0))
    return pl.pallas_call(
        _final_kernel,
        grid=(rows // ROW_TILE,),
        in_specs=[row_spec(D_MODEL), const_spec((1, D_MODEL)), const_spec((1, D_MODEL)),
                  row_spec(MOBA_W), row_spec(MOBA_W), row_spec(DSA_W), row_spec(DSA_W),
                  row_spec(MEM_W), row_spec(MEM_W), mem_spec, mem_spec,
                  const_spec(w_gates.shape), const_spec(w_branch.shape), const_spec(w_out.shape)],
        out_specs=row_spec(D_MODEL),
        out_shape=jax.ShapeDtypeStruct((rows, D_MODEL), jnp.float32),
        compiler_params=pltpu.CompilerParams(dimension_semantics=("arbitrary",), vmem_limit_bytes=VMEM_LIMIT),
        name="final",
    )(x2, gain, final_gain, ya, za, yb, zb, qm, zm, mk, mv, w_gates, w_branch, w_out)


def _layer(x, mem, norm_gain, w_in, rel_bias, mem_norm_gain, w_mem_kv, w_branch, w_out, final_gain):
    b, t, _ = x.shape
    assert t % ROW_TILE == 0 and t // MOBA_BLOCK <= FEAT_LO and t >= 4 * DSA_TOPK
    bf = jnp.bfloat16
    names = ("qa", "ka", "va", "za", "qb", "kb", "vb", "zb", "iq", "ik", "iw", "qm", "zm", "ga", "gb", "gm")
    cols, off = {}, 0
    for name, size in zip(names, IN_SIZES):
        cols[name] = w_in[:, off:off + size]
        off += size
    w_std = jnp.concatenate([cols[n] for n, _, _ in _STD_SEGS], axis=1).astype(bf)
    w_t = jnp.concatenate([cols[n[:-1]] for n, _, _ in _T_SEGS], axis=1).T.astype(bf)
    w_gates = jnp.concatenate([cols["ga"], cols["gb"], cols["gm"]], axis=1).astype(bf)

    x2 = x.reshape(b * t, D_MODEL)
    (ka, za, kb, zb, qm, zm, ik, qa_t, va_t, qb_t, vb_t, iq_t, iw_t, kmean) = _proj_call(
        x2, norm_gain.reshape(1, D_MODEL), w_std, w_t, b, t)
    seq = lambda a: a.reshape(b, t, a.shape[-1])
    n_blocks = t // MOBA_BLOCK
    kmean = jnp.pad(kmean.reshape(b, n_blocks, MOBA_W), ((0, 0), (0, LANES - n_blocks), (0, 0)))

    bias_flat = (rel_bias * LOG2E).reshape(-1)
    ya = _moba_call(bias_flat, qa_t, seq(ka), va_t, kmean, _bucket_tiles(True))
    yb = _dsa_call(bias_flat, qb_t, seq(kb), vb_t, iq_t, seq(ik), iw_t, _bucket_tiles(False))
    mk, mv = _memkv_call(mem, mem_norm_gain.reshape(1, D_MODEL), w_mem_kv.astype(bf))
    out = _final_call(x2, norm_gain.reshape(1, D_MODEL), final_gain.reshape(1, D_MODEL),
                      ya.reshape(b * t, MOBA_W), za, yb.reshape(b * t, DSA_W), zb, qm, zm, mk, mv,
                      w_gates, w_branch.astype(bf), w_out.astype(bf), t)
    return out.reshape(b, t, D_MODEL)


@jax.jit
def kernel(x, mem, norm_gain, w_in, rel_bias, mem_norm_gain, w_mem_kv, w_branch, w_out, final_norm_gain):
    assert norm_gain.shape[0] == 1, "one layer"
    return _layer(x, mem, norm_gain[0], w_in[0], rel_bias, mem_norm_gain[0], w_mem_kv[0], w_branch[0],
                  w_out[0], final_norm_gain)
```

```python
import math

import numpy as np
import jax
import jax.numpy as jnp
from jax import lax
from jax.experimental import pallas as pl
from jax.experimental.pallas import tpu as pltpu

D_MODEL = 1024
HEAD_DIM = 64
MOBA_HEADS = 6
DSA_HEADS = 6
MEM_HEADS = 4
MOBA_W = MOBA_HEADS * HEAD_DIM
DSA_W = DSA_HEADS * HEAD_DIM
MEM_W = MEM_HEADS * HEAD_DIM
IDX_HEADS = 8
IDX_DIM = 64
IDX_W = IDX_HEADS * IDX_DIM
MOBA_BLOCK = 256
MOBA_TOPK = 3
DSA_TOPK = 256
REL_BUCKETS = 32
REL_MAX_DIST = 128
N_BIAS_HEADS = MOBA_HEADS + DSA_HEADS
EPS = 1e-6
IN_SIZES = (MOBA_W,) * 4 + (DSA_W,) * 4 + (IDX_W, IDX_DIM, IDX_HEADS) + (MEM_W, MEM_W) + (D_MODEL,) * 3

LANES = 128
SUBLANES = 8
ONES_ROWS = 16
ACC_ROWS = LANES + ONES_ROWS
TILE = 256
ROW_TILE = 512
FEAT_LO = 32
LOG2E = math.log2(math.e)
NEG_BIG = -1e30
MASK_BUCKET = REL_BUCKETS
VMEM_LIMIT = 56 * 1024 * 1024
MAX_SEARCH_ITERS = 96

_NT = (((1,), (1,)), ((), ()))


def _dot_nt(a, b):
    return lax.dot_general(a, b, _NT, preferred_element_type=jnp.float32)


def _dot(a, b):
    return jnp.dot(a, b, preferred_element_type=jnp.float32)


def _rel_bucket_np(n):
    exact = REL_BUCKETS // 2
    nf = np.maximum(n, 1).astype(np.float32)
    large = exact + (np.log(nf / np.float32(exact)) / np.float32(math.log(REL_MAX_DIST / exact))
                     * np.float32(REL_BUCKETS - exact)).astype(np.int32)
    return np.where(n < exact, n, np.minimum(large, REL_BUCKETS - 1))


def _bucket_tiles(mask_future):
    j = np.arange(TILE)[:, None]
    i = np.arange(TILE)[None, :]
    d0 = i - j
    b0 = _rel_bucket_np(np.maximum(d0, 0))
    if mask_future:
        b0 = np.where(d0 >= 0, b0, MASK_BUCKET)
    b1 = _rel_bucket_np(TILE + d0)
    assert int(_rel_bucket_np(np.array([TILE + 1]))[0]) == REL_BUCKETS - 1
    return jnp.asarray(np.stack([b0, b1]).astype(np.int32))


def _build_bias_tile(bkt, bias_ref, head):
    acc = jnp.where(bkt == MASK_BUCKET, NEG_BIG, 0.0)
    for b in range(REL_BUCKETS):
        acc = jnp.where(bkt == b, bias_ref[b * N_BIAS_HEADS + head], acc)
    return acc


def _rmsnorm_rows(x, g):
    xf = x.astype(jnp.float32)
    return xf * lax.rsqrt(jnp.mean(xf * xf, axis=-1, keepdims=True) + EPS) * g


_Q_SCALE = HEAD_DIM ** -0.5 * LOG2E
_STD_SEGS = (("ka", MOBA_W, 1.0), ("za", MOBA_W, 1.0), ("kb", DSA_W, 1.0), ("zb", DSA_W, 1.0),
             ("qm", MEM_W, _Q_SCALE), ("zm", MEM_W, 1.0), ("ik", IDX_DIM, 1.0))
_T_SEGS = (("qaT", MOBA_W, _Q_SCALE), ("vaT", MOBA_W, 1.0), ("qbT", DSA_W, _Q_SCALE), ("vbT", DSA_W, 1.0),
           ("iqT", IDX_W, IDX_DIM ** -0.5), ("iwT", IDX_HEADS, IDX_HEADS ** -0.5))
_STD_W = sum(w for _, w, _ in _STD_SEGS)
_T_W = sum(w for _, w, _ in _T_SEGS)
_CHUNKS_PER_ROW_TILE = ROW_TILE // TILE


def _proj_kernel(x_ref, g_ref, w_ref, wt_ref, *out_refs):
    std_refs = out_refs[:len(_STD_SEGS)]
    t_refs = out_refs[len(_STD_SEGS):len(_STD_SEGS) + len(_T_SEGS)]
    kmean_ref = out_refs[-1]
    h = _rmsnorm_rows(x_ref[...], g_ref[...]).astype(jnp.bfloat16)
    off = 0
    for (name, width, scale), o_ref in zip(_STD_SEGS, std_refs):
        r = _dot(h, w_ref[:, off:off + width])
        if name == "ka":
            for blk in range(ROW_TILE // MOBA_BLOCK):
                kmean_ref[0, blk:blk + 1, :] = jnp.mean(
                    r[blk * MOBA_BLOCK:(blk + 1) * MOBA_BLOCK], axis=0, keepdims=True)
        if scale != 1.0:
            r = r * scale
        o_ref[...] = r.astype(o_ref.dtype)
        off += width
    off = 0
    for (name, width, scale), o_ref in zip(_T_SEGS, t_refs):
        r = _dot_nt(wt_ref[off:off + width, :], h)
        if scale != 1.0:
            r = r * scale
        r = r.astype(o_ref.dtype)
        if name in ("vaT", "vbT"):
            for c in range(_CHUNKS_PER_ROW_TILE):
                o_ref[0, c] = r[:, c * TILE:(c + 1) * TILE]
        else:
            o_ref[0] = r
        off += width


def _proj_call(x2, gain, w_std, w_t, b, t):
    rows = x2.shape[0]
    tiles_per_batch = t // ROW_TILE
    n_tiles = rows // ROW_TILE
    row_spec = lambda width: pl.BlockSpec((ROW_TILE, width), lambda i: (i, 0))
    const_spec = lambda shape: pl.BlockSpec(shape, lambda i: (0,) * len(shape))
    t_spec = lambda width: pl.BlockSpec((1, width, ROW_TILE),
                                        lambda i: (i // tiles_per_batch, 0, i % tiles_per_batch))
    chunk_spec = lambda width: pl.BlockSpec((1, _CHUNKS_PER_ROW_TILE, width, TILE),
                                            lambda i: (i // tiles_per_batch, i % tiles_per_batch, 0, 0))
    out_shape = [jax.ShapeDtypeStruct((rows, w), jnp.bfloat16) for _, w, _ in _STD_SEGS]
    out_specs = [row_spec(w) for _, w, _ in _STD_SEGS]
    for name, w, _ in _T_SEGS:
        if name in ("vaT", "vbT"):
            out_shape.append(jax.ShapeDtypeStruct((b, t // TILE, w, TILE), jnp.bfloat16))
            out_specs.append(chunk_spec(w))
        else:
            out_shape.append(jax.ShapeDtypeStruct((b, w, t), jnp.float32 if name == "iwT" else jnp.bfloat16))
            out_specs.append(t_spec(w))
    out_shape.append(jax.ShapeDtypeStruct((n_tiles, ROW_TILE // MOBA_BLOCK, MOBA_W), jnp.float32))
    out_specs.append(pl.BlockSpec((1, ROW_TILE // MOBA_BLOCK, MOBA_W), lambda i: (i, 0, 0)))
    return pl.pallas_call(
        _proj_kernel,
        grid=(n_tiles,),
        in_specs=[row_spec(D_MODEL), const_spec((1, D_MODEL)), const_spec((D_MODEL, _STD_W)),
                  const_spec((_T_W, D_MODEL))],
        out_specs=out_specs,
        out_shape=out_shape,
        compiler_params=pltpu.CompilerParams(dimension_semantics=("arbitrary",), vmem_limit_bytes=VMEM_LIMIT),
        name="proj",
    )(x2, gain, w_std, w_t)


def _memkv_kernel(mem_ref, g_ref, w_ref, mk_ref, mv_ref):
    h = _rmsnorm_rows(mem_ref[0], g_ref[...]).astype(jnp.bfloat16)
    r = _dot(h, w_ref[...])
    mk_ref[0] = r[:, :MEM_W].astype(jnp.bfloat16)
    mv_ref[0] = r[:, MEM_W:].astype(jnp.bfloat16)


def _memkv_call(mem, gain, w_kv):
    b, n_mem, _ = mem.shape
    return pl.pallas_call(
        _memkv_kernel,
        grid=(b,),
        in_specs=[pl.BlockSpec((1, n_mem, D_MODEL), lambda i: (i, 0, 0)),
                  pl.BlockSpec((1, D_MODEL), lambda i: (0, 0)),
                  pl.BlockSpec((D_MODEL, 2 * MEM_W), lambda i: (0, 0))],
        out_specs=[pl.BlockSpec((1, n_mem, MEM_W), lambda i: (i, 0, 0))] * 2,
        out_shape=[jax.ShapeDtypeStruct((b, n_mem, MEM_W), jnp.bfloat16)] * 2,
        compiler_params=pltpu.CompilerParams(dimension_semantics=("arbitrary",), vmem_limit_bytes=VMEM_LIMIT),
        name="memkv",
    )(mem, gain, w_kv)


def _row_half_mask(shape, half):
    r = lax.broadcasted_iota(jnp.int32, shape, 0)
    return (r >= half * HEAD_DIM) & (r < (half + 1) * HEAD_DIM)


def _with_ones_rows(v_t):
    return jnp.concatenate([v_t, jnp.ones((ONES_ROWS, v_t.shape[1]), v_t.dtype)], axis=0)


def _softmax_stage(s_t, m_ref, alpha_ref, p_ref):
    m_old = m_ref[:1]
    m_new = jnp.maximum(m_old, jnp.max(s_t, axis=0, keepdims=True))
    alpha_ref[...] = jnp.broadcast_to(jnp.exp2(m_old - m_new), alpha_ref.shape)
    m_ref[...] = jnp.broadcast_to(m_new, m_ref.shape)
    p_ref[...] = jnp.exp2(s_t - m_new).astype(jnp.bfloat16)


def _accumulate_stage(v_t, alpha_ref, p_ref, acc_ref):
    acc_ref[...] = alpha_ref[:1] * acc_ref[...] + _dot(_with_ones_rows(v_t), p_ref[...])


def _init_softmax_state(m_ref, alpha_ref, p_ref, acc_ref):
    m_ref[...] = jnp.full(m_ref.shape, NEG_BIG, jnp.float32)
    acc_ref[...] = jnp.zeros(acc_ref.shape, jnp.float32)
    alpha_ref[1] = jnp.ones(alpha_ref.shape[1:], jnp.float32)
    p_ref[1] = jnp.zeros(p_ref.shape[1:], p_ref.dtype)


def _normalized_pair(acc, half_cols):
    out0 = acc[:LANES, :half_cols] / acc[LANES:LANES + 1, :half_cols]
    out1 = acc[:LANES, half_cols:] / acc[LANES:LANES + 1, half_cols:]
    return jnp.where(_row_half_mask((LANES, half_cols), 0), out0, out1).T


_KIND_DIAG, _KIND_PREV, _KIND_FAR = 0, 1, 2


def _run_tile_pipeline(qi, logits, softmax, accumulate):
    n_far = jnp.maximum(qi - 1, 0)
    n_loop = n_far // 2

    def pair_body(j, carry):
        logits(1, 2 * j + 1)
        softmax(0, 2 * j, _KIND_FAR)
        accumulate(1, jnp.maximum(2 * j - 1, 0))
        logits(0, 2 * j + 2)
        softmax(1, 2 * j + 1, _KIND_FAR)
        accumulate(0, 2 * j)
        return carry

    logits(0, 0)
    lax.fori_loop(0, n_loop, pair_body, 0)
    odd = n_far % 2 == 1

    @pl.when(odd)
    def _():
        logits(1, qi - 1)
        softmax(0, qi - 2, _KIND_FAR)
        accumulate(1, jnp.maximum(qi - 3, 0))
        logits(0, qi)
        softmax(1, qi - 1, _KIND_PREV)
        accumulate(0, qi - 2)
        softmax(0, qi, _KIND_DIAG)
        accumulate(1, qi - 1)
        accumulate(0, qi)

    @pl.when(jnp.logical_not(odd) & (qi >= 1))
    def _():
        logits(1, qi)
        softmax(0, qi - 1, _KIND_PREV)
        accumulate(1, jnp.maximum(qi - 2, 0))
        softmax(1, qi, _KIND_DIAG)
        accumulate(0, qi - 1)
        accumulate(1, qi)

    @pl.when(qi == 0)
    def _():
        softmax(0, 0, _KIND_DIAG)
        accumulate(0, 0)


def _moba_kernel(bias_ref, qt_ref, k_ref, vt_ref, kmean_ref, bkt_ref, o_ref,
                 qa_ref, tab_ref, s_ref, p_ref, alpha_ref, m_ref, acc_ref):
    pair = pl.program_id(1)
    qi = pl.program_id(2)
    q_t = qt_ref[0]
    kmean = kmean_ref[0].astype(jnp.bfloat16)
    blk = lax.broadcasted_iota(jnp.int32, (LANES, TILE), 0)
    blk_f = blk.astype(jnp.float32)
    past = blk < qi
    lane = lax.broadcasted_iota(jnp.int32, (TILE, LANES), 1)
    _init_softmax_state(m_ref, alpha_ref, p_ref, acc_ref)

    @pl.when(qi == 0)
    def _():
        for half in range(2):
            for kind in (_KIND_DIAG, _KIND_PREV):
                tab_ref[kind, :, half * TILE:(half + 1) * TILE] = _build_bias_tile(
                    bkt_ref[kind], bias_ref, pair * 2 + half)

    for half in range(2):
        qh = jnp.where(_row_half_mask((LANES, TILE), half), q_t, jnp.zeros_like(q_t))
        gate = _dot(kmean, qh)
        g = jnp.where(past, gate, -jnp.inf)
        sel = jnp.zeros((LANES, TILE), jnp.bool_)
        for _ in range(MOBA_TOPK):
            mx = jnp.max(g, axis=0, keepdims=True)
            first = jnp.min(jnp.where(g == mx, blk_f, float(LANES)), axis=0, keepdims=True)
            pick = blk_f == first
            sel = sel | (pick & past)
            g = jnp.where(pick, -jnp.inf, g)
        b31 = jnp.full((LANES, TILE), bias_ref[(REL_BUCKETS - 1) * N_BIAS_HEADS + pair * 2 + half], jnp.float32)
        b31_hi = b31.astype(jnp.bfloat16).astype(jnp.float32)
        hi_part = jnp.where(sel, jnp.where(blk < qi - 1, b31_hi, 0.0), NEG_BIG)
        hi_part = jnp.where(past, hi_part, 0.0)
        lo_part = jnp.where((blk >= FEAT_LO) & (blk - FEAT_LO < qi - 1), b31 - b31_hi, 0.0)
        feat = jnp.where(blk < FEAT_LO, hi_part, lo_part).astype(jnp.bfloat16)
        qa_ref[:, half * TILE:(half + 1) * TILE] = jnp.concatenate([qh, feat], axis=0)

    def logits(slot, kj):
        k_t = k_ref[0, pl.ds(pl.multiple_of(kj * TILE, TILE), TILE), :]
        onehot = ((lane == kj) | (lane == kj + FEAT_LO)).astype(jnp.bfloat16)
        s_ref[slot] = _dot(jnp.concatenate([k_t, onehot], axis=1), qa_ref[...])

    def softmax(slot, kj, kind):
        s_t = s_ref[slot]
        if kind != _KIND_FAR:
            s_t = s_t + tab_ref[kind]
        _softmax_stage(s_t, m_ref, alpha_ref.at[slot], p_ref.at[slot])

    def accumulate(slot, kj):
        _accumulate_stage(vt_ref[0, kj], alpha_ref.at[slot], p_ref.at[slot], acc_ref)

    _run_tile_pipeline(qi, logits, softmax, accumulate)
    o_ref[0] = _normalized_pair(acc_ref[...], TILE).astype(o_ref.dtype)


def _moba_call(bias_flat, q_t, k, v_t, kmean, bkt):
    b, _, t = q_t.shape
    n_pairs = MOBA_HEADS // 2
    n_chunks = t // TILE
    grid_spec = pltpu.PrefetchScalarGridSpec(
        num_scalar_prefetch=1,
        grid=(b, n_pairs, n_chunks),
        in_specs=[
            pl.BlockSpec((1, LANES, TILE), lambda bi, p, qi, s: (bi, p, qi)),
            pl.BlockSpec((1, t, LANES), lambda bi, p, qi, s: (bi, 0, p)),
            pl.BlockSpec((1, n_chunks, LANES, TILE), lambda bi, p, qi, s: (bi, 0, p, 0)),
            pl.BlockSpec((1, LANES, LANES), lambda bi, p, qi, s: (bi, 0, p)),
            pl.BlockSpec((2, TILE, TILE), lambda bi, p, qi, s: (0, 0, 0)),
        ],
        out_specs=pl.BlockSpec((1, TILE, LANES), lambda bi, p, qi, s: (bi, qi, p)),
        scratch_shapes=[
            pltpu.VMEM((2 * LANES, 2 * TILE), jnp.bfloat16),
            pltpu.VMEM((2, TILE, 2 * TILE), jnp.float32),
            pltpu.VMEM((2, TILE, 2 * TILE), jnp.float32),
            pltpu.VMEM((2, TILE, 2 * TILE), jnp.bfloat16),
            pltpu.VMEM((2, SUBLANES, 2 * TILE), jnp.float32),
            pltpu.VMEM((SUBLANES, 2 * TILE), jnp.float32),
            pltpu.VMEM((ACC_ROWS, 2 * TILE), jnp.float32),
        ],
    )
    return pl.pallas_call(
        _moba_kernel,
        grid_spec=grid_spec,
        out_shape=jax.ShapeDtypeStruct((b, t, MOBA_W), jnp.bfloat16),
        compiler_params=pltpu.CompilerParams(dimension_semantics=("arbitrary",) * 3, vmem_limit_bytes=VMEM_LIMIT),
        name="moba",
    )(bias_flat, q_t, k, v_t, kmean, bkt)


def _dsa_kernel(bias_ref, qt_ref, k_ref, vt_ref, iqt_ref, ik_ref, iwt_ref, bkt_ref, o_ref,
                sc_ref, tab_ref, qz_ref, s_ref, p_ref, alpha_ref, seen_ref, m_ref, acc_ref):
    qi = pl.program_id(1)
    n_chunks = qi + 1
    key = lax.broadcasted_iota(jnp.int32, (TILE, TILE), 0)
    qry = lax.broadcasted_iota(jnp.int32, (TILE, TILE), 1)
    causal = key <= qry

    far_bias = [bias_ref[(REL_BUCKETS - 1) * N_BIAS_HEADS + MOBA_HEADS + h] for h in range(DSA_HEADS)]

    @pl.when(qi == 0)
    def _():
        for h in range(DSA_HEADS):
            for kind in (_KIND_DIAG, _KIND_PREV):
                tab_ref[h // 2, kind, :, (h % 2) * TILE:(h % 2 + 1) * TILE] = _build_bias_tile(
                    bkt_ref[kind], bias_ref, MOBA_HEADS + h) - far_bias[h]

    feat_row = lax.broadcasted_iota(jnp.int32, (LANES, TILE), 0)
    for h in range(DSA_HEADS):
        blk = qt_ref[0, (h // 2) * LANES:(h // 2 + 1) * LANES, :]
        qh = jnp.where(_row_half_mask((LANES, TILE), h % 2), blk, jnp.zeros_like(blk))
        b31 = jnp.full((LANES, TILE), far_bias[h], jnp.float32)
        b31_hi = b31.astype(jnp.bfloat16).astype(jnp.float32)
        feat = jnp.where(feat_row == 0, b31_hi, jnp.where(feat_row == 1, b31 - b31_hi, 0.0))
        qz_ref[h // 2, :, (h % 2) * TILE:(h % 2 + 1) * TILE] = jnp.concatenate(
            [qh, feat.astype(jnp.bfloat16)], axis=0)

    iw_t = iwt_ref[0]

    def score_chunk(kc):
        row0 = pl.multiple_of(kc * TILE, TILE)
        ik_t = ik_ref[0, pl.ds(row0, TILE), :]
        tot = jnp.zeros((TILE, TILE), jnp.float32)
        for h in range(IDX_HEADS):
            x = _dot(ik_t, iqt_ref[0, h * IDX_DIM:(h + 1) * IDX_DIM, :])
            tot = tot + iw_t[h:h + 1, :] * jnp.maximum(x, 0.0)
        return tot

    def score_body(kc, carry):
        sc_ref[kc] = score_chunk(kc)
        return carry

    lax.fori_loop(0, qi, score_body, 0)
    sc_ref[qi] = jnp.where(causal, score_chunk(qi), -jnp.inf)

    @pl.when(qi % 2 == 0)
    def _():
        sc_ref[qi + 1] = jnp.full((TILE, TILE), -jnp.inf, jnp.float32)

    def reduce_pass(fn, init, combine):
        def body(j, part):
            return combine(part, combine(fn(sc_ref[2 * j]), fn(sc_ref[2 * j + 1])))
        return lax.fori_loop(0, (n_chunks + 1) // 2, body, jnp.full((1, TILE), init, jnp.float32))

    def count(pred):
        return reduce_pass(lambda x: jnp.sum(jnp.where(pred(x), 1.0, 0.0), axis=0, keepdims=True), 0.0, jnp.add)

    def min_ge(p):
        return reduce_pass(lambda x: jnp.min(jnp.where(x >= p, x, jnp.inf), axis=0, keepdims=True),
                           jnp.inf, jnp.minimum)

    row_max = reduce_pass(lambda x: jnp.max(x, axis=0, keepdims=True), -jnp.inf, jnp.maximum)
    row_min = reduce_pass(lambda x: jnp.min(jnp.where(x == -jnp.inf, jnp.inf, x), axis=0, keepdims=True),
                          jnp.inf, jnp.minimum)
    n_adm = (qi * TILE + 1 + lax.broadcasted_iota(jnp.int32, (1, TILE), 1)).astype(jnp.float32)
    k_top = float(DSA_TOPK)
    take_all = n_adm <= k_top

    def search_cond(st):
        return jnp.logical_and(st[0] < MAX_SEARCH_ITERS, jnp.min(st[6]) < 0.5)

    def tie_check(st):
        it, lo, hi, hi_fin, c_lo, c_hi, done, last, streak = st
        a = min_ge(lo)
        c_gt = count(lambda x: x > a)
        tie = (c_gt < k_top) & (done < 0.5)
        return it, lo, hi, hi_fin, c_lo, jnp.where(tie, c_gt, c_hi), jnp.where(tie, 1.0, done), last, streak

    def search_body(st):
        it, lo, hi, hi_fin, c_lo, c_hi, done, last, streak = st
        target = k_top - 0.5 + jnp.where(streak >= 1.0, last * 0.25 * (c_lo - c_hi), 0.0)
        target = jnp.clip(target, c_hi + 0.5, c_lo - 0.5)
        frac = (c_lo - target) / jnp.maximum(c_lo - c_hi, 1.0)
        mid = 0.5 * lo + 0.5 * hi_fin
        p = jnp.where(it % 5 == 4, mid, lo + frac * (hi_fin - lo))
        p = jnp.where((p > lo) & (p < hi), p, mid)
        stuck = jnp.logical_not((mid > lo) & (mid < hi))
        c = count(lambda x: x >= p)
        up = c >= k_top
        move = jnp.logical_not((done > 0.5) | stuck)
        rise = move & up
        fall = move & jnp.logical_not(up)
        lo = jnp.where(rise, p, lo)
        c_lo = jnp.where(rise, c, c_lo)
        hi = jnp.where(fall, p, hi)
        c_hi = jnp.where(fall, c, c_hi)
        hi_fin = jnp.where(fall, p, hi_fin)
        side = jnp.where(up, -1.0, 1.0)
        streak = jnp.where(side == last, streak + 1.0, 0.0)
        done = jnp.where((c_lo - c_hi <= 1.0) | (c_lo == k_top) | stuck, 1.0, done)
        st = (it + 1, lo, hi, hi_fin, c_lo, c_hi, done, side, streak)
        check_now = (it + 1 >= 8) & ((it + 1) % 4 == 0) & (jnp.min(done) < 0.5)
        return lax.cond(check_now, tie_check, lambda s: s, st)

    zeros_row = jnp.zeros((1, TILE), jnp.float32)
    init = (jnp.int32(0), row_min, jnp.full((1, TILE), jnp.inf, jnp.float32), row_max,
            n_adm, zeros_row, take_all.astype(jnp.float32), zeros_row, zeros_row)
    _, lo, _, _, _, c_hi, _, _, _ = lax.while_loop(search_cond, search_body, init)
    thr = jnp.where(take_all, -jnp.inf, min_ge(lo))
    need = k_top - jnp.where(take_all, 0.0, c_hi)

    n_pairs = DSA_HEADS // 2
    for pr in range(n_pairs):
        _init_softmax_state(m_ref.at[pr], alpha_ref.at[:, pr], p_ref.at[:, pr], acc_ref.at[pr])
    seen_ref[...] = jnp.zeros(seen_ref.shape, jnp.float32)
    lower = (qry < key).astype(jnp.bfloat16)
    ones_feat = (lax.broadcasted_iota(jnp.int32, (TILE, LANES), 1) < 2).astype(jnp.bfloat16)

    def logits(slot, kc):
        row0 = pl.multiple_of(kc * TILE, TILE)
        for pr in range(n_pairs):
            k_t = k_ref[0, pl.ds(row0, TILE), pr * LANES:(pr + 1) * LANES]
            s_ref[slot, pr] = _dot(jnp.concatenate([k_t, ones_feat], axis=1), qz_ref[pr])

    def softmax(slot, kc, kind):
        x = sc_ref[kc]
        tie = x == thr
        tie_f = jnp.where(tie, 1.0, 0.0)
        seen = seen_ref[:1]
        rank = seen + _dot(lower, tie_f.astype(jnp.bfloat16))
        selected = (x > thr) | (tie & (rank < need))
        if kind == _KIND_DIAG:
            selected = selected & causal
        selected = jnp.concatenate([selected, selected], axis=1)
        seen_ref[...] = jnp.broadcast_to(seen + jnp.sum(tie_f, axis=0, keepdims=True), seen_ref.shape)
        for pr in range(n_pairs):
            s_t = s_ref[slot, pr]
            if kind != _KIND_FAR:
                s_t = s_t + tab_ref[pr, kind]
            _softmax_stage(jnp.where(selected, s_t, NEG_BIG), m_ref.at[pr], alpha_ref.at[slot, pr],
                           p_ref.at[slot, pr])

    def accumulate(slot, kc):
        for pr in range(n_pairs):
            _accumulate_stage(vt_ref[0, kc, pr * LANES:(pr + 1) * LANES, :], alpha_ref.at[slot, pr],
                              p_ref.at[slot, pr], acc_ref.at[pr])

    _run_tile_pipeline(qi, logits, softmax, accumulate)
    for pr in range(n_pairs):
        o_ref[0, :, pr * LANES:(pr + 1) * LANES] = _normalized_pair(acc_ref[pr], TILE).astype(o_ref.dtype)


def _dsa_call(bias_flat, q_t, k, v_t, iq_t, ik, iw_t, bkt):
    b, _, t = q_t.shape
    n_chunks = t // TILE
    n_pairs = DSA_HEADS // 2
    grid_spec = pltpu.PrefetchScalarGridSpec(
        num_scalar_prefetch=1,
        grid=(b, n_chunks),
        in_specs=[
            pl.BlockSpec((1, DSA_W, TILE), lambda bi, qi, s: (bi, 0, qi)),
            pl.BlockSpec((1, t, DSA_W), lambda bi, qi, s: (bi, 0, 0)),
            pl.BlockSpec((1, n_chunks, DSA_W, TILE), lambda bi, qi, s: (bi, 0, 0, 0)),
            pl.BlockSpec((1, IDX_W, TILE), lambda bi, qi, s: (bi, 0, qi)),
            pl.BlockSpec((1, t, IDX_DIM), lambda bi, qi, s: (bi, 0, 0)),
            pl.BlockSpec((1, IDX_HEADS, TILE), lambda bi, qi, s: (bi, 0, qi)),
            pl.BlockSpec((2, TILE, TILE), lambda bi, qi, s: (0, 0, 0)),
        ],
        out_specs=pl.BlockSpec((1, TILE, DSA_W), lambda bi, qi, s: (bi, qi, 0)),
        scratch_shapes=[
            pltpu.VMEM((n_chunks, TILE, TILE), jnp.float32),
            pltpu.VMEM((n_pairs, 2, TILE, 2 * TILE), jnp.float32),
            pltpu.VMEM((n_pairs, 2 * LANES, 2 * TILE), jnp.bfloat16),
            pltpu.VMEM((2, n_pairs, TILE, 2 * TILE), jnp.float32),
            pltpu.VMEM((2, n_pairs, TILE, 2 * TILE), jnp.bfloat16),
            pltpu.VMEM((2, n_pairs, SUBLANES, 2 * TILE), jnp.float32),
            pltpu.VMEM((SUBLANES, TILE), jnp.float32),
            pltpu.VMEM((n_pairs, SUBLANES, 2 * TILE), jnp.float32),
            pltpu.VMEM((n_pairs, ACC_ROWS, 2 * TILE), jnp.float32),
        ],
    )
    return pl.pallas_call(
        _dsa_kernel,
        grid_spec=grid_spec,
        out_shape=jax.ShapeDtypeStruct((b, t, DSA_W), jnp.bfloat16),
        compiler_params=pltpu.CompilerParams(dimension_semantics=("arbitrary",) * 2, vmem_limit_bytes=VMEM_LIMIT),
        name="dsa",
    )(bias_flat, q_t, k, v_t, iq_t, ik, iw_t, bkt)


def _lane_half_mask(shape, half):
    lane = lax.broadcasted_iota(jnp.int32, shape, 1)
    return (lane >= half * HEAD_DIM) & (lane < (half + 1) * HEAD_DIM)


def _final_kernel(x_ref, g_ref, fg_ref, ya_ref, za_ref, yb_ref, zb_ref, qm_ref, zm_ref, mk_ref, mv_ref,
                  wg_ref, wb_ref, wo_ref, o_ref):
    x = x_ref[...]
    h = _rmsnorm_rows(x, g_ref[...]).astype(jnp.bfloat16)

    ym_parts = []
    for pr in range(MEM_HEADS // 2):
        qp = qm_ref[:, pr * LANES:(pr + 1) * LANES]
        mk = mk_ref[0, :, pr * LANES:(pr + 1) * LANES]
        mv = mv_ref[0, :, pr * LANES:(pr + 1) * LANES]
        outs = []
        for half in range(2):
            qh = jnp.where(_lane_half_mask((ROW_TILE, LANES), half), qp, jnp.zeros_like(qp))
            s = _dot_nt(qh, mk)
            p = jnp.exp2(s - jnp.max(s, axis=1, keepdims=True))
            outs.append(_dot(p.astype(jnp.bfloat16), mv) / jnp.sum(p, axis=1, keepdims=True))
        ym_parts.append(jnp.where(_lane_half_mask((ROW_TILE, LANES), 0), outs[0], outs[1]))
    ym = jnp.concatenate(ym_parts, axis=1)

    def gated(y, z_ref):
        z = z_ref[...].astype(jnp.float32)
        return (y.astype(jnp.float32) * (z * jax.nn.sigmoid(z))).astype(jnp.bfloat16)

    ua = gated(ya_ref[...], za_ref)
    ub = gated(yb_ref[...], zb_ref)
    um = gated(ym, zm_ref)
    merged = jax.nn.sigmoid(_dot(h, wg_ref[:, :D_MODEL])) * _dot(ua, wb_ref[:MOBA_W])
    merged += jax.nn.sigmoid(_dot(h, wg_ref[:, D_MODEL:2 * D_MODEL])) * _dot(ub, wb_ref[MOBA_W:MOBA_W + DSA_W])
    merged += jax.nn.sigmoid(_dot(h, wg_ref[:, 2 * D_MODEL:])) * _dot(um, wb_ref[MOBA_W + DSA_W:])
    y = x + _dot(merged.astype(jnp.bfloat16), wo_ref[...])
    o_ref[...] = _rmsnorm_rows(y, fg_ref[...])


def _final_call(x2, gain, final_gain, ya, za, yb, zb, qm, zm, mk, mv, w_gates, w_branch, w_out, rows_per_batch):
    rows = x2.shape[0]
    tiles_per_batch = rows_per_batch // ROW_TILE
    row_spec = lambda width: pl.BlockSpec((ROW_TILE, width), lambda i: (i, 0))
    const_spec = lambda shape: pl.BlockSpec(shape, lambda i: (0,) * len(shape))
    mem_spec = pl.BlockSpec((1,) + mk.shape[1:], lambda i: (i // tiles_per_batch, 0, 0))
    return pl.pallas_call(
        _final_kernel,
        grid=(rows // ROW_TILE,),
        in_specs=[row_spec(D_MODEL), const_spec((1, D_MODEL)), const_spec((1, D_MODEL)),
                  row_spec(MOBA_W), row_spec(MOBA_W), row_spec(DSA_W), row_spec(DSA_W),
                  row_spec(MEM_W), row_spec(MEM_W), mem_spec, mem_spec,
                  const_spec(w_gates.shape), const_spec(w_branch.shape), const_spec(w_out.shape)],
        out_specs=row_spec(D_MODEL),
        out_shape=jax.ShapeDtypeStruct((rows, D_MODEL), jnp.float32),
        compiler_params=pltpu.CompilerParams(dimension_semantics=("arbitrary",), vmem_limit_bytes=VMEM_LIMIT),
        name="final",
    )(x2, gain, final_gain, ya, za, yb, zb, qm, zm, mk, mv, w_gates, w_branch, w_out)


def _layer(x, mem, norm_gain, w_in, rel_bias, mem_norm_gain, w_mem_kv, w_branch, w_out, final_gain):
    b, t, _ = x.shape
    assert t % (2 * TILE) == 0 and t // MOBA_BLOCK <= FEAT_LO and t >= 4 * DSA_TOPK
    bf = jnp.bfloat16
    names = ("qa", "ka", "va", "za", "qb", "kb", "vb", "zb", "iq", "ik", "iw", "qm", "zm", "ga", "gb", "gm")
    cols, off = {}, 0
    for name, size in zip(names, IN_SIZES):
        cols[name] = w_in[:, off:off + size]
        off += size
    w_std = jnp.concatenate([cols[n] for n, _, _ in _STD_SEGS], axis=1).astype(bf)
    w_t = jnp.concatenate([cols[n[:-1]] for n, _, _ in _T_SEGS], axis=1).T.astype(bf)
    w_gates = jnp.concatenate([cols["ga"], cols["gb"], cols["gm"]], axis=1).astype(bf)

    x2 = x.reshape(b * t, D_MODEL)
    (ka, za, kb, zb, qm, zm, ik, qa_t, va_t, qb_t, vb_t, iq_t, iw_t, kmean) = _proj_call(
        x2, norm_gain.reshape(1, D_MODEL), w_std, w_t, b, t)
    seq = lambda a: a.reshape(b, t, a.shape[-1])
    n_blocks = t // MOBA_BLOCK
    kmean = jnp.pad(kmean.reshape(b, n_blocks, MOBA_W), ((0, 0), (0, LANES - n_blocks), (0, 0)))

    bias_flat = (rel_bias * LOG2E).reshape(-1)
    ya = _moba_call(bias_flat, qa_t, seq(ka), va_t, kmean, _bucket_tiles(True))
    yb = _dsa_call(bias_flat, qb_t, seq(kb), vb_t, iq_t, seq(ik), iw_t, _bucket_tiles(False))
    mk, mv = _memkv_call(mem, mem_norm_gain.reshape(1, D_MODEL), w_mem_kv.astype(bf))
    out = _final_call(x2, norm_gain.reshape(1, D_MODEL), final_gain.reshape(1, D_MODEL),
                      ya.reshape(b * t, MOBA_W), za, yb.reshape(b * t, DSA_W), zb, qm, zm, mk, mv,
                      w_gates, w_branch.astype(bf), w_out.astype(bf), t)
    return out.reshape(b, t, D_MODEL)


@jax.jit
def kernel(x, mem, norm_gain, w_in, rel_bias, mem_norm_gain, w_mem_kv, w_branch, w_out, final_norm_gain):
    assert norm_gain.shape[0] == 1, "one layer"
    return _layer(x, mem, norm_gain[0], w_in[0], rel_bias, mem_norm_gain[0], w_mem_kv[0], w_branch[0],
                  w_out[0], final_norm_gain)
```

```python
import math

import numpy as np
import jax
import jax.numpy as jnp
from jax import lax
from jax.experimental import pallas as pl
from jax.experimental.pallas import tpu as pltpu

D_MODEL = 1024
HEAD_DIM = 64
MOBA_HEADS = 6
DSA_HEADS = 6
MEM_HEADS = 4
MOBA_W = MOBA_HEADS * HEAD_DIM
DSA_W = DSA_HEADS * HEAD_DIM
MEM_W = MEM_HEADS * HEAD_DIM
IDX_HEADS = 8
IDX_DIM = 64
IDX_W = IDX_HEADS * IDX_DIM
MOBA_BLOCK = 256
MOBA_TOPK = 3
DSA_TOPK = 256
REL_BUCKETS = 32
REL_MAX_DIST = 128
N_BIAS_HEADS = MOBA_HEADS + DSA_HEADS
EPS = 1e-6
IN_SIZES = (MOBA_W,) * 4 + (DSA_W,) * 4 + (IDX_W, IDX_DIM, IDX_HEADS) + (MEM_W, MEM_W) + (D_MODEL,) * 3

LANES = 128
SUBLANES = 8
ONES_ROWS = 16
ACC_ROWS = LANES + ONES_ROWS
TILE = 256
ROW_TILE = 512
FEAT_LO = 32
LOG2E = math.log2(math.e)
NEG_BIG = -1e30
MASK_BUCKET = REL_BUCKETS
VMEM_LIMIT = 56 * 1024 * 1024
MAX_SEARCH_ITERS = 96

_NT = (((1,), (1,)), ((), ()))


def _dot_nt(a, b):
    return lax.dot_general(a, b, _NT, preferred_element_type=jnp.float32)


def _dot(a, b):
    return jnp.dot(a, b, preferred_element_type=jnp.float32)


def _rel_bucket_np(n):
    exact = REL_BUCKETS // 2
    nf = np.maximum(n, 1).astype(np.float32)
    large = exact + (np.log(nf / np.float32(exact)) / np.float32(math.log(REL_MAX_DIST / exact))
                     * np.float32(REL_BUCKETS - exact)).astype(np.int32)
    return np.where(n < exact, n, np.minimum(large, REL_BUCKETS - 1))


def _bucket_tiles(mask_future):
    j = np.arange(TILE)[:, None]
    i = np.arange(TILE)[None, :]
    d0 = i - j
    b0 = _rel_bucket_np(np.maximum(d0, 0))
    if mask_future:
        b0 = np.where(d0 >= 0, b0, MASK_BUCKET)
    b1 = _rel_bucket_np(TILE + d0)
    assert int(_rel_bucket_np(np.array([TILE + 1]))[0]) == REL_BUCKETS - 1
    return jnp.asarray(np.stack([b0, b1]).astype(np.int32))


def _build_bias_tile(bkt, bias_ref, head):
    acc = jnp.where(bkt == MASK_BUCKET, NEG_BIG, 0.0)
    for b in range(REL_BUCKETS):
        acc = jnp.where(bkt == b, bias_ref[b * N_BIAS_HEADS + head], acc)
    return acc


def _rmsnorm_rows(x, g):
    xf = x.astype(jnp.float32)
    return xf * lax.rsqrt(jnp.mean(xf * xf, axis=-1, keepdims=True) + EPS) * g


_Q_SCALE = HEAD_DIM ** -0.5 * LOG2E
_STD_SEGS = (("ka", MOBA_W, 1.0), ("za", MOBA_W, 1.0), ("kb", DSA_W, 1.0), ("zb", DSA_W, 1.0),
             ("qm", MEM_W, _Q_SCALE), ("zm", MEM_W, 1.0), ("ik", IDX_DIM, 1.0))
_T_SEGS = (("qaT", MOBA_W, _Q_SCALE), ("vaT", MOBA_W, 1.0), ("qbT", DSA_W, _Q_SCALE), ("vbT", DSA_W, 1.0),
           ("iqT", IDX_W, IDX_DIM ** -0.5), ("iwT", IDX_HEADS, IDX_HEADS ** -0.5))
_STD_W = sum(w for _, w, _ in _STD_SEGS)
_T_W = sum(w for _, w, _ in _T_SEGS)
_CHUNKS_PER_ROW_TILE = ROW_TILE // TILE


def _proj_kernel(x_ref, g_ref, w_ref, wt_ref, *out_refs):
    std_refs = out_refs[:len(_STD_SEGS)]
    t_refs = out_refs[len(_STD_SEGS):len(_STD_SEGS) + len(_T_SEGS)]
    kmean_ref = out_refs[-1]
    h = _rmsnorm_rows(x_ref[...], g_ref[...]).astype(jnp.bfloat16)
    off = 0
    for (name, width, scale), o_ref in zip(_STD_SEGS, std_refs):
        r = _dot(h, w_ref[:, off:off + width])
        if name == "ka":
            for blk in range(ROW_TILE // MOBA_BLOCK):
                kmean_ref[0, blk:blk + 1, :] = jnp.mean(
                    r[blk * MOBA_BLOCK:(blk + 1) * MOBA_BLOCK], axis=0, keepdims=True)
        if scale != 1.0:
            r = r * scale
        o_ref[...] = r.astype(o_ref.dtype)
        off += width
    off = 0
    for (name, width, scale), o_ref in zip(_T_SEGS, t_refs):
        r = _dot_nt(wt_ref[off:off + width, :], h)
        if scale != 1.0:
            r = r * scale
        r = r.astype(o_ref.dtype)
        if name in ("vaT", "vbT"):
            for c in range(_CHUNKS_PER_ROW_TILE):
                o_ref[0, c] = r[:, c * TILE:(c + 1) * TILE]
        else:
            o_ref[0] = r
        off += width


def _proj_call(x2, gain, w_std, w_t, b, t):
    rows = x2.shape[0]
    tiles_per_batch = t // ROW_TILE
    n_tiles = rows // ROW_TILE
    row_spec = lambda width: pl.BlockSpec((ROW_TILE, width), lambda i: (i, 0))
    const_spec = lambda shape: pl.BlockSpec(shape, lambda i: (0,) * len(shape))
    t_spec = lambda width: pl.BlockSpec((1, width, ROW_TILE),
                                        lambda i: (i // tiles_per_batch, 0, i % tiles_per_batch))
    chunk_spec = lambda width: pl.BlockSpec((1, _CHUNKS_PER_ROW_TILE, width, TILE),
                                            lambda i: (i // tiles_per_batch, i % tiles_per_batch, 0, 0))
    out_shape = [jax.ShapeDtypeStruct((rows, w), jnp.bfloat16) for _, w, _ in _STD_SEGS]
    out_specs = [row_spec(w) for _, w, _ in _STD_SEGS]
    for name, w, _ in _T_SEGS:
        if name in ("vaT", "vbT"):
            out_shape.append(jax.ShapeDtypeStruct((b, t // TILE, w, TILE), jnp.bfloat16))
            out_specs.append(chunk_spec(w))
        else:
            out_shape.append(jax.ShapeDtypeStruct((b, w, t), jnp.float32 if name == "iwT" else jnp.bfloat16))
            out_specs.append(t_spec(w))
    out_shape.append(jax.ShapeDtypeStruct((n_tiles, ROW_TILE // MOBA_BLOCK, MOBA_W), jnp.float32))
    out_specs.append(pl.BlockSpec((1, ROW_TILE // MOBA_BLOCK, MOBA_W), lambda i: (i, 0, 0)))
    return pl.pallas_call(
        _proj_kernel,
        grid=(n_tiles,),
        in_specs=[row_spec(D_MODEL), const_spec((1, D_MODEL)), const_spec((D_MODEL, _STD_W)),
                  const_spec((_T_W, D_MODEL))],
        out_specs=out_specs,
        out_shape=out_shape,
        compiler_params=pltpu.CompilerParams(dimension_semantics=("arbitrary",), vmem_limit_bytes=VMEM_LIMIT),
        name="proj",
    )(x2, gain, w_std, w_t)


def _memkv_kernel(mem_ref, g_ref, w_ref, mk_ref, mv_ref):
    h = _rmsnorm_rows(mem_ref[0], g_ref[...]).astype(jnp.bfloat16)
    r = _dot(h, w_ref[...])
    mk_ref[0] = r[:, :MEM_W].astype(jnp.bfloat16)
    mv_ref[0] = r[:, MEM_W:].astype(jnp.bfloat16)


def _memkv_call(mem, gain, w_kv):
    b, n_mem, _ = mem.shape
    return pl.pallas_call(
        _memkv_kernel,
        grid=(b,),
        in_specs=[pl.BlockSpec((1, n_mem, D_MODEL), lambda i: (i, 0, 0)),
                  pl.BlockSpec((1, D_MODEL), lambda i: (0, 0)),
                  pl.BlockSpec((D_MODEL, 2 * MEM_W), lambda i: (0, 0))],
        out_specs=[pl.BlockSpec((1, n_mem, MEM_W), lambda i: (i, 0, 0))] * 2,
        out_shape=[jax.ShapeDtypeStruct((b, n_mem, MEM_W), jnp.bfloat16)] * 2,
        compiler_params=pltpu.CompilerParams(dimension_semantics=("arbitrary",), vmem_limit_bytes=VMEM_LIMIT),
        name="memkv",
    )(mem, gain, w_kv)


def _row_half_mask(shape, half):
    r = lax.broadcasted_iota(jnp.int32, shape, 0)
    return (r >= half * HEAD_DIM) & (r < (half + 1) * HEAD_DIM)


def _with_ones_rows(v_t):
    return jnp.concatenate([v_t, jnp.ones((ONES_ROWS, v_t.shape[1]), v_t.dtype)], axis=0)


def _softmax_stage(s_t, m_ref, alpha_ref, p_ref):
    m_old = m_ref[:1]
    m_new = jnp.maximum(m_old, jnp.max(s_t, axis=0, keepdims=True))
    alpha_ref[...] = jnp.broadcast_to(jnp.exp2(m_old - m_new), alpha_ref.shape)
    m_ref[...] = jnp.broadcast_to(m_new, m_ref.shape)
    p_ref[...] = jnp.exp2(s_t - m_new).astype(jnp.bfloat16)


def _accumulate_stage(v_t, alpha_ref, p_ref, acc_ref):
    acc_ref[...] = alpha_ref[:1] * acc_ref[...] + _dot(_with_ones_rows(v_t), p_ref[...])


def _init_softmax_state(m_ref, alpha_ref, p_ref, acc_ref):
    m_ref[...] = jnp.full(m_ref.shape, NEG_BIG, jnp.float32)
    acc_ref[...] = jnp.zeros(acc_ref.shape, jnp.float32)
    alpha_ref[...] = jnp.ones(alpha_ref.shape, jnp.float32)
    p_ref[...] = jnp.zeros(p_ref.shape, p_ref.dtype)


def _normalized_pair(acc, half_cols):
    out0 = acc[:LANES, :half_cols] / acc[LANES:LANES + 1, :half_cols]
    out1 = acc[:LANES, half_cols:] / acc[LANES:LANES + 1, half_cols:]
    return jnp.where(_row_half_mask((LANES, half_cols), 0), out0, out1).T


_KIND_DIAG, _KIND_PREV, _KIND_FAR = 0, 1, 2


def _run_tile_pipeline(qi, logits, softmax, accumulate):
    n_far = jnp.maximum(qi - 1, 0)
    n_loop = n_far // 2

    def far_pair(t):
        accumulate(0, jnp.maximum(t - 2, 0))
        accumulate(1, jnp.maximum(t - 1, 0))
        softmax(0, t, _KIND_FAR)
        logits(0, t + 2)
        softmax(1, t + 1, _KIND_FAR)
        logits(1, t + 3)

    def quad_body(j, carry):
        far_pair(4 * j)
        far_pair(4 * j + 2)
        return carry

    def pair_body(j, carry):
        far_pair(2 * j)
        return carry

    logits(0, 0)
    logits(1, jnp.minimum(1, qi))
    lax.fori_loop(0, n_loop // 2, quad_body, 0)
    lax.fori_loop(2 * (n_loop // 2), n_loop, pair_body, 0)
    t0 = 2 * n_loop
    odd = n_far % 2 == 1

    @pl.when(odd)
    def _():
        accumulate(0, jnp.maximum(t0 - 2, 0))
        accumulate(1, jnp.maximum(t0 - 1, 0))
        softmax(0, t0, _KIND_FAR)
        logits(0, qi)
        softmax(1, qi - 1, _KIND_PREV)
        accumulate(0, t0)
        softmax(0, qi, _KIND_DIAG)
        accumulate(1, qi - 1)
        accumulate(0, qi)

    @pl.when(jnp.logical_not(odd) & (qi >= 1))
    def _():
        accumulate(0, jnp.maximum(t0 - 2, 0))
        accumulate(1, jnp.maximum(t0 - 1, 0))
        softmax(0, qi - 1, _KIND_PREV)
        softmax(1, qi, _KIND_DIAG)
        accumulate(0, qi - 1)
        accumulate(1, qi)

    @pl.when(qi == 0)
    def _():
        softmax(0, 0, _KIND_DIAG)
        accumulate(0, 0)


def _moba_kernel(bias_ref, qt_ref, k_ref, vt_ref, kmean_ref, bkt_ref, o_ref,
                 qa_ref, tab_ref, s_ref, p_ref, alpha_ref, m_ref, acc_ref):
    pair = pl.program_id(1)
    qi = pl.program_id(2)
    q_t = qt_ref[0]
    kmean = kmean_ref[0].astype(jnp.bfloat16)
    blk = lax.broadcasted_iota(jnp.int32, (LANES, TILE), 0)
    blk_f = blk.astype(jnp.float32)
    past = blk < qi
    lane = lax.broadcasted_iota(jnp.int32, (TILE, LANES), 1)
    _init_softmax_state(m_ref, alpha_ref, p_ref, acc_ref)

    @pl.when(qi == 0)
    def _():
        for half in range(2):
            for kind in (_KIND_DIAG, _KIND_PREV):
                tab_ref[kind, :, half * TILE:(half + 1) * TILE] = _build_bias_tile(
                    bkt_ref[kind], bias_ref, pair * 2 + half)

    for half in range(2):
        qh = jnp.where(_row_half_mask((LANES, TILE), half), q_t, jnp.zeros_like(q_t))
        gate = _dot(kmean, qh)
        g = jnp.where(past, gate, -jnp.inf)
        sel = jnp.zeros((LANES, TILE), jnp.bool_)
        for _ in range(MOBA_TOPK):
            mx = jnp.max(g, axis=0, keepdims=True)
            first = jnp.min(jnp.where(g == mx, blk_f, float(LANES)), axis=0, keepdims=True)
            pick = blk_f == first
            sel = sel | (pick & past)
            g = jnp.where(pick, -jnp.inf, g)
        b31 = jnp.full((LANES, TILE), bias_ref[(REL_BUCKETS - 1) * N_BIAS_HEADS + pair * 2 + half], jnp.float32)
        b31_hi = b31.astype(jnp.bfloat16).astype(jnp.float32)
        hi_part = jnp.where(sel, jnp.where(blk < qi - 1, b31_hi, 0.0), NEG_BIG)
        hi_part = jnp.where(past, hi_part, 0.0)
        lo_part = jnp.where((blk >= FEAT_LO) & (blk - FEAT_LO < qi - 1), b31 - b31_hi, 0.0)
        feat = jnp.where(blk < FEAT_LO, hi_part, lo_part).astype(jnp.bfloat16)
        qa_ref[:, half * TILE:(half + 1) * TILE] = jnp.concatenate([qh, feat], axis=0)

    def logits(slot, kj):
        k_t = k_ref[0, pl.ds(pl.multiple_of(kj * TILE, TILE), TILE), :]
        onehot = ((lane == kj) | (lane == kj + FEAT_LO)).astype(jnp.bfloat16)
        s_ref[slot] = _dot(jnp.concatenate([k_t, onehot], axis=1), qa_ref[...])

    def softmax(slot, kj, kind):
        s_t = s_ref[slot]
        if kind != _KIND_FAR:
            s_t = s_t + tab_ref[kind]
        _softmax_stage(s_t, m_ref, alpha_ref.at[slot], p_ref.at[slot])

    def accumulate(slot, kj):
        _accumulate_stage(vt_ref[0, kj], alpha_ref.at[slot], p_ref.at[slot], acc_ref)

    _run_tile_pipeline(qi, logits, softmax, accumulate)
    o_ref[0] = _normalized_pair(acc_ref[...], TILE).astype(o_ref.dtype)


def _moba_call(bias_flat, q_t, k, v_t, kmean, bkt):
    b, _, t = q_t.shape
    n_pairs = MOBA_HEADS // 2
    n_chunks = t // TILE
    grid_spec = pltpu.PrefetchScalarGridSpec(
        num_scalar_prefetch=1,
        grid=(b, n_pairs, n_chunks),
        in_specs=[
            pl.BlockSpec((1, LANES, TILE), lambda bi, p, qi, s: (bi, p, qi)),
            pl.BlockSpec((1, t, LANES), lambda bi, p, qi, s: (bi, 0, p)),
            pl.BlockSpec((1, n_chunks, LANES, TILE), lambda bi, p, qi, s: (bi, 0, p, 0)),
            pl.BlockSpec((1, LANES, LANES), lambda bi, p, qi, s: (bi, 0, p)),
            pl.BlockSpec((2, TILE, TILE), lambda bi, p, qi, s: (0, 0, 0)),
        ],
        out_specs=pl.BlockSpec((1, TILE, LANES), lambda bi, p, qi, s: (bi, qi, p)),
        scratch_shapes=[
            pltpu.VMEM((2 * LANES, 2 * TILE), jnp.bfloat16),
            pltpu.VMEM((2, TILE, 2 * TILE), jnp.float32),
            pltpu.VMEM((2, TILE, 2 * TILE), jnp.float32),
            pltpu.VMEM((2, TILE, 2 * TILE), jnp.bfloat16),
            pltpu.VMEM((2, SUBLANES, 2 * TILE), jnp.float32),
            pltpu.VMEM((SUBLANES, 2 * TILE), jnp.float32),
            pltpu.VMEM((ACC_ROWS, 2 * TILE), jnp.float32),
        ],
    )
    return pl.pallas_call(
        _moba_kernel,
        grid_spec=grid_spec,
        out_shape=jax.ShapeDtypeStruct((b, t, MOBA_W), jnp.bfloat16),
        compiler_params=pltpu.CompilerParams(dimension_semantics=("arbitrary",) * 3, vmem_limit_bytes=VMEM_LIMIT),
        name="moba",
    )(bias_flat, q_t, k, v_t, kmean, bkt)


def _dsa_kernel(bias_ref, qt_ref, k_ref, vt_ref, iqt_ref, ik_ref, iwt_ref, bkt_ref, o_ref,
                sc_ref, tab_ref, qz_ref, s_ref, p_ref, alpha_ref, seen_ref, m_ref, acc_ref):
    qi = pl.program_id(1)
    n_chunks = qi + 1
    key = lax.broadcasted_iota(jnp.int32, (TILE, TILE), 0)
    qry = lax.broadcasted_iota(jnp.int32, (TILE, TILE), 1)
    causal = key <= qry

    far_bias = [bias_ref[(REL_BUCKETS - 1) * N_BIAS_HEADS + MOBA_HEADS + h] for h in range(DSA_HEADS)]

    @pl.when(qi == 0)
    def _():
        for h in range(DSA_HEADS):
            for kind in (_KIND_DIAG, _KIND_PREV):
                tab_ref[h // 2, kind, :, (h % 2) * TILE:(h % 2 + 1) * TILE] = _build_bias_tile(
                    bkt_ref[kind], bias_ref, MOBA_HEADS + h) - far_bias[h]

    feat_row = lax.broadcasted_iota(jnp.int32, (LANES, TILE), 0)
    for h in range(DSA_HEADS):
        blk = qt_ref[0, (h // 2) * LANES:(h // 2 + 1) * LANES, :]
        qh = jnp.where(_row_half_mask((LANES, TILE), h % 2), blk, jnp.zeros_like(blk))
        b31 = jnp.full((LANES, TILE), far_bias[h], jnp.float32)
        b31_hi = b31.astype(jnp.bfloat16).astype(jnp.float32)
        feat = jnp.where(feat_row == 0, b31_hi, jnp.where(feat_row == 1, b31 - b31_hi, 0.0))
        qz_ref[h // 2, :, (h % 2) * TILE:(h % 2 + 1) * TILE] = jnp.concatenate(
            [qh, feat.astype(jnp.bfloat16)], axis=0)

    iw_t = iwt_ref[0]

    def score_chunk(kc):
        row0 = pl.multiple_of(kc * TILE, TILE)
        ik_t = ik_ref[0, pl.ds(row0, TILE), :]
        tot = jnp.zeros((TILE, TILE), jnp.float32)
        for h in range(IDX_HEADS):
            x = _dot(ik_t, iqt_ref[0, h * IDX_DIM:(h + 1) * IDX_DIM, :])
            tot = tot + iw_t[h:h + 1, :] * jnp.maximum(x, 0.0)
        return tot

    def score_body(kc, carry):
        sc_ref[kc] = score_chunk(kc)
        return carry

    lax.fori_loop(0, qi, score_body, 0)
    sc_ref[qi] = jnp.where(causal, score_chunk(qi), -jnp.inf)

    @pl.when(qi % 2 == 0)
    def _():
        sc_ref[qi + 1] = jnp.full((TILE, TILE), -jnp.inf, jnp.float32)

    def reduce_pass(fn, init, combine):
        def body(j, part):
            return combine(part, combine(fn(sc_ref[2 * j]), fn(sc_ref[2 * j + 1])))
        return lax.fori_loop(0, (n_chunks + 1) // 2, body, jnp.full((1, TILE), init, jnp.float32))

    def count(pred):
        return reduce_pass(lambda x: jnp.sum(jnp.where(pred(x), 1.0, 0.0), axis=0, keepdims=True), 0.0, jnp.add)

    def min_ge(p):
        return reduce_pass(lambda x: jnp.min(jnp.where(x >= p, x, jnp.inf), axis=0, keepdims=True),
                           jnp.inf, jnp.minimum)

    row_max = reduce_pass(lambda x: jnp.max(x, axis=0, keepdims=True), -jnp.inf, jnp.maximum)
    row_min = reduce_pass(lambda x: jnp.min(jnp.where(x == -jnp.inf, jnp.inf, x), axis=0, keepdims=True),
                          jnp.inf, jnp.minimum)
    n_adm = (qi * TILE + 1 + lax.broadcasted_iota(jnp.int32, (1, TILE), 1)).astype(jnp.float32)
    k_top = float(DSA_TOPK)
    take_all = n_adm <= k_top

    def search_cond(st):
        return jnp.logical_and(st[0] < MAX_SEARCH_ITERS, jnp.min(st[6]) < 0.5)

    def tie_check(st):
        it, lo, hi, hi_fin, c_lo, c_hi, done, last, streak = st
        a = min_ge(lo)
        c_gt = count(lambda x: x > a)
        tie = (c_gt < k_top) & (done < 0.5)
        return it, lo, hi, hi_fin, c_lo, jnp.where(tie, c_gt, c_hi), jnp.where(tie, 1.0, done), last, streak

    def search_body(st):
        it, lo, hi, hi_fin, c_lo, c_hi, done, last, streak = st
        target = k_top - 0.5 + jnp.where(streak >= 1.0, last * 0.25 * (c_lo - c_hi), 0.0)
        target = jnp.clip(target, c_hi + 0.5, c_lo - 0.5)
        frac = (c_lo - target) / jnp.maximum(c_lo - c_hi, 1.0)
        mid = 0.5 * lo + 0.5 * hi_fin
        p = jnp.where(it % 5 == 4, mid, lo + frac * (hi_fin - lo))
        p = jnp.where((p > lo) & (p < hi), p, mid)
        stuck = jnp.logical_not((mid > lo) & (mid < hi))
        c = count(lambda x: x >= p)
        up = c >= k_top
        move = jnp.logical_not((done > 0.5) | stuck)
        rise = move & up
        fall = move & jnp.logical_not(up)
        lo = jnp.where(rise, p, lo)
        c_lo = jnp.where(rise, c, c_lo)
        hi = jnp.where(fall, p, hi)
        c_hi = jnp.where(fall, c, c_hi)
        hi_fin = jnp.where(fall, p, hi_fin)
        side = jnp.where(up, -1.0, 1.0)
        streak = jnp.where(side == last, streak + 1.0, 0.0)
        done = jnp.where((c_lo - c_hi <= 1.0) | (c_lo == k_top) | stuck, 1.0, done)
        st = (it + 1, lo, hi, hi_fin, c_lo, c_hi, done, side, streak)
        check_now = (it + 1 >= 8) & ((it + 1) % 4 == 0) & (jnp.min(done) < 0.5)
        return lax.cond(check_now, tie_check, lambda s: s, st)

    zeros_row = jnp.zeros((1, TILE), jnp.float32)
    init = (jnp.int32(0), row_min, jnp.full((1, TILE), jnp.inf, jnp.float32), row_max,
            n_adm, zeros_row, take_all.astype(jnp.float32), zeros_row, zeros_row)
    _, lo, _, _, _, c_hi, _, _, _ = lax.while_loop(search_cond, search_body, init)
    thr = jnp.where(take_all, -jnp.inf, min_ge(lo))
    need = k_top - jnp.where(take_all, 0.0, c_hi)

    n_pairs = DSA_HEADS // 2
    for pr in range(n_pairs):
        _init_softmax_state(m_ref.at[pr], alpha_ref.at[:, pr], p_ref.at[:, pr], acc_ref.at[pr])
    seen_ref[...] = jnp.zeros(seen_ref.shape, jnp.float32)
    lower = (qry < key).astype(jnp.bfloat16)
    ones_feat = (lax.broadcasted_iota(jnp.int32, (TILE, LANES), 1) < 2).astype(jnp.bfloat16)

    def logits(slot, kc):
        row0 = pl.multiple_of(kc * TILE, TILE)
        for pr in range(n_pairs):
            k_t = k_ref[0, pl.ds(row0, TILE), pr * LANES:(pr + 1) * LANES]
            s_ref[slot, pr] = _dot(jnp.concatenate([k_t, ones_feat], axis=1), qz_ref[pr])

    def softmax(slot, kc, kind):
        x = sc_ref[kc]
        tie = x == thr
        tie_f = jnp.where(tie, 1.0, 0.0)
        seen = seen_ref[:1]
        rank = seen + _dot(lower, tie_f.astype(jnp.bfloat16))
        selected = (x > thr) | (tie & (rank < need))
        if kind == _KIND_DIAG:
            selected = selected & causal
        mask_add = jnp.where(selected, 0.0, NEG_BIG)
        mask_add = jnp.concatenate([mask_add, mask_add], axis=1)
        seen_ref[...] = jnp.broadcast_to(seen + jnp.sum(tie_f, axis=0, keepdims=True), seen_ref.shape)
        for pr in range(n_pairs):
            s_t = s_ref[slot, pr] + mask_add
            if kind != _KIND_FAR:
                s_t = s_t + tab_ref[pr, kind]
            _softmax_stage(s_t, m_ref.at[pr], alpha_ref.at[slot, pr], p_ref.at[slot, pr])

    def accumulate(slot, kc):
        for pr in range(n_pairs):
            _accumulate_stage(vt_ref[0, kc, pr * LANES:(pr + 1) * LANES, :], alpha_ref.at[slot, pr],
                              p_ref.at[slot, pr], acc_ref.at[pr])

    _run_tile_pipeline(qi, logits, softmax, accumulate)
    for pr in range(n_pairs):
        o_ref[0, :, pr * LANES:(pr + 1) * LANES] = _normalized_pair(acc_ref[pr], TILE).astype(o_ref.dtype)


def _dsa_call(bias_flat, q_t, k, v_t, iq_t, ik, iw_t, bkt):
    b, _, t = q_t.shape
    n_chunks = t // TILE
    n_pairs = DSA_HEADS // 2
    grid_spec = pltpu.PrefetchScalarGridSpec(
        num_scalar_prefetch=1,
        grid=(b, n_chunks),
        in_specs=[
            pl.BlockSpec((1, DSA_W, TILE), lambda bi, qi, s: (bi, 0, qi)),
            pl.BlockSpec((1, t, DSA_W), lambda bi, qi, s: (bi, 0, 0)),
            pl.BlockSpec((1, n_chunks, DSA_W, TILE), lambda bi, qi, s: (bi, 0, 0, 0)),
            pl.BlockSpec((1, IDX_W, TILE), lambda bi, qi, s: (bi, 0, qi)),
            pl.BlockSpec((1, t, IDX_DIM), lambda bi, qi, s: (bi, 0, 0)),
            pl.BlockSpec((1, IDX_HEADS, TILE), lambda bi, qi, s: (bi, 0, qi)),
            pl.BlockSpec((2, TILE, TILE), lambda bi, qi, s: (0, 0, 0)),
        ],
        out_specs=pl.BlockSpec((1, TILE, DSA_W), lambda bi, qi, s: (bi, qi, 0)),
        scratch_shapes=[
            pltpu.VMEM((n_chunks, TILE, TILE), jnp.float32),
            pltpu.VMEM((n_pairs, 2, TILE, 2 * TILE), jnp.float32),
            pltpu.VMEM((n_pairs, 2 * LANES, 2 * TILE), jnp.bfloat16),
            pltpu.VMEM((2, n_pairs, TILE, 2 * TILE), jnp.float32),
            pltpu.VMEM((2, n_pairs, TILE, 2 * TILE), jnp.bfloat16),
            pltpu.VMEM((2, n_pairs, SUBLANES, 2 * TILE), jnp.float32),
            pltpu.VMEM((SUBLANES, TILE), jnp.float32),
            pltpu.VMEM((n_pairs, SUBLANES, 2 * TILE), jnp.float32),
            pltpu.VMEM((n_pairs, ACC_ROWS, 2 * TILE), jnp.float32),
        ],
    )
    return pl.pallas_call(
        _dsa_kernel,
        grid_spec=grid_spec,
        out_shape=jax.ShapeDtypeStruct((b, t, DSA_W), jnp.bfloat16),
        compiler_params=pltpu.CompilerParams(dimension_semantics=("arbitrary",) * 2, vmem_limit_bytes=VMEM_LIMIT),
        name="dsa",
    )(bias_flat, q_t, k, v_t, iq_t, ik, iw_t, bkt)


def _lane_half_mask(shape, half):
    lane = lax.broadcasted_iota(jnp.int32, shape, 1)
    return (lane >= half * HEAD_DIM) & (lane < (half + 1) * HEAD_DIM)


def _final_kernel(x_ref, g_ref, fg_ref, ya_ref, za_ref, yb_ref, zb_ref, qm_ref, zm_ref, mk_ref, mv_ref,
                  wg_ref, wb_ref, wo_ref, o_ref):
    x = x_ref[...]
    h = _rmsnorm_rows(x, g_ref[...]).astype(jnp.bfloat16)

    ym_parts = []
    for pr in range(MEM_HEADS // 2):
        qp = qm_ref[:, pr * LANES:(pr + 1) * LANES]
        mk = mk_ref[0, :, pr * LANES:(pr + 1) * LANES]
        mv = mv_ref[0, :, pr * LANES:(pr + 1) * LANES]
        outs = []
        for half in range(2):
            qh = jnp.where(_lane_half_mask((ROW_TILE, LANES), half), qp, jnp.zeros_like(qp))
            s = _dot_nt(qh, mk)
            p = jnp.exp2(s - jnp.max(s, axis=1, keepdims=True))
            outs.append(_dot(p.astype(jnp.bfloat16), mv) / jnp.sum(p, axis=1, keepdims=True))
        ym_parts.append(jnp.where(_lane_half_mask((ROW_TILE, LANES), 0), outs[0], outs[1]))
    ym = jnp.concatenate(ym_parts, axis=1)

    def gated(y, z_ref):
        z = z_ref[...].astype(jnp.float32)
        return (y.astype(jnp.float32) * (z * jax.nn.sigmoid(z))).astype(jnp.bfloat16)

    ua = gated(ya_ref[...], za_ref)
    ub = gated(yb_ref[...], zb_ref)
    um = gated(ym, zm_ref)
    merged = jax.nn.sigmoid(_dot(h, wg_ref[:, :D_MODEL])) * _dot(ua, wb_ref[:MOBA_W])
    merged += jax.nn.sigmoid(_dot(h, wg_ref[:, D_MODEL:2 * D_MODEL])) * _dot(ub, wb_ref[MOBA_W:MOBA_W + DSA_W])
    merged += jax.nn.sigmoid(_dot(h, wg_ref[:, 2 * D_MODEL:])) * _dot(um, wb_ref[MOBA_W + DSA_W:])
    y = x + _dot(merged.astype(jnp.bfloat16), wo_ref[...])
    o_ref[...] = _rmsnorm_rows(y, fg_ref[...])


def _final_call(x2, gain, final_gain, ya, za, yb, zb, qm, zm, mk, mv, w_gates, w_branch, w_out, rows_per_batch):
    rows = x2.shape[0]
    tiles_per_batch = rows_per_batch // ROW_TILE
    row_spec = lambda width: pl.BlockSpec((ROW_TILE, width), lambda i: (i, 0))
    const_spec = lambda shape: pl.BlockSpec(shape, lambda i: (0,) * len(shape))
    mem_spec = pl.BlockSpec((1,) + mk.shape[1:], lambda i: (i // tiles_per_batch, 0, 0))
    return pl.pallas_call(
        _final_kernel,
        grid=(rows // ROW_TILE,),
        in_specs=[row_spec(D_MODEL), const_spec((1, D_MODEL)), const_spec((1, D_MODEL)),
                  row_spec(MOBA_W), row_spec(MOBA_W), row_spec(DSA_W), row_spec(DSA_W),
                  row_spec(MEM_W), row_spec(MEM_W), mem_spec, mem_spec,
                  const_spec(w_gates.shape), const_spec(w_branch.shape), const_spec(w_out.shape)],
        out_specs=row_spec(D_MODEL),
        out_shape=jax.ShapeDtypeStruct((rows, D_MODEL), jnp.float32),
        compiler_params=pltpu.CompilerParams(dimension_semantics=("arbitrary",), vmem_limit_bytes=VMEM_LIMIT),
        name="final",
    )(x2, gain, final_gain, ya, za, yb, zb, qm, zm, mk, mv, w_gates, w_branch, w_out)


def _layer(x, mem, norm_gain, w_in, rel_bias, mem_norm_gain, w_mem_kv, w_branch, w_out, final_gain):
    b, t, _ = x.shape
    assert t % (2 * TILE) == 0 and t // MOBA_BLOCK <= FEAT_LO and t >= 4 * DSA_TOPK
    bf = jnp.bfloat16
    names = ("qa", "ka", "va", "za", "qb", "kb", "vb", "zb", "iq", "ik", "iw", "qm", "zm", "ga", "gb", "gm")
    cols, off = {}, 0
    for name, size in zip(names, IN_SIZES):
        cols[name] = w_in[:, off:off + size]
        off += size
    w_std = jnp.concatenate([cols[n] for n, _, _ in _STD_SEGS], axis=1).astype(bf)
    w_t = jnp.concatenate([cols[n[:-1]] for n, _, _ in _T_SEGS], axis=1).T.astype(bf)
    w_gates = jnp.concatenate([cols["ga"], cols["gb"], cols["gm"]], axis=1).astype(bf)

    x2 = x.reshape(b * t, D_MODEL)
    (ka, za, kb, zb, qm, zm, ik, qa_t, va_t, qb_t, vb_t, iq_t, iw_t, kmean) = _proj_call(
        x2, norm_gain.reshape(1, D_MODEL), w_std, w_t, b, t)
    seq = lambda a: a.reshape(b, t, a.shape[-1])
    n_blocks = t // MOBA_BLOCK
    kmean = jnp.pad(kmean.reshape(b, n_blocks, MOBA_W), ((0, 0), (0, LANES - n_blocks), (0, 0)))

    bias_flat = (rel_bias * LOG2E).reshape(-1)
    ya = _moba_call(bias_flat, qa_t, seq(ka), va_t, kmean, _bucket_tiles(True))
    yb = _dsa_call(bias_flat, qb_t, seq(kb), vb_t, iq_t, seq(ik), iw_t, _bucket_tiles(False))
    mk, mv = _memkv_call(mem, mem_norm_gain.reshape(1, D_MODEL), w_mem_kv.astype(bf))
    out = _final_call(x2, norm_gain.reshape(1, D_MODEL), final_gain.reshape(1, D_MODEL),
                      ya.reshape(b * t, MOBA_W), za, yb.reshape(b * t, DSA_W), zb, qm, zm, mk, mv,
                      w_gates, w_branch.astype(bf), w_out.astype(bf), t)
    return out.reshape(b, t, D_MODEL)


@jax.jit
def kernel(x, mem, norm_gain, w_in, rel_bias, mem_norm_gain, w_mem_kv, w_branch, w_out, final_norm_gain):
    assert norm_gain.shape[0] == 1, "one layer"
    return _layer(x, mem, norm_gain[0], w_in[0], rel_bias, mem_norm_gain[0], w_mem_kv[0], w_branch[0],
                  w_out[0], final_norm_gain)
```

```python
import math

import numpy as np
import jax
import jax.numpy as jnp
from jax import lax
from jax.experimental import pallas as pl
from jax.experimental.pallas import tpu as pltpu

D_MODEL = 1024
HEAD_DIM = 64
MOBA_HEADS = 6
DSA_HEADS = 6
MEM_HEADS = 4
MOBA_W = MOBA_HEADS * HEAD_DIM
DSA_W = DSA_HEADS * HEAD_DIM
MEM_W = MEM_HEADS * HEAD_DIM
IDX_HEADS = 8
IDX_DIM = 64
IDX_W = IDX_HEADS * IDX_DIM
MOBA_BLOCK = 256
MOBA_TOPK = 3
DSA_TOPK = 256
REL_BUCKETS = 32
REL_MAX_DIST = 128
N_BIAS_HEADS = MOBA_HEADS + DSA_HEADS
EPS = 1e-6
IN_SIZES = (MOBA_W,) * 4 + (DSA_W,) * 4 + (IDX_W, IDX_DIM, IDX_HEADS) + (MEM_W, MEM_W) + (D_MODEL,) * 3

LANES = 128
SUBLANES = 8
ONES_ROWS = 16
ACC_ROWS = LANES + ONES_ROWS
TILE = 256
ROW_TILE = 512
FEAT_LO = 32
LOG2E = math.log2(math.e)
NEG_BIG = -1e30
MASK_BUCKET = REL_BUCKETS
VMEM_LIMIT = 56 * 1024 * 1024
MAX_SEARCH_ITERS = 96

_NT = (((1,), (1,)), ((), ()))


def _dot_nt(a, b):
    return lax.dot_general(a, b, _NT, preferred_element_type=jnp.float32)


def _dot(a, b):
    return jnp.dot(a, b, preferred_element_type=jnp.float32)


def _rel_bucket_np(n):
    exact = REL_BUCKETS // 2
    nf = np.maximum(n, 1).astype(np.float32)
    large = exact + (np.log(nf / np.float32(exact)) / np.float32(math.log(REL_MAX_DIST / exact))
                     * np.float32(REL_BUCKETS - exact)).astype(np.int32)
    return np.where(n < exact, n, np.minimum(large, REL_BUCKETS - 1))


def _bucket_tiles(mask_future):
    j = np.arange(TILE)[:, None]
    i = np.arange(TILE)[None, :]
    d0 = i - j
    b0 = _rel_bucket_np(np.maximum(d0, 0))
    if mask_future:
        b0 = np.where(d0 >= 0, b0, MASK_BUCKET)
    b1 = _rel_bucket_np(TILE + d0)
    assert int(_rel_bucket_np(np.array([TILE + 1]))[0]) == REL_BUCKETS - 1
    return jnp.asarray(np.stack([b0, b1]).astype(np.int32))


def _build_bias_tile(bkt, bias_ref, head):
    acc = jnp.where(bkt == MASK_BUCKET, NEG_BIG, 0.0)
    for b in range(REL_BUCKETS):
        acc = jnp.where(bkt == b, bias_ref[b * N_BIAS_HEADS + head], acc)
    return acc


def _rmsnorm_rows(x, g):
    xf = x.astype(jnp.float32)
    return xf * lax.rsqrt(jnp.mean(xf * xf, axis=-1, keepdims=True) + EPS) * g


_Q_SCALE = HEAD_DIM ** -0.5 * LOG2E
_STD_SEGS = (("ka", MOBA_W, 1.0), ("za", MOBA_W, 1.0), ("kb", DSA_W, 1.0), ("zb", DSA_W, 1.0),
             ("qm", MEM_W, _Q_SCALE), ("zm", MEM_W, 1.0), ("ik", IDX_DIM, 1.0))
_T_SEGS = (("qaT", MOBA_W, _Q_SCALE), ("vaT", MOBA_W, 1.0), ("qbT", DSA_W, _Q_SCALE), ("vbT", DSA_W, 1.0),
           ("iqT", IDX_W, IDX_DIM ** -0.5), ("iwT", IDX_HEADS, IDX_HEADS ** -0.5))
_STD_W = sum(w for _, w, _ in _STD_SEGS)
_T_W = sum(w for _, w, _ in _T_SEGS)
_CHUNKS_PER_ROW_TILE = ROW_TILE // TILE


def _proj_kernel(x_ref, g_ref, w_ref, wt_ref, *out_refs):
    std_refs = out_refs[:len(_STD_SEGS)]
    t_refs = out_refs[len(_STD_SEGS):len(_STD_SEGS) + len(_T_SEGS)]
    kmean_ref = out_refs[-1]
    h = _rmsnorm_rows(x_ref[...], g_ref[...]).astype(jnp.bfloat16)
    off = 0
    for (name, width, scale), o_ref in zip(_STD_SEGS, std_refs):
        r = _dot(h, w_ref[:, off:off + width])
        if name == "ka":
            for blk in range(ROW_TILE // MOBA_BLOCK):
                kmean_ref[0, blk:blk + 1, :] = jnp.mean(
                    r[blk * MOBA_BLOCK:(blk + 1) * MOBA_BLOCK], axis=0, keepdims=True)
        if scale != 1.0:
            r = r * scale
        o_ref[...] = r.astype(o_ref.dtype)
        off += width
    off = 0
    for (name, width, scale), o_ref in zip(_T_SEGS, t_refs):
        r = _dot_nt(wt_ref[off:off + width, :], h)
        if scale != 1.0:
            r = r * scale
        r = r.astype(o_ref.dtype)
        if name in ("vaT", "vbT"):
            for c in range(_CHUNKS_PER_ROW_TILE):
                o_ref[0, c] = r[:, c * TILE:(c + 1) * TILE]
        else:
            o_ref[0] = r
        off += width


def _proj_call(x2, gain, w_std, w_t, b, t):
    rows = x2.shape[0]
    tiles_per_batch = t // ROW_TILE
    n_tiles = rows // ROW_TILE
    row_spec = lambda width: pl.BlockSpec((ROW_TILE, width), lambda i: (i, 0))
    const_spec = lambda shape: pl.BlockSpec(shape, lambda i: (0,) * len(shape))
    t_spec = lambda width: pl.BlockSpec((1, width, ROW_TILE),
                                        lambda i: (i // tiles_per_batch, 0, i % tiles_per_batch))
    chunk_spec = lambda width: pl.BlockSpec((1, _CHUNKS_PER_ROW_TILE, width, TILE),
                                            lambda i: (i // tiles_per_batch, i % tiles_per_batch, 0, 0))
    out_shape = [jax.ShapeDtypeStruct((rows, w), jnp.bfloat16) for _, w, _ in _STD_SEGS]
    out_specs = [row_spec(w) for _, w, _ in _STD_SEGS]
    for name, w, _ in _T_SEGS:
        if name in ("vaT", "vbT"):
            out_shape.append(jax.ShapeDtypeStruct((b, t // TILE, w, TILE), jnp.bfloat16))
            out_specs.append(chunk_spec(w))
        else:
            out_shape.append(jax.ShapeDtypeStruct((b, w, t), jnp.float32 if name == "iwT" else jnp.bfloat16))
            out_specs.append(t_spec(w))
    out_shape.append(jax.ShapeDtypeStruct((n_tiles, ROW_TILE // MOBA_BLOCK, MOBA_W), jnp.float32))
    out_specs.append(pl.BlockSpec((1, ROW_TILE // MOBA_BLOCK, MOBA_W), lambda i: (i, 0, 0)))
    return pl.pallas_call(
        _proj_kernel,
        grid=(n_tiles,),
        in_specs=[row_spec(D_MODEL), const_spec((1, D_MODEL)), const_spec((D_MODEL, _STD_W)),
                  const_spec((_T_W, D_MODEL))],
        out_specs=out_specs,
        out_shape=out_shape,
        compiler_params=pltpu.CompilerParams(dimension_semantics=("arbitrary",), vmem_limit_bytes=VMEM_LIMIT),
        name="proj",
    )(x2, gain, w_std, w_t)


def _memkv_kernel(mem_ref, g_ref, w_ref, mk_ref, mv_ref):
    h = _rmsnorm_rows(mem_ref[0], g_ref[...]).astype(jnp.bfloat16)
    r = _dot(h, w_ref[...])
    mk_ref[0] = r[:, :MEM_W].astype(jnp.bfloat16)
    mv_ref[0] = r[:, MEM_W:].astype(jnp.bfloat16)


def _memkv_call(mem, gain, w_kv):
    b, n_mem, _ = mem.shape
    return pl.pallas_call(
        _memkv_kernel,
        grid=(b,),
        in_specs=[pl.BlockSpec((1, n_mem, D_MODEL), lambda i: (i, 0, 0)),
                  pl.BlockSpec((1, D_MODEL), lambda i: (0, 0)),
                  pl.BlockSpec((D_MODEL, 2 * MEM_W), lambda i: (0, 0))],
        out_specs=[pl.BlockSpec((1, n_mem, MEM_W), lambda i: (i, 0, 0))] * 2,
        out_shape=[jax.ShapeDtypeStruct((b, n_mem, MEM_W), jnp.bfloat16)] * 2,
        compiler_params=pltpu.CompilerParams(dimension_semantics=("arbitrary",), vmem_limit_bytes=VMEM_LIMIT),
        name="memkv",
    )(mem, gain, w_kv)


def _row_half_mask(shape, half):
    r = lax.broadcasted_iota(jnp.int32, shape, 0)
    return (r >= half * HEAD_DIM) & (r < (half + 1) * HEAD_DIM)


def _with_ones_rows(v_t):
    return jnp.concatenate([v_t, jnp.ones((ONES_ROWS, v_t.shape[1]), v_t.dtype)], axis=0)


def _softmax_stage(s_t, m_ref, alpha_ref, p_ref):
    m_old = m_ref[:1]
    m_new = jnp.maximum(m_old, jnp.max(s_t, axis=0, keepdims=True))
    alpha_ref[...] = jnp.broadcast_to(jnp.exp2(m_old - m_new), alpha_ref.shape)
    m_ref[...] = jnp.broadcast_to(m_new, m_ref.shape)
    p_ref[...] = jnp.exp2(s_t - m_new).astype(jnp.bfloat16)


def _accumulate_stage(v_t, alpha_ref, p_ref, acc_ref):
    acc_ref[...] = alpha_ref[:1] * acc_ref[...] + _dot(_with_ones_rows(v_t), p_ref[...])


def _init_softmax_state(m_ref, alpha_ref, p_ref, acc_ref):
    m_ref[...] = jnp.full(m_ref.shape, NEG_BIG, jnp.float32)
    acc_ref[...] = jnp.zeros(acc_ref.shape, jnp.float32)
    alpha_ref[...] = jnp.ones(alpha_ref.shape, jnp.float32)
    p_ref[...] = jnp.zeros(p_ref.shape, p_ref.dtype)


def _normalized_pair(acc, half_cols):
    out0 = acc[:LANES, :half_cols] / acc[LANES:LANES + 1, :half_cols]
    out1 = acc[:LANES, half_cols:] / acc[LANES:LANES + 1, half_cols:]
    return jnp.where(_row_half_mask((LANES, half_cols), 0), out0, out1).T


_KIND_DIAG, _KIND_PREV, _KIND_FAR = 0, 1, 2


def _run_tile_pipeline(qi, logits, softmax, accumulate):
    n_far = jnp.maximum(qi - 1, 0)
    n_loop = n_far // 2

    def far_pair(t):
        accumulate(0, jnp.maximum(t - 2, 0))
        accumulate(1, jnp.maximum(t - 1, 0))
        softmax(0, t, _KIND_FAR)
        logits(0, t + 2)
        softmax(1, t + 1, _KIND_FAR)
        logits(1, t + 3)

    def quad_body(j, carry):
        far_pair(4 * j)
        far_pair(4 * j + 2)
        return carry

    def pair_body(j, carry):
        far_pair(2 * j)
        return carry

    logits(0, 0)
    logits(1, jnp.minimum(1, qi))
    lax.fori_loop(0, n_loop // 2, quad_body, 0)
    lax.fori_loop(2 * (n_loop // 2), n_loop, pair_body, 0)
    t0 = 2 * n_loop
    odd = n_far % 2 == 1

    @pl.when(odd)
    def _():
        accumulate(0, jnp.maximum(t0 - 2, 0))
        accumulate(1, jnp.maximum(t0 - 1, 0))
        softmax(0, t0, _KIND_FAR)
        logits(0, qi)
        softmax(1, qi - 1, _KIND_PREV)
        accumulate(0, t0)
        softmax(0, qi, _KIND_DIAG)
        accumulate(1, qi - 1)
        accumulate(0, qi)

    @pl.when(jnp.logical_not(odd) & (qi >= 1))
    def _():
        accumulate(0, jnp.maximum(t0 - 2, 0))
        accumulate(1, jnp.maximum(t0 - 1, 0))
        softmax(0, qi - 1, _KIND_PREV)
        softmax(1, qi, _KIND_DIAG)
        accumulate(0, qi - 1)
        accumulate(1, qi)

    @pl.when(qi == 0)
    def _():
        softmax(0, 0, _KIND_DIAG)
        accumulate(0, 0)


def _moba_kernel(bias_ref, qt_ref, k_ref, vt_ref, kmean_ref, bkt_ref, o_ref,
                 qa_ref, tab_ref, s_ref, p_ref, alpha_ref, m_ref, acc_ref):
    pair = pl.program_id(1)
    qi = pl.program_id(2)
    q_t = qt_ref[0]
    kmean = kmean_ref[0].astype(jnp.bfloat16)
    blk = lax.broadcasted_iota(jnp.int32, (LANES, TILE), 0)
    blk_f = blk.astype(jnp.float32)
    past = blk < qi
    lane = lax.broadcasted_iota(jnp.int32, (TILE, LANES), 1)
    _init_softmax_state(m_ref, alpha_ref, p_ref, acc_ref)

    @pl.when(qi == 0)
    def _():
        for half in range(2):
            for kind in (_KIND_DIAG, _KIND_PREV):
                tab_ref[kind, :, half * TILE:(half + 1) * TILE] = _build_bias_tile(
                    bkt_ref[kind], bias_ref, pair * 2 + half)

    for half in range(2):
        qh = jnp.where(_row_half_mask((LANES, TILE), half), q_t, jnp.zeros_like(q_t))
        gate = _dot(kmean, qh)
        g = jnp.where(past, gate, -jnp.inf)
        sel = jnp.zeros((LANES, TILE), jnp.bool_)
        for _ in range(MOBA_TOPK):
            mx = jnp.max(g, axis=0, keepdims=True)
            first = jnp.min(jnp.where(g == mx, blk_f, float(LANES)), axis=0, keepdims=True)
            pick = blk_f == first
            sel = sel | (pick & past)
            g = jnp.where(pick, -jnp.inf, g)
        b31 = jnp.full((LANES, TILE), bias_ref[(REL_BUCKETS - 1) * N_BIAS_HEADS + pair * 2 + half], jnp.float32)
        b31_hi = b31.astype(jnp.bfloat16).astype(jnp.float32)
        hi_part = jnp.where(sel, jnp.where(blk < qi - 1, b31_hi, 0.0), NEG_BIG)
        hi_part = jnp.where(past, hi_part, 0.0)
        lo_part = jnp.where((blk >= FEAT_LO) & (blk - FEAT_LO < qi - 1), b31 - b31_hi, 0.0)
        feat = jnp.where(blk < FEAT_LO, hi_part, lo_part).astype(jnp.bfloat16)
        qa_ref[:, half * TILE:(half + 1) * TILE] = jnp.concatenate([qh, feat], axis=0)

    def logits(slot, kj):
        k_t = k_ref[0, pl.ds(pl.multiple_of(kj * TILE, TILE), TILE), :]
        onehot = ((lane == kj) | (lane == kj + FEAT_LO)).astype(jnp.bfloat16)
        s_ref[slot] = _dot(jnp.concatenate([k_t, onehot], axis=1), qa_ref[...])

    def softmax(slot, kj, kind):
        s_t = s_ref[slot]
        if kind != _KIND_FAR:
            s_t = s_t + tab_ref[kind]
        _softmax_stage(s_t, m_ref, alpha_ref.at[slot], p_ref.at[slot])

    def accumulate(slot, kj):
        _accumulate_stage(vt_ref[0, kj], alpha_ref.at[slot], p_ref.at[slot], acc_ref)

    _run_tile_pipeline(qi, logits, softmax, accumulate)
    o_ref[0] = _normalized_pair(acc_ref[...], TILE).astype(o_ref.dtype)


def _moba_call(bias_flat, q_t, k, v_t, kmean, bkt):
    b, _, t = q_t.shape
    n_pairs = MOBA_HEADS // 2
    n_chunks = t // TILE
    grid_spec = pltpu.PrefetchScalarGridSpec(
        num_scalar_prefetch=1,
        grid=(b, n_pairs, n_chunks),
        in_specs=[
            pl.BlockSpec((1, LANES, TILE), lambda bi, p, qi, s: (bi, p, qi)),
            pl.BlockSpec((1, t, LANES), lambda bi, p, qi, s: (bi, 0, p)),
            pl.BlockSpec((1, n_chunks, LANES, TILE), lambda bi, p, qi, s: (bi, 0, p, 0)),
            pl.BlockSpec((1, LANES, LANES), lambda bi, p, qi, s: (bi, 0, p)),
            pl.BlockSpec((2, TILE, TILE), lambda bi, p, qi, s: (0, 0, 0)),
        ],
        out_specs=pl.BlockSpec((1, TILE, LANES), lambda bi, p, qi, s: (bi, qi, p)),
        scratch_shapes=[
            pltpu.VMEM((2 * LANES, 2 * TILE), jnp.bfloat16),
            pltpu.VMEM((2, TILE, 2 * TILE), jnp.float32),
            pltpu.VMEM((2, TILE, 2 * TILE), jnp.float32),
            pltpu.VMEM((2, TILE, 2 * TILE), jnp.bfloat16),
            pltpu.VMEM((2, SUBLANES, 2 * TILE), jnp.float32),
            pltpu.VMEM((SUBLANES, 2 * TILE), jnp.float32),
            pltpu.VMEM((ACC_ROWS, 2 * TILE), jnp.float32),
        ],
    )
    return pl.pallas_call(
        _moba_kernel,
        grid_spec=grid_spec,
        out_shape=jax.ShapeDtypeStruct((b, t, MOBA_W), jnp.bfloat16),
        compiler_params=pltpu.CompilerParams(dimension_semantics=("arbitrary",) * 3, vmem_limit_bytes=VMEM_LIMIT),
        name="moba",
    )(bias_flat, q_t, k, v_t, kmean, bkt)


def _dsa_kernel(bias_ref, qt_ref, k_ref, vt_ref, iqt_ref, ik_ref, iwt_ref, bkt_ref, o_ref,
                sc_ref, tab_ref, qz_ref, s_ref, p_ref, alpha_ref, seen_ref, m_ref, acc_ref):
    qi = pl.program_id(1)
    n_chunks = qi + 1
    key = lax.broadcasted_iota(jnp.int32, (TILE, TILE), 0)
    qry = lax.broadcasted_iota(jnp.int32, (TILE, TILE), 1)
    causal = key <= qry

    far_bias = [bias_ref[(REL_BUCKETS - 1) * N_BIAS_HEADS + MOBA_HEADS + h] for h in range(DSA_HEADS)]

    @pl.when(qi == 0)
    def _():
        for h in range(DSA_HEADS):
            for kind in (_KIND_DIAG, _KIND_PREV):
                tab_ref[h // 2, kind, :, (h % 2) * TILE:(h % 2 + 1) * TILE] = _build_bias_tile(
                    bkt_ref[kind], bias_ref, MOBA_HEADS + h) - far_bias[h]

    feat_row = lax.broadcasted_iota(jnp.int32, (LANES, TILE), 0)
    for h in range(DSA_HEADS):
        blk = qt_ref[0, (h // 2) * LANES:(h // 2 + 1) * LANES, :]
        qh = jnp.where(_row_half_mask((LANES, TILE), h % 2), blk, jnp.zeros_like(blk))
        b31 = jnp.full((LANES, TILE), far_bias[h], jnp.float32)
        b31_hi = b31.astype(jnp.bfloat16).astype(jnp.float32)
        feat = jnp.where(feat_row == 0, b31_hi, jnp.where(feat_row == 1, b31 - b31_hi, 0.0))
        qz_ref[h // 2, :, (h % 2) * TILE:(h % 2 + 1) * TILE] = jnp.concatenate(
            [qh, feat.astype(jnp.bfloat16)], axis=0)

    iw_t = iwt_ref[0]

    def score_chunk(kc):
        row0 = pl.multiple_of(kc * TILE, TILE)
        ik_t = ik_ref[0, pl.ds(row0, TILE), :]
        tot = jnp.zeros((TILE, TILE), jnp.float32)
        for h in range(IDX_HEADS):
            x = _dot(ik_t, iqt_ref[0, h * IDX_DIM:(h + 1) * IDX_DIM, :])
            tot = tot + iw_t[h:h + 1, :] * jnp.maximum(x, 0.0)
        return tot

    def score_body(kc, carry):
        mx, mn = carry
        tot = score_chunk(kc)
        sc_ref[kc] = tot
        return (jnp.maximum(mx, jnp.max(tot, axis=0, keepdims=True)),
                jnp.minimum(mn, jnp.min(tot, axis=0, keepdims=True)))

    row_max, row_min = lax.fori_loop(0, qi, score_body, (jnp.full((1, TILE), -jnp.inf, jnp.float32),
                                                         jnp.full((1, TILE), jnp.inf, jnp.float32)))
    tot = score_chunk(qi)
    sc_ref[qi] = jnp.where(causal, tot, -jnp.inf)
    row_max = jnp.maximum(row_max, jnp.max(jnp.where(causal, tot, -jnp.inf), axis=0, keepdims=True))
    row_min = jnp.minimum(row_min, jnp.min(jnp.where(causal, tot, jnp.inf), axis=0, keepdims=True))

    @pl.when(qi % 2 == 0)
    def _():
        sc_ref[qi + 1] = jnp.full((TILE, TILE), -jnp.inf, jnp.float32)

    def reduce_pass(fn, init, combine):
        def body(j, part):
            return combine(part, combine(fn(sc_ref[2 * j]), fn(sc_ref[2 * j + 1])))
        return lax.fori_loop(0, (n_chunks + 1) // 2, body, jnp.full((1, TILE), init, jnp.float32))

    def count(pred):
        return reduce_pass(lambda x: jnp.sum(jnp.where(pred(x), 1.0, 0.0), axis=0, keepdims=True), 0.0, jnp.add)

    def min_ge(p):
        return reduce_pass(lambda x: jnp.min(jnp.where(x >= p, x, jnp.inf), axis=0, keepdims=True),
                           jnp.inf, jnp.minimum)

    n_adm =(qi * TILE + 1 + lax.broadcasted_iota(jnp.int32, (1, TILE), 1)).astype(jnp.float32)
    k_top = float(DSA_TOPK)
    take_all = n_adm <= k_top

    def search_cond(st):
        return jnp.logical_and(st[0] < MAX_SEARCH_ITERS, jnp.min(st[6]) < 0.5)

    def tie_check(st):
        it, lo, hi, hi_fin, c_lo, c_hi, done, last, streak, stall = st
        a = min_ge(lo)
        c_gt = count(lambda x: x > a)
        tie = (c_gt < k_top) & (done < 0.5)
        return (it, lo, hi, hi_fin, c_lo, jnp.where(tie, c_gt, c_hi), jnp.where(tie, 1.0, done), last, streak,
                jnp.zeros_like(stall))

    def search_body(st):
        it, lo, hi, hi_fin, c_lo, c_hi, done, last, streak, stall = st
        width = c_lo - c_hi
        target = k_top - 0.5 + jnp.where(streak >= 1.0, last * 0.25 * (c_lo - c_hi), 0.0)
        target = jnp.clip(target, c_hi + 0.5, c_lo - 0.5)
        frac = (c_lo - target) / jnp.maximum(c_lo - c_hi, 1.0)
        mid = 0.5 * lo + 0.5 * hi_fin
        p = jnp.where(it % 5 == 4, mid, lo + frac * (hi_fin - lo))
        p = jnp.where((p > lo) & (p < hi), p, mid)
        stuck = jnp.logical_not((mid > lo) & (mid < hi))
        c = count(lambda x: x >= p)
        up = c >= k_top
        move = jnp.logical_not((done > 0.5) | stuck)
        rise = move & up
        fall = move & jnp.logical_not(up)
        lo = jnp.where(rise, p, lo)
        c_lo = jnp.where(rise, c, c_lo)
        hi = jnp.where(fall, p, hi)
        c_hi = jnp.where(fall, c, c_hi)
        hi_fin = jnp.where(fall, p, hi_fin)
        side = jnp.where(up, -1.0, 1.0)
        streak = jnp.where(side == last, streak + 1.0, 0.0)
        done = jnp.where((c_lo - c_hi <= 1.0) | (c_lo == k_top) | stuck, 1.0, done)
        stall = jnp.where(c_lo - c_hi == width, stall + 1.0, 0.0)
        st = (it + 1, lo, hi, hi_fin, c_lo, c_hi, done, side, streak, stall)
        check_now = jnp.max(jnp.where(done < 0.5, stall, 0.0)) >= 3.0
        return lax.cond(check_now, tie_check, lambda s: s, st)

    zeros_row = jnp.zeros((1, TILE), jnp.float32)
    init = (jnp.int32(0), row_min, jnp.full((1, TILE), jnp.inf, jnp.float32), row_max,
            n_adm, zeros_row, take_all.astype(jnp.float32), zeros_row, zeros_row, zeros_row)
    _, lo, _, _, _, c_hi, _, _, _, _ = lax.while_loop(search_cond, search_body, init)
    thr = jnp.where(take_all, -jnp.inf, min_ge(lo))
    need = k_top - jnp.where(take_all, 0.0, c_hi)

    n_pairs = DSA_HEADS // 2
    for pr in range(n_pairs):
        _init_softmax_state(m_ref.at[pr], alpha_ref.at[:, pr], p_ref.at[:, pr], acc_ref.at[pr])
    seen_ref[...] = jnp.zeros(seen_ref.shape, jnp.float32)
    lower = (qry < key).astype(jnp.bfloat16)
    ones_feat = (lax.broadcasted_iota(jnp.int32, (TILE, LANES), 1) < 2).astype(jnp.bfloat16)

    def logits(slot, kc):
        row0 = pl.multiple_of(kc * TILE, TILE)
        for pr in range(n_pairs):
            k_t = k_ref[0, pl.ds(row0, TILE), pr * LANES:(pr + 1) * LANES]
            s_ref[slot, pr] = _dot(jnp.concatenate([k_t, ones_feat], axis=1), qz_ref[pr])

    def softmax(slot, kc, kind):
        x = sc_ref[kc]
        tie = x == thr
        tie_f = jnp.where(tie, 1.0, 0.0)
        seen = seen_ref[:1]
        rank = seen + _dot(lower, tie_f.astype(jnp.bfloat16))
        selected = (x > thr) | (tie & (rank < need))
        if kind == _KIND_DIAG:
            selected = selected & causal
        mask_add = jnp.where(selected, 0.0, NEG_BIG)
        mask_add = jnp.concatenate([mask_add, mask_add], axis=1)
        seen_ref[...] = jnp.broadcast_to(seen + jnp.sum(tie_f, axis=0, keepdims=True), seen_ref.shape)
        for pr in range(n_pairs):
            s_t = s_ref[slot, pr] + mask_add
            if kind != _KIND_FAR:
                s_t = s_t + tab_ref[pr, kind]
            _softmax_stage(s_t, m_ref.at[pr], alpha_ref.at[slot, pr], p_ref.at[slot, pr])

    def accumulate(slot, kc):
        for pr in range(n_pairs):
            _accumulate_stage(vt_ref[0, kc, pr * LANES:(pr + 1) * LANES, :], alpha_ref.at[slot, pr],
                              p_ref.at[slot, pr], acc_ref.at[pr])

    _run_tile_pipeline(qi, logits, softmax, accumulate)
    for pr in range(n_pairs):
        o_ref[0, :, pr * LANES:(pr + 1) * LANES] = _normalized_pair(acc_ref[pr], TILE).astype(o_ref.dtype)


def _dsa_call(bias_flat, q_t, k, v_t, iq_t, ik, iw_t, bkt):
    b, _, t = q_t.shape
    n_chunks = t // TILE
    n_pairs = DSA_HEADS // 2
    grid_spec = pltpu.PrefetchScalarGridSpec(
        num_scalar_prefetch=1,
        grid=(b, n_chunks),
        in_specs=[
            pl.BlockSpec((1, DSA_W, TILE), lambda bi, qi, s: (bi, 0, qi)),
            pl.BlockSpec((1, t, DSA_W), lambda bi, qi, s: (bi, 0, 0)),
            pl.BlockSpec((1, n_chunks, DSA_W, TILE), lambda bi, qi, s: (bi, 0, 0, 0)),
            pl.BlockSpec((1, IDX_W, TILE), lambda bi, qi, s: (bi, 0, qi)),
            pl.BlockSpec((1, t, IDX_DIM), lambda bi, qi, s: (bi, 0, 0)),
            pl.BlockSpec((1, IDX_HEADS, TILE), lambda bi, qi, s: (bi, 0, qi)),
            pl.BlockSpec((2, TILE, TILE), lambda bi, qi, s: (0, 0, 0)),
        ],
        out_specs=pl.BlockSpec((1, TILE, DSA_W), lambda bi, qi, s: (bi, qi, 0)),
        scratch_shapes=[
            pltpu.VMEM((n_chunks, TILE, TILE), jnp.float32),
            pltpu.VMEM((n_pairs, 2, TILE, 2 * TILE), jnp.float32),
            pltpu.VMEM((n_pairs, 2 * LANES, 2 * TILE), jnp.bfloat16),
            pltpu.VMEM((2, n_pairs, TILE, 2 * TILE), jnp.float32),
            pltpu.VMEM((2, n_pairs, TILE, 2 * TILE), jnp.bfloat16),
            pltpu.VMEM((2, n_pairs, SUBLANES, 2 * TILE), jnp.float32),
            pltpu.VMEM((SUBLANES, TILE), jnp.float32),
            pltpu.VMEM((n_pairs, SUBLANES, 2 * TILE), jnp.float32),
            pltpu.VMEM((n_pairs, ACC_ROWS, 2 * TILE), jnp.float32),
        ],
    )
    return pl.pallas_call(
        _dsa_kernel,
        grid_spec=grid_spec,
        out_shape=jax.ShapeDtypeStruct((b, t, DSA_W), jnp.bfloat16),
        compiler_params=pltpu.CompilerParams(dimension_semantics=("arbitrary",) * 2, vmem_limit_bytes=VMEM_LIMIT),
        name="dsa",
    )(bias_flat, q_t, k, v_t, iq_t, ik, iw_t, bkt)


def _lane_half_mask(shape, half):
    lane = lax.broadcasted_iota(jnp.int32, shape, 1)
    return (lane >= half * HEAD_DIM) & (lane < (half + 1) * HEAD_DIM)


def _final_kernel(x_ref, g_ref, fg_ref, ya_ref, za_ref, yb_ref, zb_ref, qm_ref, zm_ref, mk_ref, mv_ref,
                  wg_ref, wb_ref, wo_ref, o_ref):
    x = x_ref[...]
    h = _rmsnorm_rows(x, g_ref[...]).astype(jnp.bfloat16)

    ym_parts = []
    for pr in range(MEM_HEADS // 2):
        qp = qm_ref[:, pr * LANES:(pr + 1) * LANES]
        mk = mk_ref[0, :, pr * LANES:(pr + 1) * LANES]
        mv = mv_ref[0, :, pr * LANES:(pr + 1) * LANES]
        outs = []
        for half in range(2):
            qh = jnp.where(_lane_half_mask((ROW_TILE, LANES), half), qp, jnp.zeros_like(qp))
            s = _dot_nt(qh, mk)
            p = jnp.exp2(s - jnp.max(s, axis=1, keepdims=True))
            outs.append(_dot(p.astype(jnp.bfloat16), mv) / jnp.sum(p, axis=1, keepdims=True))
        ym_parts.append(jnp.where(_lane_half_mask((ROW_TILE, LANES), 0), outs[0], outs[1]))
    ym = jnp.concatenate(ym_parts, axis=1)

    def gated(y, z_ref):
        z = z_ref[...].astype(jnp.float32)
        return (y.astype(jnp.float32) * (z * jax.nn.sigmoid(z))).astype(jnp.bfloat16)

    ua = gated(ya_ref[...], za_ref)
    ub = gated(yb_ref[...], zb_ref)
    um = gated(ym, zm_ref)
    merged = jax.nn.sigmoid(_dot(h, wg_ref[:, :D_MODEL])) * _dot(ua, wb_ref[:MOBA_W])
    merged += jax.nn.sigmoid(_dot(h, wg_ref[:, D_MODEL:2 * D_MODEL])) * _dot(ub, wb_ref[MOBA_W:MOBA_W + DSA_W])
    merged += jax.nn.sigmoid(_dot(h, wg_ref[:, 2 * D_MODEL:])) * _dot(um, wb_ref[MOBA_W + DSA_W:])
    y = x + _dot(merged.astype(jnp.bfloat16), wo_ref[...])
    o_ref[...] = _rmsnorm_rows(y, fg_ref[...])


def _final_call(x2, gain, final_gain, ya, za, yb, zb, qm, zm, mk, mv, w_gates, w_branch, w_out, rows_per_batch):
    rows = x2.shape[0]
    tiles_per_batch = rows_per_batch // ROW_TILE
    row_spec = lambda width: pl.BlockSpec((ROW_TILE, width), lambda i: (i, 0))
    const_spec = lambda shape: pl.BlockSpec(shape, lambda i: (0,) * len(shape))
    mem_spec = pl.BlockSpec((1,) + mk.shape[1:], lambda i: (i // tiles_per_batch, 0, 0))
    return pl.pallas_call(
        _final_kernel,
        grid=(rows // ROW_TILE,),
        in_specs=[row_spec(D_MODEL), const_spec((1, D_MODEL)), const_spec((1, D_MODEL)),
                  row_spec(MOBA_W), row_spec(MOBA_W), row_spec(DSA_W), row_spec(DSA_W),
                  row_spec(MEM_W), row_spec(MEM_W), mem_spec, mem_spec,
                  const_spec(w_gates.shape), const_spec(w_branch.shape), const_spec(w_out.shape)],
        out_specs=row_spec(D_MODEL),
        out_shape=jax.ShapeDtypeStruct((rows, D_MODEL), jnp.float32),
        compiler_params=pltpu.CompilerParams(dimension_semantics=("arbitrary",), vmem_limit_bytes=VMEM_LIMIT),
        name="final",
    )(x2, gain, final_gain, ya, za, yb, zb, qm, zm, mk, mv, w_gates, w_branch, w_out)


def _layer(x, mem, norm_gain, w_in, rel_bias, mem_norm_gain, w_mem_kv, w_branch, w_out, final_gain):
    b, t, _ = x.shape
    assert t % (2 * TILE) == 0 and t // MOBA_BLOCK <= FEAT_LO and t >= 4 * DSA_TOPK
    bf = jnp.bfloat16
    names = ("qa", "ka", "va", "za", "qb", "kb", "vb", "zb", "iq", "ik", "iw", "qm", "zm", "ga", "gb", "gm")
    cols, off = {}, 0
    for name, size in zip(names, IN_SIZES):
        cols[name] = w_in[:, off:off + size]
        off += size
    w_std = jnp.concatenate([cols[n] for n, _, _ in _STD_SEGS], axis=1).astype(bf)
    w_t = jnp.concatenate([cols[n[:-1]] for n, _, _ in _T_SEGS], axis=1).T.astype(bf)
    w_gates = jnp.concatenate([cols["ga"], cols["gb"], cols["gm"]], axis=1).astype(bf)

    x2 = x.reshape(b * t, D_MODEL)
    (ka, za, kb, zb, qm, zm, ik, qa_t, va_t, qb_t, vb_t, iq_t, iw_t, kmean) = _proj_call(
        x2, norm_gain.reshape(1, D_MODEL), w_std, w_t, b, t)
    seq = lambda a: a.reshape(b, t, a.shape[-1])
    n_blocks = t // MOBA_BLOCK
    kmean = jnp.pad(kmean.reshape(b, n_blocks, MOBA_W), ((0, 0), (0, LANES - n_blocks), (0, 0)))

    bias_flat = (rel_bias * LOG2E).reshape(-1)
    ya = _moba_call(bias_flat, qa_t, seq(ka), va_t, kmean, _bucket_tiles(True))
    yb = _dsa_call(bias_flat, qb_t, seq(kb), vb_t, iq_t, seq(ik), iw_t, _bucket_tiles(False))
    mk, mv = _memkv_call(mem, mem_norm_gain.reshape(1, D_MODEL), w_mem_kv.astype(bf))
    out = _final_call(x2, norm_gain.reshape(1, D_MODEL), final_gain.reshape(1, D_MODEL),
                      ya.reshape(b * t, MOBA_W), za, yb.reshape(b * t, DSA_W), zb, qm, zm, mk, mv,
                      w_gates, w_branch.astype(bf), w_out.astype(bf), t)
    return out.reshape(b, t, D_MODEL)


@jax.jit
def kernel(x, mem, norm_gain, w_in, rel_bias, mem_norm_gain, w_mem_kv, w_branch, w_out, final_norm_gain):
    assert norm_gain.shape[0] == 1, "one layer"
    return _layer(x, mem, norm_gain[0], w_in[0], rel_bias, mem_norm_gain[0], w_mem_kv[0], w_branch[0],
                  w_out[0], final_norm_gain)
```

```python
import math

import numpy as np
import jax
import jax.numpy as jnp
from jax import lax
from jax.experimental import pallas as pl
from jax.experimental.pallas import tpu as pltpu

D_MODEL = 1024
HEAD_DIM = 64
MOBA_HEADS = 6
DSA_HEADS = 6
MEM_HEADS = 4
MOBA_W = MOBA_HEADS * HEAD_DIM
DSA_W = DSA_HEADS * HEAD_DIM
MEM_W = MEM_HEADS * HEAD_DIM
IDX_HEADS = 8
IDX_DIM = 64
IDX_W = IDX_HEADS * IDX_DIM
MOBA_BLOCK = 256
MOBA_TOPK = 3
DSA_TOPK = 256
REL_BUCKETS = 32
REL_MAX_DIST = 128
N_BIAS_HEADS = MOBA_HEADS + DSA_HEADS
EPS = 1e-6
IN_SIZES = (MOBA_W,) * 4 + (DSA_W,) * 4 + (IDX_W, IDX_DIM, IDX_HEADS) + (MEM_W, MEM_W) + (D_MODEL,) * 3

LANES = 128
SUBLANES = 8
ONES_ROWS = 16
ACC_ROWS = LANES + ONES_ROWS
TILE = 256
ROW_TILE = 512
FEAT_LO = 32
LOG2E = math.log2(math.e)
NEG_BIG = -1e30
MASK_BUCKET = REL_BUCKETS
VMEM_LIMIT = 56 * 1024 * 1024
MAX_SEARCH_ITERS = 96

_NT = (((1,), (1,)), ((), ()))


def _dot_nt(a, b):
    return lax.dot_general(a, b, _NT, preferred_element_type=jnp.float32)


def _dot(a, b):
    return jnp.dot(a, b, preferred_element_type=jnp.float32)


def _rel_bucket_np(n):
    exact = REL_BUCKETS // 2
    nf = np.maximum(n, 1).astype(np.float32)
    large = exact + (np.log(nf / np.float32(exact)) / np.float32(math.log(REL_MAX_DIST / exact))
                     * np.float32(REL_BUCKETS - exact)).astype(np.int32)
    return np.where(n < exact, n, np.minimum(large, REL_BUCKETS - 1))


def _bucket_tiles(mask_future):
    j = np.arange(TILE)[:, None]
    i = np.arange(TILE)[None, :]
    d0 = i - j
    b0 = _rel_bucket_np(np.maximum(d0, 0))
    if mask_future:
        b0 = np.where(d0 >= 0, b0, MASK_BUCKET)
    b1 = _rel_bucket_np(TILE + d0)
    assert int(_rel_bucket_np(np.array([TILE + 1]))[0]) == REL_BUCKETS - 1
    return jnp.asarray(np.stack([b0, b1]).astype(np.int32))


def _build_bias_tile(bkt, bias_ref, head):
    acc = jnp.where(bkt == MASK_BUCKET, NEG_BIG, 0.0)
    for b in range(REL_BUCKETS):
        acc = jnp.where(bkt == b, bias_ref[b * N_BIAS_HEADS + head], acc)
    return acc


def _rmsnorm_rows(x, g):
    xf = x.astype(jnp.float32)
    return xf * lax.rsqrt(jnp.mean(xf * xf, axis=-1, keepdims=True) + EPS) * g


_Q_SCALE = HEAD_DIM ** -0.5 * LOG2E
_STD_SEGS = (("ka", MOBA_W, 1.0), ("za", MOBA_W, 1.0), ("kb", DSA_W, 1.0), ("zb", DSA_W, 1.0),
             ("qm", MEM_W, _Q_SCALE), ("zm", MEM_W, 1.0), ("ik", IDX_DIM, 1.0))
_T_SEGS = (("qaT", MOBA_W, _Q_SCALE), ("vaT", MOBA_W, 1.0), ("qbT", DSA_W, _Q_SCALE), ("vbT", DSA_W, 1.0),
           ("iqT", IDX_W, IDX_DIM ** -0.5), ("iwT", IDX_HEADS, IDX_HEADS ** -0.5))
_STD_W = sum(w for _, w, _ in _STD_SEGS)
_T_W = sum(w for _, w, _ in _T_SEGS)
_CHUNKS_PER_ROW_TILE = ROW_TILE // TILE


def _proj_kernel(x_ref, g_ref, w_ref, wt_ref, *out_refs):
    std_refs = out_refs[:len(_STD_SEGS)]
    t_refs = out_refs[len(_STD_SEGS):len(_STD_SEGS) + len(_T_SEGS)]
    kmean_ref = out_refs[-1]
    h = _rmsnorm_rows(x_ref[...], g_ref[...]).astype(jnp.bfloat16)
    off = 0
    for (name, width, scale), o_ref in zip(_STD_SEGS, std_refs):
        r = _dot(h, w_ref[:, off:off + width])
        if name == "ka":
            for blk in range(ROW_TILE // MOBA_BLOCK):
                kmean_ref[0, blk:blk + 1, :] = jnp.mean(
                    r[blk * MOBA_BLOCK:(blk + 1) * MOBA_BLOCK], axis=0, keepdims=True)
        if scale != 1.0:
            r = r * scale
        o_ref[...] = r.astype(o_ref.dtype)
        off += width
    off = 0
    for (name, width, scale), o_ref in zip(_T_SEGS, t_refs):
        r = _dot_nt(wt_ref[off:off + width, :], h)
        if scale != 1.0:
            r = r * scale
        r = r.astype(o_ref.dtype)
        if name in ("vaT", "vbT"):
            for c in range(_CHUNKS_PER_ROW_TILE):
                o_ref[0, c] = r[:, c * TILE:(c + 1) * TILE]
        else:
            o_ref[0] = r
        off += width


def _proj_call(x2, gain, w_std, w_t, b, t):
    rows = x2.shape[0]
    tiles_per_batch = t // ROW_TILE
    n_tiles = rows // ROW_TILE
    row_spec = lambda width: pl.BlockSpec((ROW_TILE, width), lambda i: (i, 0))
    const_spec = lambda shape: pl.BlockSpec(shape, lambda i: (0,) * len(shape))
    t_spec = lambda width: pl.BlockSpec((1, width, ROW_TILE),
                                        lambda i: (i // tiles_per_batch, 0, i % tiles_per_batch))
    chunk_spec = lambda width: pl.BlockSpec((1, _CHUNKS_PER_ROW_TILE, width, TILE),
                                            lambda i: (i // tiles_per_batch, i % tiles_per_batch, 0, 0))
    out_shape = [jax.ShapeDtypeStruct((rows, w), jnp.bfloat16) for _, w, _ in _STD_SEGS]
    out_specs = [row_spec(w) for _, w, _ in _STD_SEGS]
    for name, w, _ in _T_SEGS:
        if name in ("vaT", "vbT"):
            out_shape.append(jax.ShapeDtypeStruct((b, t // TILE, w, TILE), jnp.bfloat16))
            out_specs.append(chunk_spec(w))
        else:
            out_shape.append(jax.ShapeDtypeStruct((b, w, t), jnp.float32 if name == "iwT" else jnp.bfloat16))
            out_specs.append(t_spec(w))
    out_shape.append(jax.ShapeDtypeStruct((n_tiles, ROW_TILE // MOBA_BLOCK, MOBA_W), jnp.float32))
    out_specs.append(pl.BlockSpec((1, ROW_TILE // MOBA_BLOCK, MOBA_W), lambda i: (i, 0, 0)))
    return pl.pallas_call(
        _proj_kernel,
        grid=(n_tiles,),
        in_specs=[row_spec(D_MODEL), const_spec((1, D_MODEL)), const_spec((D_MODEL, _STD_W)),
                  const_spec((_T_W, D_MODEL))],
        out_specs=out_specs,
        out_shape=out_shape,
        compiler_params=pltpu.CompilerParams(dimension_semantics=("arbitrary",), vmem_limit_bytes=VMEM_LIMIT),
        name="proj",
    )(x2, gain, w_std, w_t)


def _memkv_kernel(mem_ref, g_ref, w_ref, mk_ref, mv_ref):
    h = _rmsnorm_rows(mem_ref[0], g_ref[...]).astype(jnp.bfloat16)
    r = _dot(h, w_ref[...])
    mk_ref[0] = r[:, :MEM_W].astype(jnp.bfloat16)
    mv_ref[0] = r[:, MEM_W:].astype(jnp.bfloat16)


def _memkv_call(mem, gain, w_kv):
    b, n_mem, _ = mem.shape
    return pl.pallas_call(
        _memkv_kernel,
        grid=(b,),
        in_specs=[pl.BlockSpec((1, n_mem, D_MODEL), lambda i: (i, 0, 0)),
                  pl.BlockSpec((1, D_MODEL), lambda i: (0, 0)),
                  pl.BlockSpec((D_MODEL, 2 * MEM_W), lambda i: (0, 0))],
        out_specs=[pl.BlockSpec((1, n_mem, MEM_W), lambda i: (i, 0, 0))] * 2,
        out_shape=[jax.ShapeDtypeStruct((b, n_mem, MEM_W), jnp.bfloat16)] * 2,
        compiler_params=pltpu.CompilerParams(dimension_semantics=("arbitrary",), vmem_limit_bytes=VMEM_LIMIT),
        name="memkv",
    )(mem, gain, w_kv)


def _row_half_mask(shape, half):
    r = lax.broadcasted_iota(jnp.int32, shape, 0)
    return (r >= half * HEAD_DIM) & (r < (half + 1) * HEAD_DIM)


def _with_ones_rows(v_t):
    return jnp.concatenate([v_t, jnp.ones((ONES_ROWS, v_t.shape[1]), v_t.dtype)], axis=0)


def _softmax_stage(s_t, m_ref, alpha_ref, p_ref):
    m_old = m_ref[:1]
    m_new = jnp.maximum(m_old, jnp.max(s_t, axis=0, keepdims=True))
    alpha_ref[...] = jnp.broadcast_to(jnp.exp2(m_old - m_new), alpha_ref.shape)
    m_ref[...] = jnp.broadcast_to(m_new, m_ref.shape)
    p_ref[...] = jnp.exp2(s_t - m_new).astype(jnp.bfloat16)


def _accumulate_stage(v_t, alpha_ref, p_ref, acc_ref):
    acc_ref[...] = alpha_ref[:1] * acc_ref[...] + _dot(_with_ones_rows(v_t), p_ref[...])


def _init_softmax_state(m_ref, alpha_ref, p_ref, acc_ref):
    m_ref[...] = jnp.full(m_ref.shape, NEG_BIG, jnp.float32)
    acc_ref[...] = jnp.zeros(acc_ref.shape, jnp.float32)
    alpha_ref[...] = jnp.ones(alpha_ref.shape, jnp.float32)
    p_ref[...] = jnp.zeros(p_ref.shape, p_ref.dtype)


def _normalized_pair(acc, half_cols):
    out0 = acc[:LANES, :half_cols] / acc[LANES:LANES + 1, :half_cols]
    out1 = acc[:LANES, half_cols:] / acc[LANES:LANES + 1, half_cols:]
    return jnp.where(_row_half_mask((LANES, half_cols), 0), out0, out1).T


_KIND_DIAG, _KIND_PREV, _KIND_FAR = 0, 1, 2


def _run_tile_pipeline(qi, logits, softmax, accumulate):
    n_far = jnp.maximum(qi - 1, 0)
    n_loop = n_far // 2

    def far_pair(t):
        accumulate(0, jnp.maximum(t - 2, 0))
        accumulate(1, jnp.maximum(t - 1, 0))
        softmax(0, t, _KIND_FAR)
        logits(0, t + 2)
        softmax(1, t + 1, _KIND_FAR)
        logits(1, t + 3)

    def quad_body(j, carry):
        far_pair(4 * j)
        far_pair(4 * j + 2)
        return carry

    def pair_body(j, carry):
        far_pair(2 * j)
        return carry

    logits(0, 0)
    logits(1, jnp.minimum(1, qi))
    lax.fori_loop(0, n_loop // 2, quad_body, 0)
    lax.fori_loop(2 * (n_loop // 2), n_loop, pair_body, 0)
    t0 = 2 * n_loop
    odd = n_far % 2 == 1

    @pl.when(odd)
    def _():
        accumulate(0, jnp.maximum(t0 - 2, 0))
        accumulate(1, jnp.maximum(t0 - 1, 0))
        softmax(0, t0, _KIND_FAR)
        logits(0, qi)
        softmax(1, qi - 1, _KIND_PREV)
        accumulate(0, t0)
        softmax(0, qi, _KIND_DIAG)
        accumulate(1, qi - 1)
        accumulate(0, qi)

    @pl.when(jnp.logical_not(odd) & (qi >= 1))
    def _():
        accumulate(0, jnp.maximum(t0 - 2, 0))
        accumulate(1, jnp.maximum(t0 - 1, 0))
        softmax(0, qi - 1, _KIND_PREV)
        softmax(1, qi, _KIND_DIAG)
        accumulate(0, qi - 1)
        accumulate(1, qi)

    @pl.when(qi == 0)
    def _():
        softmax(0, 0, _KIND_DIAG)
        accumulate(0, 0)


def _moba_kernel(bias_ref, qt_ref, k_ref, vt_ref, kmean_ref, bkt_ref, o_ref,
                 qa_ref, tab_ref, s_ref, p_ref, alpha_ref, m_ref, acc_ref):
    pair = pl.program_id(1)
    qi = pl.program_id(2)
    q_t = qt_ref[0]
    kmean = kmean_ref[0].astype(jnp.bfloat16)
    blk = lax.broadcasted_iota(jnp.int32, (LANES, TILE), 0)
    blk_f = blk.astype(jnp.float32)
    past = blk < qi
    lane = lax.broadcasted_iota(jnp.int32, (TILE, LANES), 1)
    _init_softmax_state(m_ref, alpha_ref, p_ref, acc_ref)

    @pl.when(qi == 0)
    def _():
        for half in range(2):
            for kind in (_KIND_DIAG, _KIND_PREV):
                tab_ref[kind, :, half * TILE:(half + 1) * TILE] = _build_bias_tile(
                    bkt_ref[kind], bias_ref, pair * 2 + half)

    for half in range(2):
        qh = jnp.where(_row_half_mask((LANES, TILE), half), q_t, jnp.zeros_like(q_t))
        gate = _dot(kmean, qh)
        g = jnp.where(past, gate, -jnp.inf)
        sel = jnp.zeros((LANES, TILE), jnp.bool_)
        for _ in range(MOBA_TOPK):
            mx = jnp.max(g, axis=0, keepdims=True)
            first = jnp.min(jnp.where(g == mx, blk_f, float(LANES)), axis=0, keepdims=True)
            pick = blk_f == first
            sel = sel | (pick & past)
            g = jnp.where(pick, -jnp.inf, g)
        b31 = jnp.full((LANES, TILE), bias_ref[(REL_BUCKETS - 1) * N_BIAS_HEADS + pair * 2 + half], jnp.float32)
        b31_hi = b31.astype(jnp.bfloat16).astype(jnp.float32)
        hi_part = jnp.where(sel, jnp.where(blk < qi - 1, b31_hi, 0.0), NEG_BIG)
        hi_part = jnp.where(past, hi_part, 0.0)
        lo_part = jnp.where((blk >= FEAT_LO) & (blk - FEAT_LO < qi - 1), b31 - b31_hi, 0.0)
        feat = jnp.where(blk < FEAT_LO, hi_part, lo_part).astype(jnp.bfloat16)
        qa_ref[:, half * TILE:(half + 1) * TILE] = jnp.concatenate([qh, feat], axis=0)

    def logits(slot, kj):
        k_t = k_ref[0, pl.ds(pl.multiple_of(kj * TILE, TILE), TILE), :]
        onehot = ((lane == kj) | (lane == kj + FEAT_LO)).astype(jnp.bfloat16)
        s_ref[slot] = _dot(jnp.concatenate([k_t, onehot], axis=1), qa_ref[...])

    def softmax(slot, kj, kind):
        s_t = s_ref[slot]
        if kind != _KIND_FAR:
            s_t = s_t + tab_ref[kind]
        _softmax_stage(s_t, m_ref, alpha_ref.at[slot], p_ref.at[slot])

    def accumulate(slot, kj):
        _accumulate_stage(vt_ref[0, kj], alpha_ref.at[slot], p_ref.at[slot], acc_ref)

    _run_tile_pipeline(qi, logits, softmax, accumulate)
    o_ref[0] = _normalized_pair(acc_ref[...], TILE).astype(o_ref.dtype)


def _moba_call(bias_flat, q_t, k, v_t, kmean, bkt):
    b, _, t = q_t.shape
    n_pairs = MOBA_HEADS // 2
    n_chunks = t // TILE
    grid_spec = pltpu.PrefetchScalarGridSpec(
        num_scalar_prefetch=1,
        grid=(b, n_pairs, n_chunks),
        in_specs=[
            pl.BlockSpec((1, LANES, TILE), lambda bi, p, qi, s: (bi, p, qi)),
            pl.BlockSpec((1, t, LANES), lambda bi, p, qi, s: (bi, 0, p)),
            pl.BlockSpec((1, n_chunks, LANES, TILE), lambda bi, p, qi, s: (bi, 0, p, 0)),
            pl.BlockSpec((1, LANES, LANES), lambda bi, p, qi, s: (bi, 0, p)),
            pl.BlockSpec((2, TILE, TILE), lambda bi, p, qi, s: (0, 0, 0)),
        ],
        out_specs=pl.BlockSpec((1, TILE, LANES), lambda bi, p, qi, s: (bi, qi, p)),
        scratch_shapes=[
            pltpu.VMEM((2 * LANES, 2 * TILE), jnp.bfloat16),
            pltpu.VMEM((2, TILE, 2 * TILE), jnp.float32),
            pltpu.VMEM((2, TILE, 2 * TILE), jnp.float32),
            pltpu.VMEM((2, TILE, 2 * TILE), jnp.bfloat16),
            pltpu.VMEM((2, SUBLANES, 2 * TILE), jnp.float32),
            pltpu.VMEM((SUBLANES, 2 * TILE), jnp.float32),
            pltpu.VMEM((ACC_ROWS, 2 * TILE), jnp.float32),
        ],
    )
    return pl.pallas_call(
        _moba_kernel,
        grid_spec=grid_spec,
        out_shape=jax.ShapeDtypeStruct((b, t, MOBA_W), jnp.bfloat16),
        compiler_params=pltpu.CompilerParams(dimension_semantics=("arbitrary",) * 3, vmem_limit_bytes=VMEM_LIMIT),
        name="moba",
    )(bias_flat, q_t, k, v_t, kmean, bkt)


def _dsa_kernel(bias_ref, qt_ref, k_ref, vt_ref, iqt_ref, ik_ref, iwt_ref, bkt_ref, o_ref,
                sc_ref, tab_ref, qz_ref, s_ref, p_ref, alpha_ref, seen_ref, m_ref, acc_ref):
    qi = pl.program_id(1)
    n_chunks = qi + 1
    key = lax.broadcasted_iota(jnp.int32, (TILE, TILE), 0)
    qry = lax.broadcasted_iota(jnp.int32, (TILE, TILE), 1)
    causal = key <= qry

    far_bias = [bias_ref[(REL_BUCKETS - 1) * N_BIAS_HEADS + MOBA_HEADS + h] for h in range(DSA_HEADS)]

    @pl.when(qi == 0)
    def _():
        for h in range(DSA_HEADS):
            for kind in (_KIND_DIAG, _KIND_PREV):
                tab_ref[h // 2, kind, :, (h % 2) * TILE:(h % 2 + 1) * TILE] = _build_bias_tile(
                    bkt_ref[kind], bias_ref, MOBA_HEADS + h) - far_bias[h]

    feat_row = lax.broadcasted_iota(jnp.int32, (LANES, TILE), 0)
    for h in range(DSA_HEADS):
        blk = qt_ref[0, (h // 2) * LANES:(h // 2 + 1) * LANES, :]
        qh = jnp.where(_row_half_mask((LANES, TILE), h % 2), blk, jnp.zeros_like(blk))
        b31 = jnp.full((LANES, TILE), far_bias[h], jnp.float32)
        b31_hi = b31.astype(jnp.bfloat16).astype(jnp.float32)
        feat = jnp.where(feat_row == 0, b31_hi, jnp.where(feat_row == 1, b31 - b31_hi, 0.0))
        qz_ref[h // 2, :, (h % 2) * TILE:(h % 2 + 1) * TILE] = jnp.concatenate(
            [qh, feat.astype(jnp.bfloat16)], axis=0)

    iw_t = iwt_ref[0]

    def score_chunk(kc):
        row0 = pl.multiple_of(kc * TILE, TILE)
        ik_t = ik_ref[0, pl.ds(row0, TILE), :]
        tot = jnp.zeros((TILE, TILE), jnp.float32)
        for h in range(IDX_HEADS):
            x = _dot(ik_t, iqt_ref[0, h * IDX_DIM:(h + 1) * IDX_DIM, :])
            tot = tot + iw_t[h:h + 1, :] * jnp.maximum(x, 0.0)
        return tot

    def score_body(kc, carry):
        mx, mn = carry
        tot = score_chunk(kc)
        sc_ref[kc] = tot
        return (jnp.maximum(mx, jnp.max(tot, axis=0, keepdims=True)),
                jnp.minimum(mn, jnp.min(tot, axis=0, keepdims=True)))

    def score_pair_body(j, carry):
        return score_body(2 * j + 1, score_body(2 * j, carry))

    extremes = lax.fori_loop(0, qi // 2, score_pair_body, (jnp.full((1, TILE), -jnp.inf, jnp.float32),
                                                           jnp.full((1, TILE), jnp.inf, jnp.float32)))
    row_max, row_min = lax.cond(qi % 2 == 1, lambda c: score_body(qi - 1, c), lambda c: c, extremes)
    tot = score_chunk(qi)
    sc_ref[qi] = jnp.where(causal, tot, -jnp.inf)
    row_max = jnp.maximum(row_max, jnp.max(jnp.where(causal, tot, -jnp.inf), axis=0, keepdims=True))
    row_min = jnp.minimum(row_min, jnp.min(jnp.where(causal, tot, jnp.inf), axis=0, keepdims=True))

    @pl.when(qi % 2 == 0)
    def _():
        sc_ref[qi + 1] = jnp.full((TILE, TILE), -jnp.inf, jnp.float32)

    def reduce_pass(fn, init, combine):
        def body(j, part):
            return combine(part, combine(fn(sc_ref[2 * j]), fn(sc_ref[2 * j + 1])))
        return lax.fori_loop(0, (n_chunks + 1) // 2, body, jnp.full((1, TILE), init, jnp.float32))

    def count(pred):
        return reduce_pass(lambda x: jnp.sum(jnp.where(pred(x), 1.0, 0.0), axis=0, keepdims=True), 0.0, jnp.add)

    def min_ge(p):
        return reduce_pass(lambda x: jnp.min(jnp.where(x >= p, x, jnp.inf), axis=0, keepdims=True),
                           jnp.inf, jnp.minimum)

    n_adm =(qi * TILE + 1 + lax.broadcasted_iota(jnp.int32, (1, TILE), 1)).astype(jnp.float32)
    k_top = float(DSA_TOPK)
    take_all = n_adm <= k_top

    def search_cond(st):
        return jnp.logical_and(st[0] < MAX_SEARCH_ITERS, st[-1] >= 1.0)

    def live_flag(done, stall):
        return jnp.max(jnp.where(done < 0.5, jnp.where(stall >= 3.0, 3.0, 1.0), 0.0))

    def tie_check(st):
        it, lo, hi, hi_fin, c_lo, c_hi, done, last, streak, stall, _ = st
        a = min_ge(lo)
        c_gt = count(lambda x: x > a)
        tie = (c_gt < k_top) & (done < 0.5)
        done = jnp.where(tie, 1.0, done)
        stall = jnp.zeros_like(stall)
        return (it, lo, hi, hi_fin, c_lo, jnp.where(tie, c_gt, c_hi), done, last, streak, stall,
                live_flag(done, stall))

    def search_step(st):
        it, lo, hi, hi_fin, c_lo, c_hi, done, last, streak, stall = st
        width = c_lo - c_hi
        target = k_top - 0.5 + jnp.where(streak >= 1.0, last * 0.25 * (c_lo - c_hi), 0.0)
        target = jnp.clip(target, c_hi + 0.5, c_lo - 0.5)
        frac = (c_lo - target) / jnp.maximum(c_lo - c_hi, 1.0)
        mid = 0.5 * lo + 0.5 * hi_fin
        p = jnp.where(it % 5 == 4, mid, lo + frac * (hi_fin - lo))
        p = jnp.where((p > lo) & (p < hi), p, mid)
        stuck = jnp.logical_not((mid > lo) & (mid < hi))
        c = count(lambda x: x >= p)
        up = c >= k_top
        move = jnp.logical_not((done > 0.5) | stuck)
        rise = move & up
        fall = move & jnp.logical_not(up)
        lo = jnp.where(rise, p, lo)
        c_lo = jnp.where(rise, c, c_lo)
        hi = jnp.where(fall, p, hi)
        c_hi = jnp.where(fall, c, c_hi)
        hi_fin = jnp.where(fall, p, hi_fin)
        side = jnp.where(up, -1.0, 1.0)
        streak = jnp.where(side == last, streak + 1.0, 0.0)
        done = jnp.where((c_lo - c_hi <= 1.0) | (c_lo == k_top) | stuck, 1.0, done)
        stall = jnp.where(c_lo - c_hi == width, stall + 1.0, 0.0)
        return it + 1, lo, hi, hi_fin, c_lo, c_hi, done, side, streak, stall

    def search_body(st):
        st = search_step(search_step(st[:-1]))
        flag = live_flag(st[6], st[9])
        return lax.cond(flag >= 3.0, tie_check, lambda s: s, st + (flag,))

    zeros_row = jnp.zeros((1, TILE), jnp.float32)
    done0 = take_all.astype(jnp.float32)
    init = (jnp.int32(0), row_min, jnp.full((1, TILE), jnp.inf, jnp.float32), row_max,
            n_adm, zeros_row, done0, zeros_row, zeros_row, zeros_row, live_flag(done0, zeros_row))
    _, lo, _, _, _, c_hi, _, _, _, _, _ = lax.while_loop(search_cond, search_body, init)
    thr = jnp.where(take_all, -jnp.inf, min_ge(lo))
    need = k_top - jnp.where(take_all, 0.0, c_hi)

    n_pairs = DSA_HEADS // 2
    for pr in range(n_pairs):
        _init_softmax_state(m_ref.at[pr], alpha_ref.at[:, pr], p_ref.at[:, pr], acc_ref.at[pr])
    seen_ref[...] = jnp.zeros(seen_ref.shape, jnp.float32)
    lower = (qry < key).astype(jnp.bfloat16)
    ones_feat = (lax.broadcasted_iota(jnp.int32, (TILE, LANES), 1) < 2).astype(jnp.bfloat16)

    def logits(slot, kc):
        row0 = pl.multiple_of(kc * TILE, TILE)
        for pr in range(n_pairs):
            k_t = k_ref[0, pl.ds(row0, TILE), pr * LANES:(pr + 1) * LANES]
            s_ref[slot, pr] = _dot(jnp.concatenate([k_t, ones_feat], axis=1), qz_ref[pr])

    def softmax(slot, kc, kind):
        x = sc_ref[kc]
        tie = x == thr
        tie_f = jnp.where(tie, 1.0, 0.0)
        seen = seen_ref[:1]
        rank = seen + _dot(lower, tie_f.astype(jnp.bfloat16))
        selected = (x > thr) | (tie & (rank < need))
        if kind == _KIND_DIAG:
            selected = selected & causal
        mask_add = jnp.where(selected, 0.0, NEG_BIG)
        mask_add = jnp.concatenate([mask_add, mask_add], axis=1)
        seen_ref[...] = jnp.broadcast_to(seen + jnp.sum(tie_f, axis=0, keepdims=True), seen_ref.shape)
        for pr in range(n_pairs):
            s_t = s_ref[slot, pr] + mask_add
            if kind != _KIND_FAR:
                s_t = s_t + tab_ref[pr, kind]
            _softmax_stage(s_t, m_ref.at[pr], alpha_ref.at[slot, pr], p_ref.at[slot, pr])

    def accumulate(slot, kc):
        for pr in range(n_pairs):
            _accumulate_stage(vt_ref[0, kc, pr * LANES:(pr + 1) * LANES, :], alpha_ref.at[slot, pr],
                              p_ref.at[slot, pr], acc_ref.at[pr])

    _run_tile_pipeline(qi, logits, softmax, accumulate)
    for pr in range(n_pairs):
        o_ref[0, :, pr * LANES:(pr + 1) * LANES] = _normalized_pair(acc_ref[pr], TILE).astype(o_ref.dtype)


def _dsa_call(bias_flat, q_t, k, v_t, iq_t, ik, iw_t, bkt):
    b, _, t = q_t.shape
    n_chunks = t // TILE
    n_pairs = DSA_HEADS // 2
    grid_spec = pltpu.PrefetchScalarGridSpec(
        num_scalar_prefetch=1,
        grid=(b, n_chunks),
        in_specs=[
            pl.BlockSpec((1, DSA_W, TILE), lambda bi, qi, s: (bi, 0, qi)),
            pl.BlockSpec((1, t, DSA_W), lambda bi, qi, s: (bi, 0, 0)),
            pl.BlockSpec((1, n_chunks, DSA_W, TILE), lambda bi, qi, s: (bi, 0, 0, 0)),
            pl.BlockSpec((1, IDX_W, TILE), lambda bi, qi, s: (bi, 0, qi)),
            pl.BlockSpec((1, t, IDX_DIM), lambda bi, qi, s: (bi, 0, 0)),
            pl.BlockSpec((1, IDX_HEADS, TILE), lambda bi, qi, s: (bi, 0, qi)),
            pl.BlockSpec((2, TILE, TILE), lambda bi, qi, s: (0, 0, 0)),
        ],
        out_specs=pl.BlockSpec((1, TILE, DSA_W), lambda bi, qi, s: (bi, qi, 0)),
        scratch_shapes=[
            pltpu.VMEM((n_chunks, TILE, TILE), jnp.float32),
            pltpu.VMEM((n_pairs, 2, TILE, 2 * TILE), jnp.float32),
            pltpu.VMEM((n_pairs, 2 * LANES, 2 * TILE), jnp.bfloat16),
            pltpu.VMEM((2, n_pairs, TILE, 2 * TILE), jnp.float32),
            pltpu.VMEM((2, n_pairs, TILE, 2 * TILE), jnp.bfloat16),
            pltpu.VMEM((2, n_pairs, SUBLANES, 2 * TILE), jnp.float32),
            pltpu.VMEM((SUBLANES, TILE), jnp.float32),
            pltpu.VMEM((n_pairs, SUBLANES, 2 * TILE), jnp.float32),
            pltpu.VMEM((n_pairs, ACC_ROWS, 2 * TILE), jnp.float32),
        ],
    )
    return pl.pallas_call(
        _dsa_kernel,
        grid_spec=grid_spec,
        out_shape=jax.ShapeDtypeStruct((b, t, DSA_W), jnp.bfloat16),
        compiler_params=pltpu.CompilerParams(dimension_semantics=("arbitrary",) * 2, vmem_limit_bytes=VMEM_LIMIT),
        name="dsa",
    )(bias_flat, q_t, k, v_t, iq_t, ik, iw_t, bkt)


def _lane_half_mask(shape, half):
    lane = lax.broadcasted_iota(jnp.int32, shape, 1)
    return (lane >= half * HEAD_DIM) & (lane < (half + 1) * HEAD_DIM)


def _final_kernel(x_ref, g_ref, fg_ref, ya_ref, za_ref, yb_ref, zb_ref, qm_ref, zm_ref, mk_ref, mv_ref,
                  wg_ref, wb_ref, wo_ref, o_ref):
    x = x_ref[...]
    h = _rmsnorm_rows(x, g_ref[...]).astype(jnp.bfloat16)

    ym_parts = []
    for pr in range(MEM_HEADS // 2):
        qp = qm_ref[:, pr * LANES:(pr + 1) * LANES]
        mk = mk_ref[0, :, pr * LANES:(pr + 1) * LANES]
        mv = mv_ref[0, :, pr * LANES:(pr + 1) * LANES]
        outs = []
        for half in range(2):
            qh = jnp.where(_lane_half_mask((ROW_TILE, LANES), half), qp, jnp.zeros_like(qp))
            s = _dot_nt(qh, mk)
            p = jnp.exp2(s - jnp.max(s, axis=1, keepdims=True))
            outs.append(_dot(p.astype(jnp.bfloat16), mv) / jnp.sum(p, axis=1, keepdims=True))
        ym_parts.append(jnp.where(_lane_half_mask((ROW_TILE, LANES), 0), outs[0], outs[1]))
    ym = jnp.concatenate(ym_parts, axis=1)

    def gated(y, z_ref):
        z = z_ref[...].astype(jnp.float32)
        return (y.astype(jnp.float32) * (z * jax.nn.sigmoid(z))).astype(jnp.bfloat16)

    ua = gated(ya_ref[...], za_ref)
    ub = gated(yb_ref[...], zb_ref)
    um = gated(ym, zm_ref)
    merged = jax.nn.sigmoid(_dot(h, wg_ref[:, :D_MODEL])) * _dot(ua, wb_ref[:MOBA_W])
    merged += jax.nn.sigmoid(_dot(h, wg_ref[:, D_MODEL:2 * D_MODEL])) * _dot(ub, wb_ref[MOBA_W:MOBA_W + DSA_W])
    merged += jax.nn.sigmoid(_dot(h, wg_ref[:, 2 * D_MODEL:])) * _dot(um, wb_ref[MOBA_W + DSA_W:])
    y = x + _dot(merged.astype(jnp.bfloat16), wo_ref[...])
    o_ref[...] = _rmsnorm_rows(y, fg_ref[...])


def _final_call(x2, gain, final_gain, ya, za, yb, zb, qm, zm, mk, mv, w_gates, w_branch, w_out, rows_per_batch):
    rows = x2.shape[0]
    tiles_per_batch = rows_per_batch // ROW_TILE
    row_spec = lambda width: pl.BlockSpec((ROW_TILE, width), lambda i: (i, 0))
    const_spec = lambda shape: pl.BlockSpec(shape, lambda i: (0,) * len(shape))
    mem_spec = pl.BlockSpec((1,) + mk.shape[1:], lambda i: (i // tiles_per_batch, 0, 0))
    return pl.pallas_call(
        _final_kernel,
        grid=(rows // ROW_TILE,),
        in_specs=[row_spec(D_MODEL), const_spec((1, D_MODEL)), const_spec((1, D_MODEL)),
                  row_spec(MOBA_W), row_spec(MOBA_W), row_spec(DSA_W), row_spec(DSA_W),
                  row_spec(MEM_W), row_spec(MEM_W), mem_spec, mem_spec,
                  const_spec(w_gates.shape), const_spec(w_branch.shape), const_spec(w_out.shape)],
        out_specs=row_spec(D_MODEL),
        out_shape=jax.ShapeDtypeStruct((rows, D_MODEL), jnp.float32),
        compiler_params=pltpu.CompilerParams(dimension_semantics=("arbitrary",), vmem_limit_bytes=VMEM_LIMIT),
        name="final",
    )(x2, gain, final_gain, ya, za, yb, zb, qm, zm, mk, mv, w_gates, w_branch, w_out)


def _layer(x, mem, norm_gain, w_in, rel_bias, mem_norm_gain, w_mem_kv, w_branch, w_out, final_gain):
    b, t, _ = x.shape
    assert t % (2 * TILE) == 0 and t // MOBA_BLOCK <= FEAT_LO and t >= 4 * DSA_TOPK
    bf = jnp.bfloat16
    names = ("qa", "ka", "va", "za", "qb", "kb", "vb", "zb", "iq", "ik", "iw", "qm", "zm", "ga", "gb", "gm")
    cols, off = {}, 0
    for name, size in zip(names, IN_SIZES):
        cols[name] = w_in[:, off:off + size]
        off += size
    w_std = jnp.concatenate([cols[n] for n, _, _ in _STD_SEGS], axis=1).astype(bf)
    w_t = jnp.concatenate([cols[n[:-1]] for n, _, _ in _T_SEGS], axis=1).T.astype(bf)
    w_gates = jnp.concatenate([cols["ga"], cols["gb"], cols["gm"]], axis=1).astype(bf)

    x2 = x.reshape(b * t, D_MODEL)
    (ka, za, kb, zb, qm, zm, ik, qa_t, va_t, qb_t, vb_t, iq_t, iw_t, kmean) = _proj_call(
        x2, norm_gain.reshape(1, D_MODEL), w_std, w_t, b, t)
    seq = lambda a: a.reshape(b, t, a.shape[-1])
    n_blocks = t // MOBA_BLOCK
    kmean = jnp.pad(kmean.reshape(b, n_blocks, MOBA_W), ((0, 0), (0, LANES - n_blocks), (0, 0)))

    bias_flat = (rel_bias * LOG2E).reshape(-1)
    ya = _moba_call(bias_flat, qa_t, seq(ka), va_t, kmean, _bucket_tiles(True))
    yb = _dsa_call(bias_flat, qb_t, seq(kb), vb_t, iq_t, seq(ik), iw_t, _bucket_tiles(False))
    mk, mv = _memkv_call(mem, mem_norm_gain.reshape(1, D_MODEL), w_mem_kv.astype(bf))
    out = _final_call(x2, norm_gain.reshape(1, D_MODEL), final_gain.reshape(1, D_MODEL),
                      ya.reshape(b * t, MOBA_W), za, yb.reshape(b * t, DSA_W), zb, qm, zm, mk, mv,
                      w_gates, w_branch.astype(bf), w_out.astype(bf), t)
    return out.reshape(b, t, D_MODEL)


@jax.jit
def kernel(x, mem, norm_gain, w_in, rel_bias, mem_norm_gain, w_mem_kv, w_branch, w_out, final_norm_gain):
    assert norm_gain.shape[0] == 1, "one layer"
    return _layer(x, mem, norm_gain[0], w_in[0], rel_bias, mem_norm_gain[0], w_mem_kv[0], w_branch[0],
                  w_out[0], final_norm_gain)
```

```python
import math

import numpy as np
import jax
import jax.numpy as jnp
from jax import lax
from jax.experimental import pallas as pl
from jax.experimental.pallas import tpu as pltpu

D_MODEL = 1024
HEAD_DIM = 64
MOBA_HEADS = 6
DSA_HEADS = 6
MEM_HEADS = 4
MOBA_W = MOBA_HEADS * HEAD_DIM
DSA_W = DSA_HEADS * HEAD_DIM
MEM_W = MEM_HEADS * HEAD_DIM
IDX_HEADS = 8
IDX_DIM = 64
IDX_W = IDX_HEADS * IDX_DIM
MOBA_BLOCK = 256
MOBA_TOPK = 3
DSA_TOPK = 256
REL_BUCKETS = 32
REL_MAX_DIST = 128
N_BIAS_HEADS = MOBA_HEADS + DSA_HEADS
EPS = 1e-6
IN_SIZES = (MOBA_W,) * 4 + (DSA_W,) * 4 + (IDX_W, IDX_DIM, IDX_HEADS) + (MEM_W, MEM_W) + (D_MODEL,) * 3

LANES = 128
SUBLANES = 8
ONES_ROWS = 16
ACC_ROWS = LANES + ONES_ROWS
TILE = 256
ROW_TILE = 512
FEAT_LO = 32
LOG2E = math.log2(math.e)
NEG_BIG = -1e30
MASK_BUCKET = REL_BUCKETS
VMEM_LIMIT = 56 * 1024 * 1024
MAX_SEARCH_ITERS = 96

_NT = (((1,), (1,)), ((), ()))


def _dot_nt(a, b):
    return lax.dot_general(a, b, _NT, preferred_element_type=jnp.float32)


def _dot(a, b):
    return jnp.dot(a, b, preferred_element_type=jnp.float32)


def _rel_bucket_np(n):
    exact = REL_BUCKETS // 2
    nf = np.maximum(n, 1).astype(np.float32)
    large = exact + (np.log(nf / np.float32(exact)) / np.float32(math.log(REL_MAX_DIST / exact))
                     * np.float32(REL_BUCKETS - exact)).astype(np.int32)
    return np.where(n < exact, n, np.minimum(large, REL_BUCKETS - 1))


def _bucket_tiles(mask_future):
    j = np.arange(TILE)[:, None]
    i = np.arange(TILE)[None, :]
    d0 = i - j
    b0 = _rel_bucket_np(np.maximum(d0, 0))
    if mask_future:
        b0 = np.where(d0 >= 0, b0, MASK_BUCKET)
    b1 = _rel_bucket_np(TILE + d0)
    assert int(_rel_bucket_np(np.array([TILE + 1]))[0]) == REL_BUCKETS - 1
    return jnp.asarray(np.stack([b0, b1]).astype(np.int32))


def _build_bias_tile(bkt, bias_ref, head):
    acc = jnp.where(bkt == MASK_BUCKET, NEG_BIG, 0.0)
    for b in range(REL_BUCKETS):
        acc = jnp.where(bkt == b, bias_ref[b * N_BIAS_HEADS + head], acc)
    return acc


def _rmsnorm_rows(x, g):
    xf = x.astype(jnp.float32)
    return xf * lax.rsqrt(jnp.mean(xf * xf, axis=-1, keepdims=True) + EPS) * g


_Q_SCALE = HEAD_DIM ** -0.5 * LOG2E
_STD_SEGS = (("ka", MOBA_W, 1.0), ("za", MOBA_W, 1.0), ("kb", DSA_W, 1.0), ("zb", DSA_W, 1.0),
             ("qm", MEM_W, _Q_SCALE), ("zm", MEM_W, 1.0), ("ik", IDX_DIM, 1.0))
_T_SEGS = (("qaT", MOBA_W, _Q_SCALE), ("vaT", MOBA_W, 1.0), ("qbT", DSA_W, _Q_SCALE), ("vbT", DSA_W, 1.0),
           ("iqT", IDX_W, IDX_DIM ** -0.5), ("iwT", IDX_HEADS, IDX_HEADS ** -0.5))
_STD_W = sum(w for _, w, _ in _STD_SEGS)
_T_W = sum(w for _, w, _ in _T_SEGS)
_CHUNKS_PER_ROW_TILE = ROW_TILE // TILE


def _proj_kernel(x_ref, g_ref, w_ref, wt_ref, *out_refs):
    std_refs = out_refs[:len(_STD_SEGS)]
    t_refs = out_refs[len(_STD_SEGS):len(_STD_SEGS) + len(_T_SEGS)]
    kmean_ref = out_refs[-1]
    h = _rmsnorm_rows(x_ref[...], g_ref[...]).astype(jnp.bfloat16)
    off = 0
    for (name, width, scale), o_ref in zip(_STD_SEGS, std_refs):
        r = _dot(h, w_ref[:, off:off + width])
        if name == "ka":
            for blk in range(ROW_TILE // MOBA_BLOCK):
                kmean_ref[0, blk:blk + 1, :] = jnp.mean(
                    r[blk * MOBA_BLOCK:(blk + 1) * MOBA_BLOCK], axis=0, keepdims=True)
        if scale != 1.0:
            r = r * scale
        o_ref[...] = r.astype(o_ref.dtype)
        off += width
    off = 0
    for (name, width, scale), o_ref in zip(_T_SEGS, t_refs):
        r = _dot_nt(wt_ref[off:off + width, :], h)
        if scale != 1.0:
            r = r * scale
        r = r.astype(o_ref.dtype)
        if name in ("vaT", "vbT"):
            for c in range(_CHUNKS_PER_ROW_TILE):
                o_ref[0, c] = r[:, c * TILE:(c + 1) * TILE]
        else:
            o_ref[0] = r
        off += width


def _proj_call(x2, gain, w_std, w_t, b, t):
    rows = x2.shape[0]
    tiles_per_batch = t // ROW_TILE
    n_tiles = rows // ROW_TILE
    row_spec = lambda width: pl.BlockSpec((ROW_TILE, width), lambda i: (i, 0))
    const_spec = lambda shape: pl.BlockSpec(shape, lambda i: (0,) * len(shape))
    t_spec = lambda width: pl.BlockSpec((1, width, ROW_TILE),
                                        lambda i: (i // tiles_per_batch, 0, i % tiles_per_batch))
    chunk_spec = lambda width: pl.BlockSpec((1, _CHUNKS_PER_ROW_TILE, width, TILE),
                                            lambda i: (i // tiles_per_batch, i % tiles_per_batch, 0, 0))
    out_shape = [jax.ShapeDtypeStruct((rows, w), jnp.bfloat16) for _, w, _ in _STD_SEGS]
    out_specs = [row_spec(w) for _, w, _ in _STD_SEGS]
    for name, w, _ in _T_SEGS:
        if name in ("vaT", "vbT"):
            out_shape.append(jax.ShapeDtypeStruct((b, t // TILE, w, TILE), jnp.bfloat16))
            out_specs.append(chunk_spec(w))
        else:
            out_shape.append(jax.ShapeDtypeStruct((b, w, t), jnp.float32 if name == "iwT" else jnp.bfloat16))
            out_specs.append(t_spec(w))
    out_shape.append(jax.ShapeDtypeStruct((n_tiles, ROW_TILE // MOBA_BLOCK, MOBA_W), jnp.float32))
    out_specs.append(pl.BlockSpec((1, ROW_TILE // MOBA_BLOCK, MOBA_W), lambda i: (i, 0, 0)))
    return pl.pallas_call(
        _proj_kernel,
        grid=(n_tiles,),
        in_specs=[row_spec(D_MODEL), const_spec((1, D_MODEL)), const_spec((D_MODEL, _STD_W)),
                  const_spec((_T_W, D_MODEL))],
        out_specs=out_specs,
        out_shape=out_shape,
        compiler_params=pltpu.CompilerParams(dimension_semantics=("arbitrary",), vmem_limit_bytes=VMEM_LIMIT),
        name="proj",
    )(x2, gain, w_std, w_t)


def _memkv_kernel(mem_ref, g_ref, w_ref, mk_ref, mv_ref):
    h = _rmsnorm_rows(mem_ref[0], g_ref[...]).astype(jnp.bfloat16)
    r = _dot(h, w_ref[...])
    mk_ref[0] = r[:, :MEM_W].astype(jnp.bfloat16)
    mv_ref[0] = r[:, MEM_W:].astype(jnp.bfloat16)


def _memkv_call(mem, gain, w_kv):
    b, n_mem, _ = mem.shape
    return pl.pallas_call(
        _memkv_kernel,
        grid=(b,),
        in_specs=[pl.BlockSpec((1, n_mem, D_MODEL), lambda i: (i, 0, 0)),
                  pl.BlockSpec((1, D_MODEL), lambda i: (0, 0)),
                  pl.BlockSpec((D_MODEL, 2 * MEM_W), lambda i: (0, 0))],
        out_specs=[pl.BlockSpec((1, n_mem, MEM_W), lambda i: (i, 0, 0))] * 2,
        out_shape=[jax.ShapeDtypeStruct((b, n_mem, MEM_W), jnp.bfloat16)] * 2,
        compiler_params=pltpu.CompilerParams(dimension_semantics=("arbitrary",), vmem_limit_bytes=VMEM_LIMIT),
        name="memkv",
    )(mem, gain, w_kv)


def _row_half_mask(shape, half):
    r = lax.broadcasted_iota(jnp.int32, shape, 0)
    return (r >= half * HEAD_DIM) & (r < (half + 1) * HEAD_DIM)


def _with_ones_rows(v_t):
    return jnp.concatenate([v_t, jnp.ones((ONES_ROWS, v_t.shape[1]), v_t.dtype)], axis=0)


def _softmax_stage(s_t, m_ref, alpha_ref, p_ref):
    m_old = m_ref[:1]
    m_new = jnp.maximum(m_old, jnp.max(s_t, axis=0, keepdims=True))
    alpha_ref[...] = jnp.broadcast_to(jnp.exp2(m_old - m_new), alpha_ref.shape)
    m_ref[...] = jnp.broadcast_to(m_new, m_ref.shape)
    p_ref[...] = jnp.exp2(s_t - m_new).astype(jnp.bfloat16)


def _accumulate_stage(v_t, alpha_ref, p_ref, acc_ref):
    acc_ref[...] = alpha_ref[:1] * acc_ref[...] + _dot(_with_ones_rows(v_t), p_ref[...])


def _init_softmax_state(m_ref, alpha_ref, p_ref, acc_ref):
    m_ref[...] = jnp.full(m_ref.shape, NEG_BIG, jnp.float32)
    acc_ref[...] = jnp.zeros(acc_ref.shape, jnp.float32)
    alpha_ref[...] = jnp.ones(alpha_ref.shape, jnp.float32)
    p_ref[...] = jnp.zeros(p_ref.shape, p_ref.dtype)


def _normalized_pair(acc, half_cols):
    out0 = acc[:LANES, :half_cols] / acc[LANES:LANES + 1, :half_cols]
    out1 = acc[:LANES, half_cols:] / acc[LANES:LANES + 1, half_cols:]
    return jnp.where(_row_half_mask((LANES, half_cols), 0), out0, out1).T


_KIND_DIAG, _KIND_PREV, _KIND_FAR = 0, 1, 2


def _run_tile_pipeline(qi, logits, softmax, accumulate):
    n_far = jnp.maximum(qi - 1, 0)
    n_loop = n_far // 2

    def far_pair(t):
        accumulate(0, jnp.maximum(t - 2, 0))
        accumulate(1, jnp.maximum(t - 1, 0))
        softmax(0, t, _KIND_FAR)
        logits(0, t + 2)
        softmax(1, t + 1, _KIND_FAR)
        logits(1, t + 3)

    def quad_body(j, carry):
        far_pair(4 * j)
        far_pair(4 * j + 2)
        return carry

    def pair_body(j, carry):
        far_pair(2 * j)
        return carry

    logits(0, 0)
    logits(1, jnp.minimum(1, qi))
    lax.fori_loop(0, n_loop // 2, quad_body, 0)
    lax.fori_loop(2 * (n_loop // 2), n_loop, pair_body, 0)
    t0 = 2 * n_loop
    odd = n_far % 2 == 1

    @pl.when(odd)
    def _():
        accumulate(0, jnp.maximum(t0 - 2, 0))
        accumulate(1, jnp.maximum(t0 - 1, 0))
        softmax(0, t0, _KIND_FAR)
        logits(0, qi)
        softmax(1, qi - 1, _KIND_PREV)
        accumulate(0, t0)
        softmax(0, qi, _KIND_DIAG)
        accumulate(1, qi - 1)
        accumulate(0, qi)

    @pl.when(jnp.logical_not(odd) & (qi >= 1))
    def _():
        accumulate(0, jnp.maximum(t0 - 2, 0))
        accumulate(1, jnp.maximum(t0 - 1, 0))
        softmax(0, qi - 1, _KIND_PREV)
        softmax(1, qi, _KIND_DIAG)
        accumulate(0, qi - 1)
        accumulate(1, qi)

    @pl.when(qi == 0)
    def _():
        softmax(0, 0, _KIND_DIAG)
        accumulate(0, 0)


def _moba_kernel(bias_ref, qt_ref, k_ref, vt_ref, kmean_ref, bkt_ref, o_ref,
                 qa_ref, tab_ref, s_ref, p_ref, alpha_ref, m_ref, acc_ref):
    pair = pl.program_id(1)
    qi = pl.program_id(2)
    q_t = qt_ref[0]
    kmean = kmean_ref[0].astype(jnp.bfloat16)
    blk = lax.broadcasted_iota(jnp.int32, (FEAT_LO, TILE), 0)
    blk_f = blk.astype(jnp.float32)
    past = blk < qi
    lane = lax.broadcasted_iota(jnp.int32, (TILE, LANES), 1)
    _init_softmax_state(m_ref, alpha_ref, p_ref, acc_ref)

    @pl.when(qi == 0)
    def _():
        for half in range(2):
            for kind in (_KIND_DIAG, _KIND_PREV):
                tab_ref[kind, :, half * TILE:(half + 1) * TILE] = _build_bias_tile(
                    bkt_ref[kind], bias_ref, pair * 2 + half)

    for half in range(2):
        qh = jnp.where(_row_half_mask((LANES, TILE), half), q_t, jnp.zeros_like(q_t))
        gate = _dot(kmean, qh)[:FEAT_LO]
        g = jnp.where(past, gate, -jnp.inf)
        sel = jnp.zeros((FEAT_LO, TILE), jnp.bool_)
        for _ in range(MOBA_TOPK):
            mx = jnp.max(g, axis=0, keepdims=True)
            first = jnp.min(jnp.where(g == mx, blk_f, float(LANES)), axis=0, keepdims=True)
            pick = blk_f == first
            sel = sel | (pick & past)
            g = jnp.where(pick, -jnp.inf, g)
        b31 = jnp.full((FEAT_LO, TILE), bias_ref[(REL_BUCKETS - 1) * N_BIAS_HEADS + pair * 2 + half], jnp.float32)
        b31_hi = b31.astype(jnp.bfloat16).astype(jnp.float32)
        hi_part = jnp.where(sel, jnp.where(blk < qi - 1, b31_hi, 0.0), NEG_BIG)
        hi_part = jnp.where(past, hi_part, 0.0)
        lo_part = jnp.where(blk < qi - 1, b31 - b31_hi, 0.0)
        feat = jnp.concatenate([hi_part, lo_part, jnp.zeros((LANES - 2 * FEAT_LO, TILE), jnp.float32)], axis=0)
        qa_ref[:, half * TILE:(half + 1) * TILE] = jnp.concatenate([qh, feat.astype(jnp.bfloat16)], axis=0)

    def logits(slot, kj):
        k_t = k_ref[0, pl.ds(pl.multiple_of(kj * TILE, TILE), TILE), :]
        onehot = ((lane == kj) | (lane == kj + FEAT_LO)).astype(jnp.bfloat16)
        s_ref[slot] = _dot(jnp.concatenate([k_t, onehot], axis=1), qa_ref[...])

    def softmax(slot, kj, kind):
        s_t = s_ref[slot]
        if kind != _KIND_FAR:
            s_t = s_t + tab_ref[kind]
        _softmax_stage(s_t, m_ref, alpha_ref.at[slot], p_ref.at[slot])

    def accumulate(slot, kj):
        _accumulate_stage(vt_ref[0, kj], alpha_ref.at[slot], p_ref.at[slot], acc_ref)

    _run_tile_pipeline(qi, logits, softmax, accumulate)
    o_ref[0] = _normalized_pair(acc_ref[...], TILE).astype(o_ref.dtype)


def _moba_call(bias_flat, q_t, k, v_t, kmean, bkt):
    b, _, t = q_t.shape
    n_pairs = MOBA_HEADS // 2
    n_chunks = t // TILE
    grid_spec = pltpu.PrefetchScalarGridSpec(
        num_scalar_prefetch=1,
        grid=(b, n_pairs, n_chunks),
        in_specs=[
            pl.BlockSpec((1, LANES, TILE), lambda bi, p, qi, s: (bi, p, qi)),
            pl.BlockSpec((1, t, LANES), lambda bi, p, qi, s: (bi, 0, p)),
            pl.BlockSpec((1, n_chunks, LANES, TILE), lambda bi, p, qi, s: (bi, 0, p, 0)),
            pl.BlockSpec((1, LANES, LANES), lambda bi, p, qi, s: (bi, 0, p)),
            pl.BlockSpec((2, TILE, TILE), lambda bi, p, qi, s: (0, 0, 0)),
        ],
        out_specs=pl.BlockSpec((1, TILE, LANES), lambda bi, p, qi, s: (bi, qi, p)),
        scratch_shapes=[
            pltpu.VMEM((2 * LANES, 2 * TILE), jnp.bfloat16),
            pltpu.VMEM((2, TILE, 2 * TILE), jnp.float32),
            pltpu.VMEM((2, TILE, 2 * TILE), jnp.float32),
            pltpu.VMEM((2, TILE, 2 * TILE), jnp.bfloat16),
            pltpu.VMEM((2, SUBLANES, 2 * TILE), jnp.float32),
            pltpu.VMEM((SUBLANES, 2 * TILE), jnp.float32),
            pltpu.VMEM((ACC_ROWS, 2 * TILE), jnp.float32),
        ],
    )
    return pl.pallas_call(
        _moba_kernel,
        grid_spec=grid_spec,
        out_shape=jax.ShapeDtypeStruct((b, t, MOBA_W), jnp.bfloat16),
        compiler_params=pltpu.CompilerParams(dimension_semantics=("arbitrary",) * 3, vmem_limit_bytes=VMEM_LIMIT),
        name="moba",
    )(bias_flat, q_t, k, v_t, kmean, bkt)


def _dsa_kernel(bias_ref, qt_ref, k_ref, vt_ref, iqt_ref, ik_ref, iwt_ref, bkt_ref, o_ref,
                sc_ref, tab_ref, qz_ref, s_ref, p_ref, alpha_ref, seen_ref, m_ref, acc_ref):
    qi = pl.program_id(1)
    n_chunks = qi + 1
    key = lax.broadcasted_iota(jnp.int32, (TILE, TILE), 0)
    qry = lax.broadcasted_iota(jnp.int32, (TILE, TILE), 1)
    causal = key <= qry

    far_bias = [bias_ref[(REL_BUCKETS - 1) * N_BIAS_HEADS + MOBA_HEADS + h] for h in range(DSA_HEADS)]

    @pl.when(qi == 0)
    def _():
        for h in range(DSA_HEADS):
            for kind in (_KIND_DIAG, _KIND_PREV):
                tab_ref[h // 2, kind, :, (h % 2) * TILE:(h % 2 + 1) * TILE] = _build_bias_tile(
                    bkt_ref[kind], bias_ref, MOBA_HEADS + h) - far_bias[h]

    feat_row = lax.broadcasted_iota(jnp.int32, (LANES, TILE), 0)
    for h in range(DSA_HEADS):
        blk = qt_ref[0, (h // 2) * LANES:(h // 2 + 1) * LANES, :]
        qh = jnp.where(_row_half_mask((LANES, TILE), h % 2), blk, jnp.zeros_like(blk))
        b31 = jnp.full((LANES, TILE), far_bias[h], jnp.float32)
        b31_hi = b31.astype(jnp.bfloat16).astype(jnp.float32)
        feat = jnp.where(feat_row == 0, b31_hi, jnp.where(feat_row == 1, b31 - b31_hi, 0.0))
        qz_ref[h // 2, :, (h % 2) * TILE:(h % 2 + 1) * TILE] = jnp.concatenate(
            [qh, feat.astype(jnp.bfloat16)], axis=0)

    iw_t = iwt_ref[0]

    def score_chunk(kc):
        row0 = pl.multiple_of(kc * TILE, TILE)
        ik_t = ik_ref[0, pl.ds(row0, TILE), :]
        tot = jnp.zeros((TILE, TILE), jnp.float32)
        for h in range(IDX_HEADS):
            x = _dot(ik_t, iqt_ref[0, h * IDX_DIM:(h + 1) * IDX_DIM, :])
            tot = tot + iw_t[h:h + 1, :] * jnp.maximum(x, 0.0)
        return tot

    def score_body(kc, carry):
        mx, mn = carry
        tot = score_chunk(kc)
        sc_ref[kc] = tot
        return (jnp.maximum(mx, jnp.max(tot, axis=0, keepdims=True)),
                jnp.minimum(mn, jnp.min(tot, axis=0, keepdims=True)))

    def score_pair_body(j, carry):
        return score_body(2 * j + 1, score_body(2 * j, carry))

    extremes = lax.fori_loop(0, qi // 2, score_pair_body, (jnp.full((1, TILE), -jnp.inf, jnp.float32),
                                                           jnp.full((1, TILE), jnp.inf, jnp.float32)))
    row_max, row_min = lax.cond(qi % 2 == 1, lambda c: score_body(qi - 1, c), lambda c: c, extremes)
    tot = score_chunk(qi)
    sc_ref[qi] = jnp.where(causal, tot, -jnp.inf)
    row_max = jnp.maximum(row_max, jnp.max(jnp.where(causal, tot, -jnp.inf), axis=0, keepdims=True))
    row_min = jnp.minimum(row_min, jnp.min(jnp.where(causal, tot, jnp.inf), axis=0, keepdims=True))

    @pl.when(qi % 2 == 0)
    def _():
        sc_ref[qi + 1] = jnp.full((TILE, TILE), -jnp.inf, jnp.float32)

    def reduce_pass(fn, init, combine):
        def body(j, part):
            return combine(part, combine(fn(sc_ref[2 * j]), fn(sc_ref[2 * j + 1])))
        return lax.fori_loop(0, (n_chunks + 1) // 2, body, jnp.full((1, TILE), init, jnp.float32))

    def count(pred):
        return reduce_pass(lambda x: jnp.sum(jnp.where(pred(x), 1.0, 0.0), axis=0, keepdims=True), 0.0, jnp.add)

    def min_ge(p):
        return reduce_pass(lambda x: jnp.min(jnp.where(x >= p, x, jnp.inf), axis=0, keepdims=True),
                           jnp.inf, jnp.minimum)

    n_adm =(qi * TILE + 1 + lax.broadcasted_iota(jnp.int32, (1, TILE), 1)).astype(jnp.float32)
    k_top = float(DSA_TOPK)
    take_all = n_adm <= k_top

    def search_cond(st):
        return jnp.logical_and(st[0] < MAX_SEARCH_ITERS, st[-1] >= 1.0)

    def live_flag(done, stall):
        return jnp.max(jnp.where(done < 0.5, jnp.where(stall >= 3.0, 3.0, 1.0), 0.0))

    def tie_check(st):
        it, lo, hi, hi_fin, c_lo, c_hi, done, last, streak, stall, _ = st
        a = min_ge(lo)
        c_gt = count(lambda x: x > a)
        tie = (c_gt < k_top) & (done < 0.5)
        done = jnp.where(tie, 1.0, done)
        stall = jnp.zeros_like(stall)
        return (it, lo, hi, hi_fin, c_lo, jnp.where(tie, c_gt, c_hi), done, last, streak, stall,
                live_flag(done, stall))

    def search_step(st):
        it, lo, hi, hi_fin, c_lo, c_hi, done, last, streak, stall = st
        width = c_lo - c_hi
        target = k_top - 0.5 + jnp.where(streak >= 1.0, last * 0.25 * (c_lo - c_hi), 0.0)
        target = jnp.clip(target, c_hi + 0.5, c_lo - 0.5)
        frac = (c_lo - target) / jnp.maximum(c_lo - c_hi, 1.0)
        mid = 0.5 * lo + 0.5 * hi_fin
        p = jnp.where(it % 5 == 4, mid, lo + frac * (hi_fin - lo))
        p = jnp.where((p > lo) & (p < hi), p, mid)
        stuck = jnp.logical_not((mid > lo) & (mid < hi))
        c = count(lambda x: x >= p)
        up = c >= k_top
        move = jnp.logical_not((done > 0.5) | stuck)
        rise = move & up
        fall = move & jnp.logical_not(up)
        lo = jnp.where(rise, p, lo)
        c_lo = jnp.where(rise, c, c_lo)
        hi = jnp.where(fall, p, hi)
        c_hi = jnp.where(fall, c, c_hi)
        hi_fin = jnp.where(fall, p, hi_fin)
        side = jnp.where(up, -1.0, 1.0)
        streak = jnp.where(side == last, streak + 1.0, 0.0)
        done = jnp.where((c_lo - c_hi <= 1.0) | (c_lo == k_top) | stuck, 1.0, done)
        stall = jnp.where(c_lo - c_hi == width, stall + 1.0, 0.0)
        return it + 1, lo, hi, hi_fin, c_lo, c_hi, done, side, streak, stall

    def search_body(st):
        st = search_step(search_step(search_step(search_step(st[:-1]))))
        flag = live_flag(st[6], st[9])
        return lax.cond(flag >= 3.0, tie_check, lambda s: s, st + (flag,))

    zeros_row = jnp.zeros((1, TILE), jnp.float32)
    done0 = take_all.astype(jnp.float32)
    init = (jnp.int32(0), row_min, jnp.full((1, TILE), jnp.inf, jnp.float32), row_max,
            n_adm, zeros_row, done0, zeros_row, zeros_row, zeros_row, live_flag(done0, zeros_row))
    _, lo, _, _, _, c_hi, _, _, _, _, _ = lax.while_loop(search_cond, search_body, init)
    thr = jnp.where(take_all, -jnp.inf, min_ge(lo))
    need = k_top - jnp.where(take_all, 0.0, c_hi)

    n_pairs = DSA_HEADS // 2
    for pr in range(n_pairs):
        _init_softmax_state(m_ref.at[pr], alpha_ref.at[:, pr], p_ref.at[:, pr], acc_ref.at[pr])
    seen_ref[...] = jnp.zeros(seen_ref.shape, jnp.float32)
    lower = (qry < key).astype(jnp.bfloat16)
    ones_feat = (lax.broadcasted_iota(jnp.int32, (TILE, LANES), 1) < 2).astype(jnp.bfloat16)

    def logits(slot, kc):
        row0 = pl.multiple_of(kc * TILE, TILE)
        for pr in range(n_pairs):
            k_t = k_ref[0, pl.ds(row0, TILE), pr * LANES:(pr + 1) * LANES]
            s_ref[slot, pr] = _dot(jnp.concatenate([k_t, ones_feat], axis=1), qz_ref[pr])

    def softmax(slot, kc, kind):
        x = sc_ref[kc]
        tie = x == thr
        tie_f = jnp.where(tie, 1.0, 0.0)
        seen = seen_ref[:1]
        rank = seen + _dot(lower, tie_f.astype(jnp.bfloat16))
        selected = (x > thr) | (tie & (rank < need))
        if kind == _KIND_DIAG:
            selected = selected & causal
        mask_add = jnp.where(selected, 0.0, NEG_BIG)
        mask_add = jnp.concatenate([mask_add, mask_add], axis=1)
        seen_ref[...] = jnp.broadcast_to(seen + jnp.sum(tie_f, axis=0, keepdims=True), seen_ref.shape)
        for pr in range(n_pairs):
            s_t = s_ref[slot, pr] + mask_add
            if kind != _KIND_FAR:
                s_t = s_t + tab_ref[pr, kind]
            _softmax_stage(s_t, m_ref.at[pr], alpha_ref.at[slot, pr], p_ref.at[slot, pr])

    def accumulate(slot, kc):
        for pr in range(n_pairs):
            _accumulate_stage(vt_ref[0, kc, pr * LANES:(pr + 1) * LANES, :], alpha_ref.at[slot, pr],
                              p_ref.at[slot, pr], acc_ref.at[pr])

    _run_tile_pipeline(qi, logits, softmax, accumulate)
    for pr in range(n_pairs):
        o_ref[0, :, pr * LANES:(pr + 1) * LANES] = _normalized_pair(acc_ref[pr], TILE).astype(o_ref.dtype)


def _dsa_call(bias_flat, q_t, k, v_t, iq_t, ik, iw_t, bkt):
    b, _, t = q_t.shape
    n_chunks = t // TILE
    n_pairs = DSA_HEADS // 2
    grid_spec = pltpu.PrefetchScalarGridSpec(
        num_scalar_prefetch=1,
        grid=(b, n_chunks),
        in_specs=[
            pl.BlockSpec((1, DSA_W, TILE), lambda bi, qi, s: (bi, 0, qi)),
            pl.BlockSpec((1, t, DSA_W), lambda bi, qi, s: (bi, 0, 0)),
            pl.BlockSpec((1, n_chunks, DSA_W, TILE), lambda bi, qi, s: (bi, 0, 0, 0)),
            pl.BlockSpec((1, IDX_W, TILE), lambda bi, qi, s: (bi, 0, qi)),
            pl.BlockSpec((1, t, IDX_DIM), lambda bi, qi, s: (bi, 0, 0)),
            pl.BlockSpec((1, IDX_HEADS, TILE), lambda bi, qi, s: (bi, 0, qi)),
            pl.BlockSpec((2, TILE, TILE), lambda bi, qi, s: (0, 0, 0)),
        ],
        out_specs=pl.BlockSpec((1, TILE, DSA_W), lambda bi, qi, s: (bi, qi, 0)),
        scratch_shapes=[
            pltpu.VMEM((n_chunks, TILE, TILE), jnp.float32),
            pltpu.VMEM((n_pairs, 2, TILE, 2 * TILE), jnp.float32),
            pltpu.VMEM((n_pairs, 2 * LANES, 2 * TILE), jnp.bfloat16),
            pltpu.VMEM((2, n_pairs, TILE, 2 * TILE), jnp.float32),
            pltpu.VMEM((2, n_pairs, TILE, 2 * TILE), jnp.bfloat16),
            pltpu.VMEM((2, n_pairs, SUBLANES, 2 * TILE), jnp.float32),
            pltpu.VMEM((SUBLANES, TILE), jnp.float32),
            pltpu.VMEM((n_pairs, SUBLANES, 2 * TILE), jnp.float32),
            pltpu.VMEM((n_pairs, ACC_ROWS, 2 * TILE), jnp.float32),
        ],
    )
    return pl.pallas_call(
        _dsa_kernel,
        grid_spec=grid_spec,
        out_shape=jax.ShapeDtypeStruct((b, t, DSA_W), jnp.bfloat16),
        compiler_params=pltpu.CompilerParams(dimension_semantics=("arbitrary",) * 2, vmem_limit_bytes=VMEM_LIMIT),
        name="dsa",
    )(bias_flat, q_t, k, v_t, iq_t, ik, iw_t, bkt)


def _lane_half_mask(shape, half):
    lane = lax.broadcasted_iota(jnp.int32, shape, 1)
    return (lane >= half * HEAD_DIM) & (lane < (half + 1) * HEAD_DIM)


def _final_kernel(x_ref, g_ref, fg_ref, ya_ref, za_ref, yb_ref, zb_ref, qm_ref, zm_ref, mk_ref, mv_ref,
                  wg_ref, wb_ref, wo_ref, o_ref):
    x = x_ref[...]
    h = _rmsnorm_rows(x, g_ref[...]).astype(jnp.bfloat16)

    ym_parts = []
    for pr in range(MEM_HEADS // 2):
        qp = qm_ref[:, pr * LANES:(pr + 1) * LANES]
        mk = mk_ref[0, :, pr * LANES:(pr + 1) * LANES]
        mv = mv_ref[0, :, pr * LANES:(pr + 1) * LANES]
        outs = []
        for half in range(2):
            qh = jnp.where(_lane_half_mask((ROW_TILE, LANES), half), qp, jnp.zeros_like(qp))
            s = _dot_nt(qh, mk)
            p = jnp.exp2(s - jnp.max(s, axis=1, keepdims=True))
            outs.append(_dot(p.astype(jnp.bfloat16), mv) / jnp.sum(p, axis=1, keepdims=True))
        ym_parts.append(jnp.where(_lane_half_mask((ROW_TILE, LANES), 0), outs[0], outs[1]))
    ym = jnp.concatenate(ym_parts, axis=1)

    def gated(y, z_ref):
        z = z_ref[...].astype(jnp.float32)
        return (y.astype(jnp.float32) * (z * jax.nn.sigmoid(z))).astype(jnp.bfloat16)

    ua = gated(ya_ref[...], za_ref)
    ub = gated(yb_ref[...], zb_ref)
    um = gated(ym, zm_ref)
    merged = jax.nn.sigmoid(_dot(h, wg_ref[:, :D_MODEL])) * _dot(ua, wb_ref[:MOBA_W])
    merged += jax.nn.sigmoid(_dot(h, wg_ref[:, D_MODEL:2 * D_MODEL])) * _dot(ub, wb_ref[MOBA_W:MOBA_W + DSA_W])
    merged += jax.nn.sigmoid(_dot(h, wg_ref[:, 2 * D_MODEL:])) * _dot(um, wb_ref[MOBA_W + DSA_W:])
    y = x + _dot(merged.astype(jnp.bfloat16), wo_ref[...])
    o_ref[...] = _rmsnorm_rows(y, fg_ref[...])


def _final_call(x2, gain, final_gain, ya, za, yb, zb, qm, zm, mk, mv, w_gates, w_branch, w_out, rows_per_batch):
    rows = x2.shape[0]
    tiles_per_batch = rows_per_batch // ROW_TILE
    row_spec = lambda width: pl.BlockSpec((ROW_TILE, width), lambda i: (i, 0))
    const_spec = lambda shape: pl.BlockSpec(shape, lambda i: (0,) * len(shape))
    mem_spec = pl.BlockSpec((1,) + mk.shape[1:], lambda i: (i // tiles_per_batch, 0, 0))
    return pl.pallas_call(
        _final_kernel,
        grid=(rows // ROW_TILE,),
        in_specs=[row_spec(D_MODEL), const_spec((1, D_MODEL)), const_spec((1, D_MODEL)),
                  row_spec(MOBA_W), row_spec(MOBA_W), row_spec(DSA_W), row_spec(DSA_W),
                  row_spec(MEM_W), row_spec(MEM_W), mem_spec, mem_spec,
                  const_spec(w_gates.shape), const_spec(w_branch.shape), const_spec(w_out.shape)],
        out_specs=row_spec(D_MODEL),
        out_shape=jax.ShapeDtypeStruct((rows, D_MODEL), jnp.float32),
        compiler_params=pltpu.CompilerParams(dimension_semantics=("arbitrary",), vmem_limit_bytes=VMEM_LIMIT),
        name="final",
    )(x2, gain, final_gain, ya, za, yb, zb, qm, zm, mk, mv, w_gates, w_branch, w_out)


def _layer(x, mem, norm_gain, w_in, rel_bias, mem_norm_gain, w_mem_kv, w_branch, w_out, final_gain):
    b, t, _ = x.shape
    assert t % (2 * TILE) == 0 and t // MOBA_BLOCK <= FEAT_LO and t >= 4 * DSA_TOPK
    bf = jnp.bfloat16
    names = ("qa", "ka", "va", "za", "qb", "kb", "vb", "zb", "iq", "ik", "iw", "qm", "zm", "ga", "gb", "gm")
    cols, off = {}, 0
    for name, size in zip(names, IN_SIZES):
        cols[name] = w_in[:, off:off + size]
        off += size
    w_std = jnp.concatenate([cols[n] for n, _, _ in _STD_SEGS], axis=1).astype(bf)
    w_t = jnp.concatenate([cols[n[:-1]] for n, _, _ in _T_SEGS], axis=1).T.astype(bf)
    w_gates = jnp.concatenate([cols["ga"], cols["gb"], cols["gm"]], axis=1).astype(bf)

    x2 = x.reshape(b * t, D_MODEL)
    (ka, za, kb, zb, qm, zm, ik, qa_t, va_t, qb_t, vb_t, iq_t, iw_t, kmean) = _proj_call(
        x2, norm_gain.reshape(1, D_MODEL), w_std, w_t, b, t)
    seq = lambda a: a.reshape(b, t, a.shape[-1])
    n_blocks = t // MOBA_BLOCK
    kmean = jnp.pad(kmean.reshape(b, n_blocks, MOBA_W), ((0, 0), (0, LANES - n_blocks), (0, 0)))

    bias_flat = (rel_bias * LOG2E).reshape(-1)
    ya = _moba_call(bias_flat, qa_t, seq(ka), va_t, kmean, _bucket_tiles(True))
    yb = _dsa_call(bias_flat, qb_t, seq(kb), vb_t, iq_t, seq(ik), iw_t, _bucket_tiles(False))
    mk, mv = _memkv_call(mem, mem_norm_gain.reshape(1, D_MODEL), w_mem_kv.astype(bf))
    out = _final_call(x2, norm_gain.reshape(1, D_MODEL), final_gain.reshape(1, D_MODEL),
                      ya.reshape(b * t, MOBA_W), za, yb.reshape(b * t, DSA_W), zb, qm, zm, mk, mv,
                      w_gates, w_branch.astype(bf), w_out.astype(bf), t)
    return out.reshape(b, t, D_MODEL)


@jax.jit
def kernel(x, mem, norm_gain, w_in, rel_bias, mem_norm_gain, w_mem_kv, w_branch, w_out, final_norm_gain):
    assert norm_gain.shape[0] == 1, "one layer"
    return _layer(x, mem, norm_gain[0], w_in[0], rel_bias, mem_norm_gain[0], w_mem_kv[0], w_branch[0],
                  w_out[0], final_norm_gain)
```

```python
import math

import numpy as np
import jax
import jax.numpy as jnp
from jax import lax
from jax.experimental import pallas as pl
from jax.experimental.pallas import tpu as pltpu

D_MODEL = 1024
HEAD_DIM = 64
MOBA_HEADS = 6
DSA_HEADS = 6
MEM_HEADS = 4
MOBA_W = MOBA_HEADS * HEAD_DIM
DSA_W = DSA_HEADS * HEAD_DIM
MEM_W = MEM_HEADS * HEAD_DIM
IDX_HEADS = 8
IDX_DIM = 64
IDX_W = IDX_HEADS * IDX_DIM
MOBA_BLOCK = 256
MOBA_TOPK = 3
DSA_TOPK = 256
REL_BUCKETS = 32
REL_MAX_DIST = 128
N_BIAS_HEADS = MOBA_HEADS + DSA_HEADS
EPS = 1e-6
IN_SIZES = (MOBA_W,) * 4 + (DSA_W,) * 4 + (IDX_W, IDX_DIM, IDX_HEADS) + (MEM_W, MEM_W) + (D_MODEL,) * 3

LANES = 128
SUBLANES = 8
ONES_ROWS = 16
ACC_ROWS = LANES + ONES_ROWS
TILE = 256
ROW_TILE = 512
FEAT_LO = 32
LOG2E = math.log2(math.e)
NEG_BIG = -1e30
MASK_BUCKET = REL_BUCKETS
VMEM_LIMIT = 56 * 1024 * 1024
MAX_SEARCH_ITERS = 96
BISECT_EVERY = 5
STALE_LEAN = 0.25
STALL_STEPS = 3.0

_NT = (((1,), (1,)), ((), ()))


def _dot_nt(a, b):
    return lax.dot_general(a, b, _NT, preferred_element_type=jnp.float32)


def _dot(a, b):
    return jnp.dot(a, b, preferred_element_type=jnp.float32)


def _rel_bucket_np(n):
    exact = REL_BUCKETS // 2
    nf = np.maximum(n, 1).astype(np.float32)
    large = exact + (np.log(nf / np.float32(exact)) / np.float32(math.log(REL_MAX_DIST / exact))
                     * np.float32(REL_BUCKETS - exact)).astype(np.int32)
    return np.where(n < exact, n, np.minimum(large, REL_BUCKETS - 1))


def _bucket_tiles(mask_future):
    j = np.arange(TILE)[:, None]
    i = np.arange(TILE)[None, :]
    d0 = i - j
    b0 = _rel_bucket_np(np.maximum(d0, 0))
    if mask_future:
        b0 = np.where(d0 >= 0, b0, MASK_BUCKET)
    b1 = _rel_bucket_np(TILE + d0)
    assert int(_rel_bucket_np(np.array([TILE + 1]))[0]) == REL_BUCKETS - 1
    return jnp.asarray(np.stack([b0, b1]).astype(np.int32))


def _build_bias_tile(bkt, bias_ref, head):
    acc = jnp.where(bkt == MASK_BUCKET, NEG_BIG, 0.0)
    for b in range(REL_BUCKETS):
        acc = jnp.where(bkt == b, bias_ref[b * N_BIAS_HEADS + head], acc)
    return acc


def _rmsnorm_rows(x, g):
    xf = x.astype(jnp.float32)
    return xf * lax.rsqrt(jnp.mean(xf * xf, axis=-1, keepdims=True) + EPS) * g


_Q_SCALE = HEAD_DIM ** -0.5 * LOG2E
_STD_SEGS = (("ka", MOBA_W, 1.0), ("za", MOBA_W, 1.0), ("kb", DSA_W, 1.0), ("zb", DSA_W, 1.0),
             ("qm", MEM_W, _Q_SCALE), ("zm", MEM_W, 1.0), ("ik", IDX_DIM, 1.0))
_T_SEGS = (("qaT", MOBA_W, _Q_SCALE), ("vaT", MOBA_W, 1.0), ("qbT", DSA_W, _Q_SCALE), ("vbT", DSA_W, 1.0),
           ("iqT", IDX_W, IDX_DIM ** -0.5), ("iwT", IDX_HEADS, IDX_HEADS ** -0.5))
_STD_W = sum(w for _, w, _ in _STD_SEGS)
_T_W = sum(w for _, w, _ in _T_SEGS)
_CHUNKS_PER_ROW_TILE = ROW_TILE // TILE


def _proj_kernel(x_ref, g_ref, w_ref, wt_ref, *out_refs):
    std_refs = out_refs[:len(_STD_SEGS)]
    t_refs = out_refs[len(_STD_SEGS):len(_STD_SEGS) + len(_T_SEGS)]
    kmean_ref = out_refs[-1]
    h = _rmsnorm_rows(x_ref[...], g_ref[...]).astype(jnp.bfloat16)
    off = 0
    for (name, width, scale), o_ref in zip(_STD_SEGS, std_refs):
        r = _dot(h, w_ref[:, off:off + width])
        if name == "ka":
            for blk in range(ROW_TILE // MOBA_BLOCK):
                kmean_ref[0, blk:blk + 1, :] = jnp.mean(
                    r[blk * MOBA_BLOCK:(blk + 1) * MOBA_BLOCK], axis=0, keepdims=True)
        if scale != 1.0:
            r = r * scale
        o_ref[...] = r.astype(o_ref.dtype)
        off += width
    off = 0
    for (name, width, scale), o_ref in zip(_T_SEGS, t_refs):
        r = _dot_nt(wt_ref[off:off + width, :], h)
        if scale != 1.0:
            r = r * scale
        r = r.astype(o_ref.dtype)
        if name in ("vaT", "vbT"):
            for c in range(_CHUNKS_PER_ROW_TILE):
                o_ref[0, c] = r[:, c * TILE:(c + 1) * TILE]
        else:
            o_ref[0] = r
        off += width


def _proj_call(x2, gain, w_std, w_t, b, t):
    rows = x2.shape[0]
    tiles_per_batch = t // ROW_TILE
    n_tiles = rows // ROW_TILE
    row_spec = lambda width: pl.BlockSpec((ROW_TILE, width), lambda i: (i, 0))
    const_spec = lambda shape: pl.BlockSpec(shape, lambda i: (0,) * len(shape))
    t_spec = lambda width: pl.BlockSpec((1, width, ROW_TILE),
                                        lambda i: (i // tiles_per_batch, 0, i % tiles_per_batch))
    chunk_spec = lambda width: pl.BlockSpec((1, _CHUNKS_PER_ROW_TILE, width, TILE),
                                            lambda i: (i // tiles_per_batch, i % tiles_per_batch, 0, 0))
    out_shape = [jax.ShapeDtypeStruct((rows, w), jnp.bfloat16) for _, w, _ in _STD_SEGS]
    out_specs = [row_spec(w) for _, w, _ in _STD_SEGS]
    for name, w, _ in _T_SEGS:
        if name in ("vaT", "vbT"):
            out_shape.append(jax.ShapeDtypeStruct((b, t // TILE, w, TILE), jnp.bfloat16))
            out_specs.append(chunk_spec(w))
        else:
            out_shape.append(jax.ShapeDtypeStruct((b, w, t), jnp.float32 if name == "iwT" else jnp.bfloat16))
            out_specs.append(t_spec(w))
    out_shape.append(jax.ShapeDtypeStruct((n_tiles, ROW_TILE // MOBA_BLOCK, MOBA_W), jnp.float32))
    out_specs.append(pl.BlockSpec((1, ROW_TILE // MOBA_BLOCK, MOBA_W), lambda i: (i, 0, 0)))
    return pl.pallas_call(
        _proj_kernel,
        grid=(n_tiles,),
        in_specs=[row_spec(D_MODEL), const_spec((1, D_MODEL)), const_spec((D_MODEL, _STD_W)),
                  const_spec((_T_W, D_MODEL))],
        out_specs=out_specs,
        out_shape=out_shape,
        compiler_params=pltpu.CompilerParams(dimension_semantics=("arbitrary",), vmem_limit_bytes=VMEM_LIMIT),
        name="proj",
    )(x2, gain, w_std, w_t)


def _memkv_kernel(mem_ref, g_ref, w_ref, mk_ref, mv_ref):
    h = _rmsnorm_rows(mem_ref[0], g_ref[...]).astype(jnp.bfloat16)
    r = _dot(h, w_ref[...])
    mk_ref[0] = r[:, :MEM_W].astype(jnp.bfloat16)
    mv_ref[0] = r[:, MEM_W:].astype(jnp.bfloat16)


def _memkv_call(mem, gain, w_kv):
    b, n_mem, _ = mem.shape
    return pl.pallas_call(
        _memkv_kernel,
        grid=(b,),
        in_specs=[pl.BlockSpec((1, n_mem, D_MODEL), lambda i: (i, 0, 0)),
                  pl.BlockSpec((1, D_MODEL), lambda i: (0, 0)),
                  pl.BlockSpec((D_MODEL, 2 * MEM_W), lambda i: (0, 0))],
        out_specs=[pl.BlockSpec((1, n_mem, MEM_W), lambda i: (i, 0, 0))] * 2,
        out_shape=[jax.ShapeDtypeStruct((b, n_mem, MEM_W), jnp.bfloat16)] * 2,
        compiler_params=pltpu.CompilerParams(dimension_semantics=("arbitrary",), vmem_limit_bytes=VMEM_LIMIT),
        name="memkv",
    )(mem, gain, w_kv)


def _row_half_mask(shape, half):
    r = lax.broadcasted_iota(jnp.int32, shape, 0)
    return (r >= half * HEAD_DIM) & (r < (half + 1) * HEAD_DIM)


def _with_ones_rows(v_t):
    return jnp.concatenate([v_t, jnp.ones((ONES_ROWS, v_t.shape[1]), v_t.dtype)], axis=0)


def _softmax_stage(s_t, m_ref, alpha_ref, p_ref):
    m_old = m_ref[:1]
    m_new = jnp.maximum(m_old, jnp.max(s_t, axis=0, keepdims=True))
    alpha_ref[...] = jnp.broadcast_to(jnp.exp2(m_old - m_new), alpha_ref.shape)
    m_ref[...] = jnp.broadcast_to(m_new, m_ref.shape)
    p_ref[...] = jnp.exp2(s_t - m_new).astype(jnp.bfloat16)


def _accumulate_stage(v_t, alpha_ref, p_ref, acc_ref):
    acc_ref[...] = alpha_ref[:1] * acc_ref[...] + _dot(_with_ones_rows(v_t), p_ref[...])


def _init_softmax_state(m_ref, alpha_ref, p_ref, acc_ref):
    m_ref[...] = jnp.full(m_ref.shape, NEG_BIG, jnp.float32)
    acc_ref[...] = jnp.zeros(acc_ref.shape, jnp.float32)
    alpha_ref[...] = jnp.ones(alpha_ref.shape, jnp.float32)
    p_ref[...] = jnp.zeros(p_ref.shape, p_ref.dtype)


def _normalized_pair(acc, half_cols):
    out0 = acc[:LANES, :half_cols] / acc[LANES:LANES + 1, :half_cols]
    out1 = acc[:LANES, half_cols:] / acc[LANES:LANES + 1, half_cols:]
    return jnp.where(_row_half_mask((LANES, half_cols), 0), out0, out1).T


_KIND_DIAG, _KIND_PREV, _KIND_FAR = 0, 1, 2


def _run_tile_pipeline(qi, logits, softmax, accumulate):
    n_far = jnp.maximum(qi - 1, 0)
    n_loop = n_far // 2

    def far_pair(t):
        accumulate(0, jnp.maximum(t - 2, 0))
        accumulate(1, jnp.maximum(t - 1, 0))
        softmax(0, t, _KIND_FAR)
        logits(0, t + 2)
        softmax(1, t + 1, _KIND_FAR)
        logits(1, t + 3)

    def quad_body(j, carry):
        far_pair(4 * j)
        far_pair(4 * j + 2)
        return carry

    def pair_body(j, carry):
        far_pair(2 * j)
        return carry

    logits(0, 0)
    logits(1, jnp.minimum(1, qi))
    lax.fori_loop(0, n_loop // 2, quad_body, 0)
    lax.fori_loop(2 * (n_loop // 2), n_loop, pair_body, 0)
    t0 = 2 * n_loop
    odd = n_far % 2 == 1

    @pl.when(odd)
    def _():
        accumulate(0, jnp.maximum(t0 - 2, 0))
        accumulate(1, jnp.maximum(t0 - 1, 0))
        softmax(0, t0, _KIND_FAR)
        logits(0, qi)
        softmax(1, qi - 1, _KIND_PREV)
        accumulate(0, t0)
        softmax(0, qi, _KIND_DIAG)
        accumulate(1, qi - 1)
        accumulate(0, qi)

    @pl.when(jnp.logical_not(odd) & (qi >= 1))
    def _():
        accumulate(0, jnp.maximum(t0 - 2, 0))
        accumulate(1, jnp.maximum(t0 - 1, 0))
        softmax(0, qi - 1, _KIND_PREV)
        softmax(1, qi, _KIND_DIAG)
        accumulate(0, qi - 1)
        accumulate(1, qi)

    @pl.when(qi == 0)
    def _():
        softmax(0, 0, _KIND_DIAG)
        accumulate(0, 0)


def _moba_kernel(bias_ref, qt_ref, k_ref, vt_ref, kmean_ref, bkt_ref, o_ref,
                 qa_ref, tab_ref, s_ref, p_ref, alpha_ref, m_ref, acc_ref):
    pair = pl.program_id(1)
    qi = pl.program_id(2)
    q_t = qt_ref[0]
    kmean = kmean_ref[0].astype(jnp.bfloat16)
    blk = lax.broadcasted_iota(jnp.int32, (FEAT_LO, TILE), 0)
    blk_f = blk.astype(jnp.float32)
    past = blk < qi
    lane = lax.broadcasted_iota(jnp.int32, (TILE, LANES), 1)
    _init_softmax_state(m_ref, alpha_ref, p_ref, acc_ref)

    @pl.when(qi == 0)
    def _():
        for half in range(2):
            for kind in (_KIND_DIAG, _KIND_PREV):
                tab_ref[kind, :, half * TILE:(half + 1) * TILE] = _build_bias_tile(
                    bkt_ref[kind], bias_ref, pair * 2 + half)

    for half in range(2):
        qh = jnp.where(_row_half_mask((LANES, TILE), half), q_t, jnp.zeros_like(q_t))
        gate = _dot(kmean, qh)[:FEAT_LO]
        g = jnp.where(past, gate, -jnp.inf)
        sel = jnp.zeros((FEAT_LO, TILE), jnp.bool_)
        for _ in range(MOBA_TOPK):
            mx = jnp.max(g, axis=0, keepdims=True)
            first = jnp.min(jnp.where(g == mx, blk_f, float(LANES)), axis=0, keepdims=True)
            pick = blk_f == first
            sel = sel | (pick & past)
            g = jnp.where(pick, -jnp.inf, g)
        b31 = jnp.full((FEAT_LO, TILE), bias_ref[(REL_BUCKETS - 1) * N_BIAS_HEADS + pair * 2 + half], jnp.float32)
        b31_hi = b31.astype(jnp.bfloat16).astype(jnp.float32)
        hi_part = jnp.where(sel, jnp.where(blk < qi - 1, b31_hi, 0.0), NEG_BIG)
        hi_part = jnp.where(past, hi_part, 0.0)
        lo_part = jnp.where(blk < qi - 1, b31 - b31_hi, 0.0)
        feat = jnp.concatenate([hi_part, lo_part, jnp.zeros((LANES - 2 * FEAT_LO, TILE), jnp.float32)], axis=0)
        qa_ref[:, half * TILE:(half + 1) * TILE] = jnp.concatenate([qh, feat.astype(jnp.bfloat16)], axis=0)

    def logits(slot, kj):
        k_t = k_ref[0, pl.ds(pl.multiple_of(kj * TILE, TILE), TILE), :]
        onehot = ((lane == kj) | (lane == kj + FEAT_LO)).astype(jnp.bfloat16)
        s_ref[slot] = _dot(jnp.concatenate([k_t, onehot], axis=1), qa_ref[...])

    def softmax(slot, kj, kind):
        s_t = s_ref[slot]
        if kind != _KIND_FAR:
            s_t = s_t + tab_ref[kind]
        _softmax_stage(s_t, m_ref, alpha_ref.at[slot], p_ref.at[slot])

    def accumulate(slot, kj):
        _accumulate_stage(vt_ref[0, kj], alpha_ref.at[slot], p_ref.at[slot], acc_ref)

    _run_tile_pipeline(qi, logits, softmax, accumulate)
    o_ref[0] = _normalized_pair(acc_ref[...], TILE).astype(o_ref.dtype)


def _moba_call(bias_flat, q_t, k, v_t, kmean, bkt):
    b, _, t = q_t.shape
    n_pairs = MOBA_HEADS // 2
    n_chunks = t // TILE
    grid_spec = pltpu.PrefetchScalarGridSpec(
        num_scalar_prefetch=1,
        grid=(b, n_pairs, n_chunks),
        in_specs=[
            pl.BlockSpec((1, LANES, TILE), lambda bi, p, qi, s: (bi, p, qi)),
            pl.BlockSpec((1, t, LANES), lambda bi, p, qi, s: (bi, 0, p)),
            pl.BlockSpec((1, n_chunks, LANES, TILE), lambda bi, p, qi, s: (bi, 0, p, 0)),
            pl.BlockSpec((1, LANES, LANES), lambda bi, p, qi, s: (bi, 0, p)),
            pl.BlockSpec((2, TILE, TILE), lambda bi, p, qi, s: (0, 0, 0)),
        ],
        out_specs=pl.BlockSpec((1, TILE, LANES), lambda bi, p, qi, s: (bi, qi, p)),
        scratch_shapes=[
            pltpu.VMEM((2 * LANES, 2 * TILE), jnp.bfloat16),
            pltpu.VMEM((2, TILE, 2 * TILE), jnp.float32),
            pltpu.VMEM((2, TILE, 2 * TILE), jnp.float32),
            pltpu.VMEM((2, TILE, 2 * TILE), jnp.bfloat16),
            pltpu.VMEM((2, SUBLANES, 2 * TILE), jnp.float32),
            pltpu.VMEM((SUBLANES, 2 * TILE), jnp.float32),
            pltpu.VMEM((ACC_ROWS, 2 * TILE), jnp.float32),
        ],
    )
    return pl.pallas_call(
        _moba_kernel,
        grid_spec=grid_spec,
        out_shape=jax.ShapeDtypeStruct((b, t, MOBA_W), jnp.bfloat16),
        compiler_params=pltpu.CompilerParams(dimension_semantics=("arbitrary",) * 3, vmem_limit_bytes=VMEM_LIMIT),
        name="moba",
    )(bias_flat, q_t, k, v_t, kmean, bkt)


def _dsa_kernel(bias_ref, qt_ref, k_ref, vt_ref, iqt_ref, ik_ref, iwt_ref, bkt_ref, o_ref,
                sc_ref, tab_ref, qz_ref, s_ref, p_ref, alpha_ref, seen_ref, m_ref, acc_ref):
    qi = pl.program_id(1)
    n_chunks = qi + 1
    key = lax.broadcasted_iota(jnp.int32, (TILE, TILE), 0)
    qry = lax.broadcasted_iota(jnp.int32, (TILE, TILE), 1)
    causal = key <= qry

    far_bias = [bias_ref[(REL_BUCKETS - 1) * N_BIAS_HEADS + MOBA_HEADS + h] for h in range(DSA_HEADS)]

    @pl.when(qi == 0)
    def _():
        for h in range(DSA_HEADS):
            for kind in (_KIND_DIAG, _KIND_PREV):
                tab_ref[h // 2, kind, :, (h % 2) * TILE:(h % 2 + 1) * TILE] = _build_bias_tile(
                    bkt_ref[kind], bias_ref, MOBA_HEADS + h) - far_bias[h]

    feat_row = lax.broadcasted_iota(jnp.int32, (LANES, TILE), 0)
    for h in range(DSA_HEADS):
        blk = qt_ref[0, (h // 2) * LANES:(h // 2 + 1) * LANES, :]
        qh = jnp.where(_row_half_mask((LANES, TILE), h % 2), blk, jnp.zeros_like(blk))
        b31 = jnp.full((LANES, TILE), far_bias[h], jnp.float32)
        b31_hi = b31.astype(jnp.bfloat16).astype(jnp.float32)
        feat = jnp.where(feat_row == 0, b31_hi, jnp.where(feat_row == 1, b31 - b31_hi, 0.0))
        qz_ref[h // 2, :, (h % 2) * TILE:(h % 2 + 1) * TILE] = jnp.concatenate(
            [qh, feat.astype(jnp.bfloat16)], axis=0)

    iw_t = iwt_ref[0]

    def score_chunk(kc):
        row0 = pl.multiple_of(kc * TILE, TILE)
        ik_t = ik_ref[0, pl.ds(row0, TILE), :]
        tot = jnp.zeros((TILE, TILE), jnp.float32)
        for h in range(IDX_HEADS):
            x = _dot(ik_t, iqt_ref[0, h * IDX_DIM:(h + 1) * IDX_DIM, :])
            tot = tot + iw_t[h:h + 1, :] * jnp.maximum(x, 0.0)
        return tot

    def score_body(kc, carry):
        mx, mn = carry
        tot = score_chunk(kc)
        sc_ref[kc] = tot
        return (jnp.maximum(mx, jnp.max(tot, axis=0, keepdims=True)),
                jnp.minimum(mn, jnp.min(tot, axis=0, keepdims=True)))

    def score_pair_body(j, carry):
        return score_body(2 * j + 1, score_body(2 * j, carry))

    extremes = lax.fori_loop(0, qi // 2, score_pair_body, (jnp.full((1, TILE), -jnp.inf, jnp.float32),
                                                           jnp.full((1, TILE), jnp.inf, jnp.float32)))
    row_max, row_min = lax.cond(qi % 2 == 1, lambda c: score_body(qi - 1, c), lambda c: c, extremes)
    tot = score_chunk(qi)
    sc_ref[qi] = jnp.where(causal, tot, -jnp.inf)
    row_max = jnp.maximum(row_max, jnp.max(jnp.where(causal, tot, -jnp.inf), axis=0, keepdims=True))
    row_min = jnp.minimum(row_min, jnp.min(jnp.where(causal, tot, jnp.inf), axis=0, keepdims=True))

    @pl.when(qi % 2 == 0)
    def _():
        sc_ref[qi + 1] = jnp.full((TILE, TILE), -jnp.inf, jnp.float32)

    def reduce_pass(fn, init, combine):
        def body(j, part):
            return combine(part, combine(fn(sc_ref[2 * j]), fn(sc_ref[2 * j + 1])))
        return lax.fori_loop(0, (n_chunks + 1) // 2, body, jnp.full((1, TILE), init, jnp.float32))

    def count(pred):
        return reduce_pass(lambda x: jnp.sum(jnp.where(pred(x), 1.0, 0.0), axis=0, keepdims=True), 0.0, jnp.add)

    def min_ge(p):
        return reduce_pass(lambda x: jnp.min(jnp.where(x >= p, x, jnp.inf), axis=0, keepdims=True),
                           jnp.inf, jnp.minimum)

    n_adm =(qi * TILE + 1 + lax.broadcasted_iota(jnp.int32, (1, TILE), 1)).astype(jnp.float32)
    k_top = float(DSA_TOPK)
    take_all = n_adm <= k_top

    def search_cond(st):
        return jnp.logical_and(st[0] < MAX_SEARCH_ITERS, st[-1] >= some_live)

    all_done, some_live, some_stalled = 0.0, 1.0, 2.0

    def live_flag(done, stall):
        return jnp.max(jnp.where(done < 0.5, jnp.where(stall >= STALL_STEPS, some_stalled, some_live), all_done))

    def tie_check(st):
        it, lo, hi, hi_fin, c_lo, c_hi, done, last, streak, stall, _ = st
        a = min_ge(lo)
        c_gt = count(lambda x: x > a)
        tie = (c_gt < k_top) & (done < 0.5)
        done = jnp.where(tie, 1.0, done)
        stall = jnp.zeros_like(stall)
        return (it, lo, hi, hi_fin, c_lo, jnp.where(tie, c_gt, c_hi), done, last, streak, stall,
                live_flag(done, stall))

    def search_step(st):
        it, lo, hi, hi_fin, c_lo, c_hi, done, last, streak, stall = st
        width = c_lo - c_hi
        target = k_top - 0.5 + jnp.where(streak >= 1.0, last * STALE_LEAN * (c_lo - c_hi), 0.0)
        target = jnp.clip(target, c_hi + 0.5, c_lo - 0.5)
        frac = (c_lo - target) / jnp.maximum(c_lo - c_hi, 1.0)
        mid = 0.5 * lo + 0.5 * hi_fin
        p = jnp.where(it % BISECT_EVERY == BISECT_EVERY - 1, mid, lo + frac * (hi_fin - lo))
        p = jnp.where((p > lo) & (p < hi), p, mid)
        stuck = jnp.logical_not((mid > lo) & (mid < hi))
        c = count(lambda x: x >= p)
        up = c >= k_top
        move = jnp.logical_not((done > 0.5) | stuck)
        rise = move & up
        fall = move & jnp.logical_not(up)
        lo = jnp.where(rise, p, lo)
        c_lo = jnp.where(rise, c, c_lo)
        hi = jnp.where(fall, p, hi)
        c_hi = jnp.where(fall, c, c_hi)
        hi_fin = jnp.where(fall, p, hi_fin)
        side = jnp.where(up, -1.0, 1.0)
        streak = jnp.where(side == last, streak + 1.0, 0.0)
        done = jnp.where((c_lo - c_hi <= 1.0) | (c_lo == k_top) | stuck, 1.0, done)
        stall = jnp.where(c_lo - c_hi == width, stall + 1.0, 0.0)
        return it + 1, lo, hi, hi_fin, c_lo, c_hi, done, side, streak, stall

    def search_body(st):
        st = search_step(search_step(search_step(search_step(st[:-1]))))
        flag = live_flag(st[6], st[9])
        return lax.cond(flag >= some_stalled, tie_check, lambda s: s, st + (flag,))

    zeros_row = jnp.zeros((1, TILE), jnp.float32)
    done0 = take_all.astype(jnp.float32)
    init = (jnp.int32(0), row_min, jnp.full((1, TILE), jnp.inf, jnp.float32), row_max,
            n_adm, zeros_row, done0, zeros_row, zeros_row, zeros_row, live_flag(done0, zeros_row))
    _, lo, _, _, _, c_hi, _, _, _, _, _ = lax.while_loop(search_cond, search_body, init)
    thr = jnp.where(take_all, -jnp.inf, min_ge(lo))
    need = k_top - jnp.where(take_all, 0.0, c_hi)

    n_pairs = DSA_HEADS // 2
    for pr in range(n_pairs):
        _init_softmax_state(m_ref.at[pr], alpha_ref.at[:, pr], p_ref.at[:, pr], acc_ref.at[pr])
    seen_ref[...] = jnp.zeros(seen_ref.shape, jnp.float32)
    lower = (qry < key).astype(jnp.bfloat16)
    ones_feat = (lax.broadcasted_iota(jnp.int32, (TILE, LANES), 1) < 2).astype(jnp.bfloat16)

    def logits(slot, kc):
        row0 = pl.multiple_of(kc * TILE, TILE)
        for pr in range(n_pairs):
            k_t = k_ref[0, pl.ds(row0, TILE), pr * LANES:(pr + 1) * LANES]
            s_ref[slot, pr] = _dot(jnp.concatenate([k_t, ones_feat], axis=1), qz_ref[pr])

    def softmax(slot, kc, kind):
        x = sc_ref[kc]
        tie = x == thr
        tie_f = jnp.where(tie, 1.0, 0.0)
        seen = seen_ref[:1]
        rank = seen + _dot(lower, tie_f.astype(jnp.bfloat16))
        selected = (x > thr) | (tie & (rank < need))
        if kind == _KIND_DIAG:
            selected = selected & causal
        mask_add = jnp.where(selected, 0.0, NEG_BIG)
        mask_add = jnp.concatenate([mask_add, mask_add], axis=1)
        seen_ref[...] = jnp.broadcast_to(seen + jnp.sum(tie_f, axis=0, keepdims=True), seen_ref.shape)
        for pr in range(n_pairs):
            s_t = s_ref[slot, pr] + mask_add
            if kind != _KIND_FAR:
                s_t = s_t + tab_ref[pr, kind]
            _softmax_stage(s_t, m_ref.at[pr], alpha_ref.at[slot, pr], p_ref.at[slot, pr])

    def accumulate(slot, kc):
        for pr in range(n_pairs):
            _accumulate_stage(vt_ref[0, kc, pr * LANES:(pr + 1) * LANES, :], alpha_ref.at[slot, pr],
                              p_ref.at[slot, pr], acc_ref.at[pr])

    _run_tile_pipeline(qi, logits, softmax, accumulate)
    for pr in range(n_pairs):
        o_ref[0, :, pr * LANES:(pr + 1) * LANES] = _normalized_pair(acc_ref[pr], TILE).astype(o_ref.dtype)


def _dsa_call(bias_flat, q_t, k, v_t, iq_t, ik, iw_t, bkt):
    b, _, t = q_t.shape
    n_chunks = t // TILE
    n_pairs = DSA_HEADS // 2
    grid_spec = pltpu.PrefetchScalarGridSpec(
        num_scalar_prefetch=1,
        grid=(b, n_chunks),
        in_specs=[
            pl.BlockSpec((1, DSA_W, TILE), lambda bi, qi, s: (bi, 0, qi)),
            pl.BlockSpec((1, t, DSA_W), lambda bi, qi, s: (bi, 0, 0)),
            pl.BlockSpec((1, n_chunks, DSA_W, TILE), lambda bi, qi, s: (bi, 0, 0, 0)),
            pl.BlockSpec((1, IDX_W, TILE), lambda bi, qi, s: (bi, 0, qi)),
            pl.BlockSpec((1, t, IDX_DIM), lambda bi, qi, s: (bi, 0, 0)),
            pl.BlockSpec((1, IDX_HEADS, TILE), lambda bi, qi, s: (bi, 0, qi)),
            pl.BlockSpec((2, TILE, TILE), lambda bi, qi, s: (0, 0, 0)),
        ],
        out_specs=pl.BlockSpec((1, TILE, DSA_W), lambda bi, qi, s: (bi, qi, 0)),
        scratch_shapes=[
            pltpu.VMEM((n_chunks, TILE, TILE), jnp.float32),
            pltpu.VMEM((n_pairs, 2, TILE, 2 * TILE), jnp.float32),
            pltpu.VMEM((n_pairs, 2 * LANES, 2 * TILE), jnp.bfloat16),
            pltpu.VMEM((2, n_pairs, TILE, 2 * TILE), jnp.float32),
            pltpu.VMEM((2, n_pairs, TILE, 2 * TILE), jnp.bfloat16),
            pltpu.VMEM((2, n_pairs, SUBLANES, 2 * TILE), jnp.float32),
            pltpu.VMEM((SUBLANES, TILE), jnp.float32),
            pltpu.VMEM((n_pairs, SUBLANES, 2 * TILE), jnp.float32),
            pltpu.VMEM((n_pairs, ACC_ROWS, 2 * TILE), jnp.float32),
        ],
    )
    return pl.pallas_call(
        _dsa_kernel,
        grid_spec=grid_spec,
        out_shape=jax.ShapeDtypeStruct((b, t, DSA_W), jnp.bfloat16),
        compiler_params=pltpu.CompilerParams(dimension_semantics=("arbitrary",) * 2, vmem_limit_bytes=VMEM_LIMIT),
        name="dsa",
    )(bias_flat, q_t, k, v_t, iq_t, ik, iw_t, bkt)


def _lane_half_mask(shape, half):
    lane = lax.broadcasted_iota(jnp.int32, shape, 1)
    return (lane >= half * HEAD_DIM) & (lane < (half + 1) * HEAD_DIM)


def _final_kernel(x_ref, g_ref, fg_ref, ya_ref, za_ref, yb_ref, zb_ref, qm_ref, zm_ref, mk_ref, mv_ref,
                  wg_ref, wb_ref, wo_ref, o_ref):
    x = x_ref[...]
    h = _rmsnorm_rows(x, g_ref[...]).astype(jnp.bfloat16)

    ym_parts = []
    for pr in range(MEM_HEADS // 2):
        qp = qm_ref[:, pr * LANES:(pr + 1) * LANES]
        mk = mk_ref[0, :, pr * LANES:(pr + 1) * LANES]
        mv = mv_ref[0, :, pr * LANES:(pr + 1) * LANES]
        outs = []
        for half in range(2):
            qh = jnp.where(_lane_half_mask((ROW_TILE, LANES), half), qp, jnp.zeros_like(qp))
            s = _dot_nt(qh, mk)
            p = jnp.exp2(s - jnp.max(s, axis=1, keepdims=True))
            outs.append(_dot(p.astype(jnp.bfloat16), mv) / jnp.sum(p, axis=1, keepdims=True))
        ym_parts.append(jnp.where(_lane_half_mask((ROW_TILE, LANES), 0), outs[0], outs[1]))
    ym = jnp.concatenate(ym_parts, axis=1)

    def gated(y, z_ref):
        z = z_ref[...].astype(jnp.float32)
        return (y.astype(jnp.float32) * (z * jax.nn.sigmoid(z))).astype(jnp.bfloat16)

    ua = gated(ya_ref[...], za_ref)
    ub = gated(yb_ref[...], zb_ref)
    um = gated(ym, zm_ref)
    merged = jax.nn.sigmoid(_dot(h, wg_ref[:, :D_MODEL])) * _dot(ua, wb_ref[:MOBA_W])
    merged += jax.nn.sigmoid(_dot(h, wg_ref[:, D_MODEL:2 * D_MODEL])) * _dot(ub, wb_ref[MOBA_W:MOBA_W + DSA_W])
    merged += jax.nn.sigmoid(_dot(h, wg_ref[:, 2 * D_MODEL:])) * _dot(um, wb_ref[MOBA_W + DSA_W:])
    y = x + _dot(merged.astype(jnp.bfloat16), wo_ref[...])
    o_ref[...] = _rmsnorm_rows(y, fg_ref[...])


def _final_call(x2, gain, final_gain, ya, za, yb, zb, qm, zm, mk, mv, w_gates, w_branch, w_out, rows_per_batch):
    rows = x2.shape[0]
    tiles_per_batch = rows_per_batch // ROW_TILE
    row_spec = lambda width: pl.BlockSpec((ROW_TILE, width), lambda i: (i, 0))
    const_spec = lambda shape: pl.BlockSpec(shape, lambda i: (0,) * len(shape))
    mem_spec = pl.BlockSpec((1,) + mk.shape[1:], lambda i: (i // tiles_per_batch, 0, 0))
    return pl.pallas_call(
        _final_kernel,
        grid=(rows // ROW_TILE,),
        in_specs=[row_spec(D_MODEL), const_spec((1, D_MODEL)), const_spec((1, D_MODEL)),
                  row_spec(MOBA_W), row_spec(MOBA_W), row_spec(DSA_W), row_spec(DSA_W),
                  row_spec(MEM_W), row_spec(MEM_W), mem_spec, mem_spec,
                  const_spec(w_gates.shape), const_spec(w_branch.shape), const_spec(w_out.shape)],
        out_specs=row_spec(D_MODEL),
        out_shape=jax.ShapeDtypeStruct((rows, D_MODEL), jnp.float32),
        compiler_params=pltpu.CompilerParams(dimension_semantics=("arbitrary",), vmem_limit_bytes=VMEM_LIMIT),
        name="final",
    )(x2, gain, final_gain, ya, za, yb, zb, qm, zm, mk, mv, w_gates, w_branch, w_out)


def _layer(x, mem, norm_gain, w_in, rel_bias, mem_norm_gain, w_mem_kv, w_branch, w_out, final_gain):
    b, t, _ = x.shape
    assert t % (2 * TILE) == 0 and t // MOBA_BLOCK <= FEAT_LO and t >= 4 * DSA_TOPK
    bf = jnp.bfloat16
    names = ("qa", "ka", "va", "za", "qb", "kb", "vb", "zb", "iq", "ik", "iw", "qm", "zm", "ga", "gb", "gm")
    cols, off = {}, 0
    for name, size in zip(names, IN_SIZES):
        cols[name] = w_in[:, off:off + size]
        off += size
    w_std = jnp.concatenate([cols[n] for n, _, _ in _STD_SEGS], axis=1).astype(bf)
    w_t = jnp.concatenate([cols[n[:-1]] for n, _, _ in _T_SEGS], axis=1).T.astype(bf)
    w_gates = jnp.concatenate([cols["ga"], cols["gb"], cols["gm"]], axis=1).astype(bf)

    x2 = x.reshape(b * t, D_MODEL)
    (ka, za, kb, zb, qm, zm, ik, qa_t, va_t, qb_t, vb_t, iq_t, iw_t, kmean) = _proj_call(
        x2, norm_gain.reshape(1, D_MODEL), w_std, w_t, b, t)
    seq = lambda a: a.reshape(b, t, a.shape[-1])
    n_blocks = t // MOBA_BLOCK
    kmean = jnp.pad(kmean.reshape(b, n_blocks, MOBA_W), ((0, 0), (0, LANES - n_blocks), (0, 0)))

    bias_flat = (rel_bias * LOG2E).reshape(-1)
    ya = _moba_call(bias_flat, qa_t, seq(ka), va_t, kmean, _bucket_tiles(True))
    yb = _dsa_call(bias_flat, qb_t, seq(kb), vb_t, iq_t, seq(ik), iw_t, _bucket_tiles(False))
    mk, mv = _memkv_call(mem, mem_norm_gain.reshape(1, D_MODEL), w_mem_kv.astype(bf))
    out = _final_call(x2, norm_gain.reshape(1, D_MODEL), final_gain.reshape(1, D_MODEL),
                      ya.reshape(b * t, MOBA_W), za, yb.reshape(b * t, DSA_W), zb, qm, zm, mk, mv,
                      w_gates, w_branch.astype(bf), w_out.astype(bf), t)
    return out.reshape(b, t, D_MODEL)


@jax.jit
def kernel(x, mem, norm_gain, w_in, rel_bias, mem_norm_gain, w_mem_kv, w_branch, w_out, final_norm_gain):
    assert norm_gain.shape[0] == 1, "one layer"
    return _layer(x, mem, norm_gain[0], w_in[0], rel_bias, mem_norm_gain[0], w_mem_kv[0], w_branch[0],
                  w_out[0], final_norm_gain)
```

```python
import math

import numpy as np
import jax
import jax.numpy as jnp
from jax import lax
from jax.experimental import pallas as pl
from jax.experimental.pallas import tpu as pltpu

D_MODEL = 1024
HEAD_DIM = 64
MOBA_HEADS = 6
DSA_HEADS = 6
MEM_HEADS = 4
MOBA_W = MOBA_HEADS * HEAD_DIM
DSA_W = DSA_HEADS * HEAD_DIM
MEM_W = MEM_HEADS * HEAD_DIM
IDX_HEADS = 8
IDX_DIM = 64
IDX_W = IDX_HEADS * IDX_DIM
MOBA_BLOCK = 256
MOBA_TOPK = 3
DSA_TOPK = 256
REL_BUCKETS = 32
REL_MAX_DIST = 128
N_BIAS_HEADS = MOBA_HEADS + DSA_HEADS
EPS = 1e-6
IN_SIZES = (MOBA_W,) * 4 + (DSA_W,) * 4 + (IDX_W, IDX_DIM, IDX_HEADS) + (MEM_W, MEM_W) + (D_MODEL,) * 3

LANES = 128
SUBLANES = 8
ONES_ROWS = 16
ACC_ROWS = LANES + ONES_ROWS
TILE = 256
ROW_TILE = 512
FEAT_LO = 32
LOG2E = math.log2(math.e)
NEG_BIG = -1e30
MASK_BUCKET = REL_BUCKETS
VMEM_LIMIT = 56 * 1024 * 1024
MAX_SEARCH_ITERS = 96
BISECT_EVERY = 5
STALE_LEAN = 0.25
STALL_STEPS = 3.0

_NT = (((1,), (1,)), ((), ()))


def _dot_nt(a, b):
    return lax.dot_general(a, b, _NT, preferred_element_type=jnp.float32)


def _dot(a, b):
    return jnp.dot(a, b, preferred_element_type=jnp.float32)


def _rel_bucket_np(n):
    exact = REL_BUCKETS // 2
    nf = np.maximum(n, 1).astype(np.float32)
    large = exact + (np.log(nf / np.float32(exact)) / np.float32(math.log(REL_MAX_DIST / exact))
                     * np.float32(REL_BUCKETS - exact)).astype(np.int32)
    return np.where(n < exact, n, np.minimum(large, REL_BUCKETS - 1))


def _bucket_tiles(mask_future):
    j = np.arange(TILE)[:, None]
    i = np.arange(TILE)[None, :]
    d0 = i - j
    b0 = _rel_bucket_np(np.maximum(d0, 0))
    if mask_future:
        b0 = np.where(d0 >= 0, b0, MASK_BUCKET)
    b1 = _rel_bucket_np(TILE + d0)
    assert int(_rel_bucket_np(np.array([TILE + 1]))[0]) == REL_BUCKETS - 1
    return jnp.asarray(np.stack([b0, b1]).astype(np.int32))


def _build_bias_tile(bkt, bias_ref, head):
    acc = jnp.where(bkt == MASK_BUCKET, NEG_BIG, 0.0)
    for b in range(REL_BUCKETS):
        acc = jnp.where(bkt == b, bias_ref[b * N_BIAS_HEADS + head], acc)
    return acc


def _rmsnorm_rows(x, g):
    xf = x.astype(jnp.float32)
    return xf * lax.rsqrt(jnp.mean(xf * xf, axis=-1, keepdims=True) + EPS) * g


_Q_SCALE = HEAD_DIM ** -0.5 * LOG2E
_STD_SEGS = (("ka", MOBA_W, 1.0), ("za", MOBA_W, 1.0), ("kb", DSA_W, 1.0), ("zb", DSA_W, 1.0),
             ("qm", MEM_W, _Q_SCALE), ("zm", MEM_W, 1.0), ("ik", IDX_DIM, 1.0))
_T_SEGS = (("qaT", MOBA_W, _Q_SCALE), ("vaT", MOBA_W, 1.0), ("qbT", DSA_W, _Q_SCALE), ("vbT", DSA_W, 1.0),
           ("iqT", IDX_W, IDX_DIM ** -0.5), ("iwT", IDX_HEADS, IDX_HEADS ** -0.5))
_STD_W = sum(w for _, w, _ in _STD_SEGS)
_T_W = sum(w for _, w, _ in _T_SEGS)
_CHUNKS_PER_ROW_TILE = ROW_TILE // TILE


def _proj_kernel(x_ref, g_ref, w_ref, wt_ref, *out_refs):
    std_refs = out_refs[:len(_STD_SEGS)]
    t_refs = out_refs[len(_STD_SEGS):len(_STD_SEGS) + len(_T_SEGS)]
    kmean_ref = out_refs[-1]
    h = _rmsnorm_rows(x_ref[...], g_ref[...]).astype(jnp.bfloat16)
    off = 0
    for (name, width, scale), o_ref in zip(_STD_SEGS, std_refs):
        r = _dot(h, w_ref[:, off:off + width])
        if name == "ka":
            for blk in range(ROW_TILE // MOBA_BLOCK):
                kmean_ref[0, blk:blk + 1, :] = jnp.mean(
                    r[blk * MOBA_BLOCK:(blk + 1) * MOBA_BLOCK], axis=0, keepdims=True)
        if scale != 1.0:
            r = r * scale
        o_ref[...] = r.astype(o_ref.dtype)
        off += width
    off = 0
    for (name, width, scale), o_ref in zip(_T_SEGS, t_refs):
        r = _dot_nt(wt_ref[off:off + width, :], h)
        if scale != 1.0:
            r = r * scale
        r = r.astype(o_ref.dtype)
        if name in ("vaT", "vbT"):
            for c in range(_CHUNKS_PER_ROW_TILE):
                o_ref[0, c] = r[:, c * TILE:(c + 1) * TILE]
        else:
            o_ref[0] = r
        off += width


def _proj_call(x2, gain, w_std, w_t, b, t):
    rows = x2.shape[0]
    tiles_per_batch = t // ROW_TILE
    n_tiles = rows // ROW_TILE
    row_spec = lambda width: pl.BlockSpec((ROW_TILE, width), lambda i: (i, 0))
    const_spec = lambda shape: pl.BlockSpec(shape, lambda i: (0,) * len(shape))
    t_spec = lambda width: pl.BlockSpec((1, width, ROW_TILE),
                                        lambda i: (i // tiles_per_batch, 0, i % tiles_per_batch))
    chunk_spec = lambda width: pl.BlockSpec((1, _CHUNKS_PER_ROW_TILE, width, TILE),
                                            lambda i: (i // tiles_per_batch, i % tiles_per_batch, 0, 0))
    out_shape = [jax.ShapeDtypeStruct((rows, w), jnp.bfloat16) for _, w, _ in _STD_SEGS]
    out_specs = [row_spec(w) for _, w, _ in _STD_SEGS]
    for name, w, _ in _T_SEGS:
        if name in ("vaT", "vbT"):
            out_shape.append(jax.ShapeDtypeStruct((b, t // TILE, w, TILE), jnp.bfloat16))
            out_specs.append(chunk_spec(w))
        else:
            out_shape.append(jax.ShapeDtypeStruct((b, w, t), jnp.float32 if name == "iwT" else jnp.bfloat16))
            out_specs.append(t_spec(w))
    out_shape.append(jax.ShapeDtypeStruct((n_tiles, ROW_TILE // MOBA_BLOCK, MOBA_W), jnp.float32))
    out_specs.append(pl.BlockSpec((1, ROW_TILE // MOBA_BLOCK, MOBA_W), lambda i: (i, 0, 0)))
    return pl.pallas_call(
        _proj_kernel,
        grid=(n_tiles,),
        in_specs=[row_spec(D_MODEL), const_spec((1, D_MODEL)), const_spec((D_MODEL, _STD_W)),
                  const_spec((_T_W, D_MODEL))],
        out_specs=out_specs,
        out_shape=out_shape,
        compiler_params=pltpu.CompilerParams(dimension_semantics=("arbitrary",), vmem_limit_bytes=VMEM_LIMIT),
        name="proj",
    )(x2, gain, w_std, w_t)


def _memkv_kernel(mem_ref, g_ref, w_ref, mk_ref, mv_ref):
    h = _rmsnorm_rows(mem_ref[0], g_ref[...]).astype(jnp.bfloat16)
    r = _dot(h, w_ref[...])
    mk_ref[0] = r[:, :MEM_W].astype(jnp.bfloat16)
    mv_ref[0] = r[:, MEM_W:].astype(jnp.bfloat16)


def _memkv_call(mem, gain, w_kv):
    b, n_mem, _ = mem.shape
    return pl.pallas_call(
        _memkv_kernel,
        grid=(b,),
        in_specs=[pl.BlockSpec((1, n_mem, D_MODEL), lambda i: (i, 0, 0)),
                  pl.BlockSpec((1, D_MODEL), lambda i: (0, 0)),
                  pl.BlockSpec((D_MODEL, 2 * MEM_W), lambda i: (0, 0))],
        out_specs=[pl.BlockSpec((1, n_mem, MEM_W), lambda i: (i, 0, 0))] * 2,
        out_shape=[jax.ShapeDtypeStruct((b, n_mem, MEM_W), jnp.bfloat16)] * 2,
        compiler_params=pltpu.CompilerParams(dimension_semantics=("arbitrary",), vmem_limit_bytes=VMEM_LIMIT),
        name="memkv",
    )(mem, gain, w_kv)


def _row_half_mask(shape, half):
    r = lax.broadcasted_iota(jnp.int32, shape, 0)
    return (r >= half * HEAD_DIM) & (r < (half + 1) * HEAD_DIM)


def _with_ones_rows(v_t):
    return jnp.concatenate([v_t, jnp.ones((ONES_ROWS, v_t.shape[1]), v_t.dtype)], axis=0)


def _softmax_stage(s_t, m_ref, alpha_ref, p_ref):
    m_old = m_ref[:1]
    m_new = jnp.maximum(m_old, jnp.max(s_t, axis=0, keepdims=True))
    alpha_ref[...] = jnp.broadcast_to(jnp.exp2(m_old - m_new), alpha_ref.shape)
    m_ref[...] = jnp.broadcast_to(m_new, m_ref.shape)
    p_ref[...] = jnp.exp2(s_t - m_new).astype(jnp.bfloat16)


def _accumulate_stage(v_t, alpha_ref, p_ref, acc_ref):
    acc_ref[...] = alpha_ref[:1] * acc_ref[...] + _dot(_with_ones_rows(v_t), p_ref[...])


def _init_softmax_state(m_ref, alpha_ref, p_ref, acc_ref):
    m_ref[...] = jnp.full(m_ref.shape, NEG_BIG, jnp.float32)
    acc_ref[...] = jnp.zeros(acc_ref.shape, jnp.float32)
    alpha_ref[...] = jnp.ones(alpha_ref.shape, jnp.float32)
    p_ref[...] = jnp.zeros(p_ref.shape, p_ref.dtype)


def _normalized_pair(acc, half_cols):
    out0 = acc[:LANES, :half_cols] / acc[LANES:LANES + 1, :half_cols]
    out1 = acc[:LANES, half_cols:] / acc[LANES:LANES + 1, half_cols:]
    return jnp.where(_row_half_mask((LANES, half_cols), 0), out0, out1).T


_KIND_DIAG, _KIND_PREV, _KIND_FAR = 0, 1, 2


def _run_tile_pipeline(qi, logits, softmax, accumulate):
    n_far = jnp.maximum(qi - 1, 0)
    n_loop = n_far // 2

    def far_pair(t):
        accumulate(0, jnp.maximum(t - 2, 0))
        accumulate(1, jnp.maximum(t - 1, 0))
        softmax(0, t, _KIND_FAR)
        logits(0, t + 2)
        softmax(1, t + 1, _KIND_FAR)
        logits(1, t + 3)

    def quad_body(j, carry):
        far_pair(4 * j)
        far_pair(4 * j + 2)
        return carry

    def pair_body(j, carry):
        far_pair(2 * j)
        return carry

    logits(0, 0)
    logits(1, jnp.minimum(1, qi))
    lax.fori_loop(0, n_loop // 2, quad_body, 0)
    lax.fori_loop(2 * (n_loop // 2), n_loop, pair_body, 0)
    t0 = 2 * n_loop
    odd = n_far % 2 == 1

    @pl.when(odd)
    def _():
        accumulate(0, jnp.maximum(t0 - 2, 0))
        accumulate(1, jnp.maximum(t0 - 1, 0))
        softmax(0, t0, _KIND_FAR)
        logits(0, qi)
        softmax(1, qi - 1, _KIND_PREV)
        accumulate(0, t0)
        softmax(0, qi, _KIND_DIAG)
        accumulate(1, qi - 1)
        accumulate(0, qi)

    @pl.when(jnp.logical_not(odd) & (qi >= 1))
    def _():
        accumulate(0, jnp.maximum(t0 - 2, 0))
        accumulate(1, jnp.maximum(t0 - 1, 0))
        softmax(0, qi - 1, _KIND_PREV)
        softmax(1, qi, _KIND_DIAG)
        accumulate(0, qi - 1)
        accumulate(1, qi)

    @pl.when(qi == 0)
    def _():
        softmax(0, 0, _KIND_DIAG)
        accumulate(0, 0)


def _moba_kernel(bias_ref, qt_ref, k_ref, vt_ref, kmean_ref, bkt_ref, o_ref,
                 qa_ref, tab_ref, s_ref, p_ref, alpha_ref, m_ref, acc_ref):
    pair = pl.program_id(1)
    qi = pl.program_id(2)
    q_t = qt_ref[0]
    kmean = kmean_ref[0].astype(jnp.bfloat16)
    blk = lax.broadcasted_iota(jnp.int32, (FEAT_LO, TILE), 0)
    blk_f = blk.astype(jnp.float32)
    past = blk < qi
    lane = lax.broadcasted_iota(jnp.int32, (TILE, LANES), 1)
    _init_softmax_state(m_ref, alpha_ref, p_ref, acc_ref)

    @pl.when(qi == 0)
    def _():
        for half in range(2):
            for kind in (_KIND_DIAG, _KIND_PREV):
                tab_ref[kind, :, half * TILE:(half + 1) * TILE] = _build_bias_tile(
                    bkt_ref[kind], bias_ref, pair * 2 + half)

    for half in range(2):
        qh = jnp.where(_row_half_mask((LANES, TILE), half), q_t, jnp.zeros_like(q_t))
        gate = _dot(kmean, qh)[:FEAT_LO]
        g = jnp.where(past, gate, -jnp.inf)
        sel = jnp.zeros((FEAT_LO, TILE), jnp.bool_)
        for _ in range(MOBA_TOPK):
            mx = jnp.max(g, axis=0, keepdims=True)
            first = jnp.min(jnp.where(g == mx, blk_f, float(LANES)), axis=0, keepdims=True)
            pick = blk_f == first
            sel = sel | (pick & past)
            g = jnp.where(pick, -jnp.inf, g)
        b31 = jnp.full((FEAT_LO, TILE), bias_ref[(REL_BUCKETS - 1) * N_BIAS_HEADS + pair * 2 + half], jnp.float32)
        b31_hi = b31.astype(jnp.bfloat16).astype(jnp.float32)
        hi_part = jnp.where(sel, jnp.where(blk < qi - 1, b31_hi, 0.0), NEG_BIG)
        hi_part = jnp.where(past, hi_part, 0.0)
        lo_part = jnp.where(blk < qi - 1, b31 - b31_hi, 0.0)
        feat = jnp.concatenate([hi_part, lo_part, jnp.zeros((LANES - 2 * FEAT_LO, TILE), jnp.float32)], axis=0)
        qa_ref[:, half * TILE:(half + 1) * TILE] = jnp.concatenate([qh, feat.astype(jnp.bfloat16)], axis=0)

    def logits(slot, kj):
        k_t = k_ref[0, pl.ds(pl.multiple_of(kj * TILE, TILE), TILE), :]
        onehot = ((lane == kj) | (lane == kj + FEAT_LO)).astype(jnp.bfloat16)
        s_ref[slot] = _dot(jnp.concatenate([k_t, onehot], axis=1), qa_ref[...])

    def softmax(slot, kj, kind):
        s_t = s_ref[slot]
        if kind != _KIND_FAR:
            s_t = s_t + tab_ref[kind]
        _softmax_stage(s_t, m_ref, alpha_ref.at[slot], p_ref.at[slot])

    def accumulate(slot, kj):
        _accumulate_stage(vt_ref[0, kj], alpha_ref.at[slot], p_ref.at[slot], acc_ref)

    _run_tile_pipeline(qi, logits, softmax, accumulate)
    o_ref[0] = _normalized_pair(acc_ref[...], TILE).astype(o_ref.dtype)


def _moba_call(bias_flat, q_t, k, v_t, kmean, bkt):
    b, _, t = q_t.shape
    n_pairs = MOBA_HEADS // 2
    n_chunks = t // TILE
    grid_spec = pltpu.PrefetchScalarGridSpec(
        num_scalar_prefetch=1,
        grid=(b, n_pairs, n_chunks),
        in_specs=[
            pl.BlockSpec((1, LANES, TILE), lambda bi, p, qi, s: (bi, p, qi)),
            pl.BlockSpec((1, t, LANES), lambda bi, p, qi, s: (bi, 0, p)),
            pl.BlockSpec((1, n_chunks, LANES, TILE), lambda bi, p, qi, s: (bi, 0, p, 0)),
            pl.BlockSpec((1, LANES, LANES), lambda bi, p, qi, s: (bi, 0, p)),
            pl.BlockSpec((2, TILE, TILE), lambda bi, p, qi, s: (0, 0, 0)),
        ],
        out_specs=pl.BlockSpec((1, TILE, LANES), lambda bi, p, qi, s: (bi, qi, p)),
        scratch_shapes=[
            pltpu.VMEM((2 * LANES, 2 * TILE), jnp.bfloat16),
            pltpu.VMEM((2, TILE, 2 * TILE), jnp.float32),
            pltpu.VMEM((2, TILE, 2 * TILE), jnp.float32),
            pltpu.VMEM((2, TILE, 2 * TILE), jnp.bfloat16),
            pltpu.VMEM((2, SUBLANES, 2 * TILE), jnp.float32),
            pltpu.VMEM((SUBLANES, 2 * TILE), jnp.float32),
            pltpu.VMEM((ACC_ROWS, 2 * TILE), jnp.float32),
        ],
    )
    return pl.pallas_call(
        _moba_kernel,
        grid_spec=grid_spec,
        out_shape=jax.ShapeDtypeStruct((b, t, MOBA_W), jnp.bfloat16),
        compiler_params=pltpu.CompilerParams(dimension_semantics=("arbitrary",) * 3, vmem_limit_bytes=VMEM_LIMIT),
        name="moba",
    )(bias_flat, q_t, k, v_t, kmean, bkt)


def _dsa_kernel(bias_ref, qt_ref, k_ref, vt_ref, iqt_ref, ik_ref, iwt_ref, bkt_ref, o_ref,
                sc_ref, tab_ref, qz_ref, s_ref, p_ref, alpha_ref, seen_ref, m_ref, acc_ref):
    qi = pl.program_id(1)
    n_chunks = qi + 1
    key = lax.broadcasted_iota(jnp.int32, (TILE, TILE), 0)
    qry = lax.broadcasted_iota(jnp.int32, (TILE, TILE), 1)
    causal = key <= qry

    far_bias = [bias_ref[(REL_BUCKETS - 1) * N_BIAS_HEADS + MOBA_HEADS + h] for h in range(DSA_HEADS)]

    @pl.when(qi == 0)
    def _():
        for h in range(DSA_HEADS):
            for kind in (_KIND_DIAG, _KIND_PREV):
                tab_ref[h // 2, kind, :, (h % 2) * TILE:(h % 2 + 1) * TILE] = _build_bias_tile(
                    bkt_ref[kind], bias_ref, MOBA_HEADS + h) - far_bias[h]

    feat_row = lax.broadcasted_iota(jnp.int32, (LANES, TILE), 0)
    for h in range(DSA_HEADS):
        blk = qt_ref[0, (h // 2) * LANES:(h // 2 + 1) * LANES, :]
        qh = jnp.where(_row_half_mask((LANES, TILE), h % 2), blk, jnp.zeros_like(blk))
        b31 = jnp.full((LANES, TILE), far_bias[h], jnp.float32)
        b31_hi = b31.astype(jnp.bfloat16).astype(jnp.float32)
        feat = jnp.where(feat_row == 0, b31_hi, jnp.where(feat_row == 1, b31 - b31_hi, 0.0))
        qz_ref[h // 2, :, (h % 2) * TILE:(h % 2 + 1) * TILE] = jnp.concatenate(
            [qh, feat.astype(jnp.bfloat16)], axis=0)

    iw_t = iwt_ref[0]

    def score_chunk(kc):
        row0 = pl.multiple_of(kc * TILE, TILE)
        ik_t = ik_ref[0, pl.ds(row0, TILE), :]
        tot = jnp.zeros((TILE, TILE), jnp.float32)
        for h in range(IDX_HEADS):
            x = _dot(ik_t, iqt_ref[0, h * IDX_DIM:(h + 1) * IDX_DIM, :])
            tot = tot + iw_t[h:h + 1, :] * jnp.maximum(x, 0.0)
        return tot

    def score_body(kc, carry):
        mx, mn = carry
        tot = score_chunk(kc)
        sc_ref[kc] = tot
        return (jnp.maximum(mx, jnp.max(tot, axis=0, keepdims=True)),
                jnp.minimum(mn, jnp.min(tot, axis=0, keepdims=True)))

    def score_pair_body(j, carry):
        return score_body(2 * j + 1, score_body(2 * j, carry))

    extremes = lax.fori_loop(0, qi // 2, score_pair_body, (jnp.full((1, TILE), -jnp.inf, jnp.float32),
                                                           jnp.full((1, TILE), jnp.inf, jnp.float32)))
    row_max, row_min = lax.cond(qi % 2 == 1, lambda c: score_body(qi - 1, c), lambda c: c, extremes)
    tot = score_chunk(qi)
    sc_ref[qi] = jnp.where(causal, tot, -jnp.inf)
    row_max = jnp.maximum(row_max, jnp.max(jnp.where(causal, tot, -jnp.inf), axis=0, keepdims=True))
    row_min = jnp.minimum(row_min, jnp.min(jnp.where(causal, tot, jnp.inf), axis=0, keepdims=True))

    @pl.when(qi % 2 == 0)
    def _():
        sc_ref[qi + 1] = jnp.full((TILE, TILE), -jnp.inf, jnp.float32)

    def reduce_pass(fn, init, combine):
        def body(j, part):
            return combine(part, combine(fn(sc_ref[2 * j]), fn(sc_ref[2 * j + 1])))
        return lax.fori_loop(0, (n_chunks + 1) // 2, body, jnp.full((1, TILE), init, jnp.float32))

    def count(pred):
        return reduce_pass(lambda x: jnp.sum(jnp.where(pred(x), 1.0, 0.0), axis=0, keepdims=True), 0.0, jnp.add)

    def min_ge(p):
        return reduce_pass(lambda x: jnp.min(jnp.where(x >= p, x, jnp.inf), axis=0, keepdims=True),
                           jnp.inf, jnp.minimum)

    n_adm =(qi * TILE + 1 + lax.broadcasted_iota(jnp.int32, (1, TILE), 1)).astype(jnp.float32)
    k_top = float(DSA_TOPK)
    take_all = n_adm <= k_top

    def search_cond(st):
        return jnp.logical_and(st[0] < MAX_SEARCH_ITERS, st[-1] >= some_live)

    all_done, some_live, some_stalled = 0.0, 1.0, 2.0

    def live_flag(done, stall):
        return jnp.max(jnp.where(done < 0.5, jnp.where(stall >= STALL_STEPS, some_stalled, some_live), all_done))

    def tie_check(st):
        it, lo, hi, hi_fin, c_lo, c_hi, done, last, streak, stall, _ = st
        a = min_ge(lo)
        c_gt = count(lambda x: x > a)
        tie = (c_gt < k_top) & (done < 0.5)
        done = jnp.where(tie, 1.0, done)
        stall = jnp.zeros_like(stall)
        return (it, lo, hi, hi_fin, c_lo, jnp.where(tie, c_gt, c_hi), done, last, streak, stall,
                live_flag(done, stall))

    def search_step(st):
        it, lo, hi, hi_fin, c_lo, c_hi, done, last, streak, stall = st
        width = c_lo - c_hi
        target = k_top - 0.5 + jnp.where(streak >= 1.0, last * STALE_LEAN * (c_lo - c_hi), 0.0)
        target = jnp.clip(target, c_hi + 0.5, c_lo - 0.5)
        frac = (c_lo - target) / jnp.maximum(c_lo - c_hi, 1.0)
        mid = 0.5 * lo + 0.5 * hi_fin
        p = jnp.where(it % BISECT_EVERY == BISECT_EVERY - 1, mid, lo + frac * (hi_fin - lo))
        p = jnp.where((p > lo) & (p < hi), p, mid)
        stuck = jnp.logical_not((mid > lo) & (mid < hi))
        c = count(lambda x: x >= p)
        up = c >= k_top
        move = jnp.logical_not((done > 0.5) | stuck)
        rise = move & up
        fall = move & jnp.logical_not(up)
        lo = jnp.where(rise, p, lo)
        c_lo = jnp.where(rise, c, c_lo)
        hi = jnp.where(fall, p, hi)
        c_hi = jnp.where(fall, c, c_hi)
        hi_fin = jnp.where(fall, p, hi_fin)
        side = jnp.where(up, -1.0, 1.0)
        streak = jnp.where(side == last, streak + 1.0, 0.0)
        done = jnp.where((c_lo - c_hi <= 1.0) | (c_lo == k_top) | stuck, 1.0, done)
        stall = jnp.where(c_lo - c_hi == width, stall + 1.0, 0.0)
        return it + 1, lo, hi, hi_fin, c_lo, c_hi, done, side, streak, stall

    def search_body(st):
        st = search_step(search_step(search_step(st[:-1])))
        flag = live_flag(st[6], st[9])
        st = search_step(st)
        return lax.cond(flag >= some_stalled, tie_check, lambda s: s, st + (flag,))

    zeros_row = jnp.zeros((1, TILE), jnp.float32)
    done0 = take_all.astype(jnp.float32)
    init = (jnp.int32(0), row_min, jnp.full((1, TILE), jnp.inf, jnp.float32), row_max,
            n_adm, zeros_row, done0, zeros_row, zeros_row, zeros_row, live_flag(done0, zeros_row))
    _, lo, _, _, _, c_hi, _, _, _, _, _ = lax.while_loop(search_cond, search_body, init)
    thr = jnp.where(take_all, -jnp.inf, min_ge(lo))
    need = k_top - jnp.where(take_all, 0.0, c_hi)

    n_pairs = DSA_HEADS // 2
    for pr in range(n_pairs):
        _init_softmax_state(m_ref.at[pr], alpha_ref.at[:, pr], p_ref.at[:, pr], acc_ref.at[pr])
    seen_ref[...] = jnp.zeros(seen_ref.shape, jnp.float32)
    lower = (qry < key).astype(jnp.bfloat16)
    ones_feat = (lax.broadcasted_iota(jnp.int32, (TILE, LANES), 1) < 2).astype(jnp.bfloat16)

    def logits(slot, kc):
        row0 = pl.multiple_of(kc * TILE, TILE)
        for pr in range(n_pairs):
            k_t = k_ref[0, pl.ds(row0, TILE), pr * LANES:(pr + 1) * LANES]
            s_ref[slot, pr] = _dot(jnp.concatenate([k_t, ones_feat], axis=1), qz_ref[pr])

    def softmax(slot, kc, kind):
        x = sc_ref[kc]
        tie = x == thr
        tie_f = jnp.where(tie, 1.0, 0.0)
        seen = seen_ref[:1]
        rank = seen + _dot(lower, tie_f.astype(jnp.bfloat16))
        selected = (x > thr) | (tie & (rank < need))
        if kind == _KIND_DIAG:
            selected = selected & causal
        mask_add = jnp.where(selected, 0.0, NEG_BIG)
        mask_add = jnp.concatenate([mask_add, mask_add], axis=1)
        seen_ref[...] = jnp.broadcast_to(seen + jnp.sum(tie_f, axis=0, keepdims=True), seen_ref.shape)
        for pr in range(n_pairs):
            s_t = s_ref[slot, pr] + mask_add
            if kind != _KIND_FAR:
                s_t = s_t + tab_ref[pr, kind]
            _softmax_stage(s_t, m_ref.at[pr], alpha_ref.at[slot, pr], p_ref.at[slot, pr])

    def accumulate(slot, kc):
        for pr in range(n_pairs):
            _accumulate_stage(vt_ref[0, kc, pr * LANES:(pr + 1) * LANES, :], alpha_ref.at[slot, pr],
                              p_ref.at[slot, pr], acc_ref.at[pr])

    _run_tile_pipeline(qi, logits, softmax, accumulate)
    for pr in range(n_pairs):
        o_ref[0, :, pr * LANES:(pr + 1) * LANES] = _normalized_pair(acc_ref[pr], TILE).astype(o_ref.dtype)


def _dsa_call(bias_flat, q_t, k, v_t, iq_t, ik, iw_t, bkt):
    b, _, t = q_t.shape
    n_chunks = t // TILE
    n_pairs = DSA_HEADS // 2
    grid_spec = pltpu.PrefetchScalarGridSpec(
        num_scalar_prefetch=1,
        grid=(b, n_chunks),
        in_specs=[
            pl.BlockSpec((1, DSA_W, TILE), lambda bi, qi, s: (bi, 0, qi)),
            pl.BlockSpec((1, t, DSA_W), lambda bi, qi, s: (bi, 0, 0)),
            pl.BlockSpec((1, n_chunks, DSA_W, TILE), lambda bi, qi, s: (bi, 0, 0, 0)),
            pl.BlockSpec((1, IDX_W, TILE), lambda bi, qi, s: (bi, 0, qi)),
            pl.BlockSpec((1, t, IDX_DIM), lambda bi, qi, s: (bi, 0, 0)),
            pl.BlockSpec((1, IDX_HEADS, TILE), lambda bi, qi, s: (bi, 0, qi)),
            pl.BlockSpec((2, TILE, TILE), lambda bi, qi, s: (0, 0, 0)),
        ],
        out_specs=pl.BlockSpec((1, TILE, DSA_W), lambda bi, qi, s: (bi, qi, 0)),
        scratch_shapes=[
            pltpu.VMEM((n_chunks, TILE, TILE), jnp.float32),
            pltpu.VMEM((n_pairs, 2, TILE, 2 * TILE), jnp.float32),
            pltpu.VMEM((n_pairs, 2 * LANES, 2 * TILE), jnp.bfloat16),
            pltpu.VMEM((2, n_pairs, TILE, 2 * TILE), jnp.float32),
            pltpu.VMEM((2, n_pairs, TILE, 2 * TILE), jnp.bfloat16),
            pltpu.VMEM((2, n_pairs, SUBLANES, 2 * TILE), jnp.float32),
            pltpu.VMEM((SUBLANES, TILE), jnp.float32),
            pltpu.VMEM((n_pairs, SUBLANES, 2 * TILE), jnp.float32),
            pltpu.VMEM((n_pairs, ACC_ROWS, 2 * TILE), jnp.float32),
        ],
    )
    return pl.pallas_call(
        _dsa_kernel,
        grid_spec=grid_spec,
        out_shape=jax.ShapeDtypeStruct((b, t, DSA_W), jnp.bfloat16),
        compiler_params=pltpu.CompilerParams(dimension_semantics=("arbitrary",) * 2, vmem_limit_bytes=VMEM_LIMIT),
        name="dsa",
    )(bias_flat, q_t, k, v_t, iq_t, ik, iw_t, bkt)


def _lane_half_mask(shape, half):
    lane = lax.broadcasted_iota(jnp.int32, shape, 1)
    return (lane >= half * HEAD_DIM) & (lane < (half + 1) * HEAD_DIM)


def _final_kernel(x_ref, g_ref, fg_ref, ya_ref, za_ref, yb_ref, zb_ref, qm_ref, zm_ref, mk_ref, mv_ref,
                  wg_ref, wb_ref, wo_ref, o_ref):
    x = x_ref[...]
    h = _rmsnorm_rows(x, g_ref[...]).astype(jnp.bfloat16)

    ym_parts = []
    for pr in range(MEM_HEADS // 2):
        qp = qm_ref[:, pr * LANES:(pr + 1) * LANES]
        mk = mk_ref[0, :, pr * LANES:(pr + 1) * LANES]
        mv = mv_ref[0, :, pr * LANES:(pr + 1) * LANES]
        outs = []
        for half in range(2):
            qh = jnp.where(_lane_half_mask((ROW_TILE, LANES), half), qp, jnp.zeros_like(qp))
            s = _dot_nt(qh, mk)
            p = jnp.exp2(s - jnp.max(s, axis=1, keepdims=True))
            outs.append(_dot(p.astype(jnp.bfloat16), mv) / jnp.sum(p, axis=1, keepdims=True))
        ym_parts.append(jnp.where(_lane_half_mask((ROW_TILE, LANES), 0), outs[0], outs[1]))
    ym = jnp.concatenate(ym_parts, axis=1)

    def gated(y, z_ref):
        z = z_ref[...].astype(jnp.float32)
        return (y.astype(jnp.float32) * (z * jax.nn.sigmoid(z))).astype(jnp.bfloat16)

    ua = gated(ya_ref[...], za_ref)
    ub = gated(yb_ref[...], zb_ref)
    um = gated(ym, zm_ref)
    merged = jax.nn.sigmoid(_dot(h, wg_ref[:, :D_MODEL])) * _dot(ua, wb_ref[:MOBA_W])
    merged += jax.nn.sigmoid(_dot(h, wg_ref[:, D_MODEL:2 * D_MODEL])) * _dot(ub, wb_ref[MOBA_W:MOBA_W + DSA_W])
    merged += jax.nn.sigmoid(_dot(h, wg_ref[:, 2 * D_MODEL:])) * _dot(um, wb_ref[MOBA_W + DSA_W:])
    y = x + _dot(merged.astype(jnp.bfloat16), wo_ref[...])
    o_ref[...] = _rmsnorm_rows(y, fg_ref[...])


def _final_call(x2, gain, final_gain, ya, za, yb, zb, qm, zm, mk, mv, w_gates, w_branch, w_out, rows_per_batch):
    rows = x2.shape[0]
    tiles_per_batch = rows_per_batch // ROW_TILE
    row_spec = lambda width: pl.BlockSpec((ROW_TILE, width), lambda i: (i, 0))
    const_spec = lambda shape: pl.BlockSpec(shape, lambda i: (0,) * len(shape))
    mem_spec = pl.BlockSpec((1,) + mk.shape[1:], lambda i: (i // tiles_per_batch, 0, 0))
    return pl.pallas_call(
        _final_kernel,
        grid=(rows // ROW_TILE,),
        in_specs=[row_spec(D_MODEL), const_spec((1, D_MODEL)), const_spec((1, D_MODEL)),
                  row_spec(MOBA_W), row_spec(MOBA_W), row_spec(DSA_W), row_spec(DSA_W),
                  row_spec(MEM_W), row_spec(MEM_W), mem_spec, mem_spec,
                  const_spec(w_gates.shape), const_spec(w_branch.shape), const_spec(w_out.shape)],
        out_specs=row_spec(D_MODEL),
        out_shape=jax.ShapeDtypeStruct((rows, D_MODEL), jnp.float32),
        compiler_params=pltpu.CompilerParams(dimension_semantics=("arbitrary",), vmem_limit_bytes=VMEM_LIMIT),
        name="final",
    )(x2, gain, final_gain, ya, za, yb, zb, qm, zm, mk, mv, w_gates, w_branch, w_out)


def _layer(x, mem, norm_gain, w_in, rel_bias, mem_norm_gain, w_mem_kv, w_branch, w_out, final_gain):
    b, t, _ = x.shape
    assert t % (2 * TILE) == 0 and t // MOBA_BLOCK <= FEAT_LO and t >= 4 * DSA_TOPK
    bf = jnp.bfloat16
    names = ("qa", "ka", "va", "za", "qb", "kb", "vb", "zb", "iq", "ik", "iw", "qm", "zm", "ga", "gb", "gm")
    cols, off = {}, 0
    for name, size in zip(names, IN_SIZES):
        cols[name] = w_in[:, off:off + size]
        off += size
    w_std = jnp.concatenate([cols[n] for n, _, _ in _STD_SEGS], axis=1).astype(bf)
    w_t = jnp.concatenate([cols[n[:-1]] for n, _, _ in _T_SEGS], axis=1).T.astype(bf)
    w_gates = jnp.concatenate([cols["ga"], cols["gb"], cols["gm"]], axis=1).astype(bf)

    x2 = x.reshape(b * t, D_MODEL)
    (ka, za, kb, zb, qm, zm, ik, qa_t, va_t, qb_t, vb_t, iq_t, iw_t, kmean) = _proj_call(
        x2, norm_gain.reshape(1, D_MODEL), w_std, w_t, b, t)
    seq = lambda a: a.reshape(b, t, a.shape[-1])
    n_blocks = t // MOBA_BLOCK
    kmean = jnp.pad(kmean.reshape(b, n_blocks, MOBA_W), ((0, 0), (0, LANES - n_blocks), (0, 0)))

    bias_flat = (rel_bias * LOG2E).reshape(-1)
    ya = _moba_call(bias_flat, qa_t, seq(ka), va_t, kmean, _bucket_tiles(True))
    yb = _dsa_call(bias_flat, qb_t, seq(kb), vb_t, iq_t, seq(ik), iw_t, _bucket_tiles(False))
    mk, mv = _memkv_call(mem, mem_norm_gain.reshape(1, D_MODEL), w_mem_kv.astype(bf))
    out = _final_call(x2, norm_gain.reshape(1, D_MODEL), final_gain.reshape(1, D_MODEL),
                      ya.reshape(b * t, MOBA_W), za, yb.reshape(b * t, DSA_W), zb, qm, zm, mk, mv,
                      w_gates, w_branch.astype(bf), w_out.astype(bf), t)
    return out.reshape(b, t, D_MODEL)


@jax.jit
def kernel(x, mem, norm_gain, w_in, rel_bias, mem_norm_gain, w_mem_kv, w_branch, w_out, final_norm_gain):
    assert norm_gain.shape[0] == 1, "one layer"
    return _layer(x, mem, norm_gain[0], w_in[0], rel_bias, mem_norm_gain[0], w_mem_kv[0], w_branch[0],
                  w_out[0], final_norm_gain)
```

```python
import math

import numpy as np
import jax
import jax.numpy as jnp
from jax import lax
from jax.experimental import pallas as pl
from jax.experimental.pallas import tpu as pltpu

D_MODEL = 1024
HEAD_DIM = 64
MOBA_HEADS = 6
DSA_HEADS = 6
MEM_HEADS = 4
MOBA_W = MOBA_HEADS * HEAD_DIM
DSA_W = DSA_HEADS * HEAD_DIM
MEM_W = MEM_HEADS * HEAD_DIM
IDX_HEADS = 8
IDX_DIM = 64
IDX_W = IDX_HEADS * IDX_DIM
MOBA_BLOCK = 256
MOBA_TOPK = 3
DSA_TOPK = 256
REL_BUCKETS = 32
REL_MAX_DIST = 128
N_BIAS_HEADS = MOBA_HEADS + DSA_HEADS
EPS = 1e-6
IN_SIZES = (MOBA_W,) * 4 + (DSA_W,) * 4 + (IDX_W, IDX_DIM, IDX_HEADS) + (MEM_W, MEM_W) + (D_MODEL,) * 3

LANES = 128
SUBLANES = 8
ONES_ROWS = 16
ACC_ROWS = LANES + ONES_ROWS
TILE = 256
ROW_TILE = 512
FEAT_LO = 32
LOG2E = math.log2(math.e)
NEG_BIG = -1e30
LOGIT_DTYPE = jnp.bfloat16
MASK_BUCKET = REL_BUCKETS
VMEM_LIMIT = 56 * 1024 * 1024
MAX_SEARCH_ITERS = 96
BISECT_EVERY = 5
STALE_LEAN = 0.25
STALL_STEPS = 3.0

_NT = (((1,), (1,)), ((), ()))


def _dot_nt(a, b):
    return lax.dot_general(a, b, _NT, preferred_element_type=jnp.float32)


def _dot(a, b):
    return jnp.dot(a, b, preferred_element_type=jnp.float32)


def _rel_bucket_np(n):
    exact = REL_BUCKETS // 2
    nf = np.maximum(n, 1).astype(np.float32)
    large = exact + (np.log(nf / np.float32(exact)) / np.float32(math.log(REL_MAX_DIST / exact))
                     * np.float32(REL_BUCKETS - exact)).astype(np.int32)
    return np.where(n < exact, n, np.minimum(large, REL_BUCKETS - 1))


def _bucket_tiles(mask_future):
    j = np.arange(TILE)[:, None]
    i = np.arange(TILE)[None, :]
    d0 = i - j
    b0 = _rel_bucket_np(np.maximum(d0, 0))
    if mask_future:
        b0 = np.where(d0 >= 0, b0, MASK_BUCKET)
    b1 = _rel_bucket_np(TILE + d0)
    assert int(_rel_bucket_np(np.array([TILE + 1]))[0]) == REL_BUCKETS - 1
    return jnp.asarray(np.stack([b0, b1]).astype(np.int32))


def _build_bias_tile(bkt, bias_ref, head):
    acc = jnp.where(bkt == MASK_BUCKET, NEG_BIG, 0.0)
    for b in range(REL_BUCKETS):
        acc = jnp.where(bkt == b, bias_ref[b * N_BIAS_HEADS + head], acc)
    return acc


def _rmsnorm_rows(x, g):
    xf = x.astype(jnp.float32)
    return xf * lax.rsqrt(jnp.mean(xf * xf, axis=-1, keepdims=True) + EPS) * g


_Q_SCALE = HEAD_DIM ** -0.5 * LOG2E
_STD_SEGS = (("ka", MOBA_W, 1.0), ("za", MOBA_W, 1.0), ("kb", DSA_W, 1.0), ("zb", DSA_W, 1.0),
             ("qm", MEM_W, _Q_SCALE), ("zm", MEM_W, 1.0), ("ik", IDX_DIM, 1.0))
_T_SEGS = (("qaT", MOBA_W, _Q_SCALE), ("vaT", MOBA_W, 1.0), ("qbT", DSA_W, _Q_SCALE), ("vbT", DSA_W, 1.0),
           ("iqT", IDX_W, IDX_DIM ** -0.5), ("iwT", IDX_HEADS, IDX_HEADS ** -0.5))
_STD_W = sum(w for _, w, _ in _STD_SEGS)
_T_W = sum(w for _, w, _ in _T_SEGS)
_CHUNKS_PER_ROW_TILE = ROW_TILE // TILE


def _proj_kernel(x_ref, g_ref, w_ref, wt_ref, *out_refs):
    std_refs = out_refs[:len(_STD_SEGS)]
    t_refs = out_refs[len(_STD_SEGS):len(_STD_SEGS) + len(_T_SEGS)]
    kmean_ref = out_refs[-1]
    h = _rmsnorm_rows(x_ref[...], g_ref[...]).astype(jnp.bfloat16)
    off = 0
    for (name, width, scale), o_ref in zip(_STD_SEGS, std_refs):
        r = _dot(h, w_ref[:, off:off + width])
        if name == "ka":
            for blk in range(ROW_TILE // MOBA_BLOCK):
                kmean_ref[0, blk:blk + 1, :] = jnp.mean(
                    r[blk * MOBA_BLOCK:(blk + 1) * MOBA_BLOCK], axis=0, keepdims=True)
        if scale != 1.0:
            r = r * scale
        o_ref[...] = r.astype(o_ref.dtype)
        off += width
    off = 0
    for (name, width, scale), o_ref in zip(_T_SEGS, t_refs):
        r = _dot_nt(wt_ref[off:off + width, :], h)
        if scale != 1.0:
            r = r * scale
        r = r.astype(o_ref.dtype)
        if name in ("vaT", "vbT"):
            for c in range(_CHUNKS_PER_ROW_TILE):
                o_ref[0, c] = r[:, c * TILE:(c + 1) * TILE]
        else:
            o_ref[0] = r
        off += width


def _proj_call(x2, gain, w_std, w_t, b, t):
    rows = x2.shape[0]
    tiles_per_batch = t // ROW_TILE
    n_tiles = rows // ROW_TILE
    row_spec = lambda width: pl.BlockSpec((ROW_TILE, width), lambda i: (i, 0))
    const_spec = lambda shape: pl.BlockSpec(shape, lambda i: (0,) * len(shape))
    t_spec = lambda width: pl.BlockSpec((1, width, ROW_TILE),
                                        lambda i: (i // tiles_per_batch, 0, i % tiles_per_batch))
    chunk_spec = lambda width: pl.BlockSpec((1, _CHUNKS_PER_ROW_TILE, width, TILE),
                                            lambda i: (i // tiles_per_batch, i % tiles_per_batch, 0, 0))
    out_shape = [jax.ShapeDtypeStruct((rows, w), jnp.bfloat16) for _, w, _ in _STD_SEGS]
    out_specs = [row_spec(w) for _, w, _ in _STD_SEGS]
    for name, w, _ in _T_SEGS:
        if name in ("vaT", "vbT"):
            out_shape.append(jax.ShapeDtypeStruct((b, t // TILE, w, TILE), jnp.bfloat16))
            out_specs.append(chunk_spec(w))
        else:
            out_shape.append(jax.ShapeDtypeStruct((b, w, t), jnp.float32 if name == "iwT" else jnp.bfloat16))
            out_specs.append(t_spec(w))
    out_shape.append(jax.ShapeDtypeStruct((n_tiles, ROW_TILE // MOBA_BLOCK, MOBA_W), jnp.float32))
    out_specs.append(pl.BlockSpec((1, ROW_TILE // MOBA_BLOCK, MOBA_W), lambda i: (i, 0, 0)))
    return pl.pallas_call(
        _proj_kernel,
        grid=(n_tiles,),
        in_specs=[row_spec(D_MODEL), const_spec((1, D_MODEL)), const_spec((D_MODEL, _STD_W)),
                  const_spec((_T_W, D_MODEL))],
        out_specs=out_specs,
        out_shape=out_shape,
        compiler_params=pltpu.CompilerParams(dimension_semantics=("arbitrary",), vmem_limit_bytes=VMEM_LIMIT),
        name="proj",
    )(x2, gain, w_std, w_t)


def _memkv_kernel(mem_ref, g_ref, w_ref, mk_ref, mv_ref):
    h = _rmsnorm_rows(mem_ref[0], g_ref[...]).astype(jnp.bfloat16)
    r = _dot(h, w_ref[...])
    mk_ref[0] = r[:, :MEM_W].astype(jnp.bfloat16)
    mv_ref[0] = r[:, MEM_W:].astype(jnp.bfloat16)


def _memkv_call(mem, gain, w_kv):
    b, n_mem, _ = mem.shape
    return pl.pallas_call(
        _memkv_kernel,
        grid=(b,),
        in_specs=[pl.BlockSpec((1, n_mem, D_MODEL), lambda i: (i, 0, 0)),
                  pl.BlockSpec((1, D_MODEL), lambda i: (0, 0)),
                  pl.BlockSpec((D_MODEL, 2 * MEM_W), lambda i: (0, 0))],
        out_specs=[pl.BlockSpec((1, n_mem, MEM_W), lambda i: (i, 0, 0))] * 2,
        out_shape=[jax.ShapeDtypeStruct((b, n_mem, MEM_W), jnp.bfloat16)] * 2,
        compiler_params=pltpu.CompilerParams(dimension_semantics=("arbitrary",), vmem_limit_bytes=VMEM_LIMIT),
        name="memkv",
    )(mem, gain, w_kv)


def _row_half_mask(shape, half):
    r = lax.broadcasted_iota(jnp.int32, shape, 0)
    return (r >= half * HEAD_DIM) & (r < (half + 1) * HEAD_DIM)


def _with_ones_rows(v_t):
    return jnp.concatenate([v_t, jnp.ones((ONES_ROWS, v_t.shape[1]), v_t.dtype)], axis=0)


def _softmax_stage(s_t, m_ref, alpha_ref, p_ref):
    m_old = m_ref[:1]
    m_new = jnp.maximum(m_old, jnp.max(s_t, axis=0, keepdims=True).astype(jnp.float32))
    alpha_ref[...] = jnp.broadcast_to(jnp.exp2(m_old - m_new), alpha_ref.shape)
    m_ref[...] = jnp.broadcast_to(m_new, m_ref.shape)
    p_ref[...] = jnp.exp2(s_t - m_new.astype(s_t.dtype)).astype(jnp.bfloat16)


def _accumulate_stage(v_t, alpha_ref, p_ref, acc_ref):
    acc_ref[...] = alpha_ref[:1] * acc_ref[...] + _dot(_with_ones_rows(v_t), p_ref[...])


def _init_softmax_state(m_ref, alpha_ref, p_ref, acc_ref):
    m_ref[...] = jnp.full(m_ref.shape, NEG_BIG, LOGIT_DTYPE).astype(jnp.float32)
    acc_ref[...] = jnp.zeros(acc_ref.shape, jnp.float32)
    alpha_ref[...] = jnp.ones(alpha_ref.shape, jnp.float32)
    p_ref[...] = jnp.zeros(p_ref.shape, p_ref.dtype)


def _normalized_pair(acc, half_cols):
    out0 = acc[:LANES, :half_cols] / acc[LANES:LANES + 1, :half_cols]
    out1 = acc[:LANES, half_cols:] / acc[LANES:LANES + 1, half_cols:]
    return jnp.where(_row_half_mask((LANES, half_cols), 0), out0, out1).T


_KIND_DIAG, _KIND_PREV, _KIND_FAR = 0, 1, 2


def _run_tile_pipeline(qi, logits, softmax, accumulate):
    n_far = jnp.maximum(qi - 1, 0)
    n_loop = n_far // 2

    def far_pair(t):
        accumulate(0, jnp.maximum(t - 2, 0))
        accumulate(1, jnp.maximum(t - 1, 0))
        softmax(0, t, _KIND_FAR)
        logits(0, t + 2)
        softmax(1, t + 1, _KIND_FAR)
        logits(1, t + 3)

    def quad_body(j, carry):
        far_pair(4 * j)
        far_pair(4 * j + 2)
        return carry

    def pair_body(j, carry):
        far_pair(2 * j)
        return carry

    logits(0, 0)
    logits(1, jnp.minimum(1, qi))
    lax.fori_loop(0, n_loop // 2, quad_body, 0)
    lax.fori_loop(2 * (n_loop // 2), n_loop, pair_body, 0)
    t0 = 2 * n_loop
    odd = n_far % 2 == 1

    @pl.when(odd)
    def _():
        accumulate(0, jnp.maximum(t0 - 2, 0))
        accumulate(1, jnp.maximum(t0 - 1, 0))
        softmax(0, t0, _KIND_FAR)
        logits(0, qi)
        softmax(1, qi - 1, _KIND_PREV)
        accumulate(0, t0)
        softmax(0, qi, _KIND_DIAG)
        accumulate(1, qi - 1)
        accumulate(0, qi)

    @pl.when(jnp.logical_not(odd) & (qi >= 1))
    def _():
        accumulate(0, jnp.maximum(t0 - 2, 0))
        accumulate(1, jnp.maximum(t0 - 1, 0))
        softmax(0, qi - 1, _KIND_PREV)
        softmax(1, qi, _KIND_DIAG)
        accumulate(0, qi - 1)
        accumulate(1, qi)

    @pl.when(qi == 0)
    def _():
        softmax(0, 0, _KIND_DIAG)
        accumulate(0, 0)


def _moba_kernel(bias_ref, qt_ref, k_ref, vt_ref, kmean_ref, bkt_ref, o_ref,
                 qa_ref, tab_ref, s_ref, p_ref, alpha_ref, m_ref, acc_ref):
    pair = pl.program_id(1)
    qi = pl.program_id(2)
    q_t = qt_ref[0]
    kmean = kmean_ref[0].astype(jnp.bfloat16)
    blk = lax.broadcasted_iota(jnp.int32, (FEAT_LO, TILE), 0)
    blk_f = blk.astype(jnp.float32)
    past = blk < qi
    lane = lax.broadcasted_iota(jnp.int32, (TILE, LANES), 1)
    _init_softmax_state(m_ref, alpha_ref, p_ref, acc_ref)

    @pl.when(qi == 0)
    def _():
        for half in range(2):
            for kind in (_KIND_DIAG, _KIND_PREV):
                tab_ref[kind, :, half * TILE:(half + 1) * TILE] = _build_bias_tile(
                    bkt_ref[kind], bias_ref, pair * 2 + half).astype(LOGIT_DTYPE)

    for half in range(2):
        qh = jnp.where(_row_half_mask((LANES, TILE), half), q_t, jnp.zeros_like(q_t))
        gate = _dot(kmean, qh)[:FEAT_LO]
        g = jnp.where(past, gate, -jnp.inf)
        sel = jnp.zeros((FEAT_LO, TILE), jnp.bool_)
        for _ in range(MOBA_TOPK):
            mx = jnp.max(g, axis=0, keepdims=True)
            first = jnp.min(jnp.where(g == mx, blk_f, float(LANES)), axis=0, keepdims=True)
            pick = blk_f == first
            sel = sel | (pick & past)
            g = jnp.where(pick, -jnp.inf, g)
        b31 = jnp.full((FEAT_LO, TILE), bias_ref[(REL_BUCKETS - 1) * N_BIAS_HEADS + pair * 2 + half], jnp.float32)
        b31_hi = b31.astype(jnp.bfloat16).astype(jnp.float32)
        hi_part = jnp.where(sel, jnp.where(blk < qi - 1, b31_hi, 0.0), NEG_BIG)
        hi_part = jnp.where(past, hi_part, 0.0)
        lo_part = jnp.where(blk < qi - 1, b31 - b31_hi, 0.0)
        feat = jnp.concatenate([hi_part, lo_part, jnp.zeros((LANES - 2 * FEAT_LO, TILE), jnp.float32)], axis=0)
        qa_ref[:, half * TILE:(half + 1) * TILE] = jnp.concatenate([qh, feat.astype(jnp.bfloat16)], axis=0)

    def logits(slot, kj):
        k_t = k_ref[0, pl.ds(pl.multiple_of(kj * TILE, TILE), TILE), :]
        onehot = ((lane == kj) | (lane == kj + FEAT_LO)).astype(jnp.bfloat16)
        s_ref[slot] = _dot(jnp.concatenate([k_t, onehot], axis=1), qa_ref[...]).astype(LOGIT_DTYPE)

    def softmax(slot, kj, kind):
        s_t = s_ref[slot]
        if kind != _KIND_FAR:
            s_t = s_t + tab_ref[kind]
        _softmax_stage(s_t, m_ref, alpha_ref.at[slot], p_ref.at[slot])

    def accumulate(slot, kj):
        _accumulate_stage(vt_ref[0, kj], alpha_ref.at[slot], p_ref.at[slot], acc_ref)

    _run_tile_pipeline(qi, logits, softmax, accumulate)
    o_ref[0] = _normalized_pair(acc_ref[...], TILE).astype(o_ref.dtype)


def _moba_call(bias_flat, q_t, k, v_t, kmean, bkt):
    b, _, t = q_t.shape
    n_pairs = MOBA_HEADS // 2
    n_chunks = t // TILE
    grid_spec = pltpu.PrefetchScalarGridSpec(
        num_scalar_prefetch=1,
        grid=(b, n_pairs, n_chunks),
        in_specs=[
            pl.BlockSpec((1, LANES, TILE), lambda bi, p, qi, s: (bi, p, qi)),
            pl.BlockSpec((1, t, LANES), lambda bi, p, qi, s: (bi, 0, p)),
            pl.BlockSpec((1, n_chunks, LANES, TILE), lambda bi, p, qi, s: (bi, 0, p, 0)),
            pl.BlockSpec((1, LANES, LANES), lambda bi, p, qi, s: (bi, 0, p)),
            pl.BlockSpec((2, TILE, TILE), lambda bi, p, qi, s: (0, 0, 0)),
        ],
        out_specs=pl.BlockSpec((1, TILE, LANES), lambda bi, p, qi, s: (bi, qi, p)),
        scratch_shapes=[
            pltpu.VMEM((2 * LANES, 2 * TILE), jnp.bfloat16),
            pltpu.VMEM((2, TILE, 2 * TILE), LOGIT_DTYPE),
            pltpu.VMEM((2, TILE, 2 * TILE), LOGIT_DTYPE),
            pltpu.VMEM((2, TILE, 2 * TILE), jnp.bfloat16),
            pltpu.VMEM((2, SUBLANES, 2 * TILE), jnp.float32),
            pltpu.VMEM((SUBLANES, 2 * TILE), jnp.float32),
            pltpu.VMEM((ACC_ROWS, 2 * TILE), jnp.float32),
        ],
    )
    return pl.pallas_call(
        _moba_kernel,
        grid_spec=grid_spec,
        out_shape=jax.ShapeDtypeStruct((b, t, MOBA_W), jnp.bfloat16),
        compiler_params=pltpu.CompilerParams(dimension_semantics=("arbitrary",) * 3, vmem_limit_bytes=VMEM_LIMIT),
        name="moba",
    )(bias_flat, q_t, k, v_t, kmean, bkt)


def _dsa_kernel(bias_ref, qt_ref, k_ref, vt_ref, iqt_ref, ik_ref, iwt_ref, bkt_ref, o_ref,
                sc_ref, tab_ref, qz_ref, s_ref, p_ref, alpha_ref, seen_ref, m_ref, acc_ref):
    qi = pl.program_id(1)
    n_chunks = qi + 1
    key = lax.broadcasted_iota(jnp.int32, (TILE, TILE), 0)
    qry = lax.broadcasted_iota(jnp.int32, (TILE, TILE), 1)
    causal = key <= qry

    far_bias = [bias_ref[(REL_BUCKETS - 1) * N_BIAS_HEADS + MOBA_HEADS + h] for h in range(DSA_HEADS)]

    @pl.when(qi == 0)
    def _():
        for h in range(DSA_HEADS):
            for kind in (_KIND_DIAG, _KIND_PREV):
                tab_ref[h // 2, kind, :, (h % 2) * TILE:(h % 2 + 1) * TILE] = (_build_bias_tile(
                    bkt_ref[kind], bias_ref, MOBA_HEADS + h) - far_bias[h]).astype(LOGIT_DTYPE)

    feat_row = lax.broadcasted_iota(jnp.int32, (LANES, TILE), 0)
    for h in range(DSA_HEADS):
        blk = qt_ref[0, (h // 2) * LANES:(h // 2 + 1) * LANES, :]
        qh = jnp.where(_row_half_mask((LANES, TILE), h % 2), blk, jnp.zeros_like(blk))
        b31 = jnp.full((LANES, TILE), far_bias[h], jnp.float32)
        b31_hi = b31.astype(jnp.bfloat16).astype(jnp.float32)
        feat = jnp.where(feat_row == 0, b31_hi, jnp.where(feat_row == 1, b31 - b31_hi, 0.0))
        qz_ref[h // 2, :, (h % 2) * TILE:(h % 2 + 1) * TILE] = jnp.concatenate(
            [qh, feat.astype(jnp.bfloat16)], axis=0)

    iw_t = iwt_ref[0]

    def score_chunk(kc):
        row0 = pl.multiple_of(kc * TILE, TILE)
        ik_t = ik_ref[0, pl.ds(row0, TILE), :]
        tot = jnp.zeros((TILE, TILE), jnp.float32)
        for h in range(IDX_HEADS):
            x = _dot(ik_t, iqt_ref[0, h * IDX_DIM:(h + 1) * IDX_DIM, :])
            tot = tot + iw_t[h:h + 1, :] * jnp.maximum(x, 0.0)
        return tot

    def score_body(kc, carry):
        mx, mn = carry
        tot = score_chunk(kc)
        sc_ref[kc] = tot
        return (jnp.maximum(mx, jnp.max(tot, axis=0, keepdims=True)),
                jnp.minimum(mn, jnp.min(tot, axis=0, keepdims=True)))

    def score_pair_body(j, carry):
        return score_body(2 * j + 1, score_body(2 * j, carry))

    extremes = lax.fori_loop(0, qi // 2, score_pair_body, (jnp.full((1, TILE), -jnp.inf, jnp.float32),
                                                           jnp.full((1, TILE), jnp.inf, jnp.float32)))
    row_max, row_min = lax.cond(qi % 2 == 1, lambda c: score_body(qi - 1, c), lambda c: c, extremes)
    tot = score_chunk(qi)
    sc_ref[qi] = jnp.where(causal, tot, -jnp.inf)
    row_max = jnp.maximum(row_max, jnp.max(jnp.where(causal, tot, -jnp.inf), axis=0, keepdims=True))
    row_min = jnp.minimum(row_min, jnp.min(jnp.where(causal, tot, jnp.inf), axis=0, keepdims=True))

    @pl.when(qi % 2 == 0)
    def _():
        sc_ref[qi + 1] = jnp.full((TILE, TILE), -jnp.inf, jnp.float32)

    def reduce_pass(fn, init, combine):
        def body(j, part):
            return combine(part, combine(fn(sc_ref[2 * j]), fn(sc_ref[2 * j + 1])))
        return lax.fori_loop(0, (n_chunks + 1) // 2, body, jnp.full((1, TILE), init, jnp.float32))

    def count(pred):
        return reduce_pass(lambda x: jnp.sum(jnp.where(pred(x), 1.0, 0.0), axis=0, keepdims=True), 0.0, jnp.add)

    def min_ge(p):
        return reduce_pass(lambda x: jnp.min(jnp.where(x >= p, x, jnp.inf), axis=0, keepdims=True),
                           jnp.inf, jnp.minimum)

    n_adm =(qi * TILE + 1 + lax.broadcasted_iota(jnp.int32, (1, TILE), 1)).astype(jnp.float32)
    k_top = float(DSA_TOPK)
    take_all = n_adm <= k_top

    def search_cond(st):
        return jnp.logical_and(st[0] < MAX_SEARCH_ITERS, st[-1] >= some_live)

    all_done, some_live, some_stalled = 0.0, 1.0, 2.0

    def live_flag(done, stall):
        return jnp.max(jnp.where(done < 0.5, jnp.where(stall >= STALL_STEPS, some_stalled, some_live), all_done))

    def tie_check(st):
        it, lo, hi, hi_fin, c_lo, c_hi, done, last, streak, stall, _ = st
        a = min_ge(lo)
        c_gt = count(lambda x: x > a)
        tie = (c_gt < k_top) & (done < 0.5)
        done = jnp.where(tie, 1.0, done)
        stall = jnp.zeros_like(stall)
        return (it, lo, hi, hi_fin, c_lo, jnp.where(tie, c_gt, c_hi), done, last, streak, stall,
                live_flag(done, stall))

    def search_step(st):
        it, lo, hi, hi_fin, c_lo, c_hi, done, last, streak, stall = st
        width = c_lo - c_hi
        target = k_top - 0.5 + jnp.where(streak >= 1.0, last * STALE_LEAN * (c_lo - c_hi), 0.0)
        target = jnp.clip(target, c_hi + 0.5, c_lo - 0.5)
        frac = (c_lo - target) / jnp.maximum(c_lo - c_hi, 1.0)
        mid = 0.5 * lo + 0.5 * hi_fin
        p = jnp.where(it % BISECT_EVERY == BISECT_EVERY - 1, mid, lo + frac * (hi_fin - lo))
        p = jnp.where((p > lo) & (p < hi), p, mid)
        stuck = jnp.logical_not((mid > lo) & (mid < hi))
        c = count(lambda x: x >= p)
        up = c >= k_top
        move = jnp.logical_not((done > 0.5) | stuck)
        rise = move & up
        fall = move & jnp.logical_not(up)
        lo = jnp.where(rise, p, lo)
        c_lo = jnp.where(rise, c, c_lo)
        hi = jnp.where(fall, p, hi)
        c_hi = jnp.where(fall, c, c_hi)
        hi_fin = jnp.where(fall, p, hi_fin)
        side = jnp.where(up, -1.0, 1.0)
        streak = jnp.where(side == last, streak + 1.0, 0.0)
        done = jnp.where((c_lo - c_hi <= 1.0) | (c_lo == k_top) | stuck, 1.0, done)
        stall = jnp.where(c_lo - c_hi == width, stall + 1.0, 0.0)
        return it + 1, lo, hi, hi_fin, c_lo, c_hi, done, side, streak, stall

    def search_body(st):
        st = search_step(search_step(search_step(search_step(st[:-1]))))
        flag = live_flag(st[6], st[9])
        return lax.cond(flag >= some_stalled, tie_check, lambda s: s, st + (flag,))

    zeros_row = jnp.zeros((1, TILE), jnp.float32)
    done0 = take_all.astype(jnp.float32)
    init = (jnp.int32(0), row_min, jnp.full((1, TILE), jnp.inf, jnp.float32), row_max,
            n_adm, zeros_row, done0, zeros_row, zeros_row, zeros_row, live_flag(done0, zeros_row))
    _, lo, _, _, _, c_hi, _, _, _, _, _ = lax.while_loop(search_cond, search_body, init)
    thr = jnp.where(take_all, -jnp.inf, min_ge(lo))
    need = k_top - jnp.where(take_all, 0.0, c_hi)

    n_pairs = DSA_HEADS // 2
    for pr in range(n_pairs):
        _init_softmax_state(m_ref.at[pr], alpha_ref.at[:, pr], p_ref.at[:, pr], acc_ref.at[pr])
    seen_ref[...] = jnp.zeros(seen_ref.shape, jnp.float32)
    lower = (qry < key).astype(jnp.bfloat16)
    ones_feat = (lax.broadcasted_iota(jnp.int32, (TILE, LANES), 1) < 2).astype(jnp.bfloat16)

    def logits(slot, kc):
        row0 = pl.multiple_of(kc * TILE, TILE)
        for pr in range(n_pairs):
            k_t = k_ref[0, pl.ds(row0, TILE), pr * LANES:(pr + 1) * LANES]
            s_ref[slot, pr] = _dot(jnp.concatenate([k_t, ones_feat], axis=1), qz_ref[pr]).astype(LOGIT_DTYPE)

    def softmax(slot, kc, kind):
        x = sc_ref[kc]
        tie = x == thr
        tie_f = jnp.where(tie, 1.0, 0.0)
        seen = seen_ref[:1]
        rank = seen + _dot(lower, tie_f.astype(jnp.bfloat16))
        selected = (x > thr) | (tie & (rank < need))
        if kind == _KIND_DIAG:
            selected = selected & causal
        mask_add = jnp.where(selected, 0.0, NEG_BIG).astype(LOGIT_DTYPE)
        mask_add = jnp.concatenate([mask_add, mask_add], axis=1)
        seen_ref[...] = jnp.broadcast_to(seen + jnp.sum(tie_f, axis=0, keepdims=True), seen_ref.shape)
        for pr in range(n_pairs):
            s_t = s_ref[slot, pr] + mask_add
            if kind != _KIND_FAR:
                s_t = s_t + tab_ref[pr, kind]
            _softmax_stage(s_t, m_ref.at[pr], alpha_ref.at[slot, pr], p_ref.at[slot, pr])

    def accumulate(slot, kc):
        for pr in range(n_pairs):
            _accumulate_stage(vt_ref[0, kc, pr * LANES:(pr + 1) * LANES, :], alpha_ref.at[slot, pr],
                              p_ref.at[slot, pr], acc_ref.at[pr])

    _run_tile_pipeline(qi, logits, softmax, accumulate)
    for pr in range(n_pairs):
        o_ref[0, :, pr * LANES:(pr + 1) * LANES] = _normalized_pair(acc_ref[pr], TILE).astype(o_ref.dtype)


def _dsa_call(bias_flat, q_t, k, v_t, iq_t, ik, iw_t, bkt):
    b, _, t = q_t.shape
    n_chunks = t // TILE
    n_pairs = DSA_HEADS // 2
    grid_spec = pltpu.PrefetchScalarGridSpec(
        num_scalar_prefetch=1,
        grid=(b, n_chunks),
        in_specs=[
            pl.BlockSpec((1, DSA_W, TILE), lambda bi, qi, s: (bi, 0, qi)),
            pl.BlockSpec((1, t, DSA_W), lambda bi, qi, s: (bi, 0, 0)),
            pl.BlockSpec((1, n_chunks, DSA_W, TILE), lambda bi, qi, s: (bi, 0, 0, 0)),
            pl.BlockSpec((1, IDX_W, TILE), lambda bi, qi, s: (bi, 0, qi)),
            pl.BlockSpec((1, t, IDX_DIM), lambda bi, qi, s: (bi, 0, 0)),
            pl.BlockSpec((1, IDX_HEADS, TILE), lambda bi, qi, s: (bi, 0, qi)),
            pl.BlockSpec((2, TILE, TILE), lambda bi, qi, s: (0, 0, 0)),
        ],
        out_specs=pl.BlockSpec((1, TILE, DSA_W), lambda bi, qi, s: (bi, qi, 0)),
        scratch_shapes=[
            pltpu.VMEM((n_chunks, TILE, TILE), jnp.float32),
            pltpu.VMEM((n_pairs, 2, TILE, 2 * TILE), LOGIT_DTYPE),
            pltpu.VMEM((n_pairs, 2 * LANES, 2 * TILE), jnp.bfloat16),
            pltpu.VMEM((2, n_pairs, TILE, 2 * TILE), LOGIT_DTYPE),
            pltpu.VMEM((2, n_pairs, TILE, 2 * TILE), jnp.bfloat16),
            pltpu.VMEM((2, n_pairs, SUBLANES, 2 * TILE), jnp.float32),
            pltpu.VMEM((SUBLANES, TILE), jnp.float32),
            pltpu.VMEM((n_pairs, SUBLANES, 2 * TILE), jnp.float32),
            pltpu.VMEM((n_pairs, ACC_ROWS, 2 * TILE), jnp.float32),
        ],
    )
    return pl.pallas_call(
        _dsa_kernel,
        grid_spec=grid_spec,
        out_shape=jax.ShapeDtypeStruct((b, t, DSA_W), jnp.bfloat16),
        compiler_params=pltpu.CompilerParams(dimension_semantics=("arbitrary",) * 2, vmem_limit_bytes=VMEM_LIMIT),
        name="dsa",
    )(bias_flat, q_t, k, v_t, iq_t, ik, iw_t, bkt)


def _lane_half_mask(shape, half):
    lane = lax.broadcasted_iota(jnp.int32, shape, 1)
    return (lane >= half * HEAD_DIM) & (lane < (half + 1) * HEAD_DIM)


def _final_kernel(x_ref, g_ref, fg_ref, ya_ref, za_ref, yb_ref, zb_ref, qm_ref, zm_ref, mk_ref, mv_ref,
                  wg_ref, wb_ref, wo_ref, o_ref):
    x = x_ref[...]
    h = _rmsnorm_rows(x, g_ref[...]).astype(jnp.bfloat16)

    ym_parts = []
    for pr in range(MEM_HEADS // 2):
        qp = qm_ref[:, pr * LANES:(pr + 1) * LANES]
        mk = mk_ref[0, :, pr * LANES:(pr + 1) * LANES]
        mv = mv_ref[0, :, pr * LANES:(pr + 1) * LANES]
        outs = []
        for half in range(2):
            qh = jnp.where(_lane_half_mask((ROW_TILE, LANES), half), qp, jnp.zeros_like(qp))
            s = _dot_nt(qh, mk)
            p = jnp.exp2(s - jnp.max(s, axis=1, keepdims=True))
            outs.append(_dot(p.astype(jnp.bfloat16), mv) / jnp.sum(p, axis=1, keepdims=True))
        ym_parts.append(jnp.where(_lane_half_mask((ROW_TILE, LANES), 0), outs[0], outs[1]))
    ym = jnp.concatenate(ym_parts, axis=1)

    def gated(y, z_ref):
        z = z_ref[...].astype(jnp.float32)
        return (y.astype(jnp.float32) * (z * jax.nn.sigmoid(z))).astype(jnp.bfloat16)

    ua = gated(ya_ref[...], za_ref)
    ub = gated(yb_ref[...], zb_ref)
    um = gated(ym, zm_ref)
    merged = jax.nn.sigmoid(_dot(h, wg_ref[:, :D_MODEL])) * _dot(ua, wb_ref[:MOBA_W])
    merged += jax.nn.sigmoid(_dot(h, wg_ref[:, D_MODEL:2 * D_MODEL])) * _dot(ub, wb_ref[MOBA_W:MOBA_W + DSA_W])
    merged += jax.nn.sigmoid(_dot(h, wg_ref[:, 2 * D_MODEL:])) * _dot(um, wb_ref[MOBA_W + DSA_W:])
    y = x + _dot(merged.astype(jnp.bfloat16), wo_ref[...])
    o_ref[...] = _rmsnorm_rows(y, fg_ref[...])


def _final_call(x2, gain, final_gain, ya, za, yb, zb, qm, zm, mk, mv, w_gates, w_branch, w_out, rows_per_batch):
    rows = x2.shape[0]
    tiles_per_batch = rows_per_batch // ROW_TILE
    row_spec = lambda width: pl.BlockSpec((ROW_TILE, width), lambda i: (i, 0))
    const_spec = lambda shape: pl.BlockSpec(shape, lambda i: (0,) * len(shape))
    mem_spec = pl.BlockSpec((1,) + mk.shape[1:], lambda i: (i // tiles_per_batch, 0, 0))
    return pl.pallas_call(
        _final_kernel,
        grid=(rows // ROW_TILE,),
        in_specs=[row_spec(D_MODEL), const_spec((1, D_MODEL)), const_spec((1, D_MODEL)),
                  row_spec(MOBA_W), row_spec(MOBA_W), row_spec(DSA_W), row_spec(DSA_W),
                  row_spec(MEM_W), row_spec(MEM_W), mem_spec, mem_spec,
                  const_spec(w_gates.shape), const_spec(w_branch.shape), const_spec(w_out.shape)],
        out_specs=row_spec(D_MODEL),
        out_shape=jax.ShapeDtypeStruct((rows, D_MODEL), jnp.float32),
        compiler_params=pltpu.CompilerParams(dimension_semantics=("arbitrary",), vmem_limit_bytes=VMEM_LIMIT),
        name="final",
    )(x2, gain, final_gain, ya, za, yb, zb, qm, zm, mk, mv, w_gates, w_branch, w_out)


def _layer(x, mem, norm_gain, w_in, rel_bias, mem_norm_gain, w_mem_kv, w_branch, w_out, final_gain):
    b, t, _ = x.shape
    assert t % (2 * TILE) == 0 and t // MOBA_BLOCK <= FEAT_LO and t >= 4 * DSA_TOPK
    bf = jnp.bfloat16
    names = ("qa", "ka", "va", "za", "qb", "kb", "vb", "zb", "iq", "ik", "iw", "qm", "zm", "ga", "gb", "gm")
    cols, off = {}, 0
    for name, size in zip(names, IN_SIZES):
        cols[name] = w_in[:, off:off + size]
        off += size
    w_std = jnp.concatenate([cols[n] for n, _, _ in _STD_SEGS], axis=1).astype(bf)
    w_t = jnp.concatenate([cols[n[:-1]] for n, _, _ in _T_SEGS], axis=1).T.astype(bf)
    w_gates = jnp.concatenate([cols["ga"], cols["gb"], cols["gm"]], axis=1).astype(bf)

    x2 = x.reshape(b * t, D_MODEL)
    (ka, za, kb, zb, qm, zm, ik, qa_t, va_t, qb_t, vb_t, iq_t, iw_t, kmean) = _proj_call(
        x2, norm_gain.reshape(1, D_MODEL), w_std, w_t, b, t)
    seq = lambda a: a.reshape(b, t, a.shape[-1])
    n_blocks = t // MOBA_BLOCK
    kmean = jnp.pad(kmean.reshape(b, n_blocks, MOBA_W), ((0, 0), (0, LANES - n_blocks), (0, 0)))

    bias_flat = (rel_bias * LOG2E).reshape(-1)
    ya = _moba_call(bias_flat, qa_t, seq(ka), va_t, kmean, _bucket_tiles(True))
    yb = _dsa_call(bias_flat, qb_t, seq(kb), vb_t, iq_t, seq(ik), iw_t, _bucket_tiles(False))
    mk, mv = _memkv_call(mem, mem_norm_gain.reshape(1, D_MODEL), w_mem_kv.astype(bf))
    out = _final_call(x2, norm_gain.reshape(1, D_MODEL), final_gain.reshape(1, D_MODEL),
                      ya.reshape(b * t, MOBA_W), za, yb.reshape(b * t, DSA_W), zb, qm, zm, mk, mv,
                      w_gates, w_branch.astype(bf), w_out.astype(bf), t)
    return out.reshape(b, t, D_MODEL)


@jax.jit
def kernel(x, mem, norm_gain, w_in, rel_bias, mem_norm_gain, w_mem_kv, w_branch, w_out, final_norm_gain):
    assert norm_gain.shape[0] == 1, "one layer"
    return _layer(x, mem, norm_gain[0], w_in[0], rel_bias, mem_norm_gain[0], w_mem_kv[0], w_branch[0],
                  w_out[0], final_norm_gain)
```

```python
import math

import numpy as np
import jax
import jax.numpy as jnp
from jax import lax
from jax.experimental import pallas as pl
from jax.experimental.pallas import tpu as pltpu

D_MODEL = 1024
HEAD_DIM = 64
MOBA_HEADS = 6
DSA_HEADS = 6
MEM_HEADS = 4
MOBA_W = MOBA_HEADS * HEAD_DIM
DSA_W = DSA_HEADS * HEAD_DIM
MEM_W = MEM_HEADS * HEAD_DIM
IDX_HEADS = 8
IDX_DIM = 64
IDX_W = IDX_HEADS * IDX_DIM
MOBA_BLOCK = 256
MOBA_TOPK = 3
DSA_TOPK = 256
REL_BUCKETS = 32
REL_MAX_DIST = 128
N_BIAS_HEADS = MOBA_HEADS + DSA_HEADS
EPS = 1e-6
IN_SIZES = (MOBA_W,) * 4 + (DSA_W,) * 4 + (IDX_W, IDX_DIM, IDX_HEADS) + (MEM_W, MEM_W) + (D_MODEL,) * 3

LANES = 128
SUBLANES = 8
ONES_ROWS = 16
ACC_ROWS = LANES + ONES_ROWS
TILE = 256
ROW_TILE = 512
FEAT_LO = 32
LOG2E = math.log2(math.e)
NEG_BIG = -1e30
MASK_BUCKET = REL_BUCKETS
VMEM_LIMIT = 56 * 1024 * 1024
MAX_SEARCH_ITERS = 96
BISECT_EVERY = 5
STALE_LEAN = 0.25
STALL_STEPS = 3.0
BOUND_SLACK = 1.001
MIN_SHIFTED_DENOM = 2.0 ** -60

_NT = (((1,), (1,)), ((), ()))


def _dot_nt(a, b):
    return lax.dot_general(a, b, _NT, preferred_element_type=jnp.float32)


def _dot(a, b):
    return jnp.dot(a, b, preferred_element_type=jnp.float32)


def _rel_bucket_np(n):
    exact = REL_BUCKETS // 2
    nf = np.maximum(n, 1).astype(np.float32)
    large = exact + (np.log(nf / np.float32(exact)) / np.float32(math.log(REL_MAX_DIST / exact))
                     * np.float32(REL_BUCKETS - exact)).astype(np.int32)
    return np.where(n < exact, n, np.minimum(large, REL_BUCKETS - 1))


def _bucket_tiles(mask_future):
    j = np.arange(TILE)[:, None]
    i = np.arange(TILE)[None, :]
    d0 = i - j
    b0 = _rel_bucket_np(np.maximum(d0, 0))
    if mask_future:
        b0 = np.where(d0 >= 0, b0, MASK_BUCKET)
    b1 = _rel_bucket_np(TILE + d0)
    assert int(_rel_bucket_np(np.array([TILE + 1]))[0]) == REL_BUCKETS - 1
    return jnp.asarray(np.stack([b0, b1]).astype(np.int32))


def _build_bias_tile(bkt, bias_ref, head):
    acc = jnp.where(bkt == MASK_BUCKET, NEG_BIG, 0.0)
    for b in range(REL_BUCKETS):
        acc = jnp.where(bkt == b, bias_ref[b * N_BIAS_HEADS + head], acc)
    return acc


def _rmsnorm_rows(x, g):
    xf = x.astype(jnp.float32)
    return xf * lax.rsqrt(jnp.mean(xf * xf, axis=-1, keepdims=True) + EPS) * g


_Q_SCALE = HEAD_DIM ** -0.5 * LOG2E
_STD_SEGS = (("ka", MOBA_W, 1.0), ("za", MOBA_W, 1.0), ("kb", DSA_W, 1.0), ("zb", DSA_W, 1.0),
             ("qm", MEM_W, _Q_SCALE), ("zm", MEM_W, 1.0), ("ik", IDX_DIM, 1.0))
_T_SEGS = (("qaT", MOBA_W, _Q_SCALE), ("vaT", MOBA_W, 1.0), ("qbT", DSA_W, _Q_SCALE), ("vbT", DSA_W, 1.0),
           ("iqT", IDX_W, IDX_DIM ** -0.5), ("iwT", IDX_HEADS, IDX_HEADS ** -0.5))
_STD_W = sum(w for _, w, _ in _STD_SEGS)
_T_W = sum(w for _, w, _ in _T_SEGS)
_CHUNKS_PER_ROW_TILE = ROW_TILE // TILE


def _proj_kernel(x_ref, g_ref, w_ref, wt_ref, *out_refs):
    std_refs = out_refs[:len(_STD_SEGS)]
    t_refs = out_refs[len(_STD_SEGS):len(_STD_SEGS) + len(_T_SEGS)]
    kmean_ref = out_refs[-1]
    h = _rmsnorm_rows(x_ref[...], g_ref[...]).astype(jnp.bfloat16)
    off = 0
    for (name, width, scale), o_ref in zip(_STD_SEGS, std_refs):
        r = _dot(h, w_ref[:, off:off + width])
        if name == "ka":
            for blk in range(ROW_TILE // MOBA_BLOCK):
                kmean_ref[0, blk:blk + 1, :] = jnp.mean(
                    r[blk * MOBA_BLOCK:(blk + 1) * MOBA_BLOCK], axis=0, keepdims=True)
        if scale != 1.0:
            r = r * scale
        o_ref[...] = r.astype(o_ref.dtype)
        off += width
    off = 0
    for (name, width, scale), o_ref in zip(_T_SEGS, t_refs):
        r = _dot_nt(wt_ref[off:off + width, :], h)
        if scale != 1.0:
            r = r * scale
        r = r.astype(o_ref.dtype)
        if name in ("vaT", "vbT"):
            for c in range(_CHUNKS_PER_ROW_TILE):
                o_ref[0, c] = r[:, c * TILE:(c + 1) * TILE]
        else:
            o_ref[0] = r
        off += width


def _proj_call(x2, gain, w_std, w_t, b, t):
    rows = x2.shape[0]
    tiles_per_batch = t // ROW_TILE
    n_tiles = rows // ROW_TILE
    row_spec = lambda width: pl.BlockSpec((ROW_TILE, width), lambda i: (i, 0))
    const_spec = lambda shape: pl.BlockSpec(shape, lambda i: (0,) * len(shape))
    t_spec = lambda width: pl.BlockSpec((1, width, ROW_TILE),
                                        lambda i: (i // tiles_per_batch, 0, i % tiles_per_batch))
    chunk_spec = lambda width: pl.BlockSpec((1, _CHUNKS_PER_ROW_TILE, width, TILE),
                                            lambda i: (i // tiles_per_batch, i % tiles_per_batch, 0, 0))
    out_shape = [jax.ShapeDtypeStruct((rows, w), jnp.bfloat16) for _, w, _ in _STD_SEGS]
    out_specs = [row_spec(w) for _, w, _ in _STD_SEGS]
    for name, w, _ in _T_SEGS:
        if name in ("vaT", "vbT"):
            out_shape.append(jax.ShapeDtypeStruct((b, t // TILE, w, TILE), jnp.bfloat16))
            out_specs.append(chunk_spec(w))
        else:
            out_shape.append(jax.ShapeDtypeStruct((b, w, t), jnp.float32 if name == "iwT" else jnp.bfloat16))
            out_specs.append(t_spec(w))
    out_shape.append(jax.ShapeDtypeStruct((n_tiles, ROW_TILE // MOBA_BLOCK, MOBA_W), jnp.float32))
    out_specs.append(pl.BlockSpec((1, ROW_TILE // MOBA_BLOCK, MOBA_W), lambda i: (i, 0, 0)))
    return pl.pallas_call(
        _proj_kernel,
        grid=(n_tiles,),
        in_specs=[row_spec(D_MODEL), const_spec((1, D_MODEL)), const_spec((D_MODEL, _STD_W)),
                  const_spec((_T_W, D_MODEL))],
        out_specs=out_specs,
        out_shape=out_shape,
        compiler_params=pltpu.CompilerParams(dimension_semantics=("arbitrary",), vmem_limit_bytes=VMEM_LIMIT),
        name="proj",
    )(x2, gain, w_std, w_t)


def _memkv_kernel(mem_ref, g_ref, w_ref, mk_ref, mv_ref):
    h = _rmsnorm_rows(mem_ref[0], g_ref[...]).astype(jnp.bfloat16)
    r = _dot(h, w_ref[...])
    mk_ref[0] = r[:, :MEM_W].astype(jnp.bfloat16)
    mv_ref[0] = r[:, MEM_W:].astype(jnp.bfloat16)


def _memkv_call(mem, gain, w_kv):
    b, n_mem, _ = mem.shape
    return pl.pallas_call(
        _memkv_kernel,
        grid=(b,),
        in_specs=[pl.BlockSpec((1, n_mem, D_MODEL), lambda i: (i, 0, 0)),
                  pl.BlockSpec((1, D_MODEL), lambda i: (0, 0)),
                  pl.BlockSpec((D_MODEL, 2 * MEM_W), lambda i: (0, 0))],
        out_specs=[pl.BlockSpec((1, n_mem, MEM_W), lambda i: (i, 0, 0))] * 2,
        out_shape=[jax.ShapeDtypeStruct((b, n_mem, MEM_W), jnp.bfloat16)] * 2,
        compiler_params=pltpu.CompilerParams(dimension_semantics=("arbitrary",), vmem_limit_bytes=VMEM_LIMIT),
        name="memkv",
    )(mem, gain, w_kv)


def _row_half_mask(shape, half):
    r = lax.broadcasted_iota(jnp.int32, shape, 0)
    return (r >= half * HEAD_DIM) & (r < (half + 1) * HEAD_DIM)


def _with_ones_rows(v_t):
    return jnp.concatenate([v_t, jnp.ones((ONES_ROWS, v_t.shape[1]), v_t.dtype)], axis=0)


def _softmax_stage(s_t, m_ref, alpha_ref, p_ref):
    m_old = m_ref[:1]
    m_new = jnp.maximum(m_old, jnp.max(s_t, axis=0, keepdims=True))
    alpha_ref[...] = jnp.broadcast_to(jnp.exp2(m_old - m_new), alpha_ref.shape)
    m_ref[...] = jnp.broadcast_to(m_new, m_ref.shape)
    p_ref[...] = jnp.exp2(s_t - m_new).astype(jnp.bfloat16)


def _accumulate_stage(v_t, alpha_ref, p_ref, acc_ref):
    acc_ref[...] = alpha_ref[:1] * acc_ref[...] + _dot(_with_ones_rows(v_t), p_ref[...])


def _init_softmax_state(m_ref, alpha_ref, p_ref, acc_ref):
    m_ref[...] = jnp.full(m_ref.shape, NEG_BIG, jnp.float32)
    acc_ref[...] = jnp.zeros(acc_ref.shape, jnp.float32)
    alpha_ref[...] = jnp.ones(alpha_ref.shape, jnp.float32)
    p_ref[...] = jnp.zeros(p_ref.shape, p_ref.dtype)


def _normalized_pair(acc, half_cols):
    out0 = acc[:LANES, :half_cols] / acc[LANES:LANES + 1, :half_cols]
    out1 = acc[:LANES, half_cols:] / acc[LANES:LANES + 1, half_cols:]
    return jnp.where(_row_half_mask((LANES, half_cols), 0), out0, out1).T


_KIND_DIAG, _KIND_PREV, _KIND_FAR = 0, 1, 2


def _run_tile_pipeline(qi, logits, softmax, accumulate):
    n_far = jnp.maximum(qi - 1, 0)
    n_loop = n_far // 2

    def far_pair(t):
        accumulate(0, jnp.maximum(t - 2, 0))
        accumulate(1, jnp.maximum(t - 1, 0))
        softmax(0, t, _KIND_FAR)
        logits(0, t + 2)
        softmax(1, t + 1, _KIND_FAR)
        logits(1, t + 3)

    def quad_body(j, carry):
        far_pair(4 * j)
        far_pair(4 * j + 2)
        return carry

    def pair_body(j, carry):
        far_pair(2 * j)
        return carry

    logits(0, 0)
    logits(1, jnp.minimum(1, qi))
    lax.fori_loop(0, n_loop // 2, quad_body, 0)
    lax.fori_loop(2 * (n_loop // 2), n_loop, pair_body, 0)
    t0 = 2 * n_loop
    odd = n_far % 2 == 1

    @pl.when(odd)
    def _():
        accumulate(0, jnp.maximum(t0 - 2, 0))
        accumulate(1, jnp.maximum(t0 - 1, 0))
        softmax(0, t0, _KIND_FAR)
        logits(0, qi)
        softmax(1, qi - 1, _KIND_PREV)
        accumulate(0, t0)
        softmax(0, qi, _KIND_DIAG)
        accumulate(1, qi - 1)
        accumulate(0, qi)

    @pl.when(jnp.logical_not(odd) & (qi >= 1))
    def _():
        accumulate(0, jnp.maximum(t0 - 2, 0))
        accumulate(1, jnp.maximum(t0 - 1, 0))
        softmax(0, qi - 1, _KIND_PREV)
        softmax(1, qi, _KIND_DIAG)
        accumulate(0, qi - 1)
        accumulate(1, qi)

    @pl.when(qi == 0)
    def _():
        softmax(0, 0, _KIND_DIAG)
        accumulate(0, 0)


def _moba_kernel(bias_ref, qt_ref, k_ref, vt_ref, kmean_ref, bkt_ref, o_ref,
                 qa_ref, tab_ref, s_ref, p_ref, alpha_ref, m_ref, acc_ref):
    pair = pl.program_id(1)
    qi = pl.program_id(2)
    q_t = qt_ref[0]
    kmean = kmean_ref[0].astype(jnp.bfloat16)
    blk = lax.broadcasted_iota(jnp.int32, (FEAT_LO, TILE), 0)
    blk_f = blk.astype(jnp.float32)
    past = blk < qi
    lane = lax.broadcasted_iota(jnp.int32, (TILE, LANES), 1)
    _init_softmax_state(m_ref, alpha_ref, p_ref, acc_ref)

    @pl.when(qi == 0)
    def _():
        for half in range(2):
            for kind in (_KIND_DIAG, _KIND_PREV):
                tab_ref[kind, :, half * TILE:(half + 1) * TILE] = _build_bias_tile(
                    bkt_ref[kind], bias_ref, pair * 2 + half)

    for half in range(2):
        qh = jnp.where(_row_half_mask((LANES, TILE), half), q_t, jnp.zeros_like(q_t))
        gate = _dot(kmean, qh)[:FEAT_LO]
        g = jnp.where(past, gate, -jnp.inf)
        sel = jnp.zeros((FEAT_LO, TILE), jnp.bool_)
        for _ in range(MOBA_TOPK):
            mx = jnp.max(g, axis=0, keepdims=True)
            first = jnp.min(jnp.where(g == mx, blk_f, float(LANES)), axis=0, keepdims=True)
            pick = blk_f == first
            sel = sel | (pick & past)
            g = jnp.where(pick, -jnp.inf, g)
        b31 = jnp.full((FEAT_LO, TILE), bias_ref[(REL_BUCKETS - 1) * N_BIAS_HEADS + pair * 2 + half], jnp.float32)
        b31_hi = b31.astype(jnp.bfloat16).astype(jnp.float32)
        hi_part = jnp.where(sel, jnp.where(blk < qi - 1, b31_hi, 0.0), NEG_BIG)
        hi_part = jnp.where(past, hi_part, 0.0)
        lo_part = jnp.where(blk < qi - 1, b31 - b31_hi, 0.0)
        feat = jnp.concatenate([hi_part, lo_part, jnp.zeros((LANES - 2 * FEAT_LO, TILE), jnp.float32)], axis=0)
        qa_ref[:, half * TILE:(half + 1) * TILE] = jnp.concatenate([qh, feat.astype(jnp.bfloat16)], axis=0)

    def logits(slot, kj):
        k_t = k_ref[0, pl.ds(pl.multiple_of(kj * TILE, TILE), TILE), :]
        onehot = ((lane == kj) | (lane == kj + FEAT_LO)).astype(jnp.bfloat16)
        s_ref[slot] = _dot(jnp.concatenate([k_t, onehot], axis=1), qa_ref[...])

    def softmax(slot, kj, kind):
        s_t = s_ref[slot]
        if kind != _KIND_FAR:
            s_t = s_t + tab_ref[kind]
        _softmax_stage(s_t, m_ref, alpha_ref.at[slot], p_ref.at[slot])

    def accumulate(slot, kj):
        _accumulate_stage(vt_ref[0, kj], alpha_ref.at[slot], p_ref.at[slot], acc_ref)

    _run_tile_pipeline(qi, logits, softmax, accumulate)
    o_ref[0] = _normalized_pair(acc_ref[...], TILE).astype(o_ref.dtype)


def _moba_call(bias_flat, q_t, k, v_t, kmean, bkt):
    b, _, t = q_t.shape
    n_pairs = MOBA_HEADS // 2
    n_chunks = t // TILE
    grid_spec = pltpu.PrefetchScalarGridSpec(
        num_scalar_prefetch=1,
        grid=(b, n_pairs, n_chunks),
        in_specs=[
            pl.BlockSpec((1, LANES, TILE), lambda bi, p, qi, s: (bi, p, qi)),
            pl.BlockSpec((1, t, LANES), lambda bi, p, qi, s: (bi, 0, p)),
            pl.BlockSpec((1, n_chunks, LANES, TILE), lambda bi, p, qi, s: (bi, 0, p, 0)),
            pl.BlockSpec((1, LANES, LANES), lambda bi, p, qi, s: (bi, 0, p)),
            pl.BlockSpec((2, TILE, TILE), lambda bi, p, qi, s: (0, 0, 0)),
        ],
        out_specs=pl.BlockSpec((1, TILE, LANES), lambda bi, p, qi, s: (bi, qi, p)),
        scratch_shapes=[
            pltpu.VMEM((2 * LANES, 2 * TILE), jnp.bfloat16),
            pltpu.VMEM((2, TILE, 2 * TILE), jnp.float32),
            pltpu.VMEM((2, TILE, 2 * TILE), jnp.float32),
            pltpu.VMEM((2, TILE, 2 * TILE), jnp.bfloat16),
            pltpu.VMEM((2, SUBLANES, 2 * TILE), jnp.float32),
            pltpu.VMEM((SUBLANES, 2 * TILE), jnp.float32),
            pltpu.VMEM((ACC_ROWS, 2 * TILE), jnp.float32),
        ],
    )
    return pl.pallas_call(
        _moba_kernel,
        grid_spec=grid_spec,
        out_shape=jax.ShapeDtypeStruct((b, t, MOBA_W), jnp.bfloat16),
        compiler_params=pltpu.CompilerParams(dimension_semantics=("arbitrary",) * 3, vmem_limit_bytes=VMEM_LIMIT),
        name="moba",
    )(bias_flat, q_t, k, v_t, kmean, bkt)


def _dsa_kernel(bias_ref, qt_ref, k_ref, vt_ref, iqt_ref, ik_ref, iwt_ref, bkt_ref, o_ref,
                sc_ref, tab_ref, bound_ref, qz_ref, s_ref, p_ref, alpha_ref, seen_ref, m_ref, acc_ref):
    qi = pl.program_id(1)
    n_chunks = qi + 1
    key = lax.broadcasted_iota(jnp.int32, (TILE, TILE), 0)
    qry = lax.broadcasted_iota(jnp.int32, (TILE, TILE), 1)
    causal = key <= qry

    far_bias = [bias_ref[(REL_BUCKETS - 1) * N_BIAS_HEADS + MOBA_HEADS + h] for h in range(DSA_HEADS)]

    @pl.when(qi == 0)
    def _():
        for h in range(DSA_HEADS):
            rel_max = jnp.zeros((1, 1), jnp.float32)
            for kind in (_KIND_DIAG, _KIND_PREV):
                tile = _build_bias_tile(bkt_ref[kind], bias_ref, MOBA_HEADS + h) - far_bias[h]
                tab_ref[h // 2, kind, :, (h % 2) * TILE:(h % 2 + 1) * TILE] = tile
                rel_max = jnp.maximum(rel_max, jnp.max(jnp.max(tile, axis=0, keepdims=True), axis=1, keepdims=True))
            bound_ref[h] = jnp.broadcast_to(rel_max + far_bias[h], (SUBLANES, LANES))

        for pr in range(DSA_HEADS // 2):
            def norm_body(c, mx, pr=pr):
                blk = k_ref[0, pl.ds(pl.multiple_of(c * TILE, TILE), TILE), pr * LANES:(pr + 1) * LANES]
                blk = blk.astype(jnp.float32)
                return jnp.maximum(mx, jnp.sum(blk * blk, axis=1, keepdims=True))
            n2 = lax.fori_loop(0, k_ref.shape[1] // TILE, norm_body, jnp.zeros((TILE, 1), jnp.float32))
            bound_ref[DSA_HEADS + pr] = jnp.broadcast_to(jnp.sqrt(jnp.max(n2, axis=0, keepdims=True)),
                                                         (SUBLANES, LANES))

    feat_row = lax.broadcasted_iota(jnp.int32, (LANES, TILE), 0)
    for h in range(DSA_HEADS):
        blk = qt_ref[0, (h // 2) * LANES:(h // 2 + 1) * LANES, :]
        qh = jnp.where(_row_half_mask((LANES, TILE), h % 2), blk, jnp.zeros_like(blk))
        qf = qh.astype(jnp.float32)
        q_norm = jnp.sqrt(jnp.sum(qf * qf, axis=0, keepdims=True))
        logit_bound = q_norm * bound_ref[DSA_HEADS + h // 2][:1, :1] * BOUND_SLACK + bound_ref[h][:1, :1]
        b31 = jnp.full((LANES, TILE), far_bias[h], jnp.float32)
        b31_hi = b31.astype(jnp.bfloat16).astype(jnp.float32)
        feat = jnp.where(feat_row == 0, b31_hi, jnp.where(feat_row == 1, b31 - b31_hi, 0.0))
        feat = jnp.where(feat_row == 2, -logit_bound, feat)
        qz_ref[h // 2, :, (h % 2) * TILE:(h % 2 + 1) * TILE] = jnp.concatenate(
            [qh, feat.astype(jnp.bfloat16)], axis=0)

    iw_t = iwt_ref[0]

    def score_chunk(kc):
        row0 = pl.multiple_of(kc * TILE, TILE)
        ik_t = ik_ref[0, pl.ds(row0, TILE), :]
        tot = jnp.zeros((TILE, TILE), jnp.float32)
        for h in range(IDX_HEADS):
            x = _dot(ik_t, iqt_ref[0, h * IDX_DIM:(h + 1) * IDX_DIM, :])
            tot = tot + iw_t[h:h + 1, :] * jnp.maximum(x, 0.0)
        return tot

    def score_body(kc, carry):
        mx, mn = carry
        tot = score_chunk(kc)
        sc_ref[kc] = tot
        return (jnp.maximum(mx, jnp.max(tot, axis=0, keepdims=True)),
                jnp.minimum(mn, jnp.min(tot, axis=0, keepdims=True)))

    def score_pair_body(j, carry):
        return score_body(2 * j + 1, score_body(2 * j, carry))

    extremes = lax.fori_loop(0, qi // 2, score_pair_body, (jnp.full((1, TILE), -jnp.inf, jnp.float32),
                                                           jnp.full((1, TILE), jnp.inf, jnp.float32)))
    row_max, row_min = lax.cond(qi % 2 == 1, lambda c: score_body(qi - 1, c), lambda c: c, extremes)
    tot = score_chunk(qi)
    sc_ref[qi] = jnp.where(causal, tot, -jnp.inf)
    row_max = jnp.maximum(row_max, jnp.max(jnp.where(causal, tot, -jnp.inf), axis=0, keepdims=True))
    row_min = jnp.minimum(row_min, jnp.min(jnp.where(causal, tot, jnp.inf), axis=0, keepdims=True))

    @pl.when(qi % 2 == 0)
    def _():
        sc_ref[qi + 1] = jnp.full((TILE, TILE), -jnp.inf, jnp.float32)

    def reduce_pass(fn, init, combine):
        def body(j, part):
            return combine(part, combine(fn(sc_ref[2 * j]), fn(sc_ref[2 * j + 1])))
        return lax.fori_loop(0, (n_chunks + 1) // 2, body, jnp.full((1, TILE), init, jnp.float32))

    def count(pred):
        return reduce_pass(lambda x: jnp.sum(jnp.where(pred(x), 1.0, 0.0), axis=0, keepdims=True), 0.0, jnp.add)

    def min_ge(p):
        return reduce_pass(lambda x: jnp.min(jnp.where(x >= p, x, jnp.inf), axis=0, keepdims=True),
                           jnp.inf, jnp.minimum)

    n_adm =(qi * TILE + 1 + lax.broadcasted_iota(jnp.int32, (1, TILE), 1)).astype(jnp.float32)
    k_top = float(DSA_TOPK)
    take_all = n_adm <= k_top

    def search_cond(st):
        return jnp.logical_and(st[0] < MAX_SEARCH_ITERS, st[-1] >= some_live)

    all_done, some_live, some_stalled = 0.0, 1.0, 2.0

    def live_flag(done, stall):
        return jnp.max(jnp.where(done < 0.5, jnp.where(stall >= STALL_STEPS, some_stalled, some_live), all_done))

    def tie_check(st):
        it, lo, hi, hi_fin, c_lo, c_hi, done, last, streak, stall, _ = st
        a = min_ge(lo)
        c_gt = count(lambda x: x > a)
        tie = (c_gt < k_top) & (done < 0.5)
        done = jnp.where(tie, 1.0, done)
        stall = jnp.zeros_like(stall)
        return (it, lo, hi, hi_fin, c_lo, jnp.where(tie, c_gt, c_hi), done, last, streak, stall,
                live_flag(done, stall))

    def search_step(st):
        it, lo, hi, hi_fin, c_lo, c_hi, done, last, streak, stall = st
        width = c_lo - c_hi
        target = k_top - 0.5 + jnp.where(streak >= 1.0, last * STALE_LEAN * (c_lo - c_hi), 0.0)
        target = jnp.clip(target, c_hi + 0.5, c_lo - 0.5)
        frac = (c_lo - target) / jnp.maximum(c_lo - c_hi, 1.0)
        mid = 0.5 * lo + 0.5 * hi_fin
        p = jnp.where(it % BISECT_EVERY == BISECT_EVERY - 1, mid, lo + frac * (hi_fin - lo))
        p = jnp.where((p > lo) & (p < hi), p, mid)
        stuck = jnp.logical_not((mid > lo) & (mid < hi))
        c = count(lambda x: x >= p)
        up = c >= k_top
        move = jnp.logical_not((done > 0.5) | stuck)
        rise = move & up
        fall = move & jnp.logical_not(up)
        lo = jnp.where(rise, p, lo)
        c_lo = jnp.where(rise, c, c_lo)
        hi = jnp.where(fall, p, hi)
        c_hi = jnp.where(fall, c, c_hi)
        hi_fin = jnp.where(fall, p, hi_fin)
        side = jnp.where(up, -1.0, 1.0)
        streak = jnp.where(side == last, streak + 1.0, 0.0)
        done = jnp.where((c_lo - c_hi <= 1.0) | (c_lo == k_top) | stuck, 1.0, done)
        stall = jnp.where(c_lo - c_hi == width, stall + 1.0, 0.0)
        return it + 1, lo, hi, hi_fin, c_lo, c_hi, done, side, streak, stall

    def search_body(st):
        st = search_step(search_step(search_step(search_step(st[:-1]))))
        flag = live_flag(st[6], st[9])
        return lax.cond(flag >= some_stalled, tie_check, lambda s: s, st + (flag,))

    zeros_row = jnp.zeros((1, TILE), jnp.float32)
    done0 = take_all.astype(jnp.float32)
    init = (jnp.int32(0), row_min, jnp.full((1, TILE), jnp.inf, jnp.float32), row_max,
            n_adm, zeros_row, done0, zeros_row, zeros_row, zeros_row, live_flag(done0, zeros_row))
    _, lo, _, _, _, c_hi, _, _, _, _, _ = lax.while_loop(search_cond, search_body, init)
    thr = jnp.where(take_all, -jnp.inf, min_ge(lo))
    need = k_top - jnp.where(take_all, 0.0, c_hi)

    n_pairs = DSA_HEADS // 2
    lower = (qry < key).astype(jnp.bfloat16)
    ones_feat = (lax.broadcasted_iota(jnp.int32, (TILE, LANES), 1) < 3).astype(jnp.bfloat16)

    def init_state():
        for pr in range(n_pairs):
            _init_softmax_state(m_ref.at[pr], alpha_ref.at[:, pr], p_ref.at[:, pr], acc_ref.at[pr])
        seen_ref[...] = jnp.zeros(seen_ref.shape, jnp.float32)

    def logits(slot, kc):
        row0 = pl.multiple_of(kc * TILE, TILE)
        for pr in range(n_pairs):
            k_t = k_ref[0, pl.ds(row0, TILE), pr * LANES:(pr + 1) * LANES]
            s_ref[slot, pr] = _dot(jnp.concatenate([k_t, ones_feat], axis=1), qz_ref[pr])

    def masked_logits(slot, kc, kind):
        x = sc_ref[kc]
        tie = x == thr
        tie_f = jnp.where(tie, 1.0, 0.0)
        seen = seen_ref[:1]
        rank = seen + _dot(lower, tie_f.astype(jnp.bfloat16))
        selected = (x > thr) | (tie & (rank < need))
        if kind == _KIND_DIAG:
            selected = selected & causal
        mask_add = jnp.where(selected, 0.0, NEG_BIG)
        mask_add = jnp.concatenate([mask_add, mask_add], axis=1)
        seen_ref[...] = jnp.broadcast_to(seen + jnp.sum(tie_f, axis=0, keepdims=True), seen_ref.shape)
        for pr in range(n_pairs):
            s_t = s_ref[slot, pr] + mask_add
            if kind != _KIND_FAR:
                s_t = s_t + tab_ref[pr, kind]
            yield pr, s_t

    def softmax_shifted(slot, kc, kind):
        for pr, s_t in masked_logits(slot, kc, kind):
            p_ref[slot, pr] = jnp.exp2(s_t).astype(jnp.bfloat16)

    def accumulate_shifted(slot, kc):
        for pr in range(n_pairs):
            acc_ref[pr] += _dot(_with_ones_rows(vt_ref[0, kc, pr * LANES:(pr + 1) * LANES, :]), p_ref[slot, pr])

    def softmax_online(slot, kc, kind):
        for pr, s_t in masked_logits(slot, kc, kind):
            _softmax_stage(s_t, m_ref.at[pr], alpha_ref.at[slot, pr], p_ref.at[slot, pr])

    def accumulate_online(slot, kc):
        for pr in range(n_pairs):
            _accumulate_stage(vt_ref[0, kc, pr * LANES:(pr + 1) * LANES, :], alpha_ref.at[slot, pr],
                              p_ref.at[slot, pr], acc_ref.at[pr])

    init_state()
    _run_tile_pipeline(qi, logits, softmax_shifted, accumulate_shifted)
    denom = acc_ref[0, LANES:LANES + 1, :]
    for pr in range(1, n_pairs):
        denom = jnp.minimum(denom, acc_ref[pr, LANES:LANES + 1, :])
    denom_min = jnp.min(denom)

    @pl.when(jnp.logical_not(denom_min > MIN_SHIFTED_DENOM))
    def _():
        init_state()
        _run_tile_pipeline(qi, logits, softmax_online, accumulate_online)

    for pr in range(n_pairs):
        o_ref[0, :, pr * LANES:(pr + 1) * LANES] = _normalized_pair(acc_ref[pr], TILE).astype(o_ref.dtype)


def _dsa_call(bias_flat, q_t, k, v_t, iq_t, ik, iw_t, bkt):
    b, _, t = q_t.shape
    n_chunks = t // TILE
    n_pairs = DSA_HEADS // 2
    grid_spec = pltpu.PrefetchScalarGridSpec(
        num_scalar_prefetch=1,
        grid=(b, n_chunks),
        in_specs=[
            pl.BlockSpec((1, DSA_W, TILE), lambda bi, qi, s: (bi, 0, qi)),
            pl.BlockSpec((1, t, DSA_W), lambda bi, qi, s: (bi, 0, 0)),
            pl.BlockSpec((1, n_chunks, DSA_W, TILE), lambda bi, qi, s: (bi, 0, 0, 0)),
            pl.BlockSpec((1, IDX_W, TILE), lambda bi, qi, s: (bi, 0, qi)),
            pl.BlockSpec((1, t, IDX_DIM), lambda bi, qi, s: (bi, 0, 0)),
            pl.BlockSpec((1, IDX_HEADS, TILE), lambda bi, qi, s: (bi, 0, qi)),
            pl.BlockSpec((2, TILE, TILE), lambda bi, qi, s: (0, 0, 0)),
        ],
        out_specs=pl.BlockSpec((1, TILE, DSA_W), lambda bi, qi, s: (bi, qi, 0)),
        scratch_shapes=[
            pltpu.VMEM((n_chunks, TILE, TILE), jnp.float32),
            pltpu.VMEM((n_pairs, 2, TILE, 2 * TILE), jnp.float32),
            pltpu.VMEM((DSA_HEADS + n_pairs, SUBLANES, LANES), jnp.float32),
            pltpu.VMEM((n_pairs, 2 * LANES, 2 * TILE), jnp.bfloat16),
            pltpu.VMEM((2, n_pairs, TILE, 2 * TILE), jnp.float32),
            pltpu.VMEM((2, n_pairs, TILE, 2 * TILE), jnp.bfloat16),
            pltpu.VMEM((2, n_pairs, SUBLANES, 2 * TILE), jnp.float32),
            pltpu.VMEM((SUBLANES, TILE), jnp.float32),
            pltpu.VMEM((n_pairs, SUBLANES, 2 * TILE), jnp.float32),
            pltpu.VMEM((n_pairs, ACC_ROWS, 2 * TILE), jnp.float32),
        ],
    )
    return pl.pallas_call(
        _dsa_kernel,
        grid_spec=grid_spec,
        out_shape=jax.ShapeDtypeStruct((b, t, DSA_W), jnp.bfloat16),
        compiler_params=pltpu.CompilerParams(dimension_semantics=("arbitrary",) * 2, vmem_limit_bytes=VMEM_LIMIT),
        name="dsa",
    )(bias_flat, q_t, k, v_t, iq_t, ik, iw_t, bkt)


def _lane_half_mask(shape, half):
    lane = lax.broadcasted_iota(jnp.int32, shape, 1)
    return (lane >= half * HEAD_DIM) & (lane < (half + 1) * HEAD_DIM)


def _final_kernel(x_ref, g_ref, fg_ref, ya_ref, za_ref, yb_ref, zb_ref, qm_ref, zm_ref, mk_ref, mv_ref,
                  wg_ref, wb_ref, wo_ref, o_ref):
    x = x_ref[...]
    h = _rmsnorm_rows(x, g_ref[...]).astype(jnp.bfloat16)

    ym_parts = []
    for pr in range(MEM_HEADS // 2):
        qp = qm_ref[:, pr * LANES:(pr + 1) * LANES]
        mk = mk_ref[0, :, pr * LANES:(pr + 1) * LANES]
        mv = mv_ref[0, :, pr * LANES:(pr + 1) * LANES]
        outs = []
        for half in range(2):
            qh = jnp.where(_lane_half_mask((ROW_TILE, LANES), half), qp, jnp.zeros_like(qp))
            s = _dot_nt(qh, mk)
            p = jnp.exp2(s - jnp.max(s, axis=1, keepdims=True))
            outs.append(_dot(p.astype(jnp.bfloat16), mv) / jnp.sum(p, axis=1, keepdims=True))
        ym_parts.append(jnp.where(_lane_half_mask((ROW_TILE, LANES), 0), outs[0], outs[1]))
    ym = jnp.concatenate(ym_parts, axis=1)

    def gated(y, z_ref):
        z = z_ref[...].astype(jnp.float32)
        return (y.astype(jnp.float32) * (z * jax.nn.sigmoid(z))).astype(jnp.bfloat16)

    ua = gated(ya_ref[...], za_ref)
    ub = gated(yb_ref[...], zb_ref)
    um = gated(ym, zm_ref)
    merged = jax.nn.sigmoid(_dot(h, wg_ref[:, :D_MODEL])) * _dot(ua, wb_ref[:MOBA_W])
    merged += jax.nn.sigmoid(_dot(h, wg_ref[:, D_MODEL:2 * D_MODEL])) * _dot(ub, wb_ref[MOBA_W:MOBA_W + DSA_W])
    merged += jax.nn.sigmoid(_dot(h, wg_ref[:, 2 * D_MODEL:])) * _dot(um, wb_ref[MOBA_W + DSA_W:])
    y = x + _dot(merged.astype(jnp.bfloat16), wo_ref[...])
    o_ref[...] = _rmsnorm_rows(y, fg_ref[...])


def _final_call(x2, gain, final_gain, ya, za, yb, zb, qm, zm, mk, mv, w_gates, w_branch, w_out, rows_per_batch):
    rows = x2.shape[0]
    tiles_per_batch = rows_per_batch // ROW_TILE
    row_spec = lambda width: pl.BlockSpec((ROW_TILE, width), lambda i: (i, 0))
    const_spec = lambda shape: pl.BlockSpec(shape, lambda i: (0,) * len(shape))
    mem_spec = pl.BlockSpec((1,) + mk.shape[1:], lambda i: (i // tiles_per_batch, 0, 0))
    return pl.pallas_call(
        _final_kernel,
        grid=(rows // ROW_TILE,),
        in_specs=[row_spec(D_MODEL), const_spec((1, D_MODEL)), const_spec((1, D_MODEL)),
                  row_spec(MOBA_W), row_spec(MOBA_W), row_spec(DSA_W), row_spec(DSA_W),
                  row_spec(MEM_W), row_spec(MEM_W), mem_spec, mem_spec,
                  const_spec(w_gates.shape), const_spec(w_branch.shape), const_spec(w_out.shape)],
        out_specs=row_spec(D_MODEL),
        out_shape=jax.ShapeDtypeStruct((rows, D_MODEL), jnp.float32),
        compiler_params=pltpu.CompilerParams(dimension_semantics=("arbitrary",), vmem_limit_bytes=VMEM_LIMIT),
        name="final",
    )(x2, gain, final_gain, ya, za, yb, zb, qm, zm, mk, mv, w_gates, w_branch, w_out)


def _layer(x, mem, norm_gain, w_in, rel_bias, mem_norm_gain, w_mem_kv, w_branch, w_out, final_gain):
    b, t, _ = x.shape
    assert t % (2 * TILE) == 0 and t // MOBA_BLOCK <= FEAT_LO and t >= 4 * DSA_TOPK
    bf = jnp.bfloat16
    names = ("qa", "ka", "va", "za", "qb", "kb", "vb", "zb", "iq", "ik", "iw", "qm", "zm", "ga", "gb", "gm")
    cols, off = {}, 0
    for name, size in zip(names, IN_SIZES):
        cols[name] = w_in[:, off:off + size]
        off += size
    w_std = jnp.concatenate([cols[n] for n, _, _ in _STD_SEGS], axis=1).astype(bf)
    w_t = jnp.concatenate([cols[n[:-1]] for n, _, _ in _T_SEGS], axis=1).T.astype(bf)
    w_gates = jnp.concatenate([cols["ga"], cols["gb"], cols["gm"]], axis=1).astype(bf)

    x2 = x.reshape(b * t, D_MODEL)
    (ka, za, kb, zb, qm, zm, ik, qa_t, va_t, qb_t, vb_t, iq_t, iw_t, kmean) = _proj_call(
        x2, norm_gain.reshape(1, D_MODEL), w_std, w_t, b, t)
    seq = lambda a: a.reshape(b, t, a.shape[-1])
    n_blocks = t // MOBA_BLOCK
    kmean = jnp.pad(kmean.reshape(b, n_blocks, MOBA_W), ((0, 0), (0, LANES - n_blocks), (0, 0)))

    bias_flat = (rel_bias * LOG2E).reshape(-1)
    ya = _moba_call(bias_flat, qa_t, seq(ka), va_t, kmean, _bucket_tiles(True))
    yb = _dsa_call(bias_flat, qb_t, seq(kb), vb_t, iq_t, seq(ik), iw_t, _bucket_tiles(False))
    mk, mv = _memkv_call(mem, mem_norm_gain.reshape(1, D_MODEL), w_mem_kv.astype(bf))
    out = _final_call(x2, norm_gain.reshape(1, D_MODEL), final_gain.reshape(1, D_MODEL),
                      ya.reshape(b * t, MOBA_W), za, yb.reshape(b * t, DSA_W), zb, qm, zm, mk, mv,
                      w_gates, w_branch.astype(bf), w_out.astype(bf), t)
    return out.reshape(b, t, D_MODEL)


@jax.jit
def kernel(x, mem, norm_gain, w_in, rel_bias, mem_norm_gain, w_mem_kv, w_branch, w_out, final_norm_gain):
    assert norm_gain.shape[0] == 1, "one layer"
    return _layer(x, mem, norm_gain[0], w_in[0], rel_bias, mem_norm_gain[0], w_mem_kv[0], w_branch[0],
                  w_out[0], final_norm_gain)
```

```python
import math

import numpy as np
import jax
import jax.numpy as jnp
from jax import lax
from jax.experimental import pallas as pl
from jax.experimental.pallas import tpu as pltpu

D_MODEL = 1024
HEAD_DIM = 64
MOBA_HEADS = 6
DSA_HEADS = 6
MEM_HEADS = 4
MOBA_W = MOBA_HEADS * HEAD_DIM
DSA_W = DSA_HEADS * HEAD_DIM
MEM_W = MEM_HEADS * HEAD_DIM
IDX_HEADS = 8
IDX_DIM = 64
IDX_W = IDX_HEADS * IDX_DIM
MOBA_BLOCK = 256
MOBA_TOPK = 3
DSA_TOPK = 256
REL_BUCKETS = 32
REL_MAX_DIST = 128
N_BIAS_HEADS = MOBA_HEADS + DSA_HEADS
EPS = 1e-6
IN_SIZES = (MOBA_W,) * 4 + (DSA_W,) * 4 + (IDX_W, IDX_DIM, IDX_HEADS) + (MEM_W, MEM_W) + (D_MODEL,) * 3

LANES = 128
SUBLANES = 8
ONES_ROWS = 16
ACC_ROWS = LANES + ONES_ROWS
TILE = 256
ROW_TILE = 512
FEAT_LO = 32
LOG2E = math.log2(math.e)
NEG_BIG = -1e30
MASK_BUCKET = REL_BUCKETS
VMEM_LIMIT = 56 * 1024 * 1024
MAX_SEARCH_ITERS = 96
BISECT_EVERY = 5
STALE_LEAN = 0.25
STALL_STEPS = 3.0
BOUND_SLACK = 1.001
MIN_SHIFTED_DENOM = 2.0 ** -60

_NT = (((1,), (1,)), ((), ()))


def _dot_nt(a, b):
    return lax.dot_general(a, b, _NT, preferred_element_type=jnp.float32)


def _dot(a, b):
    return jnp.dot(a, b, preferred_element_type=jnp.float32)


def _rel_bucket_np(n):
    exact = REL_BUCKETS // 2
    nf = np.maximum(n, 1).astype(np.float32)
    large = exact + (np.log(nf / np.float32(exact)) / np.float32(math.log(REL_MAX_DIST / exact))
                     * np.float32(REL_BUCKETS - exact)).astype(np.int32)
    return np.where(n < exact, n, np.minimum(large, REL_BUCKETS - 1))


def _bucket_tiles(mask_future):
    j = np.arange(TILE)[:, None]
    i = np.arange(TILE)[None, :]
    d0 = i - j
    b0 = _rel_bucket_np(np.maximum(d0, 0))
    if mask_future:
        b0 = np.where(d0 >= 0, b0, MASK_BUCKET)
    b1 = _rel_bucket_np(TILE + d0)
    assert int(_rel_bucket_np(np.array([TILE + 1]))[0]) == REL_BUCKETS - 1
    return jnp.asarray(np.stack([b0, b1]).astype(np.int32))


def _build_bias_tile(bkt, bias_ref, head):
    acc = jnp.where(bkt == MASK_BUCKET, NEG_BIG, 0.0)
    for b in range(REL_BUCKETS):
        acc = jnp.where(bkt == b, bias_ref[b * N_BIAS_HEADS + head], acc)
    return acc


def _rmsnorm_rows(x, g):
    xf = x.astype(jnp.float32)
    return xf * lax.rsqrt(jnp.mean(xf * xf, axis=-1, keepdims=True) + EPS) * g


_Q_SCALE = HEAD_DIM ** -0.5 * LOG2E
_STD_SEGS = (("ka", MOBA_W, 1.0), ("za", MOBA_W, 1.0), ("kb", DSA_W, 1.0), ("zb", DSA_W, 1.0),
             ("qm", MEM_W, _Q_SCALE), ("zm", MEM_W, 1.0), ("ik", IDX_DIM, 1.0))
_T_SEGS = (("qaT", MOBA_W, _Q_SCALE), ("vaT", MOBA_W, 1.0), ("qbT", DSA_W, _Q_SCALE), ("vbT", DSA_W, 1.0),
           ("iqT", IDX_W, IDX_DIM ** -0.5), ("iwT", IDX_HEADS, IDX_HEADS ** -0.5))
_STD_W = sum(w for _, w, _ in _STD_SEGS)
_T_W = sum(w for _, w, _ in _T_SEGS)
_CHUNKS_PER_ROW_TILE = ROW_TILE // TILE


def _proj_kernel(x_ref, g_ref, w_ref, wt_ref, *out_refs):
    std_refs = out_refs[:len(_STD_SEGS)]
    t_refs = out_refs[len(_STD_SEGS):len(_STD_SEGS) + len(_T_SEGS)]
    kmean_ref = out_refs[-1]
    h = _rmsnorm_rows(x_ref[...], g_ref[...]).astype(jnp.bfloat16)
    off = 0
    for (name, width, scale), o_ref in zip(_STD_SEGS, std_refs):
        r = _dot(h, w_ref[:, off:off + width])
        if name == "ka":
            for blk in range(ROW_TILE // MOBA_BLOCK):
                kmean_ref[0, blk:blk + 1, :] = jnp.mean(
                    r[blk * MOBA_BLOCK:(blk + 1) * MOBA_BLOCK], axis=0, keepdims=True)
        if scale != 1.0:
            r = r * scale
        o_ref[...] = r.astype(o_ref.dtype)
        off += width
    off = 0
    for (name, width, scale), o_ref in zip(_T_SEGS, t_refs):
        r = _dot_nt(wt_ref[off:off + width, :], h)
        if scale != 1.0:
            r = r * scale
        r = r.astype(o_ref.dtype)
        if name in ("vaT", "vbT"):
            for c in range(_CHUNKS_PER_ROW_TILE):
                o_ref[0, c] = r[:, c * TILE:(c + 1) * TILE]
        else:
            o_ref[0] = r
        off += width


def _proj_call(x2, gain, w_std, w_t, b, t):
    rows = x2.shape[0]
    tiles_per_batch = t // ROW_TILE
    n_tiles = rows // ROW_TILE
    row_spec = lambda width: pl.BlockSpec((ROW_TILE, width), lambda i: (i, 0))
    const_spec = lambda shape: pl.BlockSpec(shape, lambda i: (0,) * len(shape))
    t_spec = lambda width: pl.BlockSpec((1, width, ROW_TILE),
                                        lambda i: (i // tiles_per_batch, 0, i % tiles_per_batch))
    chunk_spec = lambda width: pl.BlockSpec((1, _CHUNKS_PER_ROW_TILE, width, TILE),
                                            lambda i: (i // tiles_per_batch, i % tiles_per_batch, 0, 0))
    out_shape = [jax.ShapeDtypeStruct((rows, w), jnp.bfloat16) for _, w, _ in _STD_SEGS]
    out_specs = [row_spec(w) for _, w, _ in _STD_SEGS]
    for name, w, _ in _T_SEGS:
        if name in ("vaT", "vbT"):
            out_shape.append(jax.ShapeDtypeStruct((b, t // TILE, w, TILE), jnp.bfloat16))
            out_specs.append(chunk_spec(w))
        else:
            out_shape.append(jax.ShapeDtypeStruct((b, w, t), jnp.float32 if name == "iwT" else jnp.bfloat16))
            out_specs.append(t_spec(w))
    out_shape.append(jax.ShapeDtypeStruct((n_tiles, ROW_TILE // MOBA_BLOCK, MOBA_W), jnp.float32))
    out_specs.append(pl.BlockSpec((1, ROW_TILE // MOBA_BLOCK, MOBA_W), lambda i: (i, 0, 0)))
    return pl.pallas_call(
        _proj_kernel,
        grid=(n_tiles,),
        in_specs=[row_spec(D_MODEL), const_spec((1, D_MODEL)), const_spec((D_MODEL, _STD_W)),
                  const_spec((_T_W, D_MODEL))],
        out_specs=out_specs,
        out_shape=out_shape,
        compiler_params=pltpu.CompilerParams(dimension_semantics=("arbitrary",), vmem_limit_bytes=VMEM_LIMIT),
        name="proj",
    )(x2, gain, w_std, w_t)


def _memkv_kernel(mem_ref, g_ref, w_ref, mk_ref, mv_ref):
    h = _rmsnorm_rows(mem_ref[0], g_ref[...]).astype(jnp.bfloat16)
    r = _dot(h, w_ref[...])
    mk_ref[0] = r[:, :MEM_W].astype(jnp.bfloat16)
    mv_ref[0] = r[:, MEM_W:].astype(jnp.bfloat16)


def _memkv_call(mem, gain, w_kv):
    b, n_mem, _ = mem.shape
    return pl.pallas_call(
        _memkv_kernel,
        grid=(b,),
        in_specs=[pl.BlockSpec((1, n_mem, D_MODEL), lambda i: (i, 0, 0)),
                  pl.BlockSpec((1, D_MODEL), lambda i: (0, 0)),
                  pl.BlockSpec((D_MODEL, 2 * MEM_W), lambda i: (0, 0))],
        out_specs=[pl.BlockSpec((1, n_mem, MEM_W), lambda i: (i, 0, 0))] * 2,
        out_shape=[jax.ShapeDtypeStruct((b, n_mem, MEM_W), jnp.bfloat16)] * 2,
        compiler_params=pltpu.CompilerParams(dimension_semantics=("arbitrary",), vmem_limit_bytes=VMEM_LIMIT),
        name="memkv",
    )(mem, gain, w_kv)


def _row_half_mask(shape, half):
    r = lax.broadcasted_iota(jnp.int32, shape, 0)
    return (r >= half * HEAD_DIM) & (r < (half + 1) * HEAD_DIM)


def _with_ones_rows(v_t):
    return jnp.concatenate([v_t, jnp.ones((ONES_ROWS, v_t.shape[1]), v_t.dtype)], axis=0)


def _softmax_stage(s_t, m_ref, alpha_ref, p_ref):
    m_old = m_ref[:1]
    m_new = jnp.maximum(m_old, jnp.max(s_t, axis=0, keepdims=True))
    alpha_ref[...] = jnp.broadcast_to(jnp.exp2(m_old - m_new), alpha_ref.shape)
    m_ref[...] = jnp.broadcast_to(m_new, m_ref.shape)
    p_ref[...] = jnp.exp2(s_t - m_new).astype(jnp.bfloat16)


def _accumulate_stage(v_t, alpha_ref, p_ref, acc_ref):
    acc_ref[...] = alpha_ref[:1] * acc_ref[...] + _dot(_with_ones_rows(v_t), p_ref[...])


def _init_softmax_state(m_ref, alpha_ref, p_ref, acc_ref):
    m_ref[...] = jnp.full(m_ref.shape, NEG_BIG, jnp.float32)
    acc_ref[...] = jnp.zeros(acc_ref.shape, jnp.float32)
    alpha_ref[...] = jnp.ones(alpha_ref.shape, jnp.float32)
    p_ref[...] = jnp.zeros(p_ref.shape, p_ref.dtype)


def _normalized_pair(acc, half_cols):
    out0 = acc[:LANES, :half_cols] / acc[LANES:LANES + 1, :half_cols]
    out1 = acc[:LANES, half_cols:] / acc[LANES:LANES + 1, half_cols:]
    return jnp.where(_row_half_mask((LANES, half_cols), 0), out0, out1).T


_KIND_DIAG, _KIND_PREV, _KIND_FAR = 0, 1, 2


def _run_tile_pipeline(qi, logits, softmax, accumulate):
    n_far = jnp.maximum(qi - 1, 0)
    n_loop = n_far // 2

    def far_pair(t):
        accumulate(0, jnp.maximum(t - 2, 0))
        accumulate(1, jnp.maximum(t - 1, 0))
        softmax(0, t, _KIND_FAR)
        logits(0, t + 2)
        softmax(1, t + 1, _KIND_FAR)
        logits(1, t + 3)

    def quad_body(j, carry):
        far_pair(4 * j)
        far_pair(4 * j + 2)
        return carry

    def pair_body(j, carry):
        far_pair(2 * j)
        return carry

    logits(0, 0)
    logits(1, jnp.minimum(1, qi))
    lax.fori_loop(0, n_loop // 2, quad_body, 0)
    lax.fori_loop(2 * (n_loop // 2), n_loop, pair_body, 0)
    t0 = 2 * n_loop
    odd = n_far % 2 == 1

    @pl.when(odd)
    def _():
        accumulate(0, jnp.maximum(t0 - 2, 0))
        accumulate(1, jnp.maximum(t0 - 1, 0))
        softmax(0, t0, _KIND_FAR)
        logits(0, qi)
        softmax(1, qi - 1, _KIND_PREV)
        accumulate(0, t0)
        softmax(0, qi, _KIND_DIAG)
        accumulate(1, qi - 1)
        accumulate(0, qi)

    @pl.when(jnp.logical_not(odd) & (qi >= 1))
    def _():
        accumulate(0, jnp.maximum(t0 - 2, 0))
        accumulate(1, jnp.maximum(t0 - 1, 0))
        softmax(0, qi - 1, _KIND_PREV)
        softmax(1, qi, _KIND_DIAG)
        accumulate(0, qi - 1)
        accumulate(1, qi)

    @pl.when(qi == 0)
    def _():
        softmax(0, 0, _KIND_DIAG)
        accumulate(0, 0)


def _moba_kernel(bias_ref, qt_ref, k_ref, vt_ref, kmean_ref, bkt_ref, o_ref,
                 qa_ref, tab_ref, bound_ref, s_ref, p_ref, alpha_ref, m_ref, acc_ref):
    pair = pl.program_id(1)
    qi = pl.program_id(2)
    q_t = qt_ref[0]
    kmean = kmean_ref[0].astype(jnp.bfloat16)
    blk = lax.broadcasted_iota(jnp.int32, (FEAT_LO, TILE), 0)
    blk_f = blk.astype(jnp.float32)
    past = blk < qi
    lane = lax.broadcasted_iota(jnp.int32, (TILE, LANES), 1)
    far_bias = [bias_ref[(REL_BUCKETS - 1) * N_BIAS_HEADS + pair * 2 + half] for half in range(2)]

    @pl.when(qi == 0)
    def _():
        for half in range(2):
            bias_max = jnp.full((1, 1), far_bias[half], jnp.float32)
            for kind in (_KIND_DIAG, _KIND_PREV):
                tile = _build_bias_tile(bkt_ref[kind], bias_ref, pair * 2 + half)
                tab_ref[kind, :, half * TILE:(half + 1) * TILE] = tile
                bias_max = jnp.maximum(bias_max, jnp.max(jnp.max(tile, axis=0, keepdims=True), axis=1, keepdims=True))
            bound_ref[half] = jnp.broadcast_to(bias_max, (SUBLANES, LANES))

        def norm_body(c, mx):
            blk_k = k_ref[0, pl.ds(pl.multiple_of(c * TILE, TILE), TILE), :].astype(jnp.float32)
            return jnp.maximum(mx, jnp.sum(blk_k * blk_k, axis=1, keepdims=True))
        n2 = lax.fori_loop(0, k_ref.shape[1] // TILE, norm_body, jnp.zeros((TILE, 1), jnp.float32))
        bound_ref[2] = jnp.broadcast_to(jnp.sqrt(jnp.max(n2, axis=0, keepdims=True)), (SUBLANES, LANES))

    for half in range(2):
        qh = jnp.where(_row_half_mask((LANES, TILE), half), q_t, jnp.zeros_like(q_t))
        qf = qh.astype(jnp.float32)
        logit_bound = (jnp.sqrt(jnp.sum(qf * qf, axis=0, keepdims=True)) * bound_ref[2][:1, :1] * BOUND_SLACK
                       + bound_ref[half][:1, :1])
        gate = _dot(kmean, qh)[:FEAT_LO]
        g = jnp.where(past, gate, -jnp.inf)
        sel = jnp.zeros((FEAT_LO, TILE), jnp.bool_)
        for _ in range(MOBA_TOPK):
            mx = jnp.max(g, axis=0, keepdims=True)
            first = jnp.min(jnp.where(g == mx, blk_f, float(LANES)), axis=0, keepdims=True)
            pick = blk_f == first
            sel = sel | (pick & past)
            g = jnp.where(pick, -jnp.inf, g)
        b31 = jnp.full((FEAT_LO, TILE), far_bias[half], jnp.float32)
        b31_hi = b31.astype(jnp.bfloat16).astype(jnp.float32)
        hi_part = jnp.where(sel, jnp.where(blk < qi - 1, b31_hi, 0.0), NEG_BIG)
        hi_part = jnp.where(past, hi_part, 0.0)
        lo_part = jnp.where(blk < qi - 1, b31 - b31_hi, 0.0)
        bound_part = jnp.where(lax.broadcasted_iota(jnp.int32, (LANES - 2 * FEAT_LO, TILE), 0) == 0,
                               -logit_bound, 0.0)
        feat = jnp.concatenate([hi_part, lo_part, bound_part], axis=0)
        qa_ref[:, half * TILE:(half + 1) * TILE] = jnp.concatenate([qh, feat.astype(jnp.bfloat16)], axis=0)

    def logits(slot, kj):
        k_t = k_ref[0, pl.ds(pl.multiple_of(kj * TILE, TILE), TILE), :]
        feats = ((lane == kj) | (lane == kj + FEAT_LO) | (lane == 2 * FEAT_LO)).astype(jnp.bfloat16)
        s_ref[slot] = _dot(jnp.concatenate([k_t, feats], axis=1), qa_ref[...])

    def biased_logits(slot, kind):
        s_t = s_ref[slot]
        if kind != _KIND_FAR:
            s_t = s_t + tab_ref[kind]
        return s_t

    def softmax_shifted(slot, kj, kind):
        p_ref[slot] = jnp.exp2(biased_logits(slot, kind)).astype(jnp.bfloat16)

    def accumulate_shifted(slot, kj):
        acc_ref[...] += _dot(_with_ones_rows(vt_ref[0, kj]), p_ref[slot])

    def softmax_online(slot, kj, kind):
        _softmax_stage(biased_logits(slot, kind), m_ref, alpha_ref.at[slot], p_ref.at[slot])

    def accumulate_online(slot, kj):
        _accumulate_stage(vt_ref[0, kj], alpha_ref.at[slot], p_ref.at[slot], acc_ref)

    _init_softmax_state(m_ref, alpha_ref, p_ref, acc_ref)
    _run_tile_pipeline(qi, logits, softmax_shifted, accumulate_shifted)

    @pl.when(jnp.logical_not(jnp.min(acc_ref[LANES:LANES + 1, :]) > MIN_SHIFTED_DENOM))
    def _():
        _init_softmax_state(m_ref, alpha_ref, p_ref, acc_ref)
        _run_tile_pipeline(qi, logits, softmax_online, accumulate_online)

    o_ref[0] = _normalized_pair(acc_ref[...], TILE).astype(o_ref.dtype)


def _moba_call(bias_flat, q_t, k, v_t, kmean, bkt):
    b, _, t = q_t.shape
    n_pairs = MOBA_HEADS // 2
    n_chunks = t // TILE
    grid_spec = pltpu.PrefetchScalarGridSpec(
        num_scalar_prefetch=1,
        grid=(b, n_pairs, n_chunks),
        in_specs=[
            pl.BlockSpec((1, LANES, TILE), lambda bi, p, qi, s: (bi, p, qi)),
            pl.BlockSpec((1, t, LANES), lambda bi, p, qi, s: (bi, 0, p)),
            pl.BlockSpec((1, n_chunks, LANES, TILE), lambda bi, p, qi, s: (bi, 0, p, 0)),
            pl.BlockSpec((1, LANES, LANES), lambda bi, p, qi, s: (bi, 0, p)),
            pl.BlockSpec((2, TILE, TILE), lambda bi, p, qi, s: (0, 0, 0)),
        ],
        out_specs=pl.BlockSpec((1, TILE, LANES), lambda bi, p, qi, s: (bi, qi, p)),
        scratch_shapes=[
            pltpu.VMEM((2 * LANES, 2 * TILE), jnp.bfloat16),
            pltpu.VMEM((2, TILE, 2 * TILE), jnp.float32),
            pltpu.VMEM((3, SUBLANES, LANES), jnp.float32),
            pltpu.VMEM((2, TILE, 2 * TILE), jnp.float32),
            pltpu.VMEM((2, TILE, 2 * TILE), jnp.bfloat16),
            pltpu.VMEM((2, SUBLANES, 2 * TILE), jnp.float32),
            pltpu.VMEM((SUBLANES, 2 * TILE), jnp.float32),
            pltpu.VMEM((ACC_ROWS, 2 * TILE), jnp.float32),
        ],
    )
    return pl.pallas_call(
        _moba_kernel,
        grid_spec=grid_spec,
        out_shape=jax.ShapeDtypeStruct((b, t, MOBA_W), jnp.bfloat16),
        compiler_params=pltpu.CompilerParams(dimension_semantics=("arbitrary",) * 3, vmem_limit_bytes=VMEM_LIMIT),
        name="moba",
    )(bias_flat, q_t, k, v_t, kmean, bkt)


def _dsa_kernel(bias_ref, qt_ref, k_ref, vt_ref, iqt_ref, ik_ref, iwt_ref, bkt_ref, o_ref,
                sc_ref, tab_ref, bound_ref, qz_ref, s_ref, p_ref, alpha_ref, seen_ref, m_ref, acc_ref):
    qi = pl.program_id(1)
    n_chunks = qi + 1
    key = lax.broadcasted_iota(jnp.int32, (TILE, TILE), 0)
    qry = lax.broadcasted_iota(jnp.int32, (TILE, TILE), 1)
    causal = key <= qry

    far_bias = [bias_ref[(REL_BUCKETS - 1) * N_BIAS_HEADS + MOBA_HEADS + h] for h in range(DSA_HEADS)]

    @pl.when(qi == 0)
    def _():
        for h in range(DSA_HEADS):
            rel_max = jnp.zeros((1, 1), jnp.float32)
            for kind in (_KIND_DIAG, _KIND_PREV):
                tile = _build_bias_tile(bkt_ref[kind], bias_ref, MOBA_HEADS + h) - far_bias[h]
                tab_ref[h // 2, kind, :, (h % 2) * TILE:(h % 2 + 1) * TILE] = tile
                rel_max = jnp.maximum(rel_max, jnp.max(jnp.max(tile, axis=0, keepdims=True), axis=1, keepdims=True))
            bound_ref[h] = jnp.broadcast_to(rel_max + far_bias[h], (SUBLANES, LANES))

        for pr in range(DSA_HEADS // 2):
            def norm_body(c, mx, pr=pr):
                blk = k_ref[0, pl.ds(pl.multiple_of(c * TILE, TILE), TILE), pr * LANES:(pr + 1) * LANES]
                blk = blk.astype(jnp.float32)
                return jnp.maximum(mx, jnp.sum(blk * blk, axis=1, keepdims=True))
            n2 = lax.fori_loop(0, k_ref.shape[1] // TILE, norm_body, jnp.zeros((TILE, 1), jnp.float32))
            bound_ref[DSA_HEADS + pr] = jnp.broadcast_to(jnp.sqrt(jnp.max(n2, axis=0, keepdims=True)),
                                                         (SUBLANES, LANES))

    feat_row = lax.broadcasted_iota(jnp.int32, (LANES, TILE), 0)
    for h in range(DSA_HEADS):
        blk = qt_ref[0, (h // 2) * LANES:(h // 2 + 1) * LANES, :]
        qh = jnp.where(_row_half_mask((LANES, TILE), h % 2), blk, jnp.zeros_like(blk))
        qf = qh.astype(jnp.float32)
        q_norm = jnp.sqrt(jnp.sum(qf * qf, axis=0, keepdims=True))
        logit_bound = q_norm * bound_ref[DSA_HEADS + h // 2][:1, :1] * BOUND_SLACK + bound_ref[h][:1, :1]
        b31 = jnp.full((LANES, TILE), far_bias[h], jnp.float32)
        b31_hi = b31.astype(jnp.bfloat16).astype(jnp.float32)
        feat = jnp.where(feat_row == 0, b31_hi, jnp.where(feat_row == 1, b31 - b31_hi, 0.0))
        feat = jnp.where(feat_row == 2, -logit_bound, feat)
        qz_ref[h // 2, :, (h % 2) * TILE:(h % 2 + 1) * TILE] = jnp.concatenate(
            [qh, feat.astype(jnp.bfloat16)], axis=0)

    iw_t = iwt_ref[0]

    def score_chunk(kc):
        row0 = pl.multiple_of(kc * TILE, TILE)
        ik_t = ik_ref[0, pl.ds(row0, TILE), :]
        tot = jnp.zeros((TILE, TILE), jnp.float32)
        for h in range(IDX_HEADS):
            x = _dot(ik_t, iqt_ref[0, h * IDX_DIM:(h + 1) * IDX_DIM, :])
            tot = tot + iw_t[h:h + 1, :] * jnp.maximum(x, 0.0)
        return tot

    def score_body(kc, carry):
        mx, mn = carry
        tot = score_chunk(kc)
        sc_ref[kc] = tot
        return (jnp.maximum(mx, jnp.max(tot, axis=0, keepdims=True)),
                jnp.minimum(mn, jnp.min(tot, axis=0, keepdims=True)))

    def score_pair_body(j, carry):
        return score_body(2 * j + 1, score_body(2 * j, carry))

    extremes = lax.fori_loop(0, qi // 2, score_pair_body, (jnp.full((1, TILE), -jnp.inf, jnp.float32),
                                                           jnp.full((1, TILE), jnp.inf, jnp.float32)))
    row_max, row_min = lax.cond(qi % 2 == 1, lambda c: score_body(qi - 1, c), lambda c: c, extremes)
    tot = score_chunk(qi)
    sc_ref[qi] = jnp.where(causal, tot, -jnp.inf)
    row_max = jnp.maximum(row_max, jnp.max(jnp.where(causal, tot, -jnp.inf), axis=0, keepdims=True))
    row_min = jnp.minimum(row_min, jnp.min(jnp.where(causal, tot, jnp.inf), axis=0, keepdims=True))

    @pl.when(qi % 2 == 0)
    def _():
        sc_ref[qi + 1] = jnp.full((TILE, TILE), -jnp.inf, jnp.float32)

    def reduce_pass(fn, init, combine):
        def body(j, part):
            return combine(part, combine(fn(sc_ref[2 * j]), fn(sc_ref[2 * j + 1])))
        return lax.fori_loop(0, (n_chunks + 1) // 2, body, jnp.full((1, TILE), init, jnp.float32))

    def count(pred):
        return reduce_pass(lambda x: jnp.sum(jnp.where(pred(x), 1.0, 0.0), axis=0, keepdims=True), 0.0, jnp.add)

    def min_ge(p):
        return reduce_pass(lambda x: jnp.min(jnp.where(x >= p, x, jnp.inf), axis=0, keepdims=True),
                           jnp.inf, jnp.minimum)

    n_adm =(qi * TILE + 1 + lax.broadcasted_iota(jnp.int32, (1, TILE), 1)).astype(jnp.float32)
    k_top = float(DSA_TOPK)
    take_all = n_adm <= k_top

    def search_cond(st):
        return jnp.logical_and(st[0] < MAX_SEARCH_ITERS, st[-1] >= some_live)

    all_done, some_live, some_stalled = 0.0, 1.0, 2.0

    def live_flag(done, stall):
        return jnp.max(jnp.where(done < 0.5, jnp.where(stall >= STALL_STEPS, some_stalled, some_live), all_done))

    def tie_check(st):
        it, lo, hi, hi_fin, c_lo, c_hi, done, last, streak, stall, _ = st
        a = min_ge(lo)
        c_gt = count(lambda x: x > a)
        tie = (c_gt < k_top) & (done < 0.5)
        done = jnp.where(tie, 1.0, done)
        stall = jnp.zeros_like(stall)
        return (it, lo, hi, hi_fin, c_lo, jnp.where(tie, c_gt, c_hi), done, last, streak, stall,
                live_flag(done, stall))

    def search_step(st):
        it, lo, hi, hi_fin, c_lo, c_hi, done, last, streak, stall = st
        width = c_lo - c_hi
        target = k_top - 0.5 + jnp.where(streak >= 1.0, last * STALE_LEAN * (c_lo - c_hi), 0.0)
        target = jnp.clip(target, c_hi + 0.5, c_lo - 0.5)
        frac = (c_lo - target) / jnp.maximum(c_lo - c_hi, 1.0)
        mid = 0.5 * lo + 0.5 * hi_fin
        p = jnp.where(it % BISECT_EVERY == BISECT_EVERY - 1, mid, lo + frac * (hi_fin - lo))
        p = jnp.where((p > lo) & (p < hi), p, mid)
        stuck = jnp.logical_not((mid > lo) & (mid < hi))
        c = count(lambda x: x >= p)
        up = c >= k_top
        move = jnp.logical_not((done > 0.5) | stuck)
        rise = move & up
        fall = move & jnp.logical_not(up)
        lo = jnp.where(rise, p, lo)
        c_lo = jnp.where(rise, c, c_lo)
        hi = jnp.where(fall, p, hi)
        c_hi = jnp.where(fall, c, c_hi)
        hi_fin = jnp.where(fall, p, hi_fin)
        side = jnp.where(up, -1.0, 1.0)
        streak = jnp.where(side == last, streak + 1.0, 0.0)
        done = jnp.where((c_lo - c_hi <= 1.0) | (c_lo == k_top) | stuck, 1.0, done)
        stall = jnp.where(c_lo - c_hi == width, stall + 1.0, 0.0)
        return it + 1, lo, hi, hi_fin, c_lo, c_hi, done, side, streak, stall

    def search_body(st):
        st = search_step(search_step(search_step(search_step(st[:-1]))))
        flag = live_flag(st[6], st[9])
        return lax.cond(flag >= some_stalled, tie_check, lambda s: s, st + (flag,))

    zeros_row = jnp.zeros((1, TILE), jnp.float32)
    done0 = take_all.astype(jnp.float32)
    init = (jnp.int32(0), row_min, jnp.full((1, TILE), jnp.inf, jnp.float32), row_max,
            n_adm, zeros_row, done0, zeros_row, zeros_row, zeros_row, live_flag(done0, zeros_row))
    _, lo, _, _, _, c_hi, _, _, _, _, _ = lax.while_loop(search_cond, search_body, init)
    thr = jnp.where(take_all, -jnp.inf, min_ge(lo))
    need = k_top - jnp.where(take_all, 0.0, c_hi)

    n_pairs = DSA_HEADS // 2
    lower = (qry < key).astype(jnp.bfloat16)
    ones_feat = (lax.broadcasted_iota(jnp.int32, (TILE, LANES), 1) < 3).astype(jnp.bfloat16)

    def init_state():
        for pr in range(n_pairs):
            _init_softmax_state(m_ref.at[pr], alpha_ref.at[:, pr], p_ref.at[:, pr], acc_ref.at[pr])
        seen_ref[...] = jnp.zeros(seen_ref.shape, jnp.float32)

    def logits(slot, kc):
        row0 = pl.multiple_of(kc * TILE, TILE)
        for pr in range(n_pairs):
            k_t = k_ref[0, pl.ds(row0, TILE), pr * LANES:(pr + 1) * LANES]
            s_ref[slot, pr] = _dot(jnp.concatenate([k_t, ones_feat], axis=1), qz_ref[pr])

    def masked_logits(slot, kc, kind):
        x = sc_ref[kc]
        tie = x == thr
        tie_f = jnp.where(tie, 1.0, 0.0)
        seen = seen_ref[:1]
        rank = seen + _dot(lower, tie_f.astype(jnp.bfloat16))
        selected = (x > thr) | (tie & (rank < need))
        if kind == _KIND_DIAG:
            selected = selected & causal
        mask_add = jnp.where(selected, 0.0, NEG_BIG)
        mask_add = jnp.concatenate([mask_add, mask_add], axis=1)
        seen_ref[...] = jnp.broadcast_to(seen + jnp.sum(tie_f, axis=0, keepdims=True), seen_ref.shape)
        for pr in range(n_pairs):
            s_t = s_ref[slot, pr] + mask_add
            if kind != _KIND_FAR:
                s_t = s_t + tab_ref[pr, kind]
            yield pr, s_t

    def softmax_shifted(slot, kc, kind):
        for pr, s_t in masked_logits(slot, kc, kind):
            p_ref[slot, pr] = jnp.exp2(s_t).astype(jnp.bfloat16)

    def accumulate_shifted(slot, kc):
        for pr in range(n_pairs):
            acc_ref[pr] += _dot(_with_ones_rows(vt_ref[0, kc, pr * LANES:(pr + 1) * LANES, :]), p_ref[slot, pr])

    def softmax_online(slot, kc, kind):
        for pr, s_t in masked_logits(slot, kc, kind):
            _softmax_stage(s_t, m_ref.at[pr], alpha_ref.at[slot, pr], p_ref.at[slot, pr])

    def accumulate_online(slot, kc):
        for pr in range(n_pairs):
            _accumulate_stage(vt_ref[0, kc, pr * LANES:(pr + 1) * LANES, :], alpha_ref.at[slot, pr],
                              p_ref.at[slot, pr], acc_ref.at[pr])

    init_state()
    _run_tile_pipeline(qi, logits, softmax_shifted, accumulate_shifted)
    denom = acc_ref[0, LANES:LANES + 1, :]
    for pr in range(1, n_pairs):
        denom = jnp.minimum(denom, acc_ref[pr, LANES:LANES + 1, :])
    denom_min = jnp.min(denom)

    @pl.when(jnp.logical_not(denom_min > MIN_SHIFTED_DENOM))
    def _():
        init_state()
        _run_tile_pipeline(qi, logits, softmax_online, accumulate_online)

    for pr in range(n_pairs):
        o_ref[0, :, pr * LANES:(pr + 1) * LANES] = _normalized_pair(acc_ref[pr], TILE).astype(o_ref.dtype)


def _dsa_call(bias_flat, q_t, k, v_t, iq_t, ik, iw_t, bkt):
    b, _, t = q_t.shape
    n_chunks = t // TILE
    n_pairs = DSA_HEADS // 2
    grid_spec = pltpu.PrefetchScalarGridSpec(
        num_scalar_prefetch=1,
        grid=(b, n_chunks),
        in_specs=[
            pl.BlockSpec((1, DSA_W, TILE), lambda bi, qi, s: (bi, 0, qi)),
            pl.BlockSpec((1, t, DSA_W), lambda bi, qi, s: (bi, 0, 0)),
            pl.BlockSpec((1, n_chunks, DSA_W, TILE), lambda bi, qi, s: (bi, 0, 0, 0)),
            pl.BlockSpec((1, IDX_W, TILE), lambda bi, qi, s: (bi, 0, qi)),
            pl.BlockSpec((1, t, IDX_DIM), lambda bi, qi, s: (bi, 0, 0)),
            pl.BlockSpec((1, IDX_HEADS, TILE), lambda bi, qi, s: (bi, 0, qi)),
            pl.BlockSpec((2, TILE, TILE), lambda bi, qi, s: (0, 0, 0)),
        ],
        out_specs=pl.BlockSpec((1, TILE, DSA_W), lambda bi, qi, s: (bi, qi, 0)),
        scratch_shapes=[
            pltpu.VMEM((n_chunks, TILE, TILE), jnp.float32),
            pltpu.VMEM((n_pairs, 2, TILE, 2 * TILE), jnp.float32),
            pltpu.VMEM((DSA_HEADS + n_pairs, SUBLANES, LANES), jnp.float32),
            pltpu.VMEM((n_pairs, 2 * LANES, 2 * TILE), jnp.bfloat16),
            pltpu.VMEM((2, n_pairs, TILE, 2 * TILE), jnp.float32),
            pltpu.VMEM((2, n_pairs, TILE, 2 * TILE), jnp.bfloat16),
            pltpu.VMEM((2, n_pairs, SUBLANES, 2 * TILE), jnp.float32),
            pltpu.VMEM((SUBLANES, TILE), jnp.float32),
            pltpu.VMEM((n_pairs, SUBLANES, 2 * TILE), jnp.float32),
            pltpu.VMEM((n_pairs, ACC_ROWS, 2 * TILE), jnp.float32),
        ],
    )
    return pl.pallas_call(
        _dsa_kernel,
        grid_spec=grid_spec,
        out_shape=jax.ShapeDtypeStruct((b, t, DSA_W), jnp.bfloat16),
        compiler_params=pltpu.CompilerParams(dimension_semantics=("arbitrary",) * 2, vmem_limit_bytes=VMEM_LIMIT),
        name="dsa",
    )(bias_flat, q_t, k, v_t, iq_t, ik, iw_t, bkt)


def _lane_half_mask(shape, half):
    lane = lax.broadcasted_iota(jnp.int32, shape, 1)
    return (lane >= half * HEAD_DIM) & (lane < (half + 1) * HEAD_DIM)


def _final_kernel(x_ref, g_ref, fg_ref, ya_ref, za_ref, yb_ref, zb_ref, qm_ref, zm_ref, mk_ref, mv_ref,
                  wg_ref, wb_ref, wo_ref, o_ref):
    x = x_ref[...]
    h = _rmsnorm_rows(x, g_ref[...]).astype(jnp.bfloat16)

    ym_parts = []
    for pr in range(MEM_HEADS // 2):
        qp = qm_ref[:, pr * LANES:(pr + 1) * LANES]
        mk = mk_ref[0, :, pr * LANES:(pr + 1) * LANES]
        mv = mv_ref[0, :, pr * LANES:(pr + 1) * LANES]
        outs = []
        for half in range(2):
            qh = jnp.where(_lane_half_mask((ROW_TILE, LANES), half), qp, jnp.zeros_like(qp))
            s = _dot_nt(qh, mk)
            p = jnp.exp2(s - jnp.max(s, axis=1, keepdims=True))
            outs.append(_dot(p.astype(jnp.bfloat16), mv) / jnp.sum(p, axis=1, keepdims=True))
        ym_parts.append(jnp.where(_lane_half_mask((ROW_TILE, LANES), 0), outs[0], outs[1]))
    ym = jnp.concatenate(ym_parts, axis=1)

    def gated(y, z_ref):
        z = z_ref[...].astype(jnp.float32)
        return (y.astype(jnp.float32) * (z * jax.nn.sigmoid(z))).astype(jnp.bfloat16)

    ua = gated(ya_ref[...], za_ref)
    ub = gated(yb_ref[...], zb_ref)
    um = gated(ym, zm_ref)
    merged = jax.nn.sigmoid(_dot(h, wg_ref[:, :D_MODEL])) * _dot(ua, wb_ref[:MOBA_W])
    merged += jax.nn.sigmoid(_dot(h, wg_ref[:, D_MODEL:2 * D_MODEL])) * _dot(ub, wb_ref[MOBA_W:MOBA_W + DSA_W])
    merged += jax.nn.sigmoid(_dot(h, wg_ref[:, 2 * D_MODEL:])) * _dot(um, wb_ref[MOBA_W + DSA_W:])
    y = x + _dot(merged.astype(jnp.bfloat16), wo_ref[...])
    o_ref[...] = _rmsnorm_rows(y, fg_ref[...])


def _final_call(x2, gain, final_gain, ya, za, yb, zb, qm, zm, mk, mv, w_gates, w_branch, w_out, rows_per_batch):
    rows = x2.shape[0]
    tiles_per_batch = rows_per_batch // ROW_TILE
    row_spec = lambda width: pl.BlockSpec((ROW_TILE, width), lambda i: (i, 0))
    const_spec = lambda shape: pl.BlockSpec(shape, lambda i: (0,) * len(shape))
    mem_spec = pl.BlockSpec((1,) + mk.shape[1:], lambda i: (i // tiles_per_batch, 0, 0))
    return pl.pallas_call(
        _final_kernel,
        grid=(rows // ROW_TILE,),
        in_specs=[row_spec(D_MODEL), const_spec((1, D_MODEL)), const_spec((1, D_MODEL)),
                  row_spec(MOBA_W), row_spec(MOBA_W), row_spec(DSA_W), row_spec(DSA_W),
                  row_spec(MEM_W), row_spec(MEM_W), mem_spec, mem_spec,
                  const_spec(w_gates.shape), const_spec(w_branch.shape), const_spec(w_out.shape)],
        out_specs=row_spec(D_MODEL),
        out_shape=jax.ShapeDtypeStruct((rows, D_MODEL), jnp.float32),
        compiler_params=pltpu.CompilerParams(dimension_semantics=("arbitrary",), vmem_limit_bytes=VMEM_LIMIT),
        name="final",
    )(x2, gain, final_gain, ya, za, yb, zb, qm, zm, mk, mv, w_gates, w_branch, w_out)


def _layer(x, mem, norm_gain, w_in, rel_bias, mem_norm_gain, w_mem_kv, w_branch, w_out, final_gain):
    b, t, _ = x.shape
    assert t % (2 * TILE) == 0 and t // MOBA_BLOCK <= FEAT_LO and t >= 4 * DSA_TOPK
    bf = jnp.bfloat16
    names = ("qa", "ka", "va", "za", "qb", "kb", "vb", "zb", "iq", "ik", "iw", "qm", "zm", "ga", "gb", "gm")
    cols, off = {}, 0
    for name, size in zip(names, IN_SIZES):
        cols[name] = w_in[:, off:off + size]
        off += size
    w_std = jnp.concatenate([cols[n] for n, _, _ in _STD_SEGS], axis=1).astype(bf)
    w_t = jnp.concatenate([cols[n[:-1]] for n, _, _ in _T_SEGS], axis=1).T.astype(bf)
    w_gates = jnp.concatenate([cols["ga"], cols["gb"], cols["gm"]], axis=1).astype(bf)

    x2 = x.reshape(b * t, D_MODEL)
    (ka, za, kb, zb, qm, zm, ik, qa_t, va_t, qb_t, vb_t, iq_t, iw_t, kmean) = _proj_call(
        x2, norm_gain.reshape(1, D_MODEL), w_std, w_t, b, t)
    seq = lambda a: a.reshape(b, t, a.shape[-1])
    n_blocks = t // MOBA_BLOCK
    kmean = jnp.pad(kmean.reshape(b, n_blocks, MOBA_W), ((0, 0), (0, LANES - n_blocks), (0, 0)))

    bias_flat = (rel_bias * LOG2E).reshape(-1)
    ya = _moba_call(bias_flat, qa_t, seq(ka), va_t, kmean, _bucket_tiles(True))
    yb = _dsa_call(bias_flat, qb_t, seq(kb), vb_t, iq_t, seq(ik), iw_t, _bucket_tiles(False))
    mk, mv = _memkv_call(mem, mem_norm_gain.reshape(1, D_MODEL), w_mem_kv.astype(bf))
    out = _final_call(x2, norm_gain.reshape(1, D_MODEL), final_gain.reshape(1, D_MODEL),
                      ya.reshape(b * t, MOBA_W), za, yb.reshape(b * t, DSA_W), zb, qm, zm, mk, mv,
                      w_gates, w_branch.astype(bf), w_out.astype(bf), t)
    return out.reshape(b, t, D_MODEL)


@jax.jit
def kernel(x, mem, norm_gain, w_in, rel_bias, mem_norm_gain, w_mem_kv, w_branch, w_out, final_norm_gain):
    assert norm_gain.shape[0] == 1, "one layer"
    return _layer(x, mem, norm_gain[0], w_in[0], rel_bias, mem_norm_gain[0], w_mem_kv[0], w_branch[0],
                  w_out[0], final_norm_gain)
```

```python
import math

import numpy as np
import jax
import jax.numpy as jnp
from jax import lax
from jax.experimental import pallas as pl
from jax.experimental.pallas import tpu as pltpu

D_MODEL = 1024
HEAD_DIM = 64
MOBA_HEADS = 6
DSA_HEADS = 6
MEM_HEADS = 4
MOBA_W = MOBA_HEADS * HEAD_DIM
DSA_W = DSA_HEADS * HEAD_DIM
MEM_W = MEM_HEADS * HEAD_DIM
IDX_HEADS = 8
IDX_DIM = 64
IDX_W = IDX_HEADS * IDX_DIM
MOBA_BLOCK = 256
MOBA_TOPK = 3
DSA_TOPK = 256
REL_BUCKETS = 32
REL_MAX_DIST = 128
N_BIAS_HEADS = MOBA_HEADS + DSA_HEADS
EPS = 1e-6
IN_SIZES = (MOBA_W,) * 4 + (DSA_W,) * 4 + (IDX_W, IDX_DIM, IDX_HEADS) + (MEM_W, MEM_W) + (D_MODEL,) * 3

LANES = 128
SUBLANES = 8
ONES_ROWS = 16
ACC_ROWS = LANES + ONES_ROWS
TILE = 256
ROW_TILE = 512
FEAT_LO = 32
LOG2E = math.log2(math.e)
NEG_BIG = -1e30
MASK_BUCKET = REL_BUCKETS
VMEM_LIMIT = 56 * 1024 * 1024
MAX_SEARCH_ITERS = 96
BISECT_EVERY = 5
STALE_LEAN = 0.25
STALL_STEPS = 3.0
BOUND_SLACK = 1.001
MIN_SHIFTED_DENOM = 2.0 ** -60

_NT = (((1,), (1,)), ((), ()))


def _dot_nt(a, b):
    return lax.dot_general(a, b, _NT, preferred_element_type=jnp.float32)


def _dot(a, b):
    return jnp.dot(a, b, preferred_element_type=jnp.float32)


def _rel_bucket_np(n):
    exact = REL_BUCKETS // 2
    nf = np.maximum(n, 1).astype(np.float32)
    large = exact + (np.log(nf / np.float32(exact)) / np.float32(math.log(REL_MAX_DIST / exact))
                     * np.float32(REL_BUCKETS - exact)).astype(np.int32)
    return np.where(n < exact, n, np.minimum(large, REL_BUCKETS - 1))


def _bucket_tiles(mask_future):
    j = np.arange(TILE)[:, None]
    i = np.arange(TILE)[None, :]
    d0 = i - j
    b0 = _rel_bucket_np(np.maximum(d0, 0))
    if mask_future:
        b0 = np.where(d0 >= 0, b0, MASK_BUCKET)
    b1 = _rel_bucket_np(TILE + d0)
    assert int(_rel_bucket_np(np.array([TILE + 1]))[0]) == REL_BUCKETS - 1
    return jnp.asarray(np.stack([b0, b1]).astype(np.int32))


def _build_bias_tile(bkt, bias_ref, head):
    acc = jnp.where(bkt == MASK_BUCKET, NEG_BIG, 0.0)
    for b in range(REL_BUCKETS):
        acc = jnp.where(bkt == b, bias_ref[b * N_BIAS_HEADS + head], acc)
    return acc


def _rmsnorm_rows(x, g):
    xf = x.astype(jnp.float32)
    return xf * lax.rsqrt(jnp.mean(xf * xf, axis=-1, keepdims=True) + EPS) * g


_Q_SCALE = HEAD_DIM ** -0.5 * LOG2E
_STD_SEGS = (("ka", MOBA_W, 1.0), ("za", MOBA_W, 1.0), ("kb", DSA_W, 1.0), ("zb", DSA_W, 1.0),
             ("qm", MEM_W, _Q_SCALE), ("zm", MEM_W, 1.0), ("ik", IDX_DIM, 1.0))
_T_SEGS = (("qaT", MOBA_W, _Q_SCALE), ("vaT", MOBA_W, 1.0), ("qbT", DSA_W, _Q_SCALE), ("vbT", DSA_W, 1.0),
           ("iqT", IDX_W, IDX_DIM ** -0.5), ("iwT", IDX_HEADS, IDX_HEADS ** -0.5))
_STD_W = sum(w for _, w, _ in _STD_SEGS)
_T_W = sum(w for _, w, _ in _T_SEGS)
_CHUNKS_PER_ROW_TILE = ROW_TILE // TILE


def _proj_kernel(x_ref, g_ref, w_ref, wt_ref, *out_refs):
    std_refs = out_refs[:len(_STD_SEGS)]
    t_refs = out_refs[len(_STD_SEGS):len(_STD_SEGS) + len(_T_SEGS)]
    kmean_ref = out_refs[-1]
    h = _rmsnorm_rows(x_ref[...], g_ref[...]).astype(jnp.bfloat16)
    off = 0
    for (name, width, scale), o_ref in zip(_STD_SEGS, std_refs):
        r = _dot(h, w_ref[:, off:off + width])
        if name == "ka":
            for blk in range(ROW_TILE // MOBA_BLOCK):
                kmean_ref[0, blk:blk + 1, :] = jnp.mean(
                    r[blk * MOBA_BLOCK:(blk + 1) * MOBA_BLOCK], axis=0, keepdims=True)
        if scale != 1.0:
            r = r * scale
        o_ref[...] = r.astype(o_ref.dtype)
        off += width
    off = 0
    for (name, width, scale), o_ref in zip(_T_SEGS, t_refs):
        r = _dot_nt(wt_ref[off:off + width, :], h)
        if scale != 1.0:
            r = r * scale
        r = r.astype(o_ref.dtype)
        if name in ("vaT", "vbT"):
            for c in range(_CHUNKS_PER_ROW_TILE):
                o_ref[0, c] = r[:, c * TILE:(c + 1) * TILE]
        else:
            o_ref[0] = r
        off += width


def _proj_call(x2, gain, w_std, w_t, b, t):
    rows = x2.shape[0]
    tiles_per_batch = t // ROW_TILE
    n_tiles = rows // ROW_TILE
    row_spec = lambda width: pl.BlockSpec((ROW_TILE, width), lambda i: (i, 0))
    const_spec = lambda shape: pl.BlockSpec(shape, lambda i: (0,) * len(shape))
    t_spec = lambda width: pl.BlockSpec((1, width, ROW_TILE),
                                        lambda i: (i // tiles_per_batch, 0, i % tiles_per_batch))
    chunk_spec = lambda width: pl.BlockSpec((1, _CHUNKS_PER_ROW_TILE, width, TILE),
                                            lambda i: (i // tiles_per_batch, i % tiles_per_batch, 0, 0))
    out_shape = [jax.ShapeDtypeStruct((rows, w), jnp.bfloat16) for _, w, _ in _STD_SEGS]
    out_specs = [row_spec(w) for _, w, _ in _STD_SEGS]
    for name, w, _ in _T_SEGS:
        if name in ("vaT", "vbT"):
            out_shape.append(jax.ShapeDtypeStruct((b, t // TILE, w, TILE), jnp.bfloat16))
            out_specs.append(chunk_spec(w))
        else:
            out_shape.append(jax.ShapeDtypeStruct((b, w, t), jnp.float32 if name == "iwT" else jnp.bfloat16))
            out_specs.append(t_spec(w))
    out_shape.append(jax.ShapeDtypeStruct((n_tiles, ROW_TILE // MOBA_BLOCK, MOBA_W), jnp.float32))
    out_specs.append(pl.BlockSpec((1, ROW_TILE // MOBA_BLOCK, MOBA_W), lambda i: (i, 0, 0)))
    return pl.pallas_call(
        _proj_kernel,
        grid=(n_tiles,),
        in_specs=[row_spec(D_MODEL), const_spec((1, D_MODEL)), const_spec((D_MODEL, _STD_W)),
                  const_spec((_T_W, D_MODEL))],
        out_specs=out_specs,
        out_shape=out_shape,
        compiler_params=pltpu.CompilerParams(dimension_semantics=("arbitrary",), vmem_limit_bytes=VMEM_LIMIT),
        name="proj",
    )(x2, gain, w_std, w_t)


def _memkv_kernel(mem_ref, g_ref, w_ref, mk_ref, mv_ref):
    h = _rmsnorm_rows(mem_ref[0], g_ref[...]).astype(jnp.bfloat16)
    r = _dot(h, w_ref[...])
    mk_ref[0] = r[:, :MEM_W].astype(jnp.bfloat16)
    mv_ref[0] = r[:, MEM_W:].astype(jnp.bfloat16)


def _memkv_call(mem, gain, w_kv):
    b, n_mem, _ = mem.shape
    return pl.pallas_call(
        _memkv_kernel,
        grid=(b,),
        in_specs=[pl.BlockSpec((1, n_mem, D_MODEL), lambda i: (i, 0, 0)),
                  pl.BlockSpec((1, D_MODEL), lambda i: (0, 0)),
                  pl.BlockSpec((D_MODEL, 2 * MEM_W), lambda i: (0, 0))],
        out_specs=[pl.BlockSpec((1, n_mem, MEM_W), lambda i: (i, 0, 0))] * 2,
        out_shape=[jax.ShapeDtypeStruct((b, n_mem, MEM_W), jnp.bfloat16)] * 2,
        compiler_params=pltpu.CompilerParams(dimension_semantics=("arbitrary",), vmem_limit_bytes=VMEM_LIMIT),
        name="memkv",
    )(mem, gain, w_kv)


def _row_half_mask(shape, half):
    r = lax.broadcasted_iota(jnp.int32, shape, 0)
    return (r >= half * HEAD_DIM) & (r < (half + 1) * HEAD_DIM)


def _with_ones_rows(v_t):
    return jnp.concatenate([v_t, jnp.ones((ONES_ROWS, v_t.shape[1]), v_t.dtype)], axis=0)


def _softmax_stage(s_t, m_ref, alpha_ref, p_ref):
    m_old = m_ref[:1]
    m_new = jnp.maximum(m_old, jnp.max(s_t, axis=0, keepdims=True))
    alpha_ref[...] = jnp.broadcast_to(jnp.exp2(m_old - m_new), alpha_ref.shape)
    m_ref[...] = jnp.broadcast_to(m_new, m_ref.shape)
    p_ref[...] = jnp.exp2(s_t - m_new).astype(jnp.bfloat16)


def _accumulate_stage(v_t, alpha_ref, p_ref, acc_ref):
    acc_ref[...] = alpha_ref[:1] * acc_ref[...] + _dot(_with_ones_rows(v_t), p_ref[...])


def _init_softmax_state(m_ref, alpha_ref, p_ref, acc_ref):
    m_ref[...] = jnp.full(m_ref.shape, NEG_BIG, jnp.float32)
    acc_ref[...] = jnp.zeros(acc_ref.shape, jnp.float32)
    alpha_ref[...] = jnp.ones(alpha_ref.shape, jnp.float32)
    p_ref[...] = jnp.zeros(p_ref.shape, p_ref.dtype)


def _normalized_pair(acc, half_cols):
    out0 = acc[:LANES, :half_cols] / acc[LANES:LANES + 1, :half_cols]
    out1 = acc[:LANES, half_cols:] / acc[LANES:LANES + 1, half_cols:]
    return jnp.where(_row_half_mask((LANES, half_cols), 0), out0, out1).T


_KIND_DIAG, _KIND_PREV, _KIND_FAR = 0, 1, 2


def _run_tile_pipeline(qi, logits, softmax, accumulate):
    n_far = jnp.maximum(qi - 1, 0)
    n_loop = n_far // 2

    def far_pair(t):
        accumulate(0, jnp.maximum(t - 2, 0))
        accumulate(1, jnp.maximum(t - 1, 0))
        softmax(0, t, _KIND_FAR)
        logits(0, t + 2)
        softmax(1, t + 1, _KIND_FAR)
        logits(1, t + 3)

    def quad_body(j, carry):
        far_pair(4 * j)
        far_pair(4 * j + 2)
        return carry

    def pair_body(j, carry):
        far_pair(2 * j)
        return carry

    logits(0, 0)
    logits(1, jnp.minimum(1, qi))
    lax.fori_loop(0, n_loop // 2, quad_body, 0)
    lax.fori_loop(2 * (n_loop // 2), n_loop, pair_body, 0)
    t0 = 2 * n_loop
    odd = n_far % 2 == 1

    @pl.when(odd)
    def _():
        accumulate(0, jnp.maximum(t0 - 2, 0))
        accumulate(1, jnp.maximum(t0 - 1, 0))
        softmax(0, t0, _KIND_FAR)
        logits(0, qi)
        softmax(1, qi - 1, _KIND_PREV)
        accumulate(0, t0)
        softmax(0, qi, _KIND_DIAG)
        accumulate(1, qi - 1)
        accumulate(0, qi)

    @pl.when(jnp.logical_not(odd) & (qi >= 1))
    def _():
        accumulate(0, jnp.maximum(t0 - 2, 0))
        accumulate(1, jnp.maximum(t0 - 1, 0))
        softmax(0, qi - 1, _KIND_PREV)
        softmax(1, qi, _KIND_DIAG)
        accumulate(0, qi - 1)
        accumulate(1, qi)

    @pl.when(qi == 0)
    def _():
        softmax(0, 0, _KIND_DIAG)
        accumulate(0, 0)


def _moba_kernel(bias_ref, qt_ref, k_ref, vt_ref, kmean_ref, bkt_ref, o_ref,
                 qa_ref, tab_ref, s_ref, p_ref, alpha_ref, m_ref, acc_ref):
    pair = pl.program_id(1)
    qi = pl.program_id(2)
    q_t = qt_ref[0]
    kmean = kmean_ref[0].astype(jnp.bfloat16)
    blk = lax.broadcasted_iota(jnp.int32, (FEAT_LO, TILE), 0)
    blk_f = blk.astype(jnp.float32)
    past = blk < qi
    lane = lax.broadcasted_iota(jnp.int32, (TILE, LANES), 1)
    _init_softmax_state(m_ref, alpha_ref, p_ref, acc_ref)

    @pl.when(qi == 0)
    def _():
        for half in range(2):
            for kind in (_KIND_DIAG, _KIND_PREV):
                tab_ref[kind, :, half * TILE:(half + 1) * TILE] = _build_bias_tile(
                    bkt_ref[kind], bias_ref, pair * 2 + half)

    for half in range(2):
        qh = jnp.where(_row_half_mask((LANES, TILE), half), q_t, jnp.zeros_like(q_t))
        gate = _dot(kmean, qh)[:FEAT_LO]
        g = jnp.where(past, gate, -jnp.inf)
        sel = jnp.zeros((FEAT_LO, TILE), jnp.bool_)
        for _ in range(MOBA_TOPK):
            mx = jnp.max(g, axis=0, keepdims=True)
            first = jnp.min(jnp.where(g == mx, blk_f, float(LANES)), axis=0, keepdims=True)
            pick = blk_f == first
            sel = sel | (pick & past)
            g = jnp.where(pick, -jnp.inf, g)
        b31 = jnp.full((FEAT_LO, TILE), bias_ref[(REL_BUCKETS - 1) * N_BIAS_HEADS + pair * 2 + half], jnp.float32)
        b31_hi = b31.astype(jnp.bfloat16).astype(jnp.float32)
        hi_part = jnp.where(sel, jnp.where(blk < qi - 1, b31_hi, 0.0), NEG_BIG)
        hi_part = jnp.where(past, hi_part, 0.0)
        lo_part = jnp.where(blk < qi - 1, b31 - b31_hi, 0.0)
        feat = jnp.concatenate([hi_part, lo_part, jnp.zeros((LANES - 2 * FEAT_LO, TILE), jnp.float32)], axis=0)
        qa_ref[:, half * TILE:(half + 1) * TILE] = jnp.concatenate([qh, feat.astype(jnp.bfloat16)], axis=0)

    def logits(slot, kj):
        k_t = k_ref[0, pl.ds(pl.multiple_of(kj * TILE, TILE), TILE), :]
        onehot = ((lane == kj) | (lane == kj + FEAT_LO)).astype(jnp.bfloat16)
        s_ref[slot] = _dot(jnp.concatenate([k_t, onehot], axis=1), qa_ref[...])

    def softmax(slot, kj, kind):
        s_t = s_ref[slot]
        if kind != _KIND_FAR:
            s_t = s_t + tab_ref[kind]
        _softmax_stage(s_t, m_ref, alpha_ref.at[slot], p_ref.at[slot])

    def accumulate(slot, kj):
        _accumulate_stage(vt_ref[0, kj], alpha_ref.at[slot], p_ref.at[slot], acc_ref)

    _run_tile_pipeline(qi, logits, softmax, accumulate)
    o_ref[0] = _normalized_pair(acc_ref[...], TILE).astype(o_ref.dtype)


def _moba_call(bias_flat, q_t, k, v_t, kmean, bkt):
    b, _, t = q_t.shape
    n_pairs = MOBA_HEADS // 2
    n_chunks = t // TILE
    grid_spec = pltpu.PrefetchScalarGridSpec(
        num_scalar_prefetch=1,
        grid=(b, n_pairs, n_chunks),
        in_specs=[
            pl.BlockSpec((1, LANES, TILE), lambda bi, p, qi, s: (bi, p, qi)),
            pl.BlockSpec((1, t, LANES), lambda bi, p, qi, s: (bi, 0, p)),
            pl.BlockSpec((1, n_chunks, LANES, TILE), lambda bi, p, qi, s: (bi, 0, p, 0)),
            pl.BlockSpec((1, LANES, LANES), lambda bi, p, qi, s: (bi, 0, p)),
            pl.BlockSpec((2, TILE, TILE), lambda bi, p, qi, s: (0, 0, 0)),
        ],
        out_specs=pl.BlockSpec((1, TILE, LANES), lambda bi, p, qi, s: (bi, qi, p)),
        scratch_shapes=[
            pltpu.VMEM((2 * LANES, 2 * TILE), jnp.bfloat16),
            pltpu.VMEM((2, TILE, 2 * TILE), jnp.float32),
            pltpu.VMEM((2, TILE, 2 * TILE), jnp.float32),
            pltpu.VMEM((2, TILE, 2 * TILE), jnp.bfloat16),
            pltpu.VMEM((2, SUBLANES, 2 * TILE), jnp.float32),
            pltpu.VMEM((SUBLANES, 2 * TILE), jnp.float32),
            pltpu.VMEM((ACC_ROWS, 2 * TILE), jnp.float32),
        ],
    )
    return pl.pallas_call(
        _moba_kernel,
        grid_spec=grid_spec,
        out_shape=jax.ShapeDtypeStruct((b, t, MOBA_W), jnp.bfloat16),
        compiler_params=pltpu.CompilerParams(dimension_semantics=("arbitrary",) * 3, vmem_limit_bytes=VMEM_LIMIT),
        name="moba",
    )(bias_flat, q_t, k, v_t, kmean, bkt)


def _dsa_kernel(bias_ref, qt_ref, k_ref, vt_ref, iqt_ref, ik_ref, iwt_ref, bkt_ref, o_ref,
                sc_ref, tab_ref, bound_ref, qz_ref, s_ref, p_ref, alpha_ref, seen_ref, m_ref, acc_ref):
    qi = pl.program_id(1)
    n_chunks = qi + 1
    key = lax.broadcasted_iota(jnp.int32, (TILE, TILE), 0)
    qry = lax.broadcasted_iota(jnp.int32, (TILE, TILE), 1)
    causal = key <= qry

    far_bias = [bias_ref[(REL_BUCKETS - 1) * N_BIAS_HEADS + MOBA_HEADS + h] for h in range(DSA_HEADS)]

    @pl.when(qi == 0)
    def _():
        for h in range(DSA_HEADS):
            rel_max = jnp.zeros((1, 1), jnp.float32)
            for kind in (_KIND_DIAG, _KIND_PREV):
                tile = _build_bias_tile(bkt_ref[kind], bias_ref, MOBA_HEADS + h) - far_bias[h]
                tab_ref[h // 2, kind, :, (h % 2) * TILE:(h % 2 + 1) * TILE] = tile
                rel_max = jnp.maximum(rel_max, jnp.max(jnp.max(tile, axis=0, keepdims=True), axis=1, keepdims=True))
            bound_ref[h] = jnp.broadcast_to(rel_max + far_bias[h], (SUBLANES, LANES))

        for pr in range(DSA_HEADS // 2):
            def norm_body(c, mx, pr=pr):
                blk = k_ref[0, pl.ds(pl.multiple_of(c * TILE, TILE), TILE), pr * LANES:(pr + 1) * LANES]
                blk = blk.astype(jnp.float32)
                return jnp.maximum(mx, jnp.sum(blk * blk, axis=1, keepdims=True))
            n2 = lax.fori_loop(0, k_ref.shape[1] // TILE, norm_body, jnp.zeros((TILE, 1), jnp.float32))
            bound_ref[DSA_HEADS + pr] = jnp.broadcast_to(jnp.sqrt(jnp.max(n2, axis=0, keepdims=True)),
                                                         (SUBLANES, LANES))

    feat_row = lax.broadcasted_iota(jnp.int32, (LANES, TILE), 0)
    for h in range(DSA_HEADS):
        blk = qt_ref[0, (h // 2) * LANES:(h // 2 + 1) * LANES, :]
        qh = jnp.where(_row_half_mask((LANES, TILE), h % 2), blk, jnp.zeros_like(blk))
        qf = qh.astype(jnp.float32)
        q_norm = jnp.sqrt(jnp.sum(qf * qf, axis=0, keepdims=True))
        logit_bound = q_norm * bound_ref[DSA_HEADS + h // 2][:1, :1] * BOUND_SLACK + bound_ref[h][:1, :1]
        b31 = jnp.full((LANES, TILE), far_bias[h], jnp.float32)
        b31_hi = b31.astype(jnp.bfloat16).astype(jnp.float32)
        feat = jnp.where(feat_row == 0, b31_hi, jnp.where(feat_row == 1, b31 - b31_hi, 0.0))
        feat = jnp.where(feat_row == 2, -logit_bound, feat)
        qz_ref[h // 2, :, (h % 2) * TILE:(h % 2 + 1) * TILE] = jnp.concatenate(
            [qh, feat.astype(jnp.bfloat16)], axis=0)

    iw_t = iwt_ref[0]

    def score_chunk(kc):
        row0 = pl.multiple_of(kc * TILE, TILE)
        ik_t = ik_ref[0, pl.ds(row0, TILE), :]
        tot = jnp.zeros((TILE, TILE), jnp.float32)
        for h in range(IDX_HEADS):
            x = _dot(ik_t, iqt_ref[0, h * IDX_DIM:(h + 1) * IDX_DIM, :])
            tot = tot + iw_t[h:h + 1, :] * jnp.maximum(x, 0.0)
        return tot

    def score_body(kc, carry):
        mx, mn = carry
        tot = score_chunk(kc)
        sc_ref[kc] = tot
        return (jnp.maximum(mx, jnp.max(tot, axis=0, keepdims=True)),
                jnp.minimum(mn, jnp.min(tot, axis=0, keepdims=True)))

    def score_pair_body(j, carry):
        return score_body(2 * j + 1, score_body(2 * j, carry))

    extremes = lax.fori_loop(0, qi // 2, score_pair_body, (jnp.full((1, TILE), -jnp.inf, jnp.float32),
                                                           jnp.full((1, TILE), jnp.inf, jnp.float32)))
    row_max, row_min = lax.cond(qi % 2 == 1, lambda c: score_body(qi - 1, c), lambda c: c, extremes)
    tot = score_chunk(qi)
    sc_ref[qi] = jnp.where(causal, tot, -jnp.inf)
    row_max = jnp.maximum(row_max, jnp.max(jnp.where(causal, tot, -jnp.inf), axis=0, keepdims=True))
    row_min = jnp.minimum(row_min, jnp.min(jnp.where(causal, tot, jnp.inf), axis=0, keepdims=True))

    @pl.when(qi % 2 == 0)
    def _():
        sc_ref[qi + 1] = jnp.full((TILE, TILE), -jnp.inf, jnp.float32)

    def reduce_pass(fn, init, combine):
        def body(j, part):
            return combine(part, combine(fn(sc_ref[2 * j]), fn(sc_ref[2 * j + 1])))
        return lax.fori_loop(0, (n_chunks + 1) // 2, body, jnp.full((1, TILE), init, jnp.float32))

    def count(pred):
        return reduce_pass(lambda x: jnp.sum(jnp.where(pred(x), 1.0, 0.0), axis=0, keepdims=True), 0.0, jnp.add)

    def min_ge(p):
        return reduce_pass(lambda x: jnp.min(jnp.where(x >= p, x, jnp.inf), axis=0, keepdims=True),
                           jnp.inf, jnp.minimum)

    n_adm =(qi * TILE + 1 + lax.broadcasted_iota(jnp.int32, (1, TILE), 1)).astype(jnp.float32)
    k_top = float(DSA_TOPK)
    take_all = n_adm <= k_top

    def search_cond(st):
        return jnp.logical_and(st[0] < MAX_SEARCH_ITERS, st[-1] >= some_live)

    all_done, some_live, some_stalled = 0.0, 1.0, 2.0

    def live_flag(done, stall):
        return jnp.max(jnp.where(done < 0.5, jnp.where(stall >= STALL_STEPS, some_stalled, some_live), all_done))

    def tie_check(st):
        it, lo, hi, hi_fin, c_lo, c_hi, done, last, streak, stall, _ = st
        a = min_ge(lo)
        c_gt = count(lambda x: x > a)
        tie = (c_gt < k_top) & (done < 0.5)
        done = jnp.where(tie, 1.0, done)
        stall = jnp.zeros_like(stall)
        return (it, lo, hi, hi_fin, c_lo, jnp.where(tie, c_gt, c_hi), done, last, streak, stall,
                live_flag(done, stall))

    def search_step(st):
        it, lo, hi, hi_fin, c_lo, c_hi, done, last, streak, stall = st
        width = c_lo - c_hi
        target = k_top - 0.5 + jnp.where(streak >= 1.0, last * STALE_LEAN * (c_lo - c_hi), 0.0)
        target = jnp.clip(target, c_hi + 0.5, c_lo - 0.5)
        frac = (c_lo - target) / jnp.maximum(c_lo - c_hi, 1.0)
        mid = 0.5 * lo + 0.5 * hi_fin
        p = jnp.where(it % BISECT_EVERY == BISECT_EVERY - 1, mid, lo + frac * (hi_fin - lo))
        p = jnp.where((p > lo) & (p < hi), p, mid)
        stuck = jnp.logical_not((mid > lo) & (mid < hi))
        c = count(lambda x: x >= p)
        up = c >= k_top
        move = jnp.logical_not((done > 0.5) | stuck)
        rise = move & up
        fall = move & jnp.logical_not(up)
        lo = jnp.where(rise, p, lo)
        c_lo = jnp.where(rise, c, c_lo)
        hi = jnp.where(fall, p, hi)
        c_hi = jnp.where(fall, c, c_hi)
        hi_fin = jnp.where(fall, p, hi_fin)
        side = jnp.where(up, -1.0, 1.0)
        streak = jnp.where(side == last, streak + 1.0, 0.0)
        done = jnp.where((c_lo - c_hi <= 1.0) | (c_lo == k_top) | stuck, 1.0, done)
        stall = jnp.where(c_lo - c_hi == width, stall + 1.0, 0.0)
        return it + 1, lo, hi, hi_fin, c_lo, c_hi, done, side, streak, stall

    def search_body(st):
        st = search_step(search_step(search_step(search_step(st[:-1]))))
        flag = live_flag(st[6], st[9])
        return lax.cond(flag >= some_stalled, tie_check, lambda s: s, st + (flag,))

    zeros_row = jnp.zeros((1, TILE), jnp.float32)
    init = (jnp.int32(0), row_min, jnp.full((1, TILE), jnp.inf, jnp.float32), row_max,
            n_adm, zeros_row, take_all.astype(jnp.float32), zeros_row, zeros_row, zeros_row)
    init = search_step(search_step(search_step(search_step(init))))
    _, lo, _, _, _, c_hi, _, _, _, _, _ = lax.while_loop(search_cond, search_body, init + (jnp.float32(some_live),))
    thr = jnp.where(take_all, -jnp.inf, min_ge(lo))
    need = k_top - jnp.where(take_all, 0.0, c_hi)

    n_pairs = DSA_HEADS // 2
    lower = (qry < key).astype(jnp.bfloat16)
    ones_feat = (lax.broadcasted_iota(jnp.int32, (TILE, LANES), 1) < 3).astype(jnp.bfloat16)

    def init_state():
        for pr in range(n_pairs):
            _init_softmax_state(m_ref.at[pr], alpha_ref.at[:, pr], p_ref.at[:, pr], acc_ref.at[pr])
        seen_ref[...] = jnp.zeros(seen_ref.shape, jnp.float32)

    def logits(slot, kc):
        row0 = pl.multiple_of(kc * TILE, TILE)
        for pr in range(n_pairs):
            k_t = k_ref[0, pl.ds(row0, TILE), pr * LANES:(pr + 1) * LANES]
            s_ref[slot, pr] = _dot(jnp.concatenate([k_t, ones_feat], axis=1), qz_ref[pr])

    def masked_logits(slot, kc, kind):
        x = sc_ref[kc]
        tie = x == thr
        tie_f = jnp.where(tie, 1.0, 0.0)
        seen = seen_ref[:1]
        rank = seen + _dot(lower, tie_f.astype(jnp.bfloat16))
        selected = (x > thr) | (tie & (rank < need))
        if kind == _KIND_DIAG:
            selected = selected & causal
        mask_add = jnp.where(selected, 0.0, NEG_BIG)
        mask_add = jnp.concatenate([mask_add, mask_add], axis=1)
        seen_ref[...] = jnp.broadcast_to(seen + jnp.sum(tie_f, axis=0, keepdims=True), seen_ref.shape)
        for pr in range(n_pairs):
            s_t = s_ref[slot, pr] + mask_add
            if kind != _KIND_FAR:
                s_t = s_t + tab_ref[pr, kind]
            yield pr, s_t

    def softmax_shifted(slot, kc, kind):
        for pr, s_t in masked_logits(slot, kc, kind):
            p_ref[slot, pr] = jnp.exp2(s_t).astype(jnp.bfloat16)

    def accumulate_shifted(slot, kc):
        for pr in range(n_pairs):
            acc_ref[pr] += _dot(_with_ones_rows(vt_ref[0, kc, pr * LANES:(pr + 1) * LANES, :]), p_ref[slot, pr])

    def softmax_online(slot, kc, kind):
        for pr, s_t in masked_logits(slot, kc, kind):
            _softmax_stage(s_t, m_ref.at[pr], alpha_ref.at[slot, pr], p_ref.at[slot, pr])

    def accumulate_online(slot, kc):
        for pr in range(n_pairs):
            _accumulate_stage(vt_ref[0, kc, pr * LANES:(pr + 1) * LANES, :], alpha_ref.at[slot, pr],
                              p_ref.at[slot, pr], acc_ref.at[pr])

    init_state()
    _run_tile_pipeline(qi, logits, softmax_shifted, accumulate_shifted)
    denom = acc_ref[0, LANES:LANES + 1, :]
    for pr in range(1, n_pairs):
        denom = jnp.minimum(denom, acc_ref[pr, LANES:LANES + 1, :])
    denom_min = jnp.min(denom)

    @pl.when(jnp.logical_not(denom_min > MIN_SHIFTED_DENOM))
    def _():
        init_state()
        _run_tile_pipeline(qi, logits, softmax_online, accumulate_online)

    for pr in range(n_pairs):
        o_ref[0, :, pr * LANES:(pr + 1) * LANES] = _normalized_pair(acc_ref[pr], TILE).astype(o_ref.dtype)


def _dsa_call(bias_flat, q_t, k, v_t, iq_t, ik, iw_t, bkt):
    b, _, t = q_t.shape
    n_chunks = t // TILE
    n_pairs = DSA_HEADS // 2
    grid_spec = pltpu.PrefetchScalarGridSpec(
        num_scalar_prefetch=1,
        grid=(b, n_chunks),
        in_specs=[
            pl.BlockSpec((1, DSA_W, TILE), lambda bi, qi, s: (bi, 0, qi)),
            pl.BlockSpec((1, t, DSA_W), lambda bi, qi, s: (bi, 0, 0)),
            pl.BlockSpec((1, n_chunks, DSA_W, TILE), lambda bi, qi, s: (bi, 0, 0, 0)),
            pl.BlockSpec((1, IDX_W, TILE), lambda bi, qi, s: (bi, 0, qi)),
            pl.BlockSpec((1, t, IDX_DIM), lambda bi, qi, s: (bi, 0, 0)),
            pl.BlockSpec((1, IDX_HEADS, TILE), lambda bi, qi, s: (bi, 0, qi)),
            pl.BlockSpec((2, TILE, TILE), lambda bi, qi, s: (0, 0, 0)),
        ],
        out_specs=pl.BlockSpec((1, TILE, DSA_W), lambda bi, qi, s: (bi, qi, 0)),
        scratch_shapes=[
            pltpu.VMEM((n_chunks, TILE, TILE), jnp.float32),
            pltpu.VMEM((n_pairs, 2, TILE, 2 * TILE), jnp.float32),
            pltpu.VMEM((DSA_HEADS + n_pairs, SUBLANES, LANES), jnp.float32),
            pltpu.VMEM((n_pairs, 2 * LANES, 2 * TILE), jnp.bfloat16),
            pltpu.VMEM((2, n_pairs, TILE, 2 * TILE), jnp.float32),
            pltpu.VMEM((2, n_pairs, TILE, 2 * TILE), jnp.bfloat16),
            pltpu.VMEM((2, n_pairs, SUBLANES, 2 * TILE), jnp.float32),
            pltpu.VMEM((SUBLANES, TILE), jnp.float32),
            pltpu.VMEM((n_pairs, SUBLANES, 2 * TILE), jnp.float32),
            pltpu.VMEM((n_pairs, ACC_ROWS, 2 * TILE), jnp.float32),
        ],
    )
    return pl.pallas_call(
        _dsa_kernel,
        grid_spec=grid_spec,
        out_shape=jax.ShapeDtypeStruct((b, t, DSA_W), jnp.bfloat16),
        compiler_params=pltpu.CompilerParams(dimension_semantics=("arbitrary",) * 2, vmem_limit_bytes=VMEM_LIMIT),
        name="dsa",
    )(bias_flat, q_t, k, v_t, iq_t, ik, iw_t, bkt)


def _lane_half_mask(shape, half):
    lane = lax.broadcasted_iota(jnp.int32, shape, 1)
    return (lane >= half * HEAD_DIM) & (lane < (half + 1) * HEAD_DIM)


def _final_kernel(x_ref, g_ref, fg_ref, ya_ref, za_ref, yb_ref, zb_ref, qm_ref, zm_ref, mk_ref, mv_ref,
                  wg_ref, wb_ref, wo_ref, o_ref):
    x = x_ref[...]
    h = _rmsnorm_rows(x, g_ref[...]).astype(jnp.bfloat16)

    ym_parts = []
    for pr in range(MEM_HEADS // 2):
        qp = qm_ref[:, pr * LANES:(pr + 1) * LANES]
        mk = mk_ref[0, :, pr * LANES:(pr + 1) * LANES]
        mv = mv_ref[0, :, pr * LANES:(pr + 1) * LANES]
        outs = []
        for half in range(2):
            qh = jnp.where(_lane_half_mask((ROW_TILE, LANES), half), qp, jnp.zeros_like(qp))
            s = _dot_nt(qh, mk)
            p = jnp.exp2(s - jnp.max(s, axis=1, keepdims=True))
            outs.append(_dot(p.astype(jnp.bfloat16), mv) / jnp.sum(p, axis=1, keepdims=True))
        ym_parts.append(jnp.where(_lane_half_mask((ROW_TILE, LANES), 0), outs[0], outs[1]))
    ym = jnp.concatenate(ym_parts, axis=1)

    def gated(y, z_ref):
        z = z_ref[...].astype(jnp.float32)
        return (y.astype(jnp.float32) * (z * jax.nn.sigmoid(z))).astype(jnp.bfloat16)

    ua = gated(ya_ref[...], za_ref)
    ub = gated(yb_ref[...], zb_ref)
    um = gated(ym, zm_ref)
    merged = jax.nn.sigmoid(_dot(h, wg_ref[:, :D_MODEL])) * _dot(ua, wb_ref[:MOBA_W])
    merged += jax.nn.sigmoid(_dot(h, wg_ref[:, D_MODEL:2 * D_MODEL])) * _dot(ub, wb_ref[MOBA_W:MOBA_W + DSA_W])
    merged += jax.nn.sigmoid(_dot(h, wg_ref[:, 2 * D_MODEL:])) * _dot(um, wb_ref[MOBA_W + DSA_W:])
    y = x + _dot(merged.astype(jnp.bfloat16), wo_ref[...])
    o_ref[...] = _rmsnorm_rows(y, fg_ref[...])


def _final_call(x2, gain, final_gain, ya, za, yb, zb, qm, zm, mk, mv, w_gates, w_branch, w_out, rows_per_batch):
    rows = x2.shape[0]
    tiles_per_batch = rows_per_batch // ROW_TILE
    row_spec = lambda width: pl.BlockSpec((ROW_TILE, width), lambda i: (i, 0))
    const_spec = lambda shape: pl.BlockSpec(shape, lambda i: (0,) * len(shape))
    mem_spec = pl.BlockSpec((1,) + mk.shape[1:], lambda i: (i // tiles_per_batch, 0, 0))
    return pl.pallas_call(
        _final_kernel,
        grid=(rows // ROW_TILE,),
        in_specs=[row_spec(D_MODEL), const_spec((1, D_MODEL)), const_spec((1, D_MODEL)),
                  row_spec(MOBA_W), row_spec(MOBA_W), row_spec(DSA_W), row_spec(DSA_W),
                  row_spec(MEM_W), row_spec(MEM_W), mem_spec, mem_spec,
                  const_spec(w_gates.shape), const_spec(w_branch.shape), const_spec(w_out.shape)],
        out_specs=row_spec(D_MODEL),
        out_shape=jax.ShapeDtypeStruct((rows, D_MODEL), jnp.float32),
        compiler_params=pltpu.CompilerParams(dimension_semantics=("arbitrary",), vmem_limit_bytes=VMEM_LIMIT),
        name="final",
    )(x2, gain, final_gain, ya, za, yb, zb, qm, zm, mk, mv, w_gates, w_branch, w_out)


def _layer(x, mem, norm_gain, w_in, rel_bias, mem_norm_gain, w_mem_kv, w_branch, w_out, final_gain):
    b, t, _ = x.shape
    assert t % (2 * TILE) == 0 and t // MOBA_BLOCK <= FEAT_LO and t >= 4 * DSA_TOPK
    bf = jnp.bfloat16
    names = ("qa", "ka", "va", "za", "qb", "kb", "vb", "zb", "iq", "ik", "iw", "qm", "zm", "ga", "gb", "gm")
    cols, off = {}, 0
    for name, size in zip(names, IN_SIZES):
        cols[name] = w_in[:, off:off + size]
        off += size
    w_std = jnp.concatenate([cols[n] for n, _, _ in _STD_SEGS], axis=1).astype(bf)
    w_t = jnp.concatenate([cols[n[:-1]] for n, _, _ in _T_SEGS], axis=1).T.astype(bf)
    w_gates = jnp.concatenate([cols["ga"], cols["gb"], cols["gm"]], axis=1).astype(bf)

    x2 = x.reshape(b * t, D_MODEL)
    (ka, za, kb, zb, qm, zm, ik, qa_t, va_t, qb_t, vb_t, iq_t, iw_t, kmean) = _proj_call(
        x2, norm_gain.reshape(1, D_MODEL), w_std, w_t, b, t)
    seq = lambda a: a.reshape(b, t, a.shape[-1])
    n_blocks = t // MOBA_BLOCK
    kmean = jnp.pad(kmean.reshape(b, n_blocks, MOBA_W), ((0, 0), (0, LANES - n_blocks), (0, 0)))

    bias_flat = (rel_bias * LOG2E).reshape(-1)
    ya = _moba_call(bias_flat, qa_t, seq(ka), va_t, kmean, _bucket_tiles(True))
    yb = _dsa_call(bias_flat, qb_t, seq(kb), vb_t, iq_t, seq(ik), iw_t, _bucket_tiles(False))
    mk, mv = _memkv_call(mem, mem_norm_gain.reshape(1, D_MODEL), w_mem_kv.astype(bf))
    out = _final_call(x2, norm_gain.reshape(1, D_MODEL), final_gain.reshape(1, D_MODEL),
                      ya.reshape(b * t, MOBA_W), za, yb.reshape(b * t, DSA_W), zb, qm, zm, mk, mv,
                      w_gates, w_branch.astype(bf), w_out.astype(bf), t)
    return out.reshape(b, t, D_MODEL)


@jax.jit
def kernel(x, mem, norm_gain, w_in, rel_bias, mem_norm_gain, w_mem_kv, w_branch, w_out, final_norm_gain):
    assert norm_gain.shape[0] == 1, "one layer"
    return _layer(x, mem, norm_gain[0], w_in[0], rel_bias, mem_norm_gain[0], w_mem_kv[0], w_branch[0],
                  w_out[0], final_norm_gain)
```

```python
import math

import numpy as np
import jax
import jax.numpy as jnp
from jax import lax
from jax.experimental import pallas as pl
from jax.experimental.pallas import tpu as pltpu

D_MODEL = 1024
HEAD_DIM = 64
MOBA_HEADS = 6
DSA_HEADS = 6
MEM_HEADS = 4
MOBA_W = MOBA_HEADS * HEAD_DIM
DSA_W = DSA_HEADS * HEAD_DIM
MEM_W = MEM_HEADS * HEAD_DIM
IDX_HEADS = 8
IDX_DIM = 64
IDX_W = IDX_HEADS * IDX_DIM
MOBA_BLOCK = 256
MOBA_TOPK = 3
DSA_TOPK = 256
REL_BUCKETS = 32
REL_MAX_DIST = 128
N_BIAS_HEADS = MOBA_HEADS + DSA_HEADS
EPS = 1e-6
IN_SIZES = (MOBA_W,) * 4 + (DSA_W,) * 4 + (IDX_W, IDX_DIM, IDX_HEADS) + (MEM_W, MEM_W) + (D_MODEL,) * 3

LANES = 128
SUBLANES = 8
ONES_ROWS = 16
ACC_ROWS = LANES + ONES_ROWS
TILE = 256
ROW_TILE = 512
FEAT_LO = 32
LOG2E = math.log2(math.e)
NEG_BIG = -1e30
MASK_BUCKET = REL_BUCKETS
VMEM_LIMIT = 56 * 1024 * 1024
MAX_SEARCH_ITERS = 96
BISECT_EVERY = 5
STALE_LEAN = 0.15
STALL_STEPS = 6.0
BOUND_SLACK = 1.001
MIN_SHIFTED_DENOM = 2.0 ** -60

_NT = (((1,), (1,)), ((), ()))


def _dot_nt(a, b):
    return lax.dot_general(a, b, _NT, preferred_element_type=jnp.float32)


def _dot(a, b):
    return jnp.dot(a, b, preferred_element_type=jnp.float32)


def _rel_bucket_np(n):
    exact = REL_BUCKETS // 2
    nf = np.maximum(n, 1).astype(np.float32)
    large = exact + (np.log(nf / np.float32(exact)) / np.float32(math.log(REL_MAX_DIST / exact))
                     * np.float32(REL_BUCKETS - exact)).astype(np.int32)
    return np.where(n < exact, n, np.minimum(large, REL_BUCKETS - 1))


def _bucket_tiles(mask_future):
    j = np.arange(TILE)[:, None]
    i = np.arange(TILE)[None, :]
    d0 = i - j
    b0 = _rel_bucket_np(np.maximum(d0, 0))
    if mask_future:
        b0 = np.where(d0 >= 0, b0, MASK_BUCKET)
    b1 = _rel_bucket_np(TILE + d0)
    assert int(_rel_bucket_np(np.array([TILE + 1]))[0]) == REL_BUCKETS - 1
    return jnp.asarray(np.stack([b0, b1]).astype(np.int32))


def _build_bias_tile(bkt, bias_ref, head):
    acc = jnp.where(bkt == MASK_BUCKET, NEG_BIG, 0.0)
    for b in range(REL_BUCKETS):
        acc = jnp.where(bkt == b, bias_ref[b * N_BIAS_HEADS + head], acc)
    return acc


def _rmsnorm_rows(x, g):
    xf = x.astype(jnp.float32)
    return xf * lax.rsqrt(jnp.mean(xf * xf, axis=-1, keepdims=True) + EPS) * g


_Q_SCALE = HEAD_DIM ** -0.5 * LOG2E
_STD_SEGS = (("ka", MOBA_W, 1.0), ("za", MOBA_W, 1.0), ("kb", DSA_W, 1.0), ("zb", DSA_W, 1.0),
             ("qm", MEM_W, _Q_SCALE), ("zm", MEM_W, 1.0), ("ik", IDX_DIM, 1.0))
_T_SEGS = (("qaT", MOBA_W, _Q_SCALE), ("vaT", MOBA_W, 1.0), ("qbT", DSA_W, _Q_SCALE), ("vbT", DSA_W, 1.0),
           ("iqT", IDX_W, IDX_DIM ** -0.5), ("iwT", IDX_HEADS, IDX_HEADS ** -0.5))
_STD_W = sum(w for _, w, _ in _STD_SEGS)
_T_W = sum(w for _, w, _ in _T_SEGS)
_CHUNKS_PER_ROW_TILE = ROW_TILE // TILE


def _proj_kernel(x_ref, g_ref, w_ref, wt_ref, *out_refs):
    std_refs = out_refs[:len(_STD_SEGS)]
    t_refs = out_refs[len(_STD_SEGS):len(_STD_SEGS) + len(_T_SEGS)]
    kmean_ref = out_refs[-1]
    h = _rmsnorm_rows(x_ref[...], g_ref[...]).astype(jnp.bfloat16)
    off = 0
    for (name, width, scale), o_ref in zip(_STD_SEGS, std_refs):
        r = _dot(h, w_ref[:, off:off + width])
        if name == "ka":
            for blk in range(ROW_TILE // MOBA_BLOCK):
                kmean_ref[0, blk:blk + 1, :] = jnp.mean(
                    r[blk * MOBA_BLOCK:(blk + 1) * MOBA_BLOCK], axis=0, keepdims=True)
        if scale != 1.0:
            r = r * scale
        o_ref[...] = r.astype(o_ref.dtype)
        off += width
    off = 0
    for (name, width, scale), o_ref in zip(_T_SEGS, t_refs):
        r = _dot_nt(wt_ref[off:off + width, :], h)
        if scale != 1.0:
            r = r * scale
        r = r.astype(o_ref.dtype)
        if name in ("vaT", "vbT"):
            for c in range(_CHUNKS_PER_ROW_TILE):
                o_ref[0, c] = r[:, c * TILE:(c + 1) * TILE]
        else:
            o_ref[0] = r
        off += width


def _proj_call(x2, gain, w_std, w_t, b, t):
    rows = x2.shape[0]
    tiles_per_batch = t // ROW_TILE
    n_tiles = rows // ROW_TILE
    row_spec = lambda width: pl.BlockSpec((ROW_TILE, width), lambda i: (i, 0))
    const_spec = lambda shape: pl.BlockSpec(shape, lambda i: (0,) * len(shape))
    t_spec = lambda width: pl.BlockSpec((1, width, ROW_TILE),
                                        lambda i: (i // tiles_per_batch, 0, i % tiles_per_batch))
    chunk_spec = lambda width: pl.BlockSpec((1, _CHUNKS_PER_ROW_TILE, width, TILE),
                                            lambda i: (i // tiles_per_batch, i % tiles_per_batch, 0, 0))
    out_shape = [jax.ShapeDtypeStruct((rows, w), jnp.bfloat16) for _, w, _ in _STD_SEGS]
    out_specs = [row_spec(w) for _, w, _ in _STD_SEGS]
    for name, w, _ in _T_SEGS:
        if name in ("vaT", "vbT"):
            out_shape.append(jax.ShapeDtypeStruct((b, t // TILE, w, TILE), jnp.bfloat16))
            out_specs.append(chunk_spec(w))
        else:
            out_shape.append(jax.ShapeDtypeStruct((b, w, t), jnp.float32 if name == "iwT" else jnp.bfloat16))
            out_specs.append(t_spec(w))
    out_shape.append(jax.ShapeDtypeStruct((n_tiles, ROW_TILE // MOBA_BLOCK, MOBA_W), jnp.float32))
    out_specs.append(pl.BlockSpec((1, ROW_TILE // MOBA_BLOCK, MOBA_W), lambda i: (i, 0, 0)))
    return pl.pallas_call(
        _proj_kernel,
        grid=(n_tiles,),
        in_specs=[row_spec(D_MODEL), const_spec((1, D_MODEL)), const_spec((D_MODEL, _STD_W)),
                  const_spec((_T_W, D_MODEL))],
        out_specs=out_specs,
        out_shape=out_shape,
        compiler_params=pltpu.CompilerParams(dimension_semantics=("arbitrary",), vmem_limit_bytes=VMEM_LIMIT),
        name="proj",
    )(x2, gain, w_std, w_t)


def _memkv_kernel(mem_ref, g_ref, w_ref, mk_ref, mv_ref):
    h = _rmsnorm_rows(mem_ref[0], g_ref[...]).astype(jnp.bfloat16)
    r = _dot(h, w_ref[...])
    mk_ref[0] = r[:, :MEM_W].astype(jnp.bfloat16)
    mv_ref[0] = r[:, MEM_W:].astype(jnp.bfloat16)


def _memkv_call(mem, gain, w_kv):
    b, n_mem, _ = mem.shape
    return pl.pallas_call(
        _memkv_kernel,
        grid=(b,),
        in_specs=[pl.BlockSpec((1, n_mem, D_MODEL), lambda i: (i, 0, 0)),
                  pl.BlockSpec((1, D_MODEL), lambda i: (0, 0)),
                  pl.BlockSpec((D_MODEL, 2 * MEM_W), lambda i: (0, 0))],
        out_specs=[pl.BlockSpec((1, n_mem, MEM_W), lambda i: (i, 0, 0))] * 2,
        out_shape=[jax.ShapeDtypeStruct((b, n_mem, MEM_W), jnp.bfloat16)] * 2,
        compiler_params=pltpu.CompilerParams(dimension_semantics=("arbitrary",), vmem_limit_bytes=VMEM_LIMIT),
        name="memkv",
    )(mem, gain, w_kv)


def _row_half_mask(shape, half):
    r = lax.broadcasted_iota(jnp.int32, shape, 0)
    return (r >= half * HEAD_DIM) & (r < (half + 1) * HEAD_DIM)


def _with_ones_rows(v_t):
    return jnp.concatenate([v_t, jnp.ones((ONES_ROWS, v_t.shape[1]), v_t.dtype)], axis=0)


def _softmax_stage(s_t, m_ref, alpha_ref, p_ref):
    m_old = m_ref[:1]
    m_new = jnp.maximum(m_old, jnp.max(s_t, axis=0, keepdims=True))
    alpha_ref[...] = jnp.broadcast_to(jnp.exp2(m_old - m_new), alpha_ref.shape)
    m_ref[...] = jnp.broadcast_to(m_new, m_ref.shape)
    p_ref[...] = jnp.exp2(s_t - m_new).astype(jnp.bfloat16)


def _accumulate_stage(v_t, alpha_ref, p_ref, acc_ref):
    acc_ref[...] = alpha_ref[:1] * acc_ref[...] + _dot(_with_ones_rows(v_t), p_ref[...])


def _init_softmax_state(m_ref, alpha_ref, p_ref, acc_ref):
    m_ref[...] = jnp.full(m_ref.shape, NEG_BIG, jnp.float32)
    acc_ref[...] = jnp.zeros(acc_ref.shape, jnp.float32)
    alpha_ref[...] = jnp.ones(alpha_ref.shape, jnp.float32)
    p_ref[...] = jnp.zeros(p_ref.shape, p_ref.dtype)


def _normalized_pair(acc, half_cols):
    out0 = acc[:LANES, :half_cols] / acc[LANES:LANES + 1, :half_cols]
    out1 = acc[:LANES, half_cols:] / acc[LANES:LANES + 1, half_cols:]
    return jnp.where(_row_half_mask((LANES, half_cols), 0), out0, out1).T


_KIND_DIAG, _KIND_PREV, _KIND_FAR = 0, 1, 2


def _run_tile_pipeline(qi, logits, softmax, accumulate):
    n_far = jnp.maximum(qi - 1, 0)
    n_loop = n_far // 2

    def far_pair(t):
        accumulate(0, jnp.maximum(t - 2, 0))
        accumulate(1, jnp.maximum(t - 1, 0))
        softmax(0, t, _KIND_FAR)
        logits(0, t + 2)
        softmax(1, t + 1, _KIND_FAR)
        logits(1, t + 3)

    def quad_body(j, carry):
        far_pair(4 * j)
        far_pair(4 * j + 2)
        return carry

    def pair_body(j, carry):
        far_pair(2 * j)
        return carry

    logits(0, 0)
    logits(1, jnp.minimum(1, qi))
    lax.fori_loop(0, n_loop // 2, quad_body, 0)
    lax.fori_loop(2 * (n_loop // 2), n_loop, pair_body, 0)
    t0 = 2 * n_loop
    odd = n_far % 2 == 1

    @pl.when(odd)
    def _():
        accumulate(0, jnp.maximum(t0 - 2, 0))
        accumulate(1, jnp.maximum(t0 - 1, 0))
        softmax(0, t0, _KIND_FAR)
        logits(0, qi)
        softmax(1, qi - 1, _KIND_PREV)
        accumulate(0, t0)
        softmax(0, qi, _KIND_DIAG)
        accumulate(1, qi - 1)
        accumulate(0, qi)

    @pl.when(jnp.logical_not(odd) & (qi >= 1))
    def _():
        accumulate(0, jnp.maximum(t0 - 2, 0))
        accumulate(1, jnp.maximum(t0 - 1, 0))
        softmax(0, qi - 1, _KIND_PREV)
        softmax(1, qi, _KIND_DIAG)
        accumulate(0, qi - 1)
        accumulate(1, qi)

    @pl.when(qi == 0)
    def _():
        softmax(0, 0, _KIND_DIAG)
        accumulate(0, 0)


def _moba_kernel(bias_ref, qt_ref, k_ref, vt_ref, kmean_ref, bkt_ref, o_ref,
                 qa_ref, tab_ref, s_ref, p_ref, alpha_ref, m_ref, acc_ref):
    pair = pl.program_id(1)
    qi = pl.program_id(2)
    q_t = qt_ref[0]
    kmean = kmean_ref[0].astype(jnp.bfloat16)
    blk = lax.broadcasted_iota(jnp.int32, (FEAT_LO, TILE), 0)
    blk_f = blk.astype(jnp.float32)
    past = blk < qi
    lane = lax.broadcasted_iota(jnp.int32, (TILE, LANES), 1)
    _init_softmax_state(m_ref, alpha_ref, p_ref, acc_ref)

    @pl.when(qi == 0)
    def _():
        for half in range(2):
            for kind in (_KIND_DIAG, _KIND_PREV):
                tab_ref[kind, :, half * TILE:(half + 1) * TILE] = _build_bias_tile(
                    bkt_ref[kind], bias_ref, pair * 2 + half)

    for half in range(2):
        qh = jnp.where(_row_half_mask((LANES, TILE), half), q_t, jnp.zeros_like(q_t))
        gate = _dot(kmean, qh)[:FEAT_LO]
        g = jnp.where(past, gate, -jnp.inf)
        sel = jnp.zeros((FEAT_LO, TILE), jnp.bool_)
        for _ in range(MOBA_TOPK):
            mx = jnp.max(g, axis=0, keepdims=True)
            first = jnp.min(jnp.where(g == mx, blk_f, float(LANES)), axis=0, keepdims=True)
            pick = blk_f == first
            sel = sel | (pick & past)
            g = jnp.where(pick, -jnp.inf, g)
        b31 = jnp.full((FEAT_LO, TILE), bias_ref[(REL_BUCKETS - 1) * N_BIAS_HEADS + pair * 2 + half], jnp.float32)
        b31_hi = b31.astype(jnp.bfloat16).astype(jnp.float32)
        hi_part = jnp.where(sel, jnp.where(blk < qi - 1, b31_hi, 0.0), NEG_BIG)
        hi_part = jnp.where(past, hi_part, 0.0)
        lo_part = jnp.where(blk < qi - 1, b31 - b31_hi, 0.0)
        feat = jnp.concatenate([hi_part, lo_part, jnp.zeros((LANES - 2 * FEAT_LO, TILE), jnp.float32)], axis=0)
        qa_ref[:, half * TILE:(half + 1) * TILE] = jnp.concatenate([qh, feat.astype(jnp.bfloat16)], axis=0)

    def logits(slot, kj):
        k_t = k_ref[0, pl.ds(pl.multiple_of(kj * TILE, TILE), TILE), :]
        onehot = ((lane == kj) | (lane == kj + FEAT_LO)).astype(jnp.bfloat16)
        s_ref[slot] = _dot(jnp.concatenate([k_t, onehot], axis=1), qa_ref[...])

    def softmax(slot, kj, kind):
        s_t = s_ref[slot]
        if kind != _KIND_FAR:
            s_t = s_t + tab_ref[kind]
        _softmax_stage(s_t, m_ref, alpha_ref.at[slot], p_ref.at[slot])

    def accumulate(slot, kj):
        _accumulate_stage(vt_ref[0, kj], alpha_ref.at[slot], p_ref.at[slot], acc_ref)

    _run_tile_pipeline(qi, logits, softmax, accumulate)
    o_ref[0] = _normalized_pair(acc_ref[...], TILE).astype(o_ref.dtype)


def _moba_call(bias_flat, q_t, k, v_t, kmean, bkt):
    b, _, t = q_t.shape
    n_pairs = MOBA_HEADS // 2
    n_chunks = t // TILE
    grid_spec = pltpu.PrefetchScalarGridSpec(
        num_scalar_prefetch=1,
        grid=(b, n_pairs, n_chunks),
        in_specs=[
            pl.BlockSpec((1, LANES, TILE), lambda bi, p, qi, s: (bi, p, qi)),
            pl.BlockSpec((1, t, LANES), lambda bi, p, qi, s: (bi, 0, p)),
            pl.BlockSpec((1, n_chunks, LANES, TILE), lambda bi, p, qi, s: (bi, 0, p, 0)),
            pl.BlockSpec((1, LANES, LANES), lambda bi, p, qi, s: (bi, 0, p)),
            pl.BlockSpec((2, TILE, TILE), lambda bi, p, qi, s: (0, 0, 0)),
        ],
        out_specs=pl.BlockSpec((1, TILE, LANES), lambda bi, p, qi, s: (bi, qi, p)),
        scratch_shapes=[
            pltpu.VMEM((2 * LANES, 2 * TILE), jnp.bfloat16),
            pltpu.VMEM((2, TILE, 2 * TILE), jnp.float32),
            pltpu.VMEM((2, TILE, 2 * TILE), jnp.float32),
            pltpu.VMEM((2, TILE, 2 * TILE), jnp.bfloat16),
            pltpu.VMEM((2, SUBLANES, 2 * TILE), jnp.float32),
            pltpu.VMEM((SUBLANES, 2 * TILE), jnp.float32),
            pltpu.VMEM((ACC_ROWS, 2 * TILE), jnp.float32),
        ],
    )
    return pl.pallas_call(
        _moba_kernel,
        grid_spec=grid_spec,
        out_shape=jax.ShapeDtypeStruct((b, t, MOBA_W), jnp.bfloat16),
        compiler_params=pltpu.CompilerParams(dimension_semantics=("arbitrary",) * 3, vmem_limit_bytes=VMEM_LIMIT),
        name="moba",
    )(bias_flat, q_t, k, v_t, kmean, bkt)


def _dsa_kernel(bias_ref, qt_ref, k_ref, vt_ref, iqt_ref, ik_ref, iwt_ref, bkt_ref, o_ref,
                sc_ref, tab_ref, bound_ref, qz_ref, s_ref, p_ref, alpha_ref, seen_ref, m_ref, acc_ref):
    qi = pl.program_id(1)
    n_chunks = qi + 1
    key = lax.broadcasted_iota(jnp.int32, (TILE, TILE), 0)
    qry = lax.broadcasted_iota(jnp.int32, (TILE, TILE), 1)
    causal = key <= qry

    far_bias = [bias_ref[(REL_BUCKETS - 1) * N_BIAS_HEADS + MOBA_HEADS + h] for h in range(DSA_HEADS)]

    @pl.when(qi == 0)
    def _():
        for h in range(DSA_HEADS):
            rel_max = jnp.zeros((1, 1), jnp.float32)
            for kind in (_KIND_DIAG, _KIND_PREV):
                tile = _build_bias_tile(bkt_ref[kind], bias_ref, MOBA_HEADS + h) - far_bias[h]
                tab_ref[h // 2, kind, :, (h % 2) * TILE:(h % 2 + 1) * TILE] = tile
                rel_max = jnp.maximum(rel_max, jnp.max(jnp.max(tile, axis=0, keepdims=True), axis=1, keepdims=True))
            bound_ref[h] = jnp.broadcast_to(rel_max + far_bias[h], (SUBLANES, LANES))

        for pr in range(DSA_HEADS // 2):
            def norm_body(c, mx, pr=pr):
                blk = k_ref[0, pl.ds(pl.multiple_of(c * TILE, TILE), TILE), pr * LANES:(pr + 1) * LANES]
                blk = blk.astype(jnp.float32)
                return jnp.maximum(mx, jnp.sum(blk * blk, axis=1, keepdims=True))
            n2 = lax.fori_loop(0, k_ref.shape[1] // TILE, norm_body, jnp.zeros((TILE, 1), jnp.float32))
            bound_ref[DSA_HEADS + pr] = jnp.broadcast_to(jnp.sqrt(jnp.max(n2, axis=0, keepdims=True)),
                                                         (SUBLANES, LANES))

    feat_row = lax.broadcasted_iota(jnp.int32, (LANES, TILE), 0)
    for h in range(DSA_HEADS):
        blk = qt_ref[0, (h // 2) * LANES:(h // 2 + 1) * LANES, :]
        qh = jnp.where(_row_half_mask((LANES, TILE), h % 2), blk, jnp.zeros_like(blk))
        qf = qh.astype(jnp.float32)
        q_norm = jnp.sqrt(jnp.sum(qf * qf, axis=0, keepdims=True))
        logit_bound = q_norm * bound_ref[DSA_HEADS + h // 2][:1, :1] * BOUND_SLACK + bound_ref[h][:1, :1]
        b31 = jnp.full((LANES, TILE), far_bias[h], jnp.float32)
        b31_hi = b31.astype(jnp.bfloat16).astype(jnp.float32)
        feat = jnp.where(feat_row == 0, b31_hi, jnp.where(feat_row == 1, b31 - b31_hi, 0.0))
        feat = jnp.where(feat_row == 2, -logit_bound, feat)
        qz_ref[h // 2, :, (h % 2) * TILE:(h % 2 + 1) * TILE] = jnp.concatenate(
            [qh, feat.astype(jnp.bfloat16)], axis=0)

    iw_t = iwt_ref[0]

    def score_chunk(kc):
        row0 = pl.multiple_of(kc * TILE, TILE)
        ik_t = ik_ref[0, pl.ds(row0, TILE), :]
        tot = jnp.zeros((TILE, TILE), jnp.float32)
        for h in range(IDX_HEADS):
            x = _dot(ik_t, iqt_ref[0, h * IDX_DIM:(h + 1) * IDX_DIM, :])
            tot = tot + iw_t[h:h + 1, :] * jnp.maximum(x, 0.0)
        return tot

    def score_body(kc, carry):
        mx, mn = carry
        tot = score_chunk(kc)
        sc_ref[kc] = tot
        return (jnp.maximum(mx, jnp.max(tot, axis=0, keepdims=True)),
                jnp.minimum(mn, jnp.min(tot, axis=0, keepdims=True)))

    def score_pair_body(j, carry):
        return score_body(2 * j + 1, score_body(2 * j, carry))

    extremes = lax.fori_loop(0, qi // 2, score_pair_body, (jnp.full((1, TILE), -jnp.inf, jnp.float32),
                                                           jnp.full((1, TILE), jnp.inf, jnp.float32)))
    row_max, row_min = lax.cond(qi % 2 == 1, lambda c: score_body(qi - 1, c), lambda c: c, extremes)
    tot = score_chunk(qi)
    sc_ref[qi] = jnp.where(causal, tot, -jnp.inf)
    row_max = jnp.maximum(row_max, jnp.max(jnp.where(causal, tot, -jnp.inf), axis=0, keepdims=True))
    row_min = jnp.minimum(row_min, jnp.min(jnp.where(causal, tot, jnp.inf), axis=0, keepdims=True))

    @pl.when(qi % 2 == 0)
    def _():
        sc_ref[qi + 1] = jnp.full((TILE, TILE), -jnp.inf, jnp.float32)

    def reduce_pass(fn, init, combine):
        def body(j, part):
            return combine(part, combine(fn(sc_ref[2 * j]), fn(sc_ref[2 * j + 1])))
        return lax.fori_loop(0, (n_chunks + 1) // 2, body, jnp.full((1, TILE), init, jnp.float32))

    def count(pred):
        return reduce_pass(lambda x: jnp.sum(jnp.where(pred(x), 1.0, 0.0), axis=0, keepdims=True), 0.0, jnp.add)

    def min_ge(p):
        return reduce_pass(lambda x: jnp.min(jnp.where(x >= p, x, jnp.inf), axis=0, keepdims=True),
                           jnp.inf, jnp.minimum)

    n_adm =(qi * TILE + 1 + lax.broadcasted_iota(jnp.int32, (1, TILE), 1)).astype(jnp.float32)
    k_top = float(DSA_TOPK)
    take_all = n_adm <= k_top

    def search_cond(st):
        return jnp.logical_and(st[0] < MAX_SEARCH_ITERS, st[-1] >= some_live)

    all_done, some_live, some_stalled = 0.0, 1.0, 2.0

    def live_flag(done, stall):
        return jnp.max(jnp.where(done < 0.5, jnp.where(stall >= STALL_STEPS, some_stalled, some_live), all_done))

    def tie_check(st):
        it, lo, hi, hi_fin, c_lo, c_hi, done, last, streak, stall, _ = st
        a = min_ge(lo)
        c_gt = count(lambda x: x > a)
        tie = (c_gt < k_top) & (done < 0.5)
        done = jnp.where(tie, 1.0, done)
        stall = jnp.zeros_like(stall)
        return (it, lo, hi, hi_fin, c_lo, jnp.where(tie, c_gt, c_hi), done, last, streak, stall,
                live_flag(done, stall))

    def search_step(st):
        it, lo, hi, hi_fin, c_lo, c_hi, done, last, streak, stall = st
        width = c_lo - c_hi
        target = k_top - 0.5 + jnp.where(streak >= 1.0, last * STALE_LEAN * (c_lo - c_hi), 0.0)
        target = jnp.clip(target, c_hi + 0.5, c_lo - 0.5)
        frac = (c_lo - target) / jnp.maximum(c_lo - c_hi, 1.0)
        mid = 0.5 * lo + 0.5 * hi_fin
        p = jnp.where(it % BISECT_EVERY == BISECT_EVERY - 1, mid, lo + frac * (hi_fin - lo))
        p = jnp.where((p > lo) & (p < hi), p, mid)
        stuck = jnp.logical_not((mid > lo) & (mid < hi))
        c = count(lambda x: x >= p)
        up = c >= k_top
        move = jnp.logical_not((done > 0.5) | stuck)
        rise = move & up
        fall = move & jnp.logical_not(up)
        lo = jnp.where(rise, p, lo)
        c_lo = jnp.where(rise, c, c_lo)
        hi = jnp.where(fall, p, hi)
        c_hi = jnp.where(fall, c, c_hi)
        hi_fin = jnp.where(fall, p, hi_fin)
        side = jnp.where(up, -1.0, 1.0)
        streak = jnp.where(side == last, streak + 1.0, 0.0)
        done = jnp.where((c_lo - c_hi <= 1.0) | (c_lo == k_top) | stuck, 1.0, done)
        stall = jnp.where(c_lo - c_hi == width, stall + 1.0, 0.0)
        return it + 1, lo, hi, hi_fin, c_lo, c_hi, done, side, streak, stall

    def search_body(st):
        st = search_step(search_step(search_step(search_step(st[:-1]))))
        flag = live_flag(st[6], st[9])
        return lax.cond(flag >= some_stalled, tie_check, lambda s: s, st + (flag,))

    zeros_row = jnp.zeros((1, TILE), jnp.float32)
    done0 = take_all.astype(jnp.float32)
    init = (jnp.int32(0), row_min, jnp.full((1, TILE), jnp.inf, jnp.float32), row_max,
            n_adm, zeros_row, done0, zeros_row, zeros_row, zeros_row, live_flag(done0, zeros_row))
    _, lo, _, _, _, c_hi, _, _, _, _, _ = lax.while_loop(search_cond, search_body, init)
    thr = jnp.where(take_all, -jnp.inf, min_ge(lo))
    need = k_top - jnp.where(take_all, 0.0, c_hi)

    n_pairs = DSA_HEADS // 2
    lower = (qry < key).astype(jnp.bfloat16)
    ones_feat = (lax.broadcasted_iota(jnp.int32, (TILE, LANES), 1) < 3).astype(jnp.bfloat16)

    def init_state():
        for pr in range(n_pairs):
            _init_softmax_state(m_ref.at[pr], alpha_ref.at[:, pr], p_ref.at[:, pr], acc_ref.at[pr])
        seen_ref[...] = jnp.zeros(seen_ref.shape, jnp.float32)

    def logits(slot, kc):
        row0 = pl.multiple_of(kc * TILE, TILE)
        for pr in range(n_pairs):
            k_t = k_ref[0, pl.ds(row0, TILE), pr * LANES:(pr + 1) * LANES]
            s_ref[slot, pr] = _dot(jnp.concatenate([k_t, ones_feat], axis=1), qz_ref[pr])

    def masked_logits(slot, kc, kind):
        x = sc_ref[kc]
        tie = x == thr
        tie_f = jnp.where(tie, 1.0, 0.0)
        seen = seen_ref[:1]
        rank = seen + _dot(lower, tie_f.astype(jnp.bfloat16))
        selected = (x > thr) | (tie & (rank < need))
        if kind == _KIND_DIAG:
            selected = selected & causal
        mask_add = jnp.where(selected, 0.0, NEG_BIG)
        mask_add = jnp.concatenate([mask_add, mask_add], axis=1)
        seen_ref[...] = jnp.broadcast_to(seen + jnp.sum(tie_f, axis=0, keepdims=True), seen_ref.shape)
        for pr in range(n_pairs):
            s_t = s_ref[slot, pr] + mask_add
            if kind != _KIND_FAR:
                s_t = s_t + tab_ref[pr, kind]
            yield pr, s_t

    def softmax_shifted(slot, kc, kind):
        for pr, s_t in masked_logits(slot, kc, kind):
            p_ref[slot, pr] = jnp.exp2(s_t).astype(jnp.bfloat16)

    def accumulate_shifted(slot, kc):
        for pr in range(n_pairs):
            acc_ref[pr] += _dot(_with_ones_rows(vt_ref[0, kc, pr * LANES:(pr + 1) * LANES, :]), p_ref[slot, pr])

    def softmax_online(slot, kc, kind):
        for pr, s_t in masked_logits(slot, kc, kind):
            _softmax_stage(s_t, m_ref.at[pr], alpha_ref.at[slot, pr], p_ref.at[slot, pr])

    def accumulate_online(slot, kc):
        for pr in range(n_pairs):
            _accumulate_stage(vt_ref[0, kc, pr * LANES:(pr + 1) * LANES, :], alpha_ref.at[slot, pr],
                              p_ref.at[slot, pr], acc_ref.at[pr])

    init_state()
    _run_tile_pipeline(qi, logits, softmax_shifted, accumulate_shifted)
    denom = acc_ref[0, LANES:LANES + 1, :]
    for pr in range(1, n_pairs):
        denom = jnp.minimum(denom, acc_ref[pr, LANES:LANES + 1, :])
    denom_min = jnp.min(denom)

    @pl.when(jnp.logical_not(denom_min > MIN_SHIFTED_DENOM))
    def _():
        init_state()
        _run_tile_pipeline(qi, logits, softmax_online, accumulate_online)

    for pr in range(n_pairs):
        o_ref[0, :, pr * LANES:(pr + 1) * LANES] = _normalized_pair(acc_ref[pr], TILE).astype(o_ref.dtype)


def _dsa_call(bias_flat, q_t, k, v_t, iq_t, ik, iw_t, bkt):
    b, _, t = q_t.shape
    n_chunks = t // TILE
    n_pairs = DSA_HEADS // 2
    grid_spec = pltpu.PrefetchScalarGridSpec(
        num_scalar_prefetch=1,
        grid=(b, n_chunks),
        in_specs=[
            pl.BlockSpec((1, DSA_W, TILE), lambda bi, qi, s: (bi, 0, qi)),
            pl.BlockSpec((1, t, DSA_W), lambda bi, qi, s: (bi, 0, 0)),
            pl.BlockSpec((1, n_chunks, DSA_W, TILE), lambda bi, qi, s: (bi, 0, 0, 0)),
            pl.BlockSpec((1, IDX_W, TILE), lambda bi, qi, s: (bi, 0, qi)),
            pl.BlockSpec((1, t, IDX_DIM), lambda bi, qi, s: (bi, 0, 0)),
            pl.BlockSpec((1, IDX_HEADS, TILE), lambda bi, qi, s: (bi, 0, qi)),
            pl.BlockSpec((2, TILE, TILE), lambda bi, qi, s: (0, 0, 0)),
        ],
        out_specs=pl.BlockSpec((1, TILE, DSA_W), lambda bi, qi, s: (bi, qi, 0)),
        scratch_shapes=[
            pltpu.VMEM((n_chunks, TILE, TILE), jnp.float32),
            pltpu.VMEM((n_pairs, 2, TILE, 2 * TILE), jnp.float32),
            pltpu.VMEM((DSA_HEADS + n_pairs, SUBLANES, LANES), jnp.float32),
            pltpu.VMEM((n_pairs, 2 * LANES, 2 * TILE), jnp.bfloat16),
            pltpu.VMEM((2, n_pairs, TILE, 2 * TILE), jnp.float32),
            pltpu.VMEM((2, n_pairs, TILE, 2 * TILE), jnp.bfloat16),
            pltpu.VMEM((2, n_pairs, SUBLANES, 2 * TILE), jnp.float32),
            pltpu.VMEM((SUBLANES, TILE), jnp.float32),
            pltpu.VMEM((n_pairs, SUBLANES, 2 * TILE), jnp.float32),
            pltpu.VMEM((n_pairs, ACC_ROWS, 2 * TILE), jnp.float32),
        ],
    )
    return pl.pallas_call(
        _dsa_kernel,
        grid_spec=grid_spec,
        out_shape=jax.ShapeDtypeStruct((b, t, DSA_W), jnp.bfloat16),
        compiler_params=pltpu.CompilerParams(dimension_semantics=("arbitrary",) * 2, vmem_limit_bytes=VMEM_LIMIT),
        name="dsa",
    )(bias_flat, q_t, k, v_t, iq_t, ik, iw_t, bkt)


def _lane_half_mask(shape, half):
    lane = lax.broadcasted_iota(jnp.int32, shape, 1)
    return (lane >= half * HEAD_DIM) & (lane < (half + 1) * HEAD_DIM)


def _final_kernel(x_ref, g_ref, fg_ref, ya_ref, za_ref, yb_ref, zb_ref, qm_ref, zm_ref, mk_ref, mv_ref,
                  wg_ref, wb_ref, wo_ref, o_ref):
    x = x_ref[...]
    h = _rmsnorm_rows(x, g_ref[...]).astype(jnp.bfloat16)

    ym_parts = []
    for pr in range(MEM_HEADS // 2):
        qp = qm_ref[:, pr * LANES:(pr + 1) * LANES]
        mk = mk_ref[0, :, pr * LANES:(pr + 1) * LANES]
        mv = mv_ref[0, :, pr * LANES:(pr + 1) * LANES]
        outs = []
        for half in range(2):
            qh = jnp.where(_lane_half_mask((ROW_TILE, LANES), half), qp, jnp.zeros_like(qp))
            s = _dot_nt(qh, mk)
            p = jnp.exp2(s - jnp.max(s, axis=1, keepdims=True))
            outs.append(_dot(p.astype(jnp.bfloat16), mv) / jnp.sum(p, axis=1, keepdims=True))
        ym_parts.append(jnp.where(_lane_half_mask((ROW_TILE, LANES), 0), outs[0], outs[1]))
    ym = jnp.concatenate(ym_parts, axis=1)

    def gated(y, z_ref):
        z = z_ref[...].astype(jnp.float32)
        return (y.astype(jnp.float32) * (z * jax.nn.sigmoid(z))).astype(jnp.bfloat16)

    ua = gated(ya_ref[...], za_ref)
    ub = gated(yb_ref[...], zb_ref)
    um = gated(ym, zm_ref)
    merged = jax.nn.sigmoid(_dot(h, wg_ref[:, :D_MODEL])) * _dot(ua, wb_ref[:MOBA_W])
    merged += jax.nn.sigmoid(_dot(h, wg_ref[:, D_MODEL:2 * D_MODEL])) * _dot(ub, wb_ref[MOBA_W:MOBA_W + DSA_W])
    merged += jax.nn.sigmoid(_dot(h, wg_ref[:, 2 * D_MODEL:])) * _dot(um, wb_ref[MOBA_W + DSA_W:])
    y = x + _dot(merged.astype(jnp.bfloat16), wo_ref[...])
    o_ref[...] = _rmsnorm_rows(y, fg_ref[...])


def _final_call(x2, gain, final_gain, ya, za, yb, zb, qm, zm, mk, mv, w_gates, w_branch, w_out, rows_per_batch):
    rows = x2.shape[0]
    tiles_per_batch = rows_per_batch // ROW_TILE
    row_spec = lambda width: pl.BlockSpec((ROW_TILE, width), lambda i: (i, 0))
    const_spec = lambda shape: pl.BlockSpec(shape, lambda i: (0,) * len(shape))
    mem_spec = pl.BlockSpec((1,) + mk.shape[1:], lambda i: (i // tiles_per_batch, 0, 0))
    return pl.pallas_call(
        _final_kernel,
        grid=(rows // ROW_TILE,),
        in_specs=[row_spec(D_MODEL), const_spec((1, D_MODEL)), const_spec((1, D_MODEL)),
                  row_spec(MOBA_W), row_spec(MOBA_W), row_spec(DSA_W), row_spec(DSA_W),
                  row_spec(MEM_W), row_spec(MEM_W), mem_spec, mem_spec,
                  const_spec(w_gates.shape), const_spec(w_branch.shape), const_spec(w_out.shape)],
        out_specs=row_spec(D_MODEL),
        out_shape=jax.ShapeDtypeStruct((rows, D_MODEL), jnp.float32),
        compiler_params=pltpu.CompilerParams(dimension_semantics=("arbitrary",), vmem_limit_bytes=VMEM_LIMIT),
        name="final",
    )(x2, gain, final_gain, ya, za, yb, zb, qm, zm, mk, mv, w_gates, w_branch, w_out)


def _layer(x, mem, norm_gain, w_in, rel_bias, mem_norm_gain, w_mem_kv, w_branch, w_out, final_gain):
    b, t, _ = x.shape
    assert t % (2 * TILE) == 0 and t // MOBA_BLOCK <= FEAT_LO and t >= 4 * DSA_TOPK
    bf = jnp.bfloat16
    names = ("qa", "ka", "va", "za", "qb", "kb", "vb", "zb", "iq", "ik", "iw", "qm", "zm", "ga", "gb", "gm")
    cols, off = {}, 0
    for name, size in zip(names, IN_SIZES):
        cols[name] = w_in[:, off:off + size]
        off += size
    w_std = jnp.concatenate([cols[n] for n, _, _ in _STD_SEGS], axis=1).astype(bf)
    w_t = jnp.concatenate([cols[n[:-1]] for n, _, _ in _T_SEGS], axis=1).T.astype(bf)
    w_gates = jnp.concatenate([cols["ga"], cols["gb"], cols["gm"]], axis=1).astype(bf)

    x2 = x.reshape(b * t, D_MODEL)
    (ka, za, kb, zb, qm, zm, ik, qa_t, va_t, qb_t, vb_t, iq_t, iw_t, kmean) = _proj_call(
        x2, norm_gain.reshape(1, D_MODEL), w_std, w_t, b, t)
    seq = lambda a: a.reshape(b, t, a.shape[-1])
    n_blocks = t // MOBA_BLOCK
    kmean = jnp.pad(kmean.reshape(b, n_blocks, MOBA_W), ((0, 0), (0, LANES - n_blocks), (0, 0)))

    bias_flat = (rel_bias * LOG2E).reshape(-1)
    ya = _moba_call(bias_flat, qa_t, seq(ka), va_t, kmean, _bucket_tiles(True))
    yb = _dsa_call(bias_flat, qb_t, seq(kb), vb_t, iq_t, seq(ik), iw_t, _bucket_tiles(False))
    mk, mv = _memkv_call(mem, mem_norm_gain.reshape(1, D_MODEL), w_mem_kv.astype(bf))
    out = _final_call(x2, norm_gain.reshape(1, D_MODEL), final_gain.reshape(1, D_MODEL),
                      ya.reshape(b * t, MOBA_W), za, yb.reshape(b * t, DSA_W), zb, qm, zm, mk, mv,
                      w_gates, w_branch.astype(bf), w_out.astype(bf), t)
    return out.reshape(b, t, D_MODEL)


@jax.jit
def kernel(x, mem, norm_gain, w_in, rel_bias, mem_norm_gain, w_mem_kv, w_branch, w_out, final_norm_gain):
    assert norm_gain.shape[0] == 1, "one layer"
    return _layer(x, mem, norm_gain[0], w_in[0], rel_bias, mem_norm_gain[0], w_mem_kv[0], w_branch[0],
                  w_out[0], final_norm_gain)
```

```python
import math

import numpy as np
import jax
import jax.numpy as jnp
from jax import lax
from jax.experimental import pallas as pl
from jax.experimental.pallas import tpu as pltpu

D_MODEL = 1024
HEAD_DIM = 64
MOBA_HEADS = 6
DSA_HEADS = 6
MEM_HEADS = 4
MOBA_W = MOBA_HEADS * HEAD_DIM
DSA_W = DSA_HEADS * HEAD_DIM
MEM_W = MEM_HEADS * HEAD_DIM
IDX_HEADS = 8
IDX_DIM = 64
IDX_W = IDX_HEADS * IDX_DIM
MOBA_BLOCK = 256
MOBA_TOPK = 3
DSA_TOPK = 256
REL_BUCKETS = 32
REL_MAX_DIST = 128
N_BIAS_HEADS = MOBA_HEADS + DSA_HEADS
EPS = 1e-6
IN_SIZES = (MOBA_W,) * 4 + (DSA_W,) * 4 + (IDX_W, IDX_DIM, IDX_HEADS) + (MEM_W, MEM_W) + (D_MODEL,) * 3

LANES = 128
SUBLANES = 8
ONES_ROWS = 16
ACC_ROWS = LANES + ONES_ROWS
TILE = 256
ROW_TILE = 512
FEAT_LO = 32
LOG2E = math.log2(math.e)
NEG_BIG = -1e30
MASK_BUCKET = REL_BUCKETS
VMEM_LIMIT = 56 * 1024 * 1024
MAX_SEARCH_ITERS = 96
BISECT_EVERY = 5
STALE_LEAN = 0.15
STALL_STEPS = 6.0
BOUND_SLACK = 1.001
MIN_SHIFTED_DENOM = 2.0 ** -60

_NT = (((1,), (1,)), ((), ()))


def _dot_nt(a, b):
    return lax.dot_general(a, b, _NT, preferred_element_type=jnp.float32)


def _dot(a, b):
    return jnp.dot(a, b, preferred_element_type=jnp.float32)


def _rel_bucket_np(n):
    exact = REL_BUCKETS // 2
    nf = np.maximum(n, 1).astype(np.float32)
    large = exact + (np.log(nf / np.float32(exact)) / np.float32(math.log(REL_MAX_DIST / exact))
                     * np.float32(REL_BUCKETS - exact)).astype(np.int32)
    return np.where(n < exact, n, np.minimum(large, REL_BUCKETS - 1))


def _bucket_tiles(mask_future):
    j = np.arange(TILE)[:, None]
    i = np.arange(TILE)[None, :]
    d0 = i - j
    b0 = _rel_bucket_np(np.maximum(d0, 0))
    if mask_future:
        b0 = np.where(d0 >= 0, b0, MASK_BUCKET)
    b1 = _rel_bucket_np(TILE + d0)
    assert int(_rel_bucket_np(np.array([TILE + 1]))[0]) == REL_BUCKETS - 1
    return jnp.asarray(np.stack([b0, b1]).astype(np.int32))


def _build_bias_tile(bkt, bias_ref, head):
    acc = jnp.where(bkt == MASK_BUCKET, NEG_BIG, 0.0)
    for b in range(REL_BUCKETS):
        acc = jnp.where(bkt == b, bias_ref[b * N_BIAS_HEADS + head], acc)
    return acc


def _rmsnorm_rows(x, g):
    xf = x.astype(jnp.float32)
    return xf * lax.rsqrt(jnp.mean(xf * xf, axis=-1, keepdims=True) + EPS) * g


_Q_SCALE = HEAD_DIM ** -0.5 * LOG2E
_STD_SEGS = (("ka", MOBA_W, 1.0), ("za", MOBA_W, 1.0), ("kb", DSA_W, 1.0), ("zb", DSA_W, 1.0),
             ("qm", MEM_W, _Q_SCALE), ("zm", MEM_W, 1.0), ("ik", IDX_DIM, 1.0))
_T_SEGS = (("qaT", MOBA_W, _Q_SCALE), ("vaT", MOBA_W, 1.0), ("qbT", DSA_W, _Q_SCALE), ("vbT", DSA_W, 1.0),
           ("iqT", IDX_W, IDX_DIM ** -0.5), ("iwT", IDX_HEADS, IDX_HEADS ** -0.5))
_STD_W = sum(w for _, w, _ in _STD_SEGS)
_T_W = sum(w for _, w, _ in _T_SEGS)
_CHUNKS_PER_ROW_TILE = ROW_TILE // TILE


def _proj_kernel(x_ref, g_ref, w_ref, wt_ref, *out_refs):
    std_refs = out_refs[:len(_STD_SEGS)]
    t_refs = out_refs[len(_STD_SEGS):len(_STD_SEGS) + len(_T_SEGS)]
    kmean_ref = out_refs[-1]
    h = _rmsnorm_rows(x_ref[...], g_ref[...]).astype(jnp.bfloat16)
    off = 0
    for (name, width, scale), o_ref in zip(_STD_SEGS, std_refs):
        r = _dot(h, w_ref[:, off:off + width])
        if name == "ka":
            for blk in range(ROW_TILE // MOBA_BLOCK):
                kmean_ref[0, blk:blk + 1, :] = jnp.mean(
                    r[blk * MOBA_BLOCK:(blk + 1) * MOBA_BLOCK], axis=0, keepdims=True)
        if scale != 1.0:
            r = r * scale
        o_ref[...] = r.astype(o_ref.dtype)
        off += width
    off = 0
    for (name, width, scale), o_ref in zip(_T_SEGS, t_refs):
        r = _dot_nt(wt_ref[off:off + width, :], h)
        if scale != 1.0:
            r = r * scale
        r = r.astype(o_ref.dtype)
        if name in ("vaT", "vbT"):
            for c in range(_CHUNKS_PER_ROW_TILE):
                o_ref[0, c] = r[:, c * TILE:(c + 1) * TILE]
        else:
            o_ref[0] = r
        off += width


def _proj_call(x2, gain, w_std, w_t, b, t):
    rows = x2.shape[0]
    tiles_per_batch = t // ROW_TILE
    n_tiles = rows // ROW_TILE
    row_spec = lambda width: pl.BlockSpec((ROW_TILE, width), lambda i: (i, 0))
    const_spec = lambda shape: pl.BlockSpec(shape, lambda i: (0,) * len(shape))
    t_spec = lambda width: pl.BlockSpec((1, width, ROW_TILE),
                                        lambda i: (i // tiles_per_batch, 0, i % tiles_per_batch))
    chunk_spec = lambda width: pl.BlockSpec((1, _CHUNKS_PER_ROW_TILE, width, TILE),
                                            lambda i: (i // tiles_per_batch, i % tiles_per_batch, 0, 0))
    out_shape = [jax.ShapeDtypeStruct((rows, w), jnp.bfloat16) for _, w, _ in _STD_SEGS]
    out_specs = [row_spec(w) for _, w, _ in _STD_SEGS]
    for name, w, _ in _T_SEGS:
        if name in ("vaT", "vbT"):
            out_shape.append(jax.ShapeDtypeStruct((b, t // TILE, w, TILE), jnp.bfloat16))
            out_specs.append(chunk_spec(w))
        else:
            out_shape.append(jax.ShapeDtypeStruct((b, w, t), jnp.float32 if name == "iwT" else jnp.bfloat16))
            out_specs.append(t_spec(w))
    out_shape.append(jax.ShapeDtypeStruct((n_tiles, ROW_TILE // MOBA_BLOCK, MOBA_W), jnp.float32))
    out_specs.append(pl.BlockSpec((1, ROW_TILE // MOBA_BLOCK, MOBA_W), lambda i: (i, 0, 0)))
    return pl.pallas_call(
        _proj_kernel,
        grid=(n_tiles,),
        in_specs=[row_spec(D_MODEL), const_spec((1, D_MODEL)), const_spec((D_MODEL, _STD_W)),
                  const_spec((_T_W, D_MODEL))],
        out_specs=out_specs,
        out_shape=out_shape,
        compiler_params=pltpu.CompilerParams(dimension_semantics=("arbitrary",), vmem_limit_bytes=VMEM_LIMIT),
        name="proj",
    )(x2, gain, w_std, w_t)


def _memkv_kernel(mem_ref, g_ref, w_ref, mk_ref, mv_ref):
    h = _rmsnorm_rows(mem_ref[0], g_ref[...]).astype(jnp.bfloat16)
    r = _dot(h, w_ref[...])
    mk_ref[0] = r[:, :MEM_W].astype(jnp.bfloat16)
    mv_ref[0] = r[:, MEM_W:].astype(jnp.bfloat16)


def _memkv_call(mem, gain, w_kv):
    b, n_mem, _ = mem.shape
    return pl.pallas_call(
        _memkv_kernel,
        grid=(b,),
        in_specs=[pl.BlockSpec((1, n_mem, D_MODEL), lambda i: (i, 0, 0)),
                  pl.BlockSpec((1, D_MODEL), lambda i: (0, 0)),
                  pl.BlockSpec((D_MODEL, 2 * MEM_W), lambda i: (0, 0))],
        out_specs=[pl.BlockSpec((1, n_mem, MEM_W), lambda i: (i, 0, 0))] * 2,
        out_shape=[jax.ShapeDtypeStruct((b, n_mem, MEM_W), jnp.bfloat16)] * 2,
        compiler_params=pltpu.CompilerParams(dimension_semantics=("arbitrary",), vmem_limit_bytes=VMEM_LIMIT),
        name="memkv",
    )(mem, gain, w_kv)


def _row_half_mask(shape, half):
    r = lax.broadcasted_iota(jnp.int32, shape, 0)
    return (r >= half * HEAD_DIM) & (r < (half + 1) * HEAD_DIM)


def _with_ones_rows(v_t):
    return jnp.concatenate([v_t, jnp.ones((ONES_ROWS, v_t.shape[1]), v_t.dtype)], axis=0)


def _softmax_stage(s_t, m_ref, alpha_ref, p_ref):
    m_old = m_ref[:1]
    m_new = jnp.maximum(m_old, jnp.max(s_t, axis=0, keepdims=True))
    alpha_ref[...] = jnp.broadcast_to(jnp.exp2(m_old - m_new), alpha_ref.shape)
    m_ref[...] = jnp.broadcast_to(m_new, m_ref.shape)
    p_ref[...] = jnp.exp2(s_t - m_new).astype(jnp.bfloat16)


def _accumulate_stage(v_t, alpha_ref, p_ref, acc_ref):
    acc_ref[...] = alpha_ref[:1] * acc_ref[...] + _dot(_with_ones_rows(v_t), p_ref[...])


def _init_softmax_state(m_ref, alpha_ref, p_ref, acc_ref):
    m_ref[...] = jnp.full(m_ref.shape, NEG_BIG, jnp.float32)
    acc_ref[...] = jnp.zeros(acc_ref.shape, jnp.float32)
    alpha_ref[...] = jnp.ones(alpha_ref.shape, jnp.float32)
    p_ref[...] = jnp.zeros(p_ref.shape, p_ref.dtype)


def _normalized_pair(acc, half_cols):
    out0 = acc[:LANES, :half_cols] / acc[LANES:LANES + 1, :half_cols]
    out1 = acc[:LANES, half_cols:] / acc[LANES:LANES + 1, half_cols:]
    return jnp.where(_row_half_mask((LANES, half_cols), 0), out0, out1).T


_KIND_DIAG, _KIND_PREV, _KIND_FAR = 0, 1, 2


def _run_tile_pipeline(qi, logits, softmax, accumulate):
    n_far = jnp.maximum(qi - 1, 0)
    n_loop = n_far // 2

    def far_pair(t):
        accumulate(0, jnp.maximum(t - 2, 0))
        accumulate(1, jnp.maximum(t - 1, 0))
        softmax(0, t, _KIND_FAR)
        logits(0, t + 2)
        softmax(1, t + 1, _KIND_FAR)
        logits(1, t + 3)

    def quad_body(j, carry):
        far_pair(4 * j)
        far_pair(4 * j + 2)
        return carry

    def pair_body(j, carry):
        far_pair(2 * j)
        return carry

    logits(0, 0)
    logits(1, jnp.minimum(1, qi))
    lax.fori_loop(0, n_loop // 2, quad_body, 0)
    lax.fori_loop(2 * (n_loop // 2), n_loop, pair_body, 0)
    t0 = 2 * n_loop
    odd = n_far % 2 == 1

    @pl.when(odd)
    def _():
        accumulate(0, jnp.maximum(t0 - 2, 0))
        accumulate(1, jnp.maximum(t0 - 1, 0))
        softmax(0, t0, _KIND_FAR)
        logits(0, qi)
        softmax(1, qi - 1, _KIND_PREV)
        accumulate(0, t0)
        softmax(0, qi, _KIND_DIAG)
        accumulate(1, qi - 1)
        accumulate(0, qi)

    @pl.when(jnp.logical_not(odd) & (qi >= 1))
    def _():
        accumulate(0, jnp.maximum(t0 - 2, 0))
        accumulate(1, jnp.maximum(t0 - 1, 0))
        softmax(0, qi - 1, _KIND_PREV)
        softmax(1, qi, _KIND_DIAG)
        accumulate(0, qi - 1)
        accumulate(1, qi)

    @pl.when(qi == 0)
    def _():
        softmax(0, 0, _KIND_DIAG)
        accumulate(0, 0)


def _moba_kernel(bias_ref, qt_ref, k_ref, vt_ref, kmean_ref, bkt_ref, o_ref,
                 qa_ref, tab_ref, s_ref, p_ref, alpha_ref, m_ref, acc_ref):
    qi = pl.program_id(1)

    @pl.when(qi == 0)
    def _():
        for head in range(MOBA_HEADS):
            for kind in (_KIND_DIAG, _KIND_PREV):
                tab_ref[head // 2, kind, :, (head % 2) * TILE:(head % 2 + 1) * TILE] = _build_bias_tile(
                    bkt_ref[kind], bias_ref, head)

    for pair in range(MOBA_HEADS // 2):
        _moba_pair(pair, qi, bias_ref, qt_ref, k_ref, vt_ref, kmean_ref, o_ref,
                   qa_ref, tab_ref.at[pair], s_ref, p_ref, alpha_ref, m_ref, acc_ref)


def _moba_pair(pair, qi, bias_ref, qt_ref, k_ref, vt_ref, kmean_ref, o_ref,
               qa_ref, tab_ref, s_ref, p_ref, alpha_ref, m_ref, acc_ref):
    lanes = slice(pair * LANES, (pair + 1) * LANES)
    q_t = qt_ref[0, lanes, :]
    kmean = kmean_ref[0, :, lanes].astype(jnp.bfloat16)
    blk = lax.broadcasted_iota(jnp.int32, (FEAT_LO, TILE), 0)
    blk_f = blk.astype(jnp.float32)
    past = blk < qi
    lane = lax.broadcasted_iota(jnp.int32, (TILE, LANES), 1)
    _init_softmax_state(m_ref, alpha_ref, p_ref, acc_ref)

    for half in range(2):
        qh = jnp.where(_row_half_mask((LANES, TILE), half), q_t, jnp.zeros_like(q_t))
        gate = _dot(kmean, qh)[:FEAT_LO]
        g = jnp.where(past, gate, -jnp.inf)
        sel = jnp.zeros((FEAT_LO, TILE), jnp.bool_)
        for _ in range(MOBA_TOPK):
            mx = jnp.max(g, axis=0, keepdims=True)
            first = jnp.min(jnp.where(g == mx, blk_f, float(LANES)), axis=0, keepdims=True)
            pick = blk_f == first
            sel = sel | (pick & past)
            g = jnp.where(pick, -jnp.inf, g)
        b31 = jnp.full((FEAT_LO, TILE), bias_ref[(REL_BUCKETS - 1) * N_BIAS_HEADS + pair * 2 + half], jnp.float32)
        b31_hi = b31.astype(jnp.bfloat16).astype(jnp.float32)
        hi_part = jnp.where(sel, jnp.where(blk < qi - 1, b31_hi, 0.0), NEG_BIG)
        hi_part = jnp.where(past, hi_part, 0.0)
        lo_part = jnp.where(blk < qi - 1, b31 - b31_hi, 0.0)
        feat = jnp.concatenate([hi_part, lo_part, jnp.zeros((LANES - 2 * FEAT_LO, TILE), jnp.float32)], axis=0)
        qa_ref[:, half * TILE:(half + 1) * TILE] = jnp.concatenate([qh, feat.astype(jnp.bfloat16)], axis=0)

    def logits(slot, kj):
        k_t = k_ref[0, pl.ds(pl.multiple_of(kj * TILE, TILE), TILE), lanes]
        onehot = ((lane == kj) | (lane == kj + FEAT_LO)).astype(jnp.bfloat16)
        s_ref[slot] = _dot(jnp.concatenate([k_t, onehot], axis=1), qa_ref[...])

    def softmax(slot, kj, kind):
        s_t = s_ref[slot]
        if kind != _KIND_FAR:
            s_t = s_t + tab_ref[kind]
        _softmax_stage(s_t, m_ref, alpha_ref.at[slot], p_ref.at[slot])

    def accumulate(slot, kj):
        _accumulate_stage(vt_ref[0, kj, lanes, :], alpha_ref.at[slot], p_ref.at[slot], acc_ref)

    _run_tile_pipeline(qi, logits, softmax, accumulate)
    o_ref[0, :, lanes] = _normalized_pair(acc_ref[...], TILE).astype(o_ref.dtype)


def _moba_call(bias_flat, q_t, k, v_t, kmean, bkt):
    b, _, t = q_t.shape
    n_pairs = MOBA_HEADS // 2
    n_chunks = t // TILE
    grid_spec = pltpu.PrefetchScalarGridSpec(
        num_scalar_prefetch=1,
        grid=(b, n_chunks),
        in_specs=[
            pl.BlockSpec((1, MOBA_W, TILE), lambda bi, qi, s: (bi, 0, qi)),
            pl.BlockSpec((1, t, MOBA_W), lambda bi, qi, s: (bi, 0, 0)),
            pl.BlockSpec((1, n_chunks, MOBA_W, TILE), lambda bi, qi, s: (bi, 0, 0, 0)),
            pl.BlockSpec((1, LANES, MOBA_W), lambda bi, qi, s: (bi, 0, 0)),
            pl.BlockSpec((2, TILE, TILE), lambda bi, qi, s: (0, 0, 0)),
        ],
        out_specs=pl.BlockSpec((1, TILE, MOBA_W), lambda bi, qi, s: (bi, qi, 0)),
        scratch_shapes=[
            pltpu.VMEM((2 * LANES, 2 * TILE), jnp.bfloat16),
            pltpu.VMEM((n_pairs, 2, TILE, 2 * TILE), jnp.float32),
            pltpu.VMEM((2, TILE, 2 * TILE), jnp.float32),
            pltpu.VMEM((2, TILE, 2 * TILE), jnp.bfloat16),
            pltpu.VMEM((2, SUBLANES, 2 * TILE), jnp.float32),
            pltpu.VMEM((SUBLANES, 2 * TILE), jnp.float32),
            pltpu.VMEM((ACC_ROWS, 2 * TILE), jnp.float32),
        ],
    )
    return pl.pallas_call(
        _moba_kernel,
        grid_spec=grid_spec,
        out_shape=jax.ShapeDtypeStruct((b, t, MOBA_W), jnp.bfloat16),
        compiler_params=pltpu.CompilerParams(dimension_semantics=("arbitrary",) * 2, vmem_limit_bytes=VMEM_LIMIT),
        name="moba",
    )(bias_flat, q_t, k, v_t, kmean, bkt)


def _dsa_kernel(bias_ref, qt_ref, k_ref, vt_ref, iqt_ref, ik_ref, iwt_ref, bkt_ref, o_ref,
                sc_ref, tab_ref, bound_ref, qz_ref, s_ref, p_ref, alpha_ref, seen_ref, m_ref, acc_ref):
    qi = pl.program_id(1)
    n_chunks = qi + 1
    key = lax.broadcasted_iota(jnp.int32, (TILE, TILE), 0)
    qry = lax.broadcasted_iota(jnp.int32, (TILE, TILE), 1)
    causal = key <= qry

    far_bias = [bias_ref[(REL_BUCKETS - 1) * N_BIAS_HEADS + MOBA_HEADS + h] for h in range(DSA_HEADS)]

    @pl.when(qi == 0)
    def _():
        for h in range(DSA_HEADS):
            rel_max = jnp.zeros((1, 1), jnp.float32)
            for kind in (_KIND_DIAG, _KIND_PREV):
                tile = _build_bias_tile(bkt_ref[kind], bias_ref, MOBA_HEADS + h) - far_bias[h]
                tab_ref[h // 2, kind, :, (h % 2) * TILE:(h % 2 + 1) * TILE] = tile
                rel_max = jnp.maximum(rel_max, jnp.max(jnp.max(tile, axis=0, keepdims=True), axis=1, keepdims=True))
            bound_ref[h] = jnp.broadcast_to(rel_max + far_bias[h], (SUBLANES, LANES))

        for pr in range(DSA_HEADS // 2):
            def norm_body(c, mx, pr=pr):
                blk = k_ref[0, pl.ds(pl.multiple_of(c * TILE, TILE), TILE), pr * LANES:(pr + 1) * LANES]
                blk = blk.astype(jnp.float32)
                return jnp.maximum(mx, jnp.sum(blk * blk, axis=1, keepdims=True))
            n2 = lax.fori_loop(0, k_ref.shape[1] // TILE, norm_body, jnp.zeros((TILE, 1), jnp.float32))
            bound_ref[DSA_HEADS + pr] = jnp.broadcast_to(jnp.sqrt(jnp.max(n2, axis=0, keepdims=True)),
                                                         (SUBLANES, LANES))

    feat_row = lax.broadcasted_iota(jnp.int32, (LANES, TILE), 0)
    for h in range(DSA_HEADS):
        blk = qt_ref[0, (h // 2) * LANES:(h // 2 + 1) * LANES, :]
        qh = jnp.where(_row_half_mask((LANES, TILE), h % 2), blk, jnp.zeros_like(blk))
        qf = qh.astype(jnp.float32)
        q_norm = jnp.sqrt(jnp.sum(qf * qf, axis=0, keepdims=True))
        logit_bound = q_norm * bound_ref[DSA_HEADS + h // 2][:1, :1] * BOUND_SLACK + bound_ref[h][:1, :1]
        b31 = jnp.full((LANES, TILE), far_bias[h], jnp.float32)
        b31_hi = b31.astype(jnp.bfloat16).astype(jnp.float32)
        feat = jnp.where(feat_row == 0, b31_hi, jnp.where(feat_row == 1, b31 - b31_hi, 0.0))
        feat = jnp.where(feat_row == 2, -logit_bound, feat)
        qz_ref[h // 2, :, (h % 2) * TILE:(h % 2 + 1) * TILE] = jnp.concatenate(
            [qh, feat.astype(jnp.bfloat16)], axis=0)

    iw_t = iwt_ref[0]

    def score_chunk(kc):
        row0 = pl.multiple_of(kc * TILE, TILE)
        ik_t = ik_ref[0, pl.ds(row0, TILE), :]
        tot = jnp.zeros((TILE, TILE), jnp.float32)
        for h in range(IDX_HEADS):
            x = _dot(ik_t, iqt_ref[0, h * IDX_DIM:(h + 1) * IDX_DIM, :])
            tot = tot + iw_t[h:h + 1, :] * jnp.maximum(x, 0.0)
        return tot

    def score_body(kc, carry):
        mx, mn = carry
        tot = score_chunk(kc)
        sc_ref[kc] = tot
        return (jnp.maximum(mx, jnp.max(tot, axis=0, keepdims=True)),
                jnp.minimum(mn, jnp.min(tot, axis=0, keepdims=True)))

    def score_pair_body(j, carry):
        return score_body(2 * j + 1, score_body(2 * j, carry))

    extremes = lax.fori_loop(0, qi // 2, score_pair_body, (jnp.full((1, TILE), -jnp.inf, jnp.float32),
                                                           jnp.full((1, TILE), jnp.inf, jnp.float32)))
    row_max, row_min = lax.cond(qi % 2 == 1, lambda c: score_body(qi - 1, c), lambda c: c, extremes)
    tot = score_chunk(qi)
    sc_ref[qi] = jnp.where(causal, tot, -jnp.inf)
    row_max = jnp.maximum(row_max, jnp.max(jnp.where(causal, tot, -jnp.inf), axis=0, keepdims=True))
    row_min = jnp.minimum(row_min, jnp.min(jnp.where(causal, tot, jnp.inf), axis=0, keepdims=True))

    @pl.when(qi % 2 == 0)
    def _():
        sc_ref[qi + 1] = jnp.full((TILE, TILE), -jnp.inf, jnp.float32)

    def reduce_pass(fn, init, combine):
        def body(j, part):
            return combine(part, combine(fn(sc_ref[2 * j]), fn(sc_ref[2 * j + 1])))
        return lax.fori_loop(0, (n_chunks + 1) // 2, body, jnp.full((1, TILE), init, jnp.float32))

    def count(pred):
        return reduce_pass(lambda x: jnp.sum(jnp.where(pred(x), 1.0, 0.0), axis=0, keepdims=True), 0.0, jnp.add)

    def min_ge(p):
        return reduce_pass(lambda x: jnp.min(jnp.where(x >= p, x, jnp.inf), axis=0, keepdims=True),
                           jnp.inf, jnp.minimum)

    n_adm =(qi * TILE + 1 + lax.broadcasted_iota(jnp.int32, (1, TILE), 1)).astype(jnp.float32)
    k_top = float(DSA_TOPK)
    take_all = n_adm <= k_top

    def search_cond(st):
        return jnp.logical_and(st[0] < MAX_SEARCH_ITERS, st[-1] >= some_live)

    all_done, some_live, some_stalled = 0.0, 1.0, 2.0

    def live_flag(done, stall):
        return jnp.max(jnp.where(done < 0.5, jnp.where(stall >= STALL_STEPS, some_stalled, some_live), all_done))

    def tie_check(st):
        it, lo, hi, hi_fin, c_lo, c_hi, done, last, streak, stall, _ = st
        a = min_ge(lo)
        c_gt = count(lambda x: x > a)
        tie = (c_gt < k_top) & (done < 0.5)
        done = jnp.where(tie, 1.0, done)
        stall = jnp.zeros_like(stall)
        return (it, lo, hi, hi_fin, c_lo, jnp.where(tie, c_gt, c_hi), done, last, streak, stall,
                live_flag(done, stall))

    def search_step(st):
        it, lo, hi, hi_fin, c_lo, c_hi, done, last, streak, stall = st
        width = c_lo - c_hi
        target = k_top - 0.5 + jnp.where(streak >= 1.0, last * STALE_LEAN * (c_lo - c_hi), 0.0)
        target = jnp.clip(target, c_hi + 0.5, c_lo - 0.5)
        frac = (c_lo - target) / jnp.maximum(c_lo - c_hi, 1.0)
        mid = 0.5 * lo + 0.5 * hi_fin
        p = jnp.where(it % BISECT_EVERY == BISECT_EVERY - 1, mid, lo + frac * (hi_fin - lo))
        p = jnp.where((p > lo) & (p < hi), p, mid)
        stuck = jnp.logical_not((mid > lo) & (mid < hi))
        c = count(lambda x: x >= p)
        up = c >= k_top
        move = jnp.logical_not((done > 0.5) | stuck)
        rise = move & up
        fall = move & jnp.logical_not(up)
        lo = jnp.where(rise, p, lo)
        c_lo = jnp.where(rise, c, c_lo)
        hi = jnp.where(fall, p, hi)
        c_hi = jnp.where(fall, c, c_hi)
        hi_fin = jnp.where(fall, p, hi_fin)
        side = jnp.where(up, -1.0, 1.0)
        streak = jnp.where(side == last, streak + 1.0, 0.0)
        done = jnp.where((c_lo - c_hi <= 1.0) | (c_lo == k_top) | stuck, 1.0, done)
        stall = jnp.where(c_lo - c_hi == width, stall + 1.0, 0.0)
        return it + 1, lo, hi, hi_fin, c_lo, c_hi, done, side, streak, stall

    def search_body(st):
        st = search_step(search_step(search_step(search_step(st[:-1]))))
        flag = live_flag(st[6], st[9])
        return lax.cond(flag >= some_stalled, tie_check, lambda s: s, st + (flag,))

    zeros_row = jnp.zeros((1, TILE), jnp.float32)
    done0 = take_all.astype(jnp.float32)
    init = (jnp.int32(0), row_min, jnp.full((1, TILE), jnp.inf, jnp.float32), row_max,
            n_adm, zeros_row, done0, zeros_row, zeros_row, zeros_row, live_flag(done0, zeros_row))
    _, lo, _, _, _, c_hi, _, _, _, _, _ = lax.while_loop(search_cond, search_body, init)
    thr = jnp.where(take_all, -jnp.inf, min_ge(lo))
    need = k_top - jnp.where(take_all, 0.0, c_hi)

    n_pairs = DSA_HEADS // 2
    lower = (qry < key).astype(jnp.bfloat16)
    ones_feat = (lax.broadcasted_iota(jnp.int32, (TILE, LANES), 1) < 3).astype(jnp.bfloat16)

    def init_state():
        for pr in range(n_pairs):
            _init_softmax_state(m_ref.at[pr], alpha_ref.at[:, pr], p_ref.at[:, pr], acc_ref.at[pr])
        seen_ref[...] = jnp.zeros(seen_ref.shape, jnp.float32)

    def logits(slot, kc):
        row0 = pl.multiple_of(kc * TILE, TILE)
        for pr in range(n_pairs):
            k_t = k_ref[0, pl.ds(row0, TILE), pr * LANES:(pr + 1) * LANES]
            s_ref[slot, pr] = _dot(jnp.concatenate([k_t, ones_feat], axis=1), qz_ref[pr])

    def masked_logits(slot, kc, kind):
        x = sc_ref[kc]
        tie = x == thr
        tie_f = jnp.where(tie, 1.0, 0.0)
        seen = seen_ref[:1]
        rank = seen + _dot(lower, tie_f.astype(jnp.bfloat16))
        selected = (x > thr) | (tie & (rank < need))
        if kind == _KIND_DIAG:
            selected = selected & causal
        mask_add = jnp.where(selected, 0.0, NEG_BIG)
        mask_add = jnp.concatenate([mask_add, mask_add], axis=1)
        seen_ref[...] = jnp.broadcast_to(seen + jnp.sum(tie_f, axis=0, keepdims=True), seen_ref.shape)
        for pr in range(n_pairs):
            s_t = s_ref[slot, pr] + mask_add
            if kind != _KIND_FAR:
                s_t = s_t + tab_ref[pr, kind]
            yield pr, s_t

    def softmax_shifted(slot, kc, kind):
        for pr, s_t in masked_logits(slot, kc, kind):
            p_ref[slot, pr] = jnp.exp2(s_t).astype(jnp.bfloat16)

    def accumulate_shifted(slot, kc):
        for pr in range(n_pairs):
            acc_ref[pr] += _dot(_with_ones_rows(vt_ref[0, kc, pr * LANES:(pr + 1) * LANES, :]), p_ref[slot, pr])

    def softmax_online(slot, kc, kind):
        for pr, s_t in masked_logits(slot, kc, kind):
            _softmax_stage(s_t, m_ref.at[pr], alpha_ref.at[slot, pr], p_ref.at[slot, pr])

    def accumulate_online(slot, kc):
        for pr in range(n_pairs):
            _accumulate_stage(vt_ref[0, kc, pr * LANES:(pr + 1) * LANES, :], alpha_ref.at[slot, pr],
                              p_ref.at[slot, pr], acc_ref.at[pr])

    init_state()
    _run_tile_pipeline(qi, logits, softmax_shifted, accumulate_shifted)
    denom = acc_ref[0, LANES:LANES + 1, :]
    for pr in range(1, n_pairs):
        denom = jnp.minimum(denom, acc_ref[pr, LANES:LANES + 1, :])
    denom_min = jnp.min(denom)

    @pl.when(jnp.logical_not(denom_min > MIN_SHIFTED_DENOM))
    def _():
        init_state()
        _run_tile_pipeline(qi, logits, softmax_online, accumulate_online)

    for pr in range(n_pairs):
        o_ref[0, :, pr * LANES:(pr + 1) * LANES] = _normalized_pair(acc_ref[pr], TILE).astype(o_ref.dtype)


def _dsa_call(bias_flat, q_t, k, v_t, iq_t, ik, iw_t, bkt):
    b, _, t = q_t.shape
    n_chunks = t // TILE
    n_pairs = DSA_HEADS // 2
    grid_spec = pltpu.PrefetchScalarGridSpec(
        num_scalar_prefetch=1,
        grid=(b, n_chunks),
        in_specs=[
            pl.BlockSpec((1, DSA_W, TILE), lambda bi, qi, s: (bi, 0, qi)),
            pl.BlockSpec((1, t, DSA_W), lambda bi, qi, s: (bi, 0, 0)),
            pl.BlockSpec((1, n_chunks, DSA_W, TILE), lambda bi, qi, s: (bi, 0, 0, 0)),
            pl.BlockSpec((1, IDX_W, TILE), lambda bi, qi, s: (bi, 0, qi)),
            pl.BlockSpec((1, t, IDX_DIM), lambda bi, qi, s: (bi, 0, 0)),
            pl.BlockSpec((1, IDX_HEADS, TILE), lambda bi, qi, s: (bi, 0, qi)),
            pl.BlockSpec((2, TILE, TILE), lambda bi, qi, s: (0, 0, 0)),
        ],
        out_specs=pl.BlockSpec((1, TILE, DSA_W), lambda bi, qi, s: (bi, qi, 0)),
        scratch_shapes=[
            pltpu.VMEM((n_chunks, TILE, TILE), jnp.float32),
            pltpu.VMEM((n_pairs, 2, TILE, 2 * TILE), jnp.float32),
            pltpu.VMEM((DSA_HEADS + n_pairs, SUBLANES, LANES), jnp.float32),
            pltpu.VMEM((n_pairs, 2 * LANES, 2 * TILE), jnp.bfloat16),
            pltpu.VMEM((2, n_pairs, TILE, 2 * TILE), jnp.float32),
            pltpu.VMEM((2, n_pairs, TILE, 2 * TILE), jnp.bfloat16),
            pltpu.VMEM((2, n_pairs, SUBLANES, 2 * TILE), jnp.float32),
            pltpu.VMEM((SUBLANES, TILE), jnp.float32),
            pltpu.VMEM((n_pairs, SUBLANES, 2 * TILE), jnp.float32),
            pltpu.VMEM((n_pairs, ACC_ROWS, 2 * TILE), jnp.float32),
        ],
    )
    return pl.pallas_call(
        _dsa_kernel,
        grid_spec=grid_spec,
        out_shape=jax.ShapeDtypeStruct((b, t, DSA_W), jnp.bfloat16),
        compiler_params=pltpu.CompilerParams(dimension_semantics=("arbitrary",) * 2, vmem_limit_bytes=VMEM_LIMIT),
        name="dsa",
    )(bias_flat, q_t, k, v_t, iq_t, ik, iw_t, bkt)


def _lane_half_mask(shape, half):
    lane = lax.broadcasted_iota(jnp.int32, shape, 1)
    return (lane >= half * HEAD_DIM) & (lane < (half + 1) * HEAD_DIM)


def _final_kernel(x_ref, g_ref, fg_ref, ya_ref, za_ref, yb_ref, zb_ref, qm_ref, zm_ref, mk_ref, mv_ref,
                  wg_ref, wb_ref, wo_ref, o_ref):
    x = x_ref[...]
    h = _rmsnorm_rows(x, g_ref[...]).astype(jnp.bfloat16)

    ym_parts = []
    for pr in range(MEM_HEADS // 2):
        qp = qm_ref[:, pr * LANES:(pr + 1) * LANES]
        mk = mk_ref[0, :, pr * LANES:(pr + 1) * LANES]
        mv = mv_ref[0, :, pr * LANES:(pr + 1) * LANES]
        outs = []
        for half in range(2):
            qh = jnp.where(_lane_half_mask((ROW_TILE, LANES), half), qp, jnp.zeros_like(qp))
            s = _dot_nt(qh, mk)
            p = jnp.exp2(s - jnp.max(s, axis=1, keepdims=True))
            outs.append(_dot(p.astype(jnp.bfloat16), mv) / jnp.sum(p, axis=1, keepdims=True))
        ym_parts.append(jnp.where(_lane_half_mask((ROW_TILE, LANES), 0), outs[0], outs[1]))
    ym = jnp.concatenate(ym_parts, axis=1)

    def gated(y, z_ref):
        z = z_ref[...].astype(jnp.float32)
        return (y.astype(jnp.float32) * (z * jax.nn.sigmoid(z))).astype(jnp.bfloat16)

    ua = gated(ya_ref[...], za_ref)
    ub = gated(yb_ref[...], zb_ref)
    um = gated(ym, zm_ref)
    merged = jax.nn.sigmoid(_dot(h, wg_ref[:, :D_MODEL])) * _dot(ua, wb_ref[:MOBA_W])
    merged += jax.nn.sigmoid(_dot(h, wg_ref[:, D_MODEL:2 * D_MODEL])) * _dot(ub, wb_ref[MOBA_W:MOBA_W + DSA_W])
    merged += jax.nn.sigmoid(_dot(h, wg_ref[:, 2 * D_MODEL:])) * _dot(um, wb_ref[MOBA_W + DSA_W:])
    y = x + _dot(merged.astype(jnp.bfloat16), wo_ref[...])
    o_ref[...] = _rmsnorm_rows(y, fg_ref[...])


def _final_call(x2, gain, final_gain, ya, za, yb, zb, qm, zm, mk, mv, w_gates, w_branch, w_out, rows_per_batch):
    rows = x2.shape[0]
    tiles_per_batch = rows_per_batch // ROW_TILE
    row_spec = lambda width: pl.BlockSpec((ROW_TILE, width), lambda i: (i, 0))
    const_spec = lambda shape: pl.BlockSpec(shape, lambda i: (0,) * len(shape))
    mem_spec = pl.BlockSpec((1,) + mk.shape[1:], lambda i: (i // tiles_per_batch, 0, 0))
    return pl.pallas_call(
        _final_kernel,
        grid=(rows // ROW_TILE,),
        in_specs=[row_spec(D_MODEL), const_spec((1, D_MODEL)), const_spec((1, D_MODEL)),
                  row_spec(MOBA_W), row_spec(MOBA_W), row_spec(DSA_W), row_spec(DSA_W),
                  row_spec(MEM_W), row_spec(MEM_W), mem_spec, mem_spec,
                  const_spec(w_gates.shape), const_spec(w_branch.shape), const_spec(w_out.shape)],
        out_specs=row_spec(D_MODEL),
        out_shape=jax.ShapeDtypeStruct((rows, D_MODEL), jnp.float32),
        compiler_params=pltpu.CompilerParams(dimension_semantics=("arbitrary",), vmem_limit_bytes=VMEM_LIMIT),
        name="final",
    )(x2, gain, final_gain, ya, za, yb, zb, qm, zm, mk, mv, w_gates, w_branch, w_out)


def _layer(x, mem, norm_gain, w_in, rel_bias, mem_norm_gain, w_mem_kv, w_branch, w_out, final_gain):
    b, t, _ = x.shape
    assert t % (2 * TILE) == 0 and t // MOBA_BLOCK <= FEAT_LO and t >= 4 * DSA_TOPK
    bf = jnp.bfloat16
    names = ("qa", "ka", "va", "za", "qb", "kb", "vb", "zb", "iq", "ik", "iw", "qm", "zm", "ga", "gb", "gm")
    cols, off = {}, 0
    for name, size in zip(names, IN_SIZES):
        cols[name] = w_in[:, off:off + size]
        off += size
    w_std = jnp.concatenate([cols[n] for n, _, _ in _STD_SEGS], axis=1).astype(bf)
    w_t = jnp.concatenate([cols[n[:-1]] for n, _, _ in _T_SEGS], axis=1).T.astype(bf)
    w_gates = jnp.concatenate([cols["ga"], cols["gb"], cols["gm"]], axis=1).astype(bf)

    x2 = x.reshape(b * t, D_MODEL)
    (ka, za, kb, zb, qm, zm, ik, qa_t, va_t, qb_t, vb_t, iq_t, iw_t, kmean) = _proj_call(
        x2, norm_gain.reshape(1, D_MODEL), w_std, w_t, b, t)
    seq = lambda a: a.reshape(b, t, a.shape[-1])
    n_blocks = t // MOBA_BLOCK
    kmean = jnp.pad(kmean.reshape(b, n_blocks, MOBA_W), ((0, 0), (0, LANES - n_blocks), (0, 0)))

    bias_flat = (rel_bias * LOG2E).reshape(-1)
    ya = _moba_call(bias_flat, qa_t, seq(ka), va_t, kmean, _bucket_tiles(True))
    yb = _dsa_call(bias_flat, qb_t, seq(kb), vb_t, iq_t, seq(ik), iw_t, _bucket_tiles(False))
    mk, mv = _memkv_call(mem, mem_norm_gain.reshape(1, D_MODEL), w_mem_kv.astype(bf))
    out = _final_call(x2, norm_gain.reshape(1, D_MODEL), final_gain.reshape(1, D_MODEL),
                      ya.reshape(b * t, MOBA_W), za, yb.reshape(b * t, DSA_W), zb, qm, zm, mk, mv,
                      w_gates, w_branch.astype(bf), w_out.astype(bf), t)
    return out.reshape(b, t, D_MODEL)


@jax.jit
def kernel(x, mem, norm_gain, w_in, rel_bias, mem_norm_gain, w_mem_kv, w_branch, w_out, final_norm_gain):
    assert norm_gain.shape[0] == 1, "one layer"
    return _layer(x, mem, norm_gain[0], w_in[0], rel_bias, mem_norm_gain[0], w_mem_kv[0], w_branch[0],
                  w_out[0], final_norm_gain)
```

```python
import math

import numpy as np
import jax
import jax.numpy as jnp
from jax import lax
from jax.experimental import pallas as pl
from jax.experimental.pallas import tpu as pltpu

D_MODEL = 1024
HEAD_DIM = 64
MOBA_HEADS = 6
DSA_HEADS = 6
MEM_HEADS = 4
MOBA_W = MOBA_HEADS * HEAD_DIM
DSA_W = DSA_HEADS * HEAD_DIM
MEM_W = MEM_HEADS * HEAD_DIM
IDX_HEADS = 8
IDX_DIM = 64
IDX_W = IDX_HEADS * IDX_DIM
MOBA_BLOCK = 256
MOBA_TOPK = 3
DSA_TOPK = 256
REL_BUCKETS = 32
REL_MAX_DIST = 128
N_BIAS_HEADS = MOBA_HEADS + DSA_HEADS
EPS = 1e-6
IN_SIZES = (MOBA_W,) * 4 + (DSA_W,) * 4 + (IDX_W, IDX_DIM, IDX_HEADS) + (MEM_W, MEM_W) + (D_MODEL,) * 3

LANES = 128
SUBLANES = 8
ONES_ROWS = 16
ACC_ROWS = LANES + ONES_ROWS
TILE = 256
ROW_TILE = 512
FEAT_LO = 32
LOG2E = math.log2(math.e)
NEG_BIG = -1e30
MASK_BUCKET = REL_BUCKETS
VMEM_LIMIT = 56 * 1024 * 1024
MAX_SEARCH_ITERS = 96
BISECT_EVERY = 5
STALE_LEAN = 0.15
STALL_STEPS = 6.0
BOUND_SLACK = 1.001
MIN_SHIFTED_DENOM = 2.0 ** -60

_NT = (((1,), (1,)), ((), ()))


def _dot_nt(a, b):
    return lax.dot_general(a, b, _NT, preferred_element_type=jnp.float32)


def _dot(a, b):
    return jnp.dot(a, b, preferred_element_type=jnp.float32)


def _rel_bucket_np(n):
    exact = REL_BUCKETS // 2
    nf = np.maximum(n, 1).astype(np.float32)
    large = exact + (np.log(nf / np.float32(exact)) / np.float32(math.log(REL_MAX_DIST / exact))
                     * np.float32(REL_BUCKETS - exact)).astype(np.int32)
    return np.where(n < exact, n, np.minimum(large, REL_BUCKETS - 1))


def _bucket_tiles(mask_future):
    j = np.arange(TILE)[:, None]
    i = np.arange(TILE)[None, :]
    d0 = i - j
    b0 = _rel_bucket_np(np.maximum(d0, 0))
    if mask_future:
        b0 = np.where(d0 >= 0, b0, MASK_BUCKET)
    b1 = _rel_bucket_np(TILE + d0)
    assert int(_rel_bucket_np(np.array([TILE + 1]))[0]) == REL_BUCKETS - 1
    return jnp.asarray(np.stack([b0, b1]).astype(np.int32))


def _build_bias_tile(bkt, bias_ref, head):
    acc = jnp.where(bkt == MASK_BUCKET, NEG_BIG, 0.0)
    for b in range(REL_BUCKETS):
        acc = jnp.where(bkt == b, bias_ref[b * N_BIAS_HEADS + head], acc)
    return acc


def _rmsnorm_rows(x, g):
    xf = x.astype(jnp.float32)
    return xf * lax.rsqrt(jnp.mean(xf * xf, axis=-1, keepdims=True) + EPS) * g


_Q_SCALE = HEAD_DIM ** -0.5 * LOG2E
_STD_SEGS = (("ka", MOBA_W, 1.0), ("za", MOBA_W, 1.0), ("kb", DSA_W, 1.0), ("zb", DSA_W, 1.0),
             ("qm", MEM_W, _Q_SCALE), ("zm", MEM_W, 1.0), ("ik", IDX_DIM, 1.0))
_T_SEGS = (("qaT", MOBA_W, _Q_SCALE), ("vaT", MOBA_W, 1.0), ("qbT", DSA_W, _Q_SCALE), ("vbT", DSA_W, 1.0),
           ("iqT", IDX_W, IDX_DIM ** -0.5), ("iwT", IDX_HEADS, IDX_HEADS ** -0.5))
_STD_W = sum(w for _, w, _ in _STD_SEGS)
_T_W = sum(w for _, w, _ in _T_SEGS)
_CHUNKS_PER_ROW_TILE = ROW_TILE // TILE


def _proj_kernel(x_ref, g_ref, w_ref, wt_ref, *out_refs):
    std_refs = out_refs[:len(_STD_SEGS)]
    t_refs = out_refs[len(_STD_SEGS):len(_STD_SEGS) + len(_T_SEGS)]
    kmean_ref = out_refs[-1]
    h = _rmsnorm_rows(x_ref[...], g_ref[...]).astype(jnp.bfloat16)
    off = 0
    for (name, width, scale), o_ref in zip(_STD_SEGS, std_refs):
        r = _dot(h, w_ref[:, off:off + width])
        if name == "ka":
            for blk in range(ROW_TILE // MOBA_BLOCK):
                kmean_ref[0, blk:blk + 1, :] = jnp.mean(
                    r[blk * MOBA_BLOCK:(blk + 1) * MOBA_BLOCK], axis=0, keepdims=True)
        if scale != 1.0:
            r = r * scale
        o_ref[...] = r.astype(o_ref.dtype)
        off += width
    off = 0
    for (name, width, scale), o_ref in zip(_T_SEGS, t_refs):
        r = _dot_nt(wt_ref[off:off + width, :], h)
        if scale != 1.0:
            r = r * scale
        r = r.astype(o_ref.dtype)
        if name in ("vaT", "vbT"):
            for c in range(_CHUNKS_PER_ROW_TILE):
                o_ref[0, c] = r[:, c * TILE:(c + 1) * TILE]
        else:
            o_ref[0] = r
        off += width


def _proj_call(x2, gain, w_std, w_t, b, t):
    rows = x2.shape[0]
    tiles_per_batch = t // ROW_TILE
    n_tiles = rows // ROW_TILE
    row_spec = lambda width: pl.BlockSpec((ROW_TILE, width), lambda i: (i, 0))
    const_spec = lambda shape: pl.BlockSpec(shape, lambda i: (0,) * len(shape))
    t_spec = lambda width: pl.BlockSpec((1, width, ROW_TILE),
                                        lambda i: (i // tiles_per_batch, 0, i % tiles_per_batch))
    chunk_spec = lambda width: pl.BlockSpec((1, _CHUNKS_PER_ROW_TILE, width, TILE),
                                            lambda i: (i // tiles_per_batch, i % tiles_per_batch, 0, 0))
    out_shape = [jax.ShapeDtypeStruct((rows, w), jnp.bfloat16) for _, w, _ in _STD_SEGS]
    out_specs = [row_spec(w) for _, w, _ in _STD_SEGS]
    for name, w, _ in _T_SEGS:
        if name in ("vaT", "vbT"):
            out_shape.append(jax.ShapeDtypeStruct((b, t // TILE, w, TILE), jnp.bfloat16))
            out_specs.append(chunk_spec(w))
        else:
            out_shape.append(jax.ShapeDtypeStruct((b, w, t), jnp.float32 if name == "iwT" else jnp.bfloat16))
            out_specs.append(t_spec(w))
    out_shape.append(jax.ShapeDtypeStruct((n_tiles, ROW_TILE // MOBA_BLOCK, MOBA_W), jnp.float32))
    out_specs.append(pl.BlockSpec((1, ROW_TILE // MOBA_BLOCK, MOBA_W), lambda i: (i, 0, 0)))
    return pl.pallas_call(
        _proj_kernel,
        grid=(n_tiles,),
        in_specs=[row_spec(D_MODEL), const_spec((1, D_MODEL)), const_spec((D_MODEL, _STD_W)),
                  const_spec((_T_W, D_MODEL))],
        out_specs=out_specs,
        out_shape=out_shape,
        compiler_params=pltpu.CompilerParams(dimension_semantics=("arbitrary",), vmem_limit_bytes=VMEM_LIMIT),
        name="proj",
    )(x2, gain, w_std, w_t)


def _memkv_kernel(mem_ref, g_ref, w_ref, mk_ref, mv_ref):
    h = _rmsnorm_rows(mem_ref[0], g_ref[...]).astype(jnp.bfloat16)
    r = _dot(h, w_ref[...])
    mk_ref[0] = r[:, :MEM_W].astype(jnp.bfloat16)
    mv_ref[0] = r[:, MEM_W:].astype(jnp.bfloat16)


def _memkv_call(mem, gain, w_kv):
    b, n_mem, _ = mem.shape
    return pl.pallas_call(
        _memkv_kernel,
        grid=(b,),
        in_specs=[pl.BlockSpec((1, n_mem, D_MODEL), lambda i: (i, 0, 0)),
                  pl.BlockSpec((1, D_MODEL), lambda i: (0, 0)),
                  pl.BlockSpec((D_MODEL, 2 * MEM_W), lambda i: (0, 0))],
        out_specs=[pl.BlockSpec((1, n_mem, MEM_W), lambda i: (i, 0, 0))] * 2,
        out_shape=[jax.ShapeDtypeStruct((b, n_mem, MEM_W), jnp.bfloat16)] * 2,
        compiler_params=pltpu.CompilerParams(dimension_semantics=("arbitrary",), vmem_limit_bytes=VMEM_LIMIT),
        name="memkv",
    )(mem, gain, w_kv)


def _row_half_mask(shape, half):
    r = lax.broadcasted_iota(jnp.int32, shape, 0)
    return (r >= half * HEAD_DIM) & (r < (half + 1) * HEAD_DIM)


def _with_ones_rows(v_t):
    return jnp.concatenate([v_t, jnp.ones((ONES_ROWS, v_t.shape[1]), v_t.dtype)], axis=0)


def _softmax_stage(s_t, m_ref, alpha_ref, p_ref):
    m_old = m_ref[:1]
    m_new = jnp.maximum(m_old, jnp.max(s_t, axis=0, keepdims=True))
    alpha_ref[...] = jnp.broadcast_to(jnp.exp2(m_old - m_new), alpha_ref.shape)
    m_ref[...] = jnp.broadcast_to(m_new, m_ref.shape)
    p_ref[...] = jnp.exp2(s_t - m_new).astype(jnp.bfloat16)


def _accumulate_stage(v_t, alpha_ref, p_ref, acc_ref):
    acc_ref[...] = alpha_ref[:1] * acc_ref[...] + _dot(_with_ones_rows(v_t), p_ref[...])


def _init_softmax_state(m_ref, alpha_ref, p_ref, acc_ref):
    m_ref[...] = jnp.full(m_ref.shape, NEG_BIG, jnp.float32)
    acc_ref[...] = jnp.zeros(acc_ref.shape, jnp.float32)
    alpha_ref[...] = jnp.ones(alpha_ref.shape, jnp.float32)
    p_ref[...] = jnp.zeros(p_ref.shape, p_ref.dtype)


def _normalized_pair(acc, half_cols):
    out0 = acc[:LANES, :half_cols] / acc[LANES:LANES + 1, :half_cols]
    out1 = acc[:LANES, half_cols:] / acc[LANES:LANES + 1, half_cols:]
    return jnp.where(_row_half_mask((LANES, half_cols), 0), out0, out1).T


_KIND_DIAG, _KIND_PREV, _KIND_FAR = 0, 1, 2


def _run_tile_pipeline(qi, logits, softmax, accumulate):
    n_far = jnp.maximum(qi - 1, 0)
    n_loop = n_far // 2

    def far_pair(t):
        accumulate(0, jnp.maximum(t - 2, 0))
        accumulate(1, jnp.maximum(t - 1, 0))
        softmax(0, t, _KIND_FAR)
        logits(0, t + 2)
        softmax(1, t + 1, _KIND_FAR)
        logits(1, t + 3)

    def quad_body(j, carry):
        far_pair(4 * j)
        far_pair(4 * j + 2)
        return carry

    def pair_body(j, carry):
        far_pair(2 * j)
        return carry

    logits(0, 0)
    logits(1, jnp.minimum(1, qi))
    lax.fori_loop(0, n_loop // 2, quad_body, 0)
    lax.fori_loop(2 * (n_loop // 2), n_loop, pair_body, 0)
    t0 = 2 * n_loop
    odd = n_far % 2 == 1

    @pl.when(odd)
    def _():
        accumulate(0, jnp.maximum(t0 - 2, 0))
        accumulate(1, jnp.maximum(t0 - 1, 0))
        softmax(0, t0, _KIND_FAR)
        logits(0, qi)
        softmax(1, qi - 1, _KIND_PREV)
        accumulate(0, t0)
        softmax(0, qi, _KIND_DIAG)
        accumulate(1, qi - 1)
        accumulate(0, qi)

    @pl.when(jnp.logical_not(odd) & (qi >= 1))
    def _():
        accumulate(0, jnp.maximum(t0 - 2, 0))
        accumulate(1, jnp.maximum(t0 - 1, 0))
        softmax(0, qi - 1, _KIND_PREV)
        softmax(1, qi, _KIND_DIAG)
        accumulate(0, qi - 1)
        accumulate(1, qi)

    @pl.when(qi == 0)
    def _():
        softmax(0, 0, _KIND_DIAG)
        accumulate(0, 0)


def _moba_kernel(bias_ref, qt_ref, k_ref, vt_ref, kmean_ref, bkt_ref, o_ref,
                 qa_ref, tab_ref, s_ref, p_ref, alpha_ref, m_ref, acc_ref):
    qi = pl.program_id(1)

    @pl.when(qi == 0)
    def _():
        for head in range(MOBA_HEADS):
            for kind in (_KIND_DIAG, _KIND_PREV):
                tab_ref[head // 2, kind, :, (head % 2) * TILE:(head % 2 + 1) * TILE] = _build_bias_tile(
                    bkt_ref[kind], bias_ref, head)

    for pair in range(MOBA_HEADS // 2):
        _moba_pair(pair, qi, bias_ref, qt_ref, k_ref, vt_ref, kmean_ref, o_ref,
                   qa_ref, tab_ref.at[pair], s_ref, p_ref, alpha_ref, m_ref, acc_ref)


def _moba_pair(pair, qi, bias_ref, qt_ref, k_ref, vt_ref, kmean_ref, o_ref,
               qa_ref, tab_ref, s_ref, p_ref, alpha_ref, m_ref, acc_ref):
    lanes = slice(pair * LANES, (pair + 1) * LANES)
    q_t = qt_ref[0, lanes, :]
    kmean = kmean_ref[0, :, lanes].astype(jnp.bfloat16)
    blk = lax.broadcasted_iota(jnp.int32, (FEAT_LO, TILE), 0)
    blk_f = blk.astype(jnp.float32)
    past = blk < qi
    lane = lax.broadcasted_iota(jnp.int32, (TILE, LANES), 1)
    _init_softmax_state(m_ref, alpha_ref, p_ref, acc_ref)

    for half in range(2):
        qh = jnp.where(_row_half_mask((LANES, TILE), half), q_t, jnp.zeros_like(q_t))
        gate = _dot(kmean, qh)[:FEAT_LO]
        g = jnp.where(past, gate, -jnp.inf)
        sel = jnp.zeros((FEAT_LO, TILE), jnp.bool_)
        for _ in range(MOBA_TOPK):
            mx = jnp.max(g, axis=0, keepdims=True)
            first = jnp.min(jnp.where(g == mx, blk_f, float(LANES)), axis=0, keepdims=True)
            pick = blk_f == first
            sel = sel | (pick & past)
            g = jnp.where(pick, -jnp.inf, g)
        b31 = jnp.full((FEAT_LO, TILE), bias_ref[(REL_BUCKETS - 1) * N_BIAS_HEADS + pair * 2 + half], jnp.float32)
        b31_hi = b31.astype(jnp.bfloat16).astype(jnp.float32)
        hi_part = jnp.where(sel, jnp.where(blk < qi - 1, b31_hi, 0.0), NEG_BIG)
        hi_part = jnp.where(past, hi_part, 0.0)
        lo_part = jnp.where(blk < qi - 1, b31 - b31_hi, 0.0)
        feat = jnp.concatenate([hi_part, lo_part, jnp.zeros((LANES - 2 * FEAT_LO, TILE), jnp.float32)], axis=0)
        qa_ref[:, half * TILE:(half + 1) * TILE] = jnp.concatenate([qh, feat.astype(jnp.bfloat16)], axis=0)

    def logits(slot, kj):
        k_t = k_ref[0, pl.ds(pl.multiple_of(kj * TILE, TILE), TILE), lanes]
        onehot = ((lane == kj) | (lane == kj + FEAT_LO)).astype(jnp.bfloat16)
        s_ref[slot] = _dot(jnp.concatenate([k_t, onehot], axis=1), qa_ref[...])

    def softmax(slot, kj, kind):
        s_t = s_ref[slot]
        if kind != _KIND_FAR:
            s_t = s_t + tab_ref[kind]
        _softmax_stage(s_t, m_ref, alpha_ref.at[slot], p_ref.at[slot])

    def accumulate(slot, kj):
        _accumulate_stage(vt_ref[0, kj, lanes, :], alpha_ref.at[slot], p_ref.at[slot], acc_ref)

    _run_tile_pipeline(qi, logits, softmax, accumulate)
    o_ref[0, :, lanes] = _normalized_pair(acc_ref[...], TILE).astype(o_ref.dtype)


def _moba_call(bias_flat, q_t, k, v_t, kmean, bkt):
    b, _, t = q_t.shape
    n_pairs = MOBA_HEADS // 2
    n_chunks = t // TILE
    grid_spec = pltpu.PrefetchScalarGridSpec(
        num_scalar_prefetch=1,
        grid=(b, n_chunks),
        in_specs=[
            pl.BlockSpec((1, MOBA_W, TILE), lambda bi, qi, s: (bi, 0, qi)),
            pl.BlockSpec((1, t, MOBA_W), lambda bi, qi, s: (bi, 0, 0)),
            pl.BlockSpec((1, n_chunks, MOBA_W, TILE), lambda bi, qi, s: (bi, 0, 0, 0)),
            pl.BlockSpec((1, LANES, MOBA_W), lambda bi, qi, s: (bi, 0, 0)),
            pl.BlockSpec((2, TILE, TILE), lambda bi, qi, s: (0, 0, 0)),
        ],
        out_specs=pl.BlockSpec((1, TILE, MOBA_W), lambda bi, qi, s: (bi, qi, 0)),
        scratch_shapes=[
            pltpu.VMEM((2 * LANES, 2 * TILE), jnp.bfloat16),
            pltpu.VMEM((n_pairs, 2, TILE, 2 * TILE), jnp.float32),
            pltpu.VMEM((2, TILE, 2 * TILE), jnp.float32),
            pltpu.VMEM((2, TILE, 2 * TILE), jnp.bfloat16),
            pltpu.VMEM((2, SUBLANES, 2 * TILE), jnp.float32),
            pltpu.VMEM((SUBLANES, 2 * TILE), jnp.float32),
            pltpu.VMEM((ACC_ROWS, 2 * TILE), jnp.float32),
        ],
    )
    return pl.pallas_call(
        _moba_kernel,
        grid_spec=grid_spec,
        out_shape=jax.ShapeDtypeStruct((b, t, MOBA_W), jnp.bfloat16),
        compiler_params=pltpu.CompilerParams(dimension_semantics=("arbitrary",) * 2, vmem_limit_bytes=VMEM_LIMIT),
        name="moba",
    )(bias_flat, q_t, k, v_t, kmean, bkt)


def _dsa_kernel(bias_ref, qt_ref, k_ref, vt_ref, iqt_ref, ik_ref, iwt_ref, bkt_ref, o_ref,
                sc_ref, tab_ref, bound_ref, qz_ref, s_ref, p_ref, alpha_ref, seen_ref, m_ref, acc_ref):
    qi = pl.program_id(1)
    n_chunks = qi + 1
    key = lax.broadcasted_iota(jnp.int32, (TILE, TILE), 0)
    qry = lax.broadcasted_iota(jnp.int32, (TILE, TILE), 1)
    causal = key <= qry

    far_bias = [bias_ref[(REL_BUCKETS - 1) * N_BIAS_HEADS + MOBA_HEADS + h] for h in range(DSA_HEADS)]

    @pl.when(qi == 0)
    def _():
        for h in range(DSA_HEADS):
            rel_max = jnp.zeros((1, 1), jnp.float32)
            for kind in (_KIND_DIAG, _KIND_PREV):
                tile = _build_bias_tile(bkt_ref[kind], bias_ref, MOBA_HEADS + h) - far_bias[h]
                tab_ref[h // 2, kind, :, (h % 2) * TILE:(h % 2 + 1) * TILE] = tile
                rel_max = jnp.maximum(rel_max, jnp.max(jnp.max(tile, axis=0, keepdims=True), axis=1, keepdims=True))
            bound_ref[h] = jnp.broadcast_to(rel_max + far_bias[h], (SUBLANES, LANES))

        for pr in range(DSA_HEADS // 2):
            def norm_body(c, mx, pr=pr):
                blk = k_ref[0, pl.ds(pl.multiple_of(c * TILE, TILE), TILE), pr * LANES:(pr + 1) * LANES]
                blk = blk.astype(jnp.float32)
                return jnp.maximum(mx, jnp.sum(blk * blk, axis=1, keepdims=True))
            n2 = lax.fori_loop(0, k_ref.shape[1] // TILE, norm_body, jnp.zeros((TILE, 1), jnp.float32))
            bound_ref[DSA_HEADS + pr] = jnp.broadcast_to(jnp.sqrt(jnp.max(n2, axis=0, keepdims=True)),
                                                         (SUBLANES, LANES))

    feat_row = lax.broadcasted_iota(jnp.int32, (LANES, TILE), 0)
    for h in range(DSA_HEADS):
        blk = qt_ref[0, (h // 2) * LANES:(h // 2 + 1) * LANES, :]
        qh = jnp.where(_row_half_mask((LANES, TILE), h % 2), blk, jnp.zeros_like(blk))
        qf = qh.astype(jnp.float32)
        q_norm = jnp.sqrt(jnp.sum(qf * qf, axis=0, keepdims=True))
        logit_bound = q_norm * bound_ref[DSA_HEADS + h // 2][:1, :1] * BOUND_SLACK + bound_ref[h][:1, :1]
        b31 = jnp.full((LANES, TILE), far_bias[h], jnp.float32)
        b31_hi = b31.astype(jnp.bfloat16).astype(jnp.float32)
        feat = jnp.where(feat_row == 0, b31_hi, jnp.where(feat_row == 1, b31 - b31_hi, 0.0))
        feat = jnp.where(feat_row == 2, -logit_bound, feat)
        qz_ref[h // 2, :, (h % 2) * TILE:(h % 2 + 1) * TILE] = jnp.concatenate(
            [qh, feat.astype(jnp.bfloat16)], axis=0)

    iw_t = iwt_ref[0]

    def score_chunk(kc):
        row0 = pl.multiple_of(kc * TILE, TILE)
        ik_t = ik_ref[0, pl.ds(row0, TILE), :]
        tot = jnp.zeros((TILE, TILE), jnp.float32)
        for h in range(IDX_HEADS):
            x = _dot(ik_t, iqt_ref[0, h * IDX_DIM:(h + 1) * IDX_DIM, :])
            tot = tot + iw_t[h:h + 1, :] * jnp.maximum(x, 0.0)
        return tot

    def score_body(kc, carry):
        mx, mn = carry
        tot = score_chunk(kc)
        sc_ref[kc] = tot
        return (jnp.maximum(mx, jnp.max(tot, axis=0, keepdims=True)),
                jnp.minimum(mn, jnp.min(tot, axis=0, keepdims=True)))

    def score_pair_body(j, carry):
        return score_body(2 * j + 1, score_body(2 * j, carry))

    def score_quad_body(j, carry):
        return score_pair_body(2 * j + 1, score_pair_body(2 * j, carry))

    extremes = lax.fori_loop(0, qi // 4, score_quad_body, (jnp.full((1, TILE), -jnp.inf, jnp.float32),
                                                           jnp.full((1, TILE), jnp.inf, jnp.float32)))
    extremes = lax.fori_loop(2 * (qi // 4), qi // 2, score_pair_body, extremes)
    row_max, row_min = lax.cond(qi % 2 == 1, lambda c: score_body(qi - 1, c), lambda c: c, extremes)
    tot = score_chunk(qi)
    sc_ref[qi] = jnp.where(causal, tot, -jnp.inf)
    row_max = jnp.maximum(row_max, jnp.max(jnp.where(causal, tot, -jnp.inf), axis=0, keepdims=True))
    row_min = jnp.minimum(row_min, jnp.min(jnp.where(causal, tot, jnp.inf), axis=0, keepdims=True))

    @pl.when(qi % 2 == 0)
    def _():
        sc_ref[qi + 1] = jnp.full((TILE, TILE), -jnp.inf, jnp.float32)

    def reduce_pass(fn, init, combine):
        def body(j, part):
            return combine(part, combine(fn(sc_ref[2 * j]), fn(sc_ref[2 * j + 1])))
        return lax.fori_loop(0, (n_chunks + 1) // 2, body, jnp.full((1, TILE), init, jnp.float32))

    def count(pred):
        return reduce_pass(lambda x: jnp.sum(jnp.where(pred(x), 1.0, 0.0), axis=0, keepdims=True), 0.0, jnp.add)

    def min_ge(p):
        return reduce_pass(lambda x: jnp.min(jnp.where(x >= p, x, jnp.inf), axis=0, keepdims=True),
                           jnp.inf, jnp.minimum)

    n_adm =(qi * TILE + 1 + lax.broadcasted_iota(jnp.int32, (1, TILE), 1)).astype(jnp.float32)
    k_top = float(DSA_TOPK)
    take_all = n_adm <= k_top

    def search_cond(st):
        return jnp.logical_and(st[0] < MAX_SEARCH_ITERS, st[-1] >= some_live)

    all_done, some_live, some_stalled = 0.0, 1.0, 2.0

    def live_flag(done, stall):
        return jnp.max(jnp.where(done < 0.5, jnp.where(stall >= STALL_STEPS, some_stalled, some_live), all_done))

    def tie_check(st):
        it, lo, hi, hi_fin, c_lo, c_hi, done, last, streak, stall, _ = st
        a = min_ge(lo)
        c_gt = count(lambda x: x > a)
        tie = (c_gt < k_top) & (done < 0.5)
        done = jnp.where(tie, 1.0, done)
        stall = jnp.zeros_like(stall)
        return (it, lo, hi, hi_fin, c_lo, jnp.where(tie, c_gt, c_hi), done, last, streak, stall,
                live_flag(done, stall))

    def search_step(st):
        it, lo, hi, hi_fin, c_lo, c_hi, done, last, streak, stall = st
        width = c_lo - c_hi
        target = k_top - 0.5 + jnp.where(streak >= 1.0, last * STALE_LEAN * (c_lo - c_hi), 0.0)
        target = jnp.clip(target, c_hi + 0.5, c_lo - 0.5)
        frac = (c_lo - target) / jnp.maximum(c_lo - c_hi, 1.0)
        mid = 0.5 * lo + 0.5 * hi_fin
        p = jnp.where(it % BISECT_EVERY == BISECT_EVERY - 1, mid, lo + frac * (hi_fin - lo))
        p = jnp.where((p > lo) & (p < hi), p, mid)
        stuck = jnp.logical_not((mid > lo) & (mid < hi))
        c = count(lambda x: x >= p)
        up = c >= k_top
        move = jnp.logical_not((done > 0.5) | stuck)
        rise = move & up
        fall = move & jnp.logical_not(up)
        lo = jnp.where(rise, p, lo)
        c_lo = jnp.where(rise, c, c_lo)
        hi = jnp.where(fall, p, hi)
        c_hi = jnp.where(fall, c, c_hi)
        hi_fin = jnp.where(fall, p, hi_fin)
        side = jnp.where(up, -1.0, 1.0)
        streak = jnp.where(side == last, streak + 1.0, 0.0)
        done = jnp.where((c_lo - c_hi <= 1.0) | (c_lo == k_top) | stuck, 1.0, done)
        stall = jnp.where(c_lo - c_hi == width, stall + 1.0, 0.0)
        return it + 1, lo, hi, hi_fin, c_lo, c_hi, done, side, streak, stall

    def search_body(st):
        st = search_step(search_step(search_step(search_step(st[:-1]))))
        flag = live_flag(st[6], st[9])
        return lax.cond(flag >= some_stalled, tie_check, lambda s: s, st + (flag,))

    zeros_row = jnp.zeros((1, TILE), jnp.float32)
    done0 = take_all.astype(jnp.float32)
    init = (jnp.int32(0), row_min, jnp.full((1, TILE), jnp.inf, jnp.float32), row_max,
            n_adm, zeros_row, done0, zeros_row, zeros_row, zeros_row, live_flag(done0, zeros_row))
    _, lo, _, _, _, c_hi, _, _, _, _, _ = lax.while_loop(search_cond, search_body, init)
    thr = jnp.where(take_all, -jnp.inf, min_ge(lo))
    need = k_top - jnp.where(take_all, 0.0, c_hi)

    n_pairs = DSA_HEADS // 2
    lower = (qry < key).astype(jnp.bfloat16)
    ones_feat = (lax.broadcasted_iota(jnp.int32, (TILE, LANES), 1) < 3).astype(jnp.bfloat16)

    def init_state():
        for pr in range(n_pairs):
            _init_softmax_state(m_ref.at[pr], alpha_ref.at[:, pr], p_ref.at[:, pr], acc_ref.at[pr])
        seen_ref[...] = jnp.zeros(seen_ref.shape, jnp.float32)

    def logits(slot, kc):
        row0 = pl.multiple_of(kc * TILE, TILE)
        for pr in range(n_pairs):
            k_t = k_ref[0, pl.ds(row0, TILE), pr * LANES:(pr + 1) * LANES]
            s_ref[slot, pr] = _dot(jnp.concatenate([k_t, ones_feat], axis=1), qz_ref[pr])

    def masked_logits(slot, kc, kind):
        x = sc_ref[kc]
        tie = x == thr
        tie_f = jnp.where(tie, 1.0, 0.0)
        seen = seen_ref[:1]
        rank = seen + _dot(lower, tie_f.astype(jnp.bfloat16))
        selected = (x > thr) | (tie & (rank < need))
        if kind == _KIND_DIAG:
            selected = selected & causal
        mask_add = jnp.where(selected, 0.0, NEG_BIG)
        mask_add = jnp.concatenate([mask_add, mask_add], axis=1)
        seen_ref[...] = jnp.broadcast_to(seen + jnp.sum(tie_f, axis=0, keepdims=True), seen_ref.shape)
        for pr in range(n_pairs):
            s_t = s_ref[slot, pr] + mask_add
            if kind != _KIND_FAR:
                s_t = s_t + tab_ref[pr, kind]
            yield pr, s_t

    def softmax_shifted(slot, kc, kind):
        for pr, s_t in masked_logits(slot, kc, kind):
            p_ref[slot, pr] = jnp.exp2(s_t).astype(jnp.bfloat16)

    def accumulate_shifted(slot, kc):
        for pr in range(n_pairs):
            acc_ref[pr] += _dot(_with_ones_rows(vt_ref[0, kc, pr * LANES:(pr + 1) * LANES, :]), p_ref[slot, pr])

    def softmax_online(slot, kc, kind):
        for pr, s_t in masked_logits(slot, kc, kind):
            _softmax_stage(s_t, m_ref.at[pr], alpha_ref.at[slot, pr], p_ref.at[slot, pr])

    def accumulate_online(slot, kc):
        for pr in range(n_pairs):
            _accumulate_stage(vt_ref[0, kc, pr * LANES:(pr + 1) * LANES, :], alpha_ref.at[slot, pr],
                              p_ref.at[slot, pr], acc_ref.at[pr])

    init_state()
    _run_tile_pipeline(qi, logits, softmax_shifted, accumulate_shifted)
    denom = acc_ref[0, LANES:LANES + 1, :]
    for pr in range(1, n_pairs):
        denom = jnp.minimum(denom, acc_ref[pr, LANES:LANES + 1, :])
    denom_min = jnp.min(denom)

    @pl.when(jnp.logical_not(denom_min > MIN_SHIFTED_DENOM))
    def _():
        init_state()
        _run_tile_pipeline(qi, logits, softmax_online, accumulate_online)

    for pr in range(n_pairs):
        o_ref[0, :, pr * LANES:(pr + 1) * LANES] = _normalized_pair(acc_ref[pr], TILE).astype(o_ref.dtype)


def _dsa_call(bias_flat, q_t, k, v_t, iq_t, ik, iw_t, bkt):
    b, _, t = q_t.shape
    n_chunks = t // TILE
    n_pairs = DSA_HEADS // 2
    grid_spec = pltpu.PrefetchScalarGridSpec(
        num_scalar_prefetch=1,
        grid=(b, n_chunks),
        in_specs=[
            pl.BlockSpec((1, DSA_W, TILE), lambda bi, qi, s: (bi, 0, qi)),
            pl.BlockSpec((1, t, DSA_W), lambda bi, qi, s: (bi, 0, 0)),
            pl.BlockSpec((1, n_chunks, DSA_W, TILE), lambda bi, qi, s: (bi, 0, 0, 0)),
            pl.BlockSpec((1, IDX_W, TILE), lambda bi, qi, s: (bi, 0, qi)),
            pl.BlockSpec((1, t, IDX_DIM), lambda bi, qi, s: (bi, 0, 0)),
            pl.BlockSpec((1, IDX_HEADS, TILE), lambda bi, qi, s: (bi, 0, qi)),
            pl.BlockSpec((2, TILE, TILE), lambda bi, qi, s: (0, 0, 0)),
        ],
        out_specs=pl.BlockSpec((1, TILE, DSA_W), lambda bi, qi, s: (bi, qi, 0)),
        scratch_shapes=[
            pltpu.VMEM((n_chunks, TILE, TILE), jnp.float32),
            pltpu.VMEM((n_pairs, 2, TILE, 2 * TILE), jnp.float32),
            pltpu.VMEM((DSA_HEADS + n_pairs, SUBLANES, LANES), jnp.float32),
            pltpu.VMEM((n_pairs, 2 * LANES, 2 * TILE), jnp.bfloat16),
            pltpu.VMEM((2, n_pairs, TILE, 2 * TILE), jnp.float32),
            pltpu.VMEM((2, n_pairs, TILE, 2 * TILE), jnp.bfloat16),
            pltpu.VMEM((2, n_pairs, SUBLANES, 2 * TILE), jnp.float32),
            pltpu.VMEM((SUBLANES, TILE), jnp.float32),
            pltpu.VMEM((n_pairs, SUBLANES, 2 * TILE), jnp.float32),
            pltpu.VMEM((n_pairs, ACC_ROWS, 2 * TILE), jnp.float32),
        ],
    )
    return pl.pallas_call(
        _dsa_kernel,
        grid_spec=grid_spec,
        out_shape=jax.ShapeDtypeStruct((b, t, DSA_W), jnp.bfloat16),
        compiler_params=pltpu.CompilerParams(dimension_semantics=("arbitrary",) * 2, vmem_limit_bytes=VMEM_LIMIT),
        name="dsa",
    )(bias_flat, q_t, k, v_t, iq_t, ik, iw_t, bkt)


def _lane_half_mask(shape, half):
    lane = lax.broadcasted_iota(jnp.int32, shape, 1)
    return (lane >= half * HEAD_DIM) & (lane < (half + 1) * HEAD_DIM)


def _final_kernel(x_ref, g_ref, fg_ref, ya_ref, za_ref, yb_ref, zb_ref, qm_ref, zm_ref, mk_ref, mv_ref,
                  wg_ref, wb_ref, wo_ref, o_ref):
    x = x_ref[...]
    h = _rmsnorm_rows(x, g_ref[...]).astype(jnp.bfloat16)

    ym_parts = []
    for pr in range(MEM_HEADS // 2):
        qp = qm_ref[:, pr * LANES:(pr + 1) * LANES]
        mk = mk_ref[0, :, pr * LANES:(pr + 1) * LANES]
        mv = mv_ref[0, :, pr * LANES:(pr + 1) * LANES]
        outs = []
        for half in range(2):
            qh = jnp.where(_lane_half_mask((ROW_TILE, LANES), half), qp, jnp.zeros_like(qp))
            s = _dot_nt(qh, mk)
            p = jnp.exp2(s - jnp.max(s, axis=1, keepdims=True))
            outs.append(_dot(p.astype(jnp.bfloat16), mv) / jnp.sum(p, axis=1, keepdims=True))
        ym_parts.append(jnp.where(_lane_half_mask((ROW_TILE, LANES), 0), outs[0], outs[1]))
    ym = jnp.concatenate(ym_parts, axis=1)

    def gated(y, z_ref):
        z = z_ref[...].astype(jnp.float32)
        return (y.astype(jnp.float32) * (z * jax.nn.sigmoid(z))).astype(jnp.bfloat16)

    ua = gated(ya_ref[...], za_ref)
    ub = gated(yb_ref[...], zb_ref)
    um = gated(ym, zm_ref)
    merged = jax.nn.sigmoid(_dot(h, wg_ref[:, :D_MODEL])) * _dot(ua, wb_ref[:MOBA_W])
    merged += jax.nn.sigmoid(_dot(h, wg_ref[:, D_MODEL:2 * D_MODEL])) * _dot(ub, wb_ref[MOBA_W:MOBA_W + DSA_W])
    merged += jax.nn.sigmoid(_dot(h, wg_ref[:, 2 * D_MODEL:])) * _dot(um, wb_ref[MOBA_W + DSA_W:])
    y = x + _dot(merged.astype(jnp.bfloat16), wo_ref[...])
    o_ref[...] = _rmsnorm_rows(y, fg_ref[...])


def _final_call(x2, gain, final_gain, ya, za, yb, zb, qm, zm, mk, mv, w_gates, w_branch, w_out, rows_per_batch):
    rows = x2.shape[0]
    tiles_per_batch = rows_per_batch // ROW_TILE
    row_spec = lambda width: pl.BlockSpec((ROW_TILE, width), lambda i: (i, 0))
    const_spec = lambda shape: pl.BlockSpec(shape, lambda i: (0,) * len(shape))
    mem_spec = pl.BlockSpec((1,) + mk.shape[1:], lambda i: (i // tiles_per_batch, 0, 0))
    return pl.pallas_call(
        _final_kernel,
        grid=(rows // ROW_TILE,),
        in_specs=[row_spec(D_MODEL), const_spec((1, D_MODEL)), const_spec((1, D_MODEL)),
                  row_spec(MOBA_W), row_spec(MOBA_W), row_spec(DSA_W), row_spec(DSA_W),
                  row_spec(MEM_W), row_spec(MEM_W), mem_spec, mem_spec,
                  const_spec(w_gates.shape), const_spec(w_branch.shape), const_spec(w_out.shape)],
        out_specs=row_spec(D_MODEL),
        out_shape=jax.ShapeDtypeStruct((rows, D_MODEL), jnp.float32),
        compiler_params=pltpu.CompilerParams(dimension_semantics=("arbitrary",), vmem_limit_bytes=VMEM_LIMIT),
        name="final",
    )(x2, gain, final_gain, ya, za, yb, zb, qm, zm, mk, mv, w_gates, w_branch, w_out)


def _layer(x, mem, norm_gain, w_in, rel_bias, mem_norm_gain, w_mem_kv, w_branch, w_out, final_gain):
    b, t, _ = x.shape
    assert t % (2 * TILE) == 0 and t // MOBA_BLOCK <= FEAT_LO and t >= 4 * DSA_TOPK
    bf = jnp.bfloat16
    names = ("qa", "ka", "va", "za", "qb", "kb", "vb", "zb", "iq", "ik", "iw", "qm", "zm", "ga", "gb", "gm")
    cols, off = {}, 0
    for name, size in zip(names, IN_SIZES):
        cols[name] = w_in[:, off:off + size]
        off += size
    w_std = jnp.concatenate([cols[n] for n, _, _ in _STD_SEGS], axis=1).astype(bf)
    w_t = jnp.concatenate([cols[n[:-1]] for n, _, _ in _T_SEGS], axis=1).T.astype(bf)
    w_gates = jnp.concatenate([cols["ga"], cols["gb"], cols["gm"]], axis=1).astype(bf)

    x2 = x.reshape(b * t, D_MODEL)
    (ka, za, kb, zb, qm, zm, ik, qa_t, va_t, qb_t, vb_t, iq_t, iw_t, kmean) = _proj_call(
        x2, norm_gain.reshape(1, D_MODEL), w_std, w_t, b, t)
    seq = lambda a: a.reshape(b, t, a.shape[-1])
    n_blocks = t // MOBA_BLOCK
    kmean = jnp.pad(kmean.reshape(b, n_blocks, MOBA_W), ((0, 0), (0, LANES - n_blocks), (0, 0)))

    bias_flat = (rel_bias * LOG2E).reshape(-1)
    ya = _moba_call(bias_flat, qa_t, seq(ka), va_t, kmean, _bucket_tiles(True))
    yb = _dsa_call(bias_flat, qb_t, seq(kb), vb_t, iq_t, seq(ik), iw_t, _bucket_tiles(False))
    mk, mv = _memkv_call(mem, mem_norm_gain.reshape(1, D_MODEL), w_mem_kv.astype(bf))
    out = _final_call(x2, norm_gain.reshape(1, D_MODEL), final_gain.reshape(1, D_MODEL),
                      ya.reshape(b * t, MOBA_W), za, yb.reshape(b * t, DSA_W), zb, qm, zm, mk, mv,
                      w_gates, w_branch.astype(bf), w_out.astype(bf), t)
    return out.reshape(b, t, D_MODEL)


@jax.jit
def kernel(x, mem, norm_gain, w_in, rel_bias, mem_norm_gain, w_mem_kv, w_branch, w_out, final_norm_gain):
    assert norm_gain.shape[0] == 1, "one layer"
    return _layer(x, mem, norm_gain[0], w_in[0], rel_bias, mem_norm_gain[0], w_mem_kv[0], w_branch[0],
                  w_out[0], final_norm_gain)
```
